```python
import math
import jax, jax.numpy as jnp
from jax import lax
import numpy as np

D_MODEL = 1024
BATCH = 8
SEQ = 8192
DEPTH = 2

HEAD_DIM = 64
N_HEADS = D_MODEL // HEAD_DIM
N_FOX = N_HEADS // 2
N_SB = N_HEADS - N_FOX
FOX_W = N_FOX * HEAD_DIM
SB_W = N_SB * HEAD_DIM
EVEN_IN = 3 * FOX_W + 3 * SB_W + N_FOX
N_Q = N_HEADS
N_KV = 4
GROUP = N_Q // N_KV
ODD_IN = N_Q * HEAD_DIM + 2 * N_KV * HEAD_DIM
WINDOW = 128
BLOCK_Q = 128
ROPE_THETA = 10000.0
D_FF = ((8 * D_MODEL // 3 + 255) // 256) * 256
PLE_DIM = 256
N_EVEN = (DEPTH + 1) // 2
N_ODD = DEPTH // 2
EPS = 1e-6
NEG_INF = -1e30

kernel_name = "hybrid_fox_stickbreak_swa_sink_block"


def _rmsnorm(x, g):
    xf = x.astype(jnp.float32)
    y = xf * lax.rsqrt(jnp.mean(xf * xf, axis=-1, keepdims=True) + EPS)
    return (y * g.astype(jnp.float32)).astype(x.dtype)


def _rope(x, pos):
    half = x.shape[-1] // 2
    inv = ROPE_THETA ** (-jnp.arange(half, dtype=jnp.float32) / half)
    ang = pos.astype(jnp.float32)[..., None] * inv
    cos = jnp.cos(ang)[:, :, None, :]
    sin = jnp.sin(ang)[:, :, None, :]
    xf = x.astype(jnp.float32)
    x1, x2 = xf[..., :half], xf[..., half:]
    out = jnp.concatenate([x1 * cos - x2 * sin, x2 * cos + x1 * sin], axis=-1)
    return out.astype(x.dtype)


def _forgetting_attention(q, k, v, log_f):
    S, d = q.shape[1], q.shape[-1]
    scale = d ** -0.5
    cum = jnp.cumsum(log_f, axis=1).transpose(0, 2, 1)
    outs = []
    for i in range(S // BLOCK_Q):
        q0, q1 = i * BLOCK_Q, (i + 1) * BLOCK_Q
        s = jnp.einsum('bqhd,bkhd->bhqk', q[:, q0:q1], k[:, :q1],
                       preferred_element_type=jnp.float32) * scale
        s = s + cum[:, :, q0:q1, None] - cum[:, :, None, :q1]
        causal = jnp.arange(q1)[None, :] <= jnp.arange(q0, q1)[:, None]
        s = jnp.where(causal, s, NEG_INF)
        w = jax.nn.softmax(s, axis=-1)
        outs.append(jnp.einsum('bhqk,bkhd->bqhd', w.astype(v.dtype), v[:, :q1]))
    return jnp.concatenate(outs, axis=1)


def _stick_breaking_attention(q, k, v):
    S, d = q.shape[1], q.shape[-1]
    scale = d ** -0.5
    outs = []
    for i in range(S // BLOCK_Q):
        q0, q1 = i * BLOCK_Q, (i + 1) * BLOCK_Q
        z = jnp.einsum('bqhd,bkhd->bhqk', q[:, q0:q1], k[:, :q1],
                       preferred_element_type=jnp.float32) * scale
        strict = jnp.arange(q1)[None, :] < jnp.arange(q0, q1)[:, None]
        log_1mb = jnp.where(strict, jax.nn.log_sigmoid(-z), 0.0)
        suffix = lax.cumsum(log_1mb, axis=3, reverse=True) - log_1mb
        a = jnp.where(strict, jnp.exp(jax.nn.log_sigmoid(z) + suffix), 0.0)
        outs.append(jnp.einsum('bhqk,bkhd->bqhd', a.astype(v.dtype), v[:, :q1]))
    return jnp.concatenate(outs, axis=1)


def _sliding_window_sink_attention(q, k, v, sinks):
    B, S, _, d = q.shape
    nb = S // WINDOW
    scale = d ** -0.5
    qb = q.reshape(B, nb, WINDOW, N_KV, GROUP, d)

    def band(x):
        xb = x.reshape(B, nb, WINDOW, N_KV, d)
        prev = jnp.concatenate([jnp.zeros_like(xb[:, :1]), xb[:, :-1]], axis=1)
        return jnp.concatenate([prev, xb], axis=2)

    kb, vb = band(k), band(v)
    s = jnp.einsum('bnqhgd,bnkhd->bnhgqk', qb, kb,
                   preferred_element_type=jnp.float32) * scale
    qi = jnp.arange(WINDOW)[:, None]
    kj = jnp.arange(2 * WINDOW)[None, :]
    rel = qi + WINDOW - kj
    valid = (rel >= 0) & (rel < WINDOW)
    first = (jnp.arange(nb)[:, None, None] == 0) & (kj[None] < WINDOW)
    mask = valid[None] & ~first
    s = jnp.where(mask[None, :, None, None], s, NEG_INF)
    sink = jnp.broadcast_to(
        sinks.astype(jnp.float32).reshape(N_KV, GROUP)[None, None, :, :, None, None],
        s.shape[:-1] + (1,))
    w = jax.nn.softmax(jnp.concatenate([s, sink], axis=-1), axis=-1)[..., :-1]
    o = jnp.einsum('bnhgqk,bnkhd->bnqhgd', w.astype(v.dtype), vb)
    return o.reshape(B, S, N_Q, d)


def _swiglu(x, w_gate, w_up, w_down):
    return (jax.nn.silu(x @ w_gate) * (x @ w_up)) @ w_down


def _fwd_setup_inputs(seed: int = 0) -> dict:
    key = jax.random.key(seed)
    ks = jax.random.split(key, 20)
    f32 = jnp.float32

    def w(k, shape, fan_in):
        return jax.random.normal(k, shape, f32) * fan_in ** -0.5

    def gain(k, shape):
        return 1.0 + 0.02 * jax.random.normal(k, shape, f32)

    return {
        "x": jax.random.normal(ks[0], (BATCH, SEQ, D_MODEL), f32),
        "p": jax.random.normal(ks[1], (DEPTH, BATCH, SEQ, PLE_DIM), f32),
        "positions": jnp.broadcast_to(jnp.arange(SEQ, dtype=jnp.int32), (BATCH, SEQ)),
        "norm_mix": gain(ks[2], (DEPTH, D_MODEL)),
        "norm_ffn": gain(ks[3], (DEPTH, D_MODEL)),
        "norm_ple": gain(ks[4], (DEPTH, D_MODEL)),
        "norm_final": gain(ks[5], (D_MODEL,)),
        "ev_w_in": w(ks[6], (N_EVEN, D_MODEL, EVEN_IN), D_MODEL),
        "ev_b_f": 0.1 * jax.random.normal(ks[7], (N_EVEN, N_FOX), f32),
        "ev_w_out": w(ks[8], (N_EVEN, FOX_W + SB_W, D_MODEL), FOX_W + SB_W),
        "od_w_in": w(ks[9], (N_ODD, D_MODEL, ODD_IN), D_MODEL),
        "od_sinks": 0.5 * jax.random.normal(ks[10], (N_ODD, N_Q), f32),
        "od_w_out": w(ks[11], (N_ODD, N_Q * HEAD_DIM, D_MODEL), N_Q * HEAD_DIM),
        "ffn_w_gate": w(ks[12], (DEPTH, D_MODEL, D_FF), D_MODEL),
        "ffn_w_up": w(ks[13], (DEPTH, D_MODEL, D_FF), D_MODEL),
        "ffn_w_down": w(ks[14], (DEPTH, D_FF, D_MODEL), D_FF),
        "ple_w_proj": w(ks[15], (DEPTH, PLE_DIM, D_MODEL), PLE_DIM),
        "ple_w_gate": w(ks[16], (DEPTH, D_MODEL, D_MODEL), D_MODEL),
    }


def _fwd_reference(x, p, positions, norm_mix, norm_ffn, norm_ple, norm_final,
              ev_w_in, ev_b_f, ev_w_out, od_w_in, od_sinks, od_w_out,
              ffn_w_gate, ffn_w_up, ffn_w_down, ple_w_proj, ple_w_gate):
    B, S, _ = x.shape
    h = x
    for i in range(DEPTH):
        hn = _rmsnorm(h, norm_mix[i])
        if i % 2 == 0:
            j = i // 2
            proj = hn @ ev_w_in[j]
            c = [0, FOX_W, 2 * FOX_W, 3 * FOX_W,
                 3 * FOX_W + SB_W, 3 * FOX_W + 2 * SB_W, 3 * FOX_W + 3 * SB_W]
            qa = proj[..., c[0]:c[1]].reshape(B, S, N_FOX, HEAD_DIM)
            ka = proj[..., c[1]:c[2]].reshape(B, S, N_FOX, HEAD_DIM)
            va = proj[..., c[2]:c[3]].reshape(B, S, N_FOX, HEAD_DIM)
            qs = proj[..., c[3]:c[4]].reshape(B, S, N_SB, HEAD_DIM)
            ks_ = proj[..., c[4]:c[5]].reshape(B, S, N_SB, HEAD_DIM)
            vs = proj[..., c[5]:c[6]].reshape(B, S, N_SB, HEAD_DIM)
            log_f = jax.nn.log_sigmoid(
                (proj[..., c[6]:] + ev_b_f[j]).astype(jnp.float32))
            o_fox = _forgetting_attention(qa, ka, va, log_f)
            o_sb = _stick_breaking_attention(qs, ks_, vs)
            mix = jnp.concatenate([o_fox.reshape(B, S, FOX_W),
                                   o_sb.reshape(B, S, SB_W)], axis=-1) @ ev_w_out[j]
        else:
            j = i // 2
            proj = hn @ od_w_in[j]
            qw, kw = N_Q * HEAD_DIM, N_KV * HEAD_DIM
            q = _rope(proj[..., :qw].reshape(B, S, N_Q, HEAD_DIM), positions)
            k = _rope(proj[..., qw:qw + kw].reshape(B, S, N_KV, HEAD_DIM), positions)
            v = proj[..., qw + kw:].reshape(B, S, N_KV, HEAD_DIM)
            o = _sliding_window_sink_attention(q, k, v, od_sinks[j])
            mix = o.reshape(B, S, qw) @ od_w_out[j]
        h = h + mix
        h = h + _swiglu(_rmsnorm(h, norm_ffn[i]), ffn_w_gate[i], ffn_w_up[i], ffn_w_down[i])
        gate = jax.nn.sigmoid(_rmsnorm(h, norm_ple[i]) @ ple_w_gate[i])
        h = h + gate * (p[i] @ ple_w_proj[i])
    return _rmsnorm(h, norm_final)


import jax as _jax
import jax.numpy as _jnp

TWIN_FORMAT = 'train_step'
FWD_PARAMS = ['x', 'p', 'positions', 'norm_mix', 'norm_ffn', 'norm_ple', 'norm_final', 'ev_w_in', 'ev_b_f', 'ev_w_out', 'od_w_in', 'od_sinks', 'od_w_out', 'ffn_w_gate', 'ffn_w_up', 'ffn_w_down', 'ple_w_proj', 'ple_w_gate']
TWIN_WEIGHTS = ['norm_mix', 'norm_ffn', 'norm_ple', 'norm_final', 'ev_w_in', 'ev_b_f', 'ev_w_out', 'od_w_in', 'od_sinks', 'od_w_out', 'ffn_w_gate', 'ffn_w_up', 'ffn_w_down', 'ple_w_proj', 'ple_w_gate']
TWIN_DIFF_INPUT = 'x'
TWIN_INPUTS = ['x', 'p', 'positions', 'norm_mix', 'norm_ffn', 'norm_ple', 'norm_final', 'ev_w_in', 'ev_b_f', 'ev_w_out', 'od_w_in', 'od_sinks', 'od_w_out', 'ffn_w_gate', 'ffn_w_up', 'ffn_w_down', 'ple_w_proj', 'ple_w_gate', 'loss_target', 'm_norm_mix', 'm_norm_ffn', 'm_norm_ple', 'm_norm_final', 'm_ev_w_in', 'm_ev_b_f', 'm_ev_w_out', 'm_od_w_in', 'm_od_sinks', 'm_od_w_out', 'm_ffn_w_gate', 'm_ffn_w_up', 'm_ffn_w_down', 'm_ple_w_proj', 'm_ple_w_gate', 'v_norm_mix', 'v_norm_ffn', 'v_norm_ple', 'v_norm_final', 'v_ev_w_in', 'v_ev_b_f', 'v_ev_w_out', 'v_od_w_in', 'v_od_sinks', 'v_od_w_out', 'v_ffn_w_gate', 'v_ffn_w_up', 'v_ffn_w_down', 'v_ple_w_proj', 'v_ple_w_gate']
TWIN_OUTPUTS = ['loss', 'grad_x', 'grad_norm_mix', 'grad_norm_ffn', 'grad_norm_ple', 'grad_norm_final', 'grad_ev_w_in', 'grad_ev_b_f', 'grad_ev_w_out', 'grad_od_w_in', 'grad_od_sinks', 'grad_od_w_out', 'grad_ffn_w_gate', 'grad_ffn_w_up', 'grad_ffn_w_down', 'grad_ple_w_proj', 'grad_ple_w_gate', 'delta_norm_mix', 'delta_norm_ffn', 'delta_norm_ple', 'delta_norm_final', 'delta_ev_w_in', 'delta_ev_b_f', 'delta_ev_w_out', 'delta_od_w_in', 'delta_od_sinks', 'delta_od_w_out', 'delta_ffn_w_gate', 'delta_ffn_w_up', 'delta_ffn_w_down', 'delta_ple_w_proj', 'delta_ple_w_gate', 'new_m_norm_mix', 'new_m_norm_ffn', 'new_m_norm_ple', 'new_m_norm_final', 'new_m_ev_w_in', 'new_m_ev_b_f', 'new_m_ev_w_out', 'new_m_od_w_in', 'new_m_od_sinks', 'new_m_od_w_out', 'new_m_ffn_w_gate', 'new_m_ffn_w_up', 'new_m_ffn_w_down', 'new_m_ple_w_proj', 'new_m_ple_w_gate', 'new_v_norm_mix', 'new_v_norm_ffn', 'new_v_norm_ple', 'new_v_norm_final', 'new_v_ev_w_in', 'new_v_ev_b_f', 'new_v_ev_w_out', 'new_v_od_w_in', 'new_v_od_sinks', 'new_v_od_w_out', 'new_v_ffn_w_gate', 'new_v_ffn_w_up', 'new_v_ffn_w_down', 'new_v_ple_w_proj', 'new_v_ple_w_gate']
TWIN_LEAF_KINDS = {'loss': 'loss', 'grad_x': 'grad_x', 'grad_norm_mix': 'grad_w', 'grad_norm_ffn': 'grad_w', 'grad_norm_ple': 'grad_w', 'grad_norm_final': 'grad_w', 'grad_ev_w_in': 'grad_w', 'grad_ev_b_f': 'grad_w', 'grad_ev_w_out': 'grad_w', 'grad_od_w_in': 'grad_w', 'grad_od_sinks': 'grad_w', 'grad_od_w_out': 'grad_w', 'grad_ffn_w_gate': 'grad_w', 'grad_ffn_w_up': 'grad_w', 'grad_ffn_w_down': 'grad_w', 'grad_ple_w_proj': 'grad_w', 'grad_ple_w_gate': 'grad_w', 'delta_norm_mix': 'delta_w', 'delta_norm_ffn': 'delta_w', 'delta_norm_ple': 'delta_w', 'delta_norm_final': 'delta_w', 'delta_ev_w_in': 'delta_w', 'delta_ev_b_f': 'delta_w', 'delta_ev_w_out': 'delta_w', 'delta_od_w_in': 'delta_w', 'delta_od_sinks': 'delta_w', 'delta_od_w_out': 'delta_w', 'delta_ffn_w_gate': 'delta_w', 'delta_ffn_w_up': 'delta_w', 'delta_ffn_w_down': 'delta_w', 'delta_ple_w_proj': 'delta_w', 'delta_ple_w_gate': 'delta_w', 'new_m_norm_mix': 'new_m', 'new_m_norm_ffn': 'new_m', 'new_m_norm_ple': 'new_m', 'new_m_norm_final': 'new_m', 'new_m_ev_w_in': 'new_m', 'new_m_ev_b_f': 'new_m', 'new_m_ev_w_out': 'new_m', 'new_m_od_w_in': 'new_m', 'new_m_od_sinks': 'new_m', 'new_m_od_w_out': 'new_m', 'new_m_ffn_w_gate': 'new_m', 'new_m_ffn_w_up': 'new_m', 'new_m_ffn_w_down': 'new_m', 'new_m_ple_w_proj': 'new_m', 'new_m_ple_w_gate': 'new_m', 'new_v_norm_mix': 'new_v', 'new_v_norm_ffn': 'new_v', 'new_v_norm_ple': 'new_v', 'new_v_norm_final': 'new_v', 'new_v_ev_w_in': 'new_v', 'new_v_ev_b_f': 'new_v', 'new_v_ev_w_out': 'new_v', 'new_v_od_w_in': 'new_v', 'new_v_od_sinks': 'new_v', 'new_v_od_w_out': 'new_v', 'new_v_ffn_w_gate': 'new_v', 'new_v_ffn_w_up': 'new_v', 'new_v_ffn_w_down': 'new_v', 'new_v_ple_w_proj': 'new_v', 'new_v_ple_w_gate': 'new_v'}


def _forward(args):
    return _fwd_reference(*[args[k] for k in FWD_PARAMS])


def _output_shape():
    def fwd():
        inp = _fwd_setup_inputs(0)
        return _fwd_reference(*[inp[k] for k in FWD_PARAMS])
    out = _jax.eval_shape(fwd)
    return out.shape, out.dtype

N_MICROBATCH = 1
ADAM_LR = 0.001
ADAM_B1 = 0.9
ADAM_B2 = 0.999
ADAM_EPS = 1e-08
ADAM_WD = 0.01
ADAM_STEP = 10
PER_EXAMPLE_BATCH_AXIS = {'x': 0, 'p': 1, 'positions': 0, 'loss_target': 0}
SHARED_INPUTS = []
_WEIGHT_DTYPES = {'norm_mix': _jnp.float32, 'norm_ffn': _jnp.float32, 'norm_ple': _jnp.float32, 'norm_final': _jnp.float32, 'ev_w_in': _jnp.float32, 'ev_b_f': _jnp.float32, 'ev_w_out': _jnp.float32, 'od_w_in': _jnp.float32, 'od_sinks': _jnp.float32, 'od_w_out': _jnp.float32, 'ffn_w_gate': _jnp.float32, 'ffn_w_up': _jnp.float32, 'ffn_w_down': _jnp.float32, 'ple_w_proj': _jnp.float32, 'ple_w_gate': _jnp.float32}
MOMENT_SCALE = {'norm_mix': 1.394625e-01, 'norm_ffn': 1.547619e-01, 'norm_ple': 3.545097e-02, 'norm_final': 6.397168e+01, 'ev_w_in': 1.057386e-01, 'ev_b_f': 9.168847e-01, 'ev_w_out': 1.483917e-01, 'od_w_in': 4.719089e-02, 'od_sinks': 2.485424e-02, 'od_w_out': 3.766741e-02, 'ffn_w_gate': 6.227042e-02, 'ffn_w_up': 6.032993e-02, 'ffn_w_down': 9.996962e-02, 'ple_w_proj': 9.026315e-02, 'ple_w_gate': 3.518646e-02}


def _to_microbatches(a, axis):
    t = _jnp.moveaxis(a, axis, 0)
    t = t.reshape((N_MICROBATCH, t.shape[0] // N_MICROBATCH) + t.shape[1:])
    return _jnp.moveaxis(t, 1, axis + 1)


def setup_inputs(seed: int = 0) -> dict:
    inp = _fwd_setup_inputs(seed)
    key = _jax.random.fold_in(_jax.random.key(seed), 7919)
    shape, _ = _output_shape()
    out = dict(inp)
    out["loss_target"] = _jax.random.normal(_jax.random.fold_in(key, 0), shape, _jnp.float32)
    for i, name in enumerate(TWIN_WEIGHTS):
        w = inp[name].astype(_jnp.float32)
        if MOMENT_SCALE is None:
            s = _jnp.sqrt(_jnp.mean(_jnp.square(w)) + 1e-30)
        else:
            s = MOMENT_SCALE[name]
        km, kv = _jax.random.split(_jax.random.fold_in(key, i + 1))
        out[name] = w
        out["m_" + name] = s * _jax.random.normal(km, w.shape, _jnp.float32)
        out["v_" + name] = (s * s) * _jax.random.uniform(kv, w.shape, _jnp.float32, 0.5, 1.5)
    if N_MICROBATCH > 1:
        for name, axis in PER_EXAMPLE_BATCH_AXIS.items():
            out[name] = _to_microbatches(out[name], axis)
    return {'x': out['x'], 'p': out['p'], 'positions': out['positions'], 'norm_mix': out['norm_mix'], 'norm_ffn': out['norm_ffn'], 'norm_ple': out['norm_ple'], 'norm_final': out['norm_final'], 'ev_w_in': out['ev_w_in'], 'ev_b_f': out['ev_b_f'], 'ev_w_out': out['ev_w_out'], 'od_w_in': out['od_w_in'], 'od_sinks': out['od_sinks'], 'od_w_out': out['od_w_out'], 'ffn_w_gate': out['ffn_w_gate'], 'ffn_w_up': out['ffn_w_up'], 'ffn_w_down': out['ffn_w_down'], 'ple_w_proj': out['ple_w_proj'], 'ple_w_gate': out['ple_w_gate'], 'loss_target': out['loss_target'], 'm_norm_mix': out['m_norm_mix'], 'm_norm_ffn': out['m_norm_ffn'], 'm_norm_ple': out['m_norm_ple'], 'm_norm_final': out['m_norm_final'], 'm_ev_w_in': out['m_ev_w_in'], 'm_ev_b_f': out['m_ev_b_f'], 'm_ev_w_out': out['m_ev_w_out'], 'm_od_w_in': out['m_od_w_in'], 'm_od_sinks': out['m_od_sinks'], 'm_od_w_out': out['m_od_w_out'], 'm_ffn_w_gate': out['m_ffn_w_gate'], 'm_ffn_w_up': out['m_ffn_w_up'], 'm_ffn_w_down': out['m_ffn_w_down'], 'm_ple_w_proj': out['m_ple_w_proj'], 'm_ple_w_gate': out['m_ple_w_gate'], 'v_norm_mix': out['v_norm_mix'], 'v_norm_ffn': out['v_norm_ffn'], 'v_norm_ple': out['v_norm_ple'], 'v_norm_final': out['v_norm_final'], 'v_ev_w_in': out['v_ev_w_in'], 'v_ev_b_f': out['v_ev_b_f'], 'v_ev_w_out': out['v_ev_w_out'], 'v_od_w_in': out['v_od_w_in'], 'v_od_sinks': out['v_od_sinks'], 'v_od_w_out': out['v_od_w_out'], 'v_ffn_w_gate': out['v_ffn_w_gate'], 'v_ffn_w_up': out['v_ffn_w_up'], 'v_ffn_w_down': out['v_ffn_w_down'], 'v_ple_w_proj': out['v_ple_w_proj'], 'v_ple_w_gate': out['v_ple_w_gate']}


def _loss(weights, diff, rest, loss_target):
    with _jax.named_scope("forward"):
        args = {**rest, TWIN_DIFF_INPUT: diff, **{k: w.astype(_WEIGHT_DTYPES[k]) for k, w in weights.items()}}
        y = _forward(args)
    with _jax.named_scope("loss_head"):
        err = _jnp.square(y.astype(_jnp.float32) - loss_target)
        return 0.5 * _jnp.sum(_jnp.mean(err, axis=-1)) if err.ndim else 0.5 * err


def _adamw(w, g, m, v):
    m = ADAM_B1 * m + (1.0 - ADAM_B1) * g
    v = ADAM_B2 * v + (1.0 - ADAM_B2) * _jnp.square(g)
    m_hat = m / (1.0 - ADAM_B1 ** ADAM_STEP)
    v_hat = v / (1.0 - ADAM_B2 ** ADAM_STEP)
    delta = -ADAM_LR * (m_hat / (_jnp.sqrt(v_hat) + ADAM_EPS) + ADAM_WD * w)
    return delta, m, v


def reference(x, p, positions, norm_mix, norm_ffn, norm_ple, norm_final, ev_w_in, ev_b_f, ev_w_out, od_w_in, od_sinks, od_w_out, ffn_w_gate, ffn_w_up, ffn_w_down, ple_w_proj, ple_w_gate, loss_target, m_norm_mix, m_norm_ffn, m_norm_ple, m_norm_final, m_ev_w_in, m_ev_b_f, m_ev_w_out, m_od_w_in, m_od_sinks, m_od_w_out, m_ffn_w_gate, m_ffn_w_up, m_ffn_w_down, m_ple_w_proj, m_ple_w_gate, v_norm_mix, v_norm_ffn, v_norm_ple, v_norm_final, v_ev_w_in, v_ev_b_f, v_ev_w_out, v_od_w_in, v_od_sinks, v_od_w_out, v_ffn_w_gate, v_ffn_w_up, v_ffn_w_down, v_ple_w_proj, v_ple_w_gate):
    given = dict(x=x, p=p, positions=positions, norm_mix=norm_mix, norm_ffn=norm_ffn, norm_ple=norm_ple, norm_final=norm_final, ev_w_in=ev_w_in, ev_b_f=ev_b_f, ev_w_out=ev_w_out, od_w_in=od_w_in, od_sinks=od_sinks, od_w_out=od_w_out, ffn_w_gate=ffn_w_gate, ffn_w_up=ffn_w_up, ffn_w_down=ffn_w_down, ple_w_proj=ple_w_proj, ple_w_gate=ple_w_gate, loss_target=loss_target, m_norm_mix=m_norm_mix, m_norm_ffn=m_norm_ffn, m_norm_ple=m_norm_ple, m_norm_final=m_norm_final, m_ev_w_in=m_ev_w_in, m_ev_b_f=m_ev_b_f, m_ev_w_out=m_ev_w_out, m_od_w_in=m_od_w_in, m_od_sinks=m_od_sinks, m_od_w_out=m_od_w_out, m_ffn_w_gate=m_ffn_w_gate, m_ffn_w_up=m_ffn_w_up, m_ffn_w_down=m_ffn_w_down, m_ple_w_proj=m_ple_w_proj, m_ple_w_gate=m_ple_w_gate, v_norm_mix=v_norm_mix, v_norm_ffn=v_norm_ffn, v_norm_ple=v_norm_ple, v_norm_final=v_norm_final, v_ev_w_in=v_ev_w_in, v_ev_b_f=v_ev_b_f, v_ev_w_out=v_ev_w_out, v_od_w_in=v_od_w_in, v_od_sinks=v_od_sinks, v_od_w_out=v_od_w_out, v_ffn_w_gate=v_ffn_w_gate, v_ffn_w_up=v_ffn_w_up, v_ffn_w_down=v_ffn_w_down, v_ple_w_proj=v_ple_w_proj, v_ple_w_gate=v_ple_w_gate)
    weights = {n: given[n] for n in TWIN_WEIGHTS}
    shared = {n: given[n] for n in SHARED_INPUTS}
    per_example = {n: given[n] for n in ['x', 'p', 'positions']}
    grad_fn = _jax.value_and_grad(_loss, argnums=(0, 1))

    def one_microbatch(ex, loss_target):
        ex = dict(ex)
        diff = ex.pop(TWIN_DIFF_INPUT)
        return grad_fn(weights, diff, {**shared, **ex}, loss_target)

    if N_MICROBATCH == 1:
        loss, (grad_w, grad_x) = one_microbatch(per_example, given["loss_target"])
    else:
        def body(carry, xs):
            loss_sum, grad_sum = carry
            l_k, (gw_k, gx_k) = one_microbatch(xs[0], xs[1])
            with _jax.named_scope("update"):
                return (loss_sum + l_k, _jax.tree.map(_jnp.add, grad_sum, gw_k)), gx_k

        init = (_jnp.zeros((), _jnp.float32), _jax.tree.map(_jnp.zeros_like, weights))
        (loss, grad_w), grad_x = _jax.lax.scan(body, init, (per_example, given["loss_target"]))
    with _jax.named_scope("update"):
        delta_w, new_m, new_v = {}, {}, {}
        for n in TWIN_WEIGHTS:
            delta_w[n], new_m[n], new_v[n] = _adamw(weights[n], grad_w[n], given["m_" + n], given["v_" + n])
    return (loss, grad_x, *[grad_w[n] for n in TWIN_WEIGHTS], *[delta_w[n] for n in TWIN_WEIGHTS],
            *[new_m[n] for n in TWIN_WEIGHTS], *[new_v[n] for n in TWIN_WEIGHTS])
```

```python
import functools

import jax
import jax.numpy as jnp
from jax import lax
from jax.experimental import pallas as pl
from jax.experimental.pallas import tpu as pltpu

F32 = jnp.float32
BF16 = jnp.bfloat16

D_MODEL = 1024
HEAD_DIM = 64
N_FOX = 8
N_SB = 8
FOX_W = N_FOX * HEAD_DIM
SB_W = N_SB * HEAD_DIM
EVEN_QKV = 3 * FOX_W + 3 * SB_W
GATE_PAD = 128
EVEN_IN_PAD = EVEN_QKV + GATE_PAD
N_Q = 16
N_KV = 4
GROUP = N_Q // N_KV
Q_W = N_Q * HEAD_DIM
KV_W = N_KV * HEAD_DIM
ODD_IN = Q_W + 2 * KV_W
WINDOW = 128
ROPE_THETA = 10000.0
D_FF = 2816
PLE_DIM = 256
EPS = 1e-6
NEG_INF = -1e30
SCALE = HEAD_DIM ** -0.5

ADAM_LR = 0.001
ADAM_B1 = 0.9
ADAM_B2 = 0.999
ADAM_EPS = 1e-08
ADAM_WD = 0.01
ADAM_STEP = 10

N_CHIPS = 4
VMEM_LIMIT = 48 * 1024 * 1024
MESH = pl.DeviceIdType.MESH

_PACK = (
    ("ev_w_in", 0, 1024, 770, True),
    ("ev_w_out", 0, 256, 1024, False),
    ("od_w_in", 0, 1024, 384, True),
    ("od_w_out", 0, 256, 1024, False),
    ("ffn_w_gate", 0, 1024, 704, True),
    ("ffn_w_gate", 1, 1024, 704, True),
    ("ffn_w_up", 0, 1024, 704, True),
    ("ffn_w_up", 1, 1024, 704, True),
    ("ffn_w_down", 0, 704, 1024, False),
    ("ffn_w_down", 1, 704, 1024, False),
    ("ple_w_proj", 0, 256, 256, True),
    ("ple_w_proj", 1, 256, 256, True),
    ("ple_w_gate", 0, 256, 1024, False),
    ("ple_w_gate", 1, 256, 1024, False),
)
_ROW_ALIGN = 16


def _pack_rows(r, c):
    n = r * c // D_MODEL
    return -(-n // _ROW_ALIGN) * _ROW_ALIGN


PACK_ROWS_HALF = 3328
PACK_ROWS = 2 * PACK_ROWS_HALF
assert sum(_pack_rows(r, c) for _, _, r, c, _ in _PACK) <= PACK_ROWS


def _pick(n, cands):
    for c in cands:
        if n % c == 0:
            return c
    return n


def _cparams(sem):
    return pltpu.CompilerParams(dimension_semantics=sem, vmem_limit_bytes=VMEM_LIMIT)


_DIMS = {
    "nn": (((1,), (0,)), ((), ())),
    "nt": (((1,), (1,)), ((), ())),
    "tn": (((0,), (0,)), ((), ())),
}


def matmul(a, b, mode="nn", out_dtype=F32, residual=None, name="mm"):
    if mode == "nn":
        (m, k), (k2, n) = a.shape, b.shape
    elif mode == "nt":
        (m, k), (n, k2) = a.shape, b.shape
    else:
        (k, m), (k2, n) = a.shape, b.shape
    assert k == k2, (a.shape, b.shape, mode)
    tm = _pick(m, (512, 256, 128))
    tn = _pick(n, (1024, 512, 640, 384, 256, 128))
    tk = _pick(k, (512, 256, 128))
    nk = k // tk
    dims = _DIMS[mode]
    has_res = residual is not None

    def body(*refs):
        if has_res:
            a_ref, b_ref, r_ref, o_ref, acc = refs
        else:
            a_ref, b_ref, o_ref, acc = refs
        kk = pl.program_id(2)

        @pl.when(kk == 0)
        def _():
            acc[...] = jnp.zeros_like(acc)

        acc[...] += lax.dot_general(a_ref[...].astype(BF16), b_ref[...].astype(BF16), dims,
                                    preferred_element_type=F32)

        @pl.when(kk == nk - 1)
        def _():
            r = acc[...]
            if has_res:
                r = r + r_ref[...]
            o_ref[...] = r.astype(out_dtype)

    if mode == "nn":
        a_spec = pl.BlockSpec((tm, tk), lambda i, j, kk: (i, kk))
        b_spec = pl.BlockSpec((tk, tn), lambda i, j, kk: (kk, j))
    elif mode == "nt":
        a_spec = pl.BlockSpec((tm, tk), lambda i, j, kk: (i, kk))
        b_spec = pl.BlockSpec((tn, tk), lambda i, j, kk: (j, kk))
    else:
        a_spec = pl.BlockSpec((tk, tm), lambda i, j, kk: (kk, i))
        b_spec = pl.BlockSpec((tk, tn), lambda i, j, kk: (kk, j))
    o_spec = pl.BlockSpec((tm, tn), lambda i, j, kk: (i, j))
    in_specs = [a_spec, b_spec] + ([o_spec] if has_res else [])
    args = (a, b) + ((residual,) if has_res else ())
    return pl.pallas_call(
        body, name=name, grid=(m // tm, n // tn, nk),
        in_specs=in_specs, out_specs=o_spec,
        out_shape=jax.ShapeDtypeStruct((m, n), out_dtype),
        scratch_shapes=[pltpu.VMEM((tm, tn), F32)],
        compiler_params=_cparams(("parallel", "parallel", "arbitrary")),
    )(*args)


def _fold8(v):
    r, w = v.shape
    return v.reshape(r // 8, 8, w).sum(axis=0)


def rowwise(fn, rows, bcasts, outs, accs=(), tr=256, name="rowwise", reverse=False):
    t = rows[0].shape[0]
    tr = _pick(t, (tr, 128, 64, 32, 16, 8))
    nr, nb, no, na = len(rows), len(bcasts), len(outs), len(accs)
    steps = t // tr

    def body(*refs):
        ins = [r[...] for r in refs[:nr + nb]]
        out_refs = refs[nr + nb:nr + nb + no]
        acc_refs = refs[nr + nb + no:]
        o, a = fn(*ins)
        for r, v in zip(out_refs, o):
            r[...] = v.astype(r.dtype)
        if na:
            @pl.when(pl.program_id(0) == 0)
            def _():
                for r in acc_refs:
                    r[...] = jnp.zeros_like(r)

            for r, v in zip(acc_refs, a):
                r[...] += v

    if reverse:
        ridx = lambda i: (steps - 1 - i, 0)
    else:
        ridx = lambda i: (i, 0)
    in_specs = [pl.BlockSpec((tr, x.shape[1]), ridx) for x in rows]
    in_specs += [pl.BlockSpec(x.shape, lambda i: (0, 0)) for x in bcasts]
    out_specs = [pl.BlockSpec((tr, w), ridx) for w, _ in outs]
    out_specs += [pl.BlockSpec((8, w), lambda i: (0, 0)) for w in accs]
    out_shape = [jax.ShapeDtypeStruct((t, w), dt) for w, dt in outs]
    out_shape += [jax.ShapeDtypeStruct((8, w), F32) for w in accs]
    res = pl.pallas_call(
        body, name=name, grid=(steps,), in_specs=in_specs, out_specs=out_specs, out_shape=out_shape,
        compiler_params=_cparams(("arbitrary",)),
    )(*rows, *bcasts)
    return res


def _rstd(x):
    return lax.rsqrt(jnp.mean(x * x, axis=-1, keepdims=True) + EPS)


def rmsnorm_fwd(h, g, name):
    def fn(x, gg):
        return ((x * _rstd(x)) * gg,), ()

    return rowwise(fn, [h], [g.reshape(1, -1)], [(D_MODEL, BF16)], name=name)[0]


def _rms_bwd_math(x, gg, dy):
    r = _rstd(x)
    xh = x * r
    u = dy * gg
    dx = r * (u - xh * jnp.mean(u * xh, axis=-1, keepdims=True))
    return dx, dy * xh


def rmsnorm_bwd(h, g, dn, dres, name):
    def fn(x, dy, dr, gg):
        dx, dgp = _rms_bwd_math(x, gg, dy)
        return (dr + dx,), (_fold8(dgp),)

    return rowwise(fn, [h, dn, dres], [g.reshape(1, -1)], [(D_MODEL, F32)], [D_MODEL], name=name)


def loss_head(h, g, tgt, name):
    def fn(x, tg, gg):
        y = (x * _rstd(x)) * gg
        e = y - tg
        dy = e * (1.0 / D_MODEL)
        dx, dgp = _rms_bwd_math(x, gg, dy)
        return (dx,), (_fold8(dgp), _fold8(e * e))

    return rowwise(fn, [h, tgt], [g.reshape(1, -1)], [(D_MODEL, F32)], [D_MODEL, D_MODEL], name=name)


def _sigmoid(x):
    return 1.0 / (1.0 + jnp.exp(-x))


def swiglu_fwd(gu, name):
    def fn(x):
        gg, uu = x[:, :D_FF], x[:, D_FF:]
        return ((gg * _sigmoid(gg)) * uu,), ()

    return rowwise(fn, [gu], [], [(D_FF, BF16)], name=name)[0]


def swiglu_bwd(gu, da, name):
    def fn(x, d):
        gg, uu = x[:, :D_FF], x[:, D_FF:]
        s = _sigmoid(gg)
        silu = gg * s
        dgg = d * uu * (s + silu * (1.0 - s))
        duu = d * silu
        return (jnp.concatenate([dgg, duu], axis=1),), ()

    return rowwise(fn, [gu, da], [], [(2 * D_FF, BF16)], name=name)[0]


def ple_fwd(h, pre, pp, name):
    def fn(x, a, b):
        return (x + _sigmoid(a) * b,), ()

    return rowwise(fn, [h, pre, pp], [], [(D_MODEL, F32)], name=name)[0]


def ple_bwd(dh, pre, pp, name):
    def fn(d, a, b):
        s = _sigmoid(a)
        return (d * b * s * (1.0 - s), d * s), ()

    return rowwise(fn, [dh, pre, pp], [], [(D_MODEL, BF16), (D_MODEL, BF16)], name=name)


def _rot_half_partner(x, first_half):
    w = x.shape[1]
    return jnp.where(first_half, pltpu.roll(x, w - HEAD_DIM // 2, 1), pltpu.roll(x, HEAD_DIM // 2, 1))


def rope_apply(xx, cosw, sinw, backward, name):
    width = xx.shape[1]
    reps = width // 128

    def fn(x, c, s):
        cw = jnp.tile(c, (1, reps))
        sw = jnp.tile(s, (1, reps))
        lane = lax.broadcasted_iota(jnp.int32, x.shape, 1)
        first = (lane % HEAD_DIM) < (HEAD_DIM // 2)
        if backward:
            return (x * cw + _rot_half_partner(x * sw, first),), ()
        return (x * cw + _rot_half_partner(x, first) * sw,), ()

    return rowwise(fn, [xx, cosw, sinw], [], [(width, F32)], name=name)[0]


def _log_sigmoid(x):
    return jnp.minimum(x, 0.0) - jnp.log(1.0 + jnp.exp(-jnp.abs(x)))


CUM_BLOCK = 256


def forget_cumsum(flog, bias, name):
    t = flog.shape[0]
    tb = _pick(t, (CUM_BLOCK,))

    def body(x_ref, b_ref, o_ref, carry):
        @pl.when(pl.program_id(0) == 0)
        def _():
            carry[...] = jnp.zeros_like(carry)

        lf = _log_sigmoid(x_ref[...] + b_ref[...])
        r = lax.broadcasted_iota(jnp.int32, (tb, tb), 0)
        c = lax.broadcasted_iota(jnp.int32, (tb, tb), 1)
        tri = (c <= r).astype(F32)
        cum = jnp.dot(tri, lf, preferred_element_type=F32, precision=lax.Precision.HIGHEST) + carry[...]
        o_ref[...] = cum
        carry[...] = cum[tb - 1:tb, :]

    return pl.pallas_call(
        body, name=name, grid=(t // tb,),
        in_specs=[pl.BlockSpec((tb, GATE_PAD), lambda i: (i, 0)), pl.BlockSpec((1, GATE_PAD), lambda i: (0, 0))],
        out_specs=pl.BlockSpec((tb, GATE_PAD), lambda i: (i, 0)),
        out_shape=jax.ShapeDtypeStruct((t, GATE_PAD), F32),
        scratch_shapes=[pltpu.VMEM((1, GATE_PAD), F32)],
        compiler_params=_cparams(("arbitrary",)),
    )(flog, bias)


def forget_cumsum_bwd(d_cum, flog, bias, name):
    t = flog.shape[0]
    tb = _pick(t, (CUM_BLOCK,))
    nb = t // tb

    def body(d_ref, x_ref, b_ref, o_ref, db_ref, carry):
        @pl.when(pl.program_id(0) == 0)
        def _():
            carry[...] = jnp.zeros_like(carry)
            db_ref[...] = jnp.zeros_like(db_ref)

        r = lax.broadcasted_iota(jnp.int32, (tb, tb), 0)
        c = lax.broadcasted_iota(jnp.int32, (tb, tb), 1)
        tri = (c >= r).astype(F32)
        dlf = jnp.dot(tri, d_ref[...], preferred_element_type=F32, precision=lax.Precision.HIGHEST) + carry[...]
        carry[...] = dlf[0:1, :]
        dx = dlf * (1.0 - _sigmoid(x_ref[...] + b_ref[...]))
        o_ref[...] = dx
        db_ref[...] += _fold8(dx)

    rev = lambda i: (nb - 1 - i, 0)
    return pl.pallas_call(
        body, name=name, grid=(nb,),
        in_specs=[pl.BlockSpec((tb, GATE_PAD), rev), pl.BlockSpec((tb, GATE_PAD), rev),
                  pl.BlockSpec((1, GATE_PAD), lambda i: (0, 0))],
        out_specs=[pl.BlockSpec((tb, GATE_PAD), rev), pl.BlockSpec((8, GATE_PAD), lambda i: (0, 0))],
        out_shape=[jax.ShapeDtypeStruct((t, GATE_PAD), F32), jax.ShapeDtypeStruct((8, GATE_PAD), F32)],
        scratch_shapes=[pltpu.VMEM((1, GATE_PAD), F32)],
        compiler_params=_cparams(("arbitrary",)),
    )(d_cum, flog, bias)


ATT_BLOCK = 256


def _dot_nt(a, b):
    return lax.dot_general(a, b, _DIMS["nt"], preferred_element_type=F32)


def _dot_tn(a, b):
    return lax.dot_general(a, b, _DIMS["tn"], preferred_element_type=F32)


def _dot(a, b):
    return jnp.dot(a, b, preferred_element_type=F32)


def _tile_iota(tb):
    r = lax.broadcasted_iota(jnp.int32, (tb, tb), 0)
    c = lax.broadcasted_iota(jnp.int32, (tb, tb), 1)
    return r, c


def _head_spec(tb):
    return pl.BlockSpec((None, tb, HEAD_DIM), lambda h, i: (h, i, 0))


def _head_full_spec(t):
    return pl.BlockSpec((None, t, HEAD_DIM), lambda h, i: (h, 0, 0))


def _col_spec(tb):
    return pl.BlockSpec((None, tb, 1), lambda h, i: (h, i, 0))


def _rowvec_spec(nk, tb):
    return pl.BlockSpec((None, nk, 1, tb), lambda h, i: (h, 0, 0, 0))


def fox_fwd(q, k, v, fcol, frow):
    nh, t, _ = q.shape
    tb = _pick(t, (ATT_BLOCK,))
    nk = t // tb

    def body(q_ref, k_ref, v_ref, fc_ref, fr_ref, o_ref, lse_ref):
        i = pl.program_id(1)
        qs = q_ref[...] * SCALE
        fq = fc_ref[...]
        row, col = _tile_iota(tb)

        def tile(j, carry, masked):
            m, l, acc = carry
            off = pl.multiple_of(j * tb, tb)
            kb = k_ref[pl.ds(off, tb), :]
            vb = v_ref[pl.ds(off, tb), :]
            s = _dot_nt(qs, kb) + (fq - fr_ref[j])
            if masked:
                s = jnp.where(col <= row, s, NEG_INF)
            m_new = jnp.maximum(m, jnp.max(s, axis=1, keepdims=True))
            p = jnp.exp(s - m_new)
            alpha = jnp.exp(m - m_new)
            l = alpha * l + jnp.sum(p, axis=1, keepdims=True)
            acc = alpha * acc + _dot(p.astype(BF16), vb)
            return m_new, l, acc

        init = (jnp.full((tb, 1), NEG_INF, F32), jnp.zeros((tb, 1), F32), jnp.zeros((tb, HEAD_DIM), F32))
        carry = lax.fori_loop(0, i, lambda j, c: tile(j, c, False), init)
        m, l, acc = tile(i, carry, True)
        o_ref[...] = acc / l
        lse_ref[...] = m + jnp.log(l)

    return pl.pallas_call(
        body, name="fox_fwd", grid=(nh, nk),
        in_specs=[_head_spec(tb), _head_full_spec(t), _head_full_spec(t), _col_spec(tb), _rowvec_spec(nk, tb)],
        out_specs=[_head_spec(tb), _col_spec(tb)],
        out_shape=[jax.ShapeDtypeStruct((nh, t, HEAD_DIM), F32), jax.ShapeDtypeStruct((nh, t, 1), F32)],
        compiler_params=_cparams(("parallel", "arbitrary")),
    )(q, k, v, fcol, frow)


def fox_bwd(q, k, v, fcol, frow, o, do, lse):
    nh, t, _ = q.shape
    tb = _pick(t, (ATT_BLOCK,))
    nk = t // tb

    def body(q_ref, k_ref, v_ref, fc_ref, fr_ref, o_ref, do_ref, lse_ref, dq_ref, dk_ref, dv_ref, df_ref, dfc_ref):
        i = pl.program_id(1)

        @pl.when(i == 0)
        def _():
            dk_ref[...] = jnp.zeros_like(dk_ref)
            dv_ref[...] = jnp.zeros_like(dv_ref)
            df_ref[...] = jnp.zeros_like(df_ref)

        qs = q_ref[...] * SCALE
        dof = do_ref[...]
        dob = dof.astype(BF16)
        delta = jnp.sum(o_ref[...] * dof, axis=1, keepdims=True)
        lse = lse_ref[...]
        fq = fc_ref[...]
        row, col = _tile_iota(tb)

        def tile(j, carry, masked):
            dq, rs = carry
            off = pl.multiple_of(j * tb, tb)
            kb = k_ref[pl.ds(off, tb), :]
            vb = v_ref[pl.ds(off, tb), :]
            s = _dot_nt(qs, kb) + (fq - fr_ref[j])
            p = jnp.exp(s - lse)
            if masked:
                p = jnp.where(col <= row, p, 0.0)
            dp = _dot_nt(dob, vb)
            ds = p * (dp - delta)
            dsb = ds.astype(BF16)
            dk_ref[pl.ds(off, tb), :] += _dot_tn(dsb, qs)
            dv_ref[pl.ds(off, tb), :] += _dot_tn(p.astype(BF16), dob)
            df_ref[j] += -jnp.sum(ds, axis=0, keepdims=True)
            return dq + _dot(dsb, kb), rs + jnp.sum(ds, axis=1, keepdims=True)

        init = (jnp.zeros((tb, HEAD_DIM), F32), jnp.zeros((tb, 1), F32))
        carry = lax.fori_loop(0, i, lambda j, c: tile(j, c, False), init)
        dq, rs = tile(i, carry, True)
        dq_ref[...] = dq * SCALE
        dfc_ref[...] = rs

    hs, hf = _head_spec(tb), _head_full_spec(t)
    return pl.pallas_call(
        body, name="fox_bwd", grid=(nh, nk),
        in_specs=[hs, hf, hf, _col_spec(tb), _rowvec_spec(nk, tb), hs, hs, _col_spec(tb)],
        out_specs=[hs, hf, hf, _rowvec_spec(nk, tb), _col_spec(tb)],
        out_shape=[jax.ShapeDtypeStruct((nh, t, HEAD_DIM), F32)] * 3
        + [jax.ShapeDtypeStruct((nh, nk, 1, tb), F32), jax.ShapeDtypeStruct((nh, t, 1), F32)],
        compiler_params=_cparams(("arbitrary", "arbitrary")),
    )(q, k, v, fcol, frow, o, do, lse)


def _split_dot(x, tri):
    hi = x.astype(BF16)
    lo = (x - hi.astype(F32)).astype(BF16)
    return _dot(hi, tri) + _dot(lo, tri)


def _sb_logits(qs, kb, strict):
    z = _dot_nt(qs, kb)
    e = jnp.exp(-jnp.abs(z))
    ll = -(jnp.maximum(z, 0.0) + jnp.log(1.0 + e))
    if strict is not None:
        ll = jnp.where(strict, ll, 0.0)
    return z, e, ll


def sb_fwd(q, k, v):
    nh, t, _ = q.shape
    tb = _pick(t, (ATT_BLOCK,))
    nk = t // tb

    def body(q_ref, k_ref, v_ref, o_ref, tot_ref):
        i = pl.program_id(1)
        qs = q_ref[...] * SCALE
        row, col = _tile_iota(tb)
        tri_after = (row > col).astype(BF16)
        strict = col < row

        def tile(j, carry, masked):
            c_l, acc = carry
            off = pl.multiple_of(j * tb, tb)
            kb = k_ref[pl.ds(off, tb), :]
            vb = v_ref[pl.ds(off, tb), :]
            z, _, ll = _sb_logits(qs, kb, strict if masked else None)
            a = jnp.exp(z + ll + _split_dot(ll, tri_after) + c_l)
            if masked:
                a = jnp.where(strict, a, 0.0)
            return c_l + jnp.sum(ll, axis=1, keepdims=True), acc + _dot(a.astype(BF16), vb)

        carry = tile(i, (jnp.zeros((tb, 1), F32), jnp.zeros((tb, HEAD_DIM), F32)), True)
        c_l, acc = lax.fori_loop(0, i, lambda jj, c: tile(i - 1 - jj, c, False), carry)
        o_ref[...] = acc
        tot_ref[...] = c_l

    return pl.pallas_call(
        body, name="sb_fwd", grid=(nh, nk),
        in_specs=[_head_spec(tb), _head_full_spec(t), _head_full_spec(t)],
        out_specs=[_head_spec(tb), _col_spec(tb)],
        out_shape=[jax.ShapeDtypeStruct((nh, t, HEAD_DIM), F32), jax.ShapeDtypeStruct((nh, t, 1), F32)],
        compiler_params=_cparams(("parallel", "arbitrary")),
    )(q, k, v)


def sb_bwd(q, k, v, ltot, do):
    nh, t, _ = q.shape
    tb = _pick(t, (ATT_BLOCK,))
    nk = t // tb

    def body(q_ref, k_ref, v_ref, tot_ref, do_ref, dq_ref, dk_ref, dv_ref):
        i = pl.program_id(1)

        @pl.when(i == 0)
        def _():
            dk_ref[...] = jnp.zeros_like(dk_ref)
            dv_ref[...] = jnp.zeros_like(dv_ref)

        qs = q_ref[...] * SCALE
        dob = do_ref[...].astype(BF16)
        row, col = _tile_iota(tb)
        tri_upto = (row <= col).astype(BF16)
        tri_before = (row < col).astype(BF16)
        strict = col < row

        def tile(j, carry, masked):
            rest, c_w, dq = carry
            off = pl.multiple_of(j * tb, tb)
            kb = k_ref[pl.ds(off, tb), :]
            vb = v_ref[pl.ds(off, tb), :]
            z, e, ll = _sb_logits(qs, kb, strict if masked else None)
            a = jnp.exp(z + ll + (rest - _split_dot(ll, tri_upto)))
            if masked:
                a = jnp.where(strict, a, 0.0)
            w = a * _dot_nt(dob, vb)
            before = _split_dot(w, tri_before) + c_w
            r = 1.0 / (1.0 + e)
            sig = jnp.where(z >= 0.0, r, e * r)
            dz = w * (1.0 - sig) - before * sig
            if masked:
                dz = jnp.where(strict, dz, 0.0)
            dzb = dz.astype(BF16)
            dk_ref[pl.ds(off, tb), :] += _dot_tn(dzb, qs)
            dv_ref[pl.ds(off, tb), :] += _dot_tn(a.astype(BF16), dob)
            return (rest - jnp.sum(ll, axis=1, keepdims=True), c_w + jnp.sum(w, axis=1, keepdims=True),
                    dq + _dot(dzb, kb))

        init = (tot_ref[...], jnp.zeros((tb, 1), F32), jnp.zeros((tb, HEAD_DIM), F32))
        carry = lax.fori_loop(0, i, lambda j, c: tile(j, c, False), init)
        _, _, dq = tile(i, carry, True)
        dq_ref[...] = dq * SCALE

    hs, hf = _head_spec(tb), _head_full_spec(t)
    return pl.pallas_call(
        body, name="sb_bwd", grid=(nh, nk),
        in_specs=[hs, hf, hf, _col_spec(tb), hs],
        out_specs=[hs, hf, hf],
        out_shape=[jax.ShapeDtypeStruct((nh, t, HEAD_DIM), F32)] * 3,
        compiler_params=_cparams(("arbitrary", "arbitrary")),
    )(q, k, v, ltot, do)


def _swa_specs(t):
    qs = pl.BlockSpec((None, WINDOW, HEAD_DIM), lambda h, i: (h, i, 0))
    kvs = pl.BlockSpec((None, t, HEAD_DIM), lambda h, i: (h // GROUP, 0, 0))
    cs = pl.BlockSpec((None, WINDOW, 1), lambda h, i: (h, i, 0))
    ss = pl.BlockSpec((None, 1, 128), lambda h, i: (h, 0, 0))
    return qs, kvs, cs, ss


def _swa_scores(qs, k_ref, i):
    start = pl.multiple_of(jnp.maximum(i - 1, 0) * WINDOW, WINDOW)
    kwin = k_ref[pl.ds(start, 2 * WINDOW), :]
    s = _dot_nt(qs, kwin)
    tq = i * WINDOW + lax.broadcasted_iota(jnp.int32, s.shape, 0)
    tk = start + lax.broadcasted_iota(jnp.int32, s.shape, 1)
    rel = tq - tk
    return start, kwin, s, (rel >= 0) & (rel < WINDOW)


def swa_fwd(q, k, v, sinks):
    nh, t, _ = q.shape
    qspec, kvspec, cspec, sspec = _swa_specs(t)

    def body(q_ref, k_ref, v_ref, s_ref, o_ref, lse_ref):
        i = pl.program_id(1)
        qs = q_ref[...] * SCALE
        sink = s_ref[:, 0:1]
        start, _, s, valid = _swa_scores(qs, k_ref, i)
        s = jnp.where(valid, s, NEG_INF)
        m = jnp.maximum(jnp.max(s, axis=1, keepdims=True), sink)
        p = jnp.where(valid, jnp.exp(s - m), 0.0)
        l = jnp.sum(p, axis=1, keepdims=True) + jnp.exp(sink - m)
        o_ref[...] = _dot(p.astype(BF16), v_ref[pl.ds(start, 2 * WINDOW), :]) / l
        lse_ref[...] = m + jnp.log(l)

    return pl.pallas_call(
        body, name="swa_fwd", grid=(nh, t // WINDOW),
        in_specs=[qspec, kvspec, kvspec, sspec],
        out_specs=[qspec, cspec],
        out_shape=[jax.ShapeDtypeStruct((nh, t, HEAD_DIM), F32), jax.ShapeDtypeStruct((nh, t, 1), F32)],
        compiler_params=_cparams(("parallel", "arbitrary")),
    )(q, k, v, sinks)


def swa_bwd(q, k, v, sinks, o, do, lse):
    nh, t, _ = q.shape
    qspec, kvspec, cspec, sspec = _swa_specs(t)
    dsspec = pl.BlockSpec((None, 8, 128), lambda h, i: (h, 0, 0))

    def body(q_ref, k_ref, v_ref, s_ref, o_ref, do_ref, lse_ref, dq_ref, dk_ref, dv_ref, dsink_ref):
        h = pl.program_id(0)
        i = pl.program_id(1)

        @pl.when((i == 0) & (h % GROUP == 0))
        def _():
            dk_ref[...] = jnp.zeros_like(dk_ref)
            dv_ref[...] = jnp.zeros_like(dv_ref)

        @pl.when(i == 0)
        def _():
            dsink_ref[...] = jnp.zeros_like(dsink_ref)

        qs = q_ref[...] * SCALE
        sink = s_ref[:, 0:1]
        dof = do_ref[...]
        dob = dof.astype(BF16)
        delta = jnp.sum(o_ref[...] * dof, axis=1, keepdims=True)
        lse = lse_ref[...]
        start, kwin, s, valid = _swa_scores(qs, k_ref, i)
        p = jnp.where(valid, jnp.exp(s - lse), 0.0)
        dp = _dot_nt(dob, v_ref[pl.ds(start, 2 * WINDOW), :])
        ds = p * (dp - delta)
        dsb = ds.astype(BF16)
        dq_ref[...] = _dot(dsb, kwin) * SCALE
        dk_ref[pl.ds(start, 2 * WINDOW), :] += _dot_tn(dsb, qs)
        dv_ref[pl.ds(start, 2 * WINDOW), :] += _dot_tn(p.astype(BF16), dob)
        dsink_ref[...] += jnp.sum(-jnp.exp(sink - lse) * delta, axis=0, keepdims=True)

    return pl.pallas_call(
        body, name="swa_bwd", grid=(nh, t // WINDOW),
        in_specs=[qspec, kvspec, kvspec, sspec, qspec, qspec, cspec],
        out_specs=[qspec, kvspec, kvspec, dsspec],
        out_shape=[jax.ShapeDtypeStruct((nh, t, HEAD_DIM), F32),
                   jax.ShapeDtypeStruct((N_KV, t, HEAD_DIM), F32),
                   jax.ShapeDtypeStruct((N_KV, t, HEAD_DIM), F32),
                   jax.ShapeDtypeStruct((nh, 8, 128), F32)],
        compiler_params=_cparams(("arbitrary", "arbitrary")),
    )(q, k, v, sinks, o, do, lse)


def adamw(w, g, m, v, name):
    cols = w.shape[1]

    def fn(ww, gg, mm, vv):
        mn = ADAM_B1 * mm + (1.0 - ADAM_B1) * gg
        vn = ADAM_B2 * vv + (1.0 - ADAM_B2) * (gg * gg)
        m_hat = mn / (1.0 - ADAM_B1 ** ADAM_STEP)
        v_hat = vn / (1.0 - ADAM_B2 ** ADAM_STEP)
        delta = -ADAM_LR * (m_hat / (jnp.sqrt(v_hat) + ADAM_EPS) + ADAM_WD * ww)
        return (delta, mn, vn), ()

    return rowwise(fn, [w, g, m, v], [], [(cols, F32)] * 3, name=name)


ANY = pl.BlockSpec(memory_space=pl.ANY)


def _my_place():
    return lax.axis_index("x"), lax.axis_index("y"), lax.axis_index("c")


def _flip(coord, bit):
    return 1 - coord if bit else coord


def allgather_chips(w):
    r, c = w.shape

    def body(w_ref, out_ref, send_sems, recv_sems, local_sem):
        x, y, cc = _my_place()
        me = 2 * x + y
        local = pltpu.make_async_copy(w_ref, out_ref.at[me], local_sem)
        local.start()
        copies = []
        for kbits in (1, 2, 3):
            peer = (_flip(x, kbits >> 1), _flip(y, kbits & 1), cc)
            cp = pltpu.make_async_remote_copy(
                src_ref=w_ref, dst_ref=out_ref.at[me], send_sem=send_sems.at[kbits - 1],
                recv_sem=recv_sems.at[kbits - 1], device_id=peer, device_id_type=MESH)
            cp.start()
            copies.append(cp)
        for cp in copies:
            cp.wait()
        local.wait()

    return pl.pallas_call(
        body, name="allgather_chips", in_specs=[ANY], out_specs=ANY,
        out_shape=jax.ShapeDtypeStruct((N_CHIPS, r, c), w.dtype),
        scratch_shapes=[pltpu.SemaphoreType.DMA((3,)), pltpu.SemaphoreType.DMA((3,)), pltpu.SemaphoreType.DMA],
    )(w)


def pair_send_other_half(g):
    n, _, rh, c = g.shape

    def body(g_ref, out_ref, send_sem, recv_sem):
        x, y, cc = _my_place()
        cp = pltpu.make_async_remote_copy(
            src_ref=g_ref.at[:, 1 - cc], dst_ref=out_ref, send_sem=send_sem, recv_sem=recv_sem,
            device_id=(x, y, 1 - cc), device_id_type=MESH)
        cp.start()
        cp.wait()

    return pl.pallas_call(
        body, name="pair_send_other_half", in_specs=[ANY], out_specs=ANY,
        out_shape=jax.ShapeDtypeStruct((n, rh, c), g.dtype),
        scratch_shapes=[pltpu.SemaphoreType.DMA, pltpu.SemaphoreType.DMA],
    )(g)


def exchange_chips(s):
    n, rh, c = s.shape

    def body(s_ref, out_ref, send_sems, recv_sems, local_sem):
        x, y, cc = _my_place()
        me = 2 * x + y
        local = pltpu.make_async_copy(s_ref.at[me], out_ref.at[me], local_sem)
        local.start()
        copies = []
        for kbits in (1, 2, 3):
            px, py = _flip(x, kbits >> 1), _flip(y, kbits & 1)
            cp = pltpu.make_async_remote_copy(
                src_ref=s_ref.at[2 * px + py], dst_ref=out_ref.at[me], send_sem=send_sems.at[kbits - 1],
                recv_sem=recv_sems.at[kbits - 1], device_id=(px, py, cc), device_id_type=MESH)
            cp.start()
            copies.append(cp)
        for cp in copies:
            cp.wait()
        local.wait()

    return pl.pallas_call(
        body, name="exchange_chips", in_specs=[ANY], out_specs=ANY,
        out_shape=jax.ShapeDtypeStruct((n, rh, c), s.dtype),
        scratch_shapes=[pltpu.SemaphoreType.DMA((3,)), pltpu.SemaphoreType.DMA((3,)), pltpu.SemaphoreType.DMA],
    )(s)


def pair_gather(tt):
    rh, c = tt.shape

    def body(t_ref, out_ref, send_sem, recv_sem, local_sem):
        x, y, cc = _my_place()
        local = pltpu.make_async_copy(t_ref, out_ref.at[cc], local_sem)
        local.start()
        cp = pltpu.make_async_remote_copy(
            src_ref=t_ref, dst_ref=out_ref.at[cc], send_sem=send_sem, recv_sem=recv_sem,
            device_id=(x, y, 1 - cc), device_id_type=MESH)
        cp.start()
        cp.wait()
        local.wait()

    return pl.pallas_call(
        body, name="pair_gather", in_specs=[ANY], out_specs=ANY,
        out_shape=jax.ShapeDtypeStruct((2, rh, c), tt.dtype),
        scratch_shapes=[pltpu.SemaphoreType.DMA, pltpu.SemaphoreType.DMA, pltpu.SemaphoreType.DMA],
    )(tt)


SMALL_ROWS = 16


def allreduce_small(v):
    r, c = v.shape
    vm = pl.BlockSpec(memory_space=pltpu.VMEM)

    def body(v_ref, out_ref, slots, send_sems, recv_sems):
        x, y, cc = _my_place()
        me = 4 * x + 2 * y + cc
        slots[me] = v_ref[...]
        copies = []
        for kbits in range(1, 8):
            peer = (_flip(x, kbits >> 2), _flip(y, (kbits >> 1) & 1), _flip(cc, kbits & 1))
            cp = pltpu.make_async_remote_copy(
                src_ref=v_ref, dst_ref=slots.at[me], send_sem=send_sems.at[kbits - 1],
                recv_sem=recv_sems.at[kbits - 1], device_id=peer, device_id_type=MESH)
            cp.start()
            copies.append(cp)
        for cp in copies:
            cp.wait()
        total = slots[0]
        for dev in range(1, 8):
            total = total + slots[dev]
        out_ref[...] = total

    return pl.pallas_call(
        body, name="allreduce_small", in_specs=[vm], out_specs=vm,
        out_shape=jax.ShapeDtypeStruct((r, c), F32),
        scratch_shapes=[pltpu.VMEM((8, r, c), F32), pltpu.SemaphoreType.DMA((7,)), pltpu.SemaphoreType.DMA((7,))],
    )(v)


def add_pair(mine, theirs, name):
    return rowwise(lambda a, b: ((a + b,), ()), [mine, theirs], [], [(mine.shape[1], BF16)], name=name)[0]


def sum_chips(r4, name):
    _, rh, c = r4.shape
    tr = _pick(rh, (256, 128, 64, 32, 16))

    def body(r_ref, o_ref):
        total = r_ref[0].astype(F32)
        for j in range(1, N_CHIPS):
            total = total + r_ref[j].astype(F32)
        o_ref[...] = total

    return pl.pallas_call(
        body, name=name, grid=(rh // tr,),
        in_specs=[pl.BlockSpec((N_CHIPS, tr, c), lambda i: (0, i, 0))],
        out_specs=pl.BlockSpec((tr, c), lambda i: (i, 0)),
        out_shape=jax.ShapeDtypeStruct((rh, c), F32),
        compiler_params=_cparams(("parallel",)),
    )(r4)


def _pad_rows(a, rows):
    return jnp.pad(a, ((0, rows - a.shape[0]), (0, 0))) if rows != a.shape[0] else a


def pack_shards(local):
    parts = []
    for name, layer, r, c, _ in _PACK:
        flat = local[name][layer].astype(BF16).reshape(r * c // D_MODEL, D_MODEL)
        parts.append(_pad_rows(flat, _pack_rows(r, c)))
    used = sum(p.shape[0] for p in parts)
    parts.append(jnp.zeros((PACK_ROWS - used, D_MODEL), BF16))
    return jnp.concatenate(parts, axis=0)


def unpack_full(gathered):
    full, off = {}, 0
    for name, layer, r, c, by_cols in _PACK:
        n = r * c // D_MODEL
        blk = gathered[:, off:off + n, :].reshape(N_CHIPS, r, c)
        if by_cols:
            full[(name, layer)] = blk.transpose(1, 0, 2).reshape(r, N_CHIPS * c)
        else:
            full[(name, layer)] = blk.reshape(N_CHIPS * r, c)
        off += _pack_rows(r, c)
    return full


def pack_grads(grads):
    parts = []
    for name, layer, r, c, by_cols in _PACK:
        g = grads[(name, layer)]
        if by_cols:
            blk = g.reshape(r, N_CHIPS, c).transpose(1, 0, 2)
        else:
            blk = g.reshape(N_CHIPS, r, c)
        flat = blk.reshape(N_CHIPS, r * c // D_MODEL, D_MODEL)
        rows = _pack_rows(r, c)
        if rows != flat.shape[1]:
            flat = jnp.pad(flat, ((0, 0), (0, rows - flat.shape[1]), (0, 0)))
        parts.append(flat)
    used = sum(p.shape[1] for p in parts)
    parts.append(jnp.zeros((N_CHIPS, PACK_ROWS - used, D_MODEL), F32))
    return jnp.concatenate(parts, axis=1)


def unpack_local(flat):
    out, off = {}, 0
    for name, layer, r, c, _ in _PACK:
        n = r * c // D_MODEL
        out[(name, layer)] = flat[off:off + n, :].reshape(r, c)
        off += _pack_rows(r, c)
    return out


def to_heads(a, nh, dtype=BF16):
    t = a.shape[0]
    return a.reshape(t, nh, HEAD_DIM).transpose(1, 0, 2).astype(dtype)


def from_heads(a):
    nh, t, _ = a.shape
    return a.transpose(1, 0, 2).reshape(t, nh * HEAD_DIM)


def _forget_vectors(cum, tb):
    t = cum.shape[0]
    fh = cum[:, :N_FOX].T
    return fh.reshape(N_FOX, t, 1), fh.reshape(N_FOX, t // tb, 1, tb)


def rope_tables(pos):
    half = HEAD_DIM // 2
    lane = jnp.arange(128)
    inv = ROPE_THETA ** (-(lane % half).astype(F32) / half)
    ang = pos.astype(F32)[:, None] * inv[None, :]
    sign = jnp.where((lane % HEAD_DIM) < half, -1.0, 1.0).astype(F32)
    return jnp.cos(ang), jnp.sin(ang) * sign[None, :]


def local_step(x, p, pos, tgt, norm_mix, norm_ffn, norm_ple, norm_final, ev_b_f, od_sinks, wfull):
    t = x.shape[0]
    tb = _pick(t, (ATT_BLOCK,))
    w_in0 = wfull[("ev_w_in", 0)]
    w_in0 = jnp.concatenate([w_in0, jnp.zeros((D_MODEL, EVEN_IN_PAD - w_in0.shape[1]), w_in0.dtype)], axis=1)
    w_gu = [jnp.concatenate([wfull[("ffn_w_gate", i)], wfull[("ffn_w_up", i)]], axis=1) for i in range(2)]
    b_pad = jnp.zeros((1, GATE_PAD), F32).at[0, :N_FOX].set(ev_b_f[0])
    sinks_b = jnp.broadcast_to(od_sinks[0][:, None, None], (N_Q, 1, 128)).astype(F32)
    cosw, sinw = rope_tables(pos)

    saved = []
    h = x
    for i in range(2):
        s = {"h0": h}
        n1 = rmsnorm_fwd(h, norm_mix[i], f"norm_mix_fwd{i}")
        s["n1"] = n1
        if i == 0:
            proj = matmul(n1, w_in0, name="ev_in")
            qkv = proj[:, :EVEN_QKV].reshape(t, 6, N_FOX, HEAD_DIM).transpose(1, 2, 0, 3).astype(BF16)
            flog = proj[:, EVEN_QKV:]
            cum = forget_cumsum(flog, b_pad, "forget_cumsum")
            fcol, frow = _forget_vectors(cum, tb)
            o_fox, lse = fox_fwd(qkv[0], qkv[1], qkv[2], fcol, frow)
            o_sb, ltot = sb_fwd(qkv[3], qkv[4], qkv[5])
            s.update(qkv=qkv, flog=flog, fcol=fcol, frow=frow, o_fox=o_fox, lse=lse, ltot=ltot)
            mixin = jnp.concatenate([from_heads(o_fox), from_heads(o_sb)], axis=1).astype(BF16)
            w_out = wfull[("ev_w_out", 0)]
        else:
            proj = matmul(n1, wfull[("od_w_in", 0)], name="od_in")
            qk = rope_apply(proj[:, :Q_W + KV_W], cosw, sinw, False, "rope_fwd")
            q = to_heads(qk[:, :Q_W], N_Q)
            k = to_heads(qk[:, Q_W:], N_KV)
            v = to_heads(proj[:, Q_W + KV_W:], N_KV)
            o_swa, lse = swa_fwd(q, k, v, sinks_b)
            s.update(q=q, k=k, v=v, o_swa=o_swa, lse=lse)
            mixin = from_heads(o_swa).astype(BF16)
            w_out = wfull[("od_w_out", 0)]
        s["mixin"] = mixin
        h = matmul(mixin, w_out, residual=h, name=f"mix_out{i}")
        s["h1"] = h
        n2 = rmsnorm_fwd(h, norm_ffn[i], f"norm_ffn_fwd{i}")
        gu = matmul(n2, w_gu[i], name=f"ffn_gu{i}")
        act = swiglu_fwd(gu, f"swiglu_fwd{i}")
        s.update(n2=n2, gu=gu, act=act)
        h = matmul(act, wfull[("ffn_w_down", i)], residual=h, name=f"ffn_down{i}")
        s["h2"] = h
        n3 = rmsnorm_fwd(h, norm_ple[i], f"norm_ple_fwd{i}")
        pre = matmul(n3, wfull[("ple_w_gate", i)], name=f"ple_gate{i}")
        pp = matmul(p[i], wfull[("ple_w_proj", i)], name=f"ple_proj{i}")
        s.update(n3=n3, pre=pre, pp=pp)
        h = ple_fwd(h, pre, pp, f"ple_fwd{i}")
        saved.append(s)

    dh, dg_final, loss8 = loss_head(h, norm_final, tgt, "loss_head")
    gw = {}
    small = {"norm_final": dg_final, "loss": loss8}
    for i in (1, 0):
        s = saved[i]
        dpre, dpp = ple_bwd(dh, s["pre"], s["pp"], f"ple_bwd{i}")
        gw[("ple_w_gate", i)] = matmul(s["n3"], dpre, "tn", name=f"d_ple_gate{i}")
        gw[("ple_w_proj", i)] = matmul(p[i], dpp, "tn", name=f"d_ple_proj{i}")
        dn3 = matmul(dpre, wfull[("ple_w_gate", i)], "nt", name=f"dn_ple{i}")
        dh, small[("norm_ple", i)] = rmsnorm_bwd(s["h2"], norm_ple[i], dn3, dh, f"norm_ple_bwd{i}")

        dact = matmul(dh, wfull[("ffn_w_down", i)], "nt", name=f"d_act{i}")
        gw[("ffn_w_down", i)] = matmul(s["act"], dh, "tn", name=f"d_ffn_down{i}")
        dgu = swiglu_bwd(s["gu"], dact, f"swiglu_bwd{i}")
        dwgu = matmul(s["n2"], dgu, "tn", name=f"d_ffn_gu{i}")
        gw[("ffn_w_gate", i)] = dwgu[:, :D_FF]
        gw[("ffn_w_up", i)] = dwgu[:, D_FF:]
        dn2 = matmul(dgu, w_gu[i], "nt", name=f"dn_ffn{i}")
        dh, small[("norm_ffn", i)] = rmsnorm_bwd(s["h1"], norm_ffn[i], dn2, dh, f"norm_ffn_bwd{i}")

        if i == 0:
            dmix = matmul(dh, wfull[("ev_w_out", 0)], "nt", name="d_mix0")
            gw[("ev_w_out", 0)] = matmul(s["mixin"], dh, "tn", name="d_ev_out")
            qkv = s["qkv"]
            do_fox = to_heads(dmix[:, :FOX_W], N_FOX, F32)
            do_sb = to_heads(dmix[:, FOX_W:], N_SB, F32)
            dqf, dkf, dvf, dfrow, dfcol = fox_bwd(qkv[0], qkv[1], qkv[2], s["fcol"], s["frow"], s["o_fox"], do_fox,
                                                  s["lse"])
            dqs, dks, dvs = sb_bwd(qkv[3], qkv[4], qkv[5], s["ltot"], do_sb)
            d_cum = jnp.pad((dfrow.reshape(N_FOX, t) + dfcol.reshape(N_FOX, t)).T, ((0, 0), (0, GATE_PAD - N_FOX)))
            dflog, db8 = forget_cumsum_bwd(d_cum, s["flog"], b_pad, "forget_cumsum_bwd")
            small["ev_b_f"] = db8
            dproj = jnp.concatenate(
                [from_heads(a).astype(BF16) for a in (dqf, dkf, dvf, dqs, dks, dvs)] + [dflog.astype(BF16)], axis=1)
            dw = matmul(s["n1"], dproj, "tn", name="d_ev_in")
            gw[("ev_w_in", 0)] = dw[:, :EVEN_QKV + N_FOX]
            dn1 = matmul(dproj, w_in0, "nt", name="dn_mix0")
        else:
            dmix = matmul(dh, wfull[("od_w_out", 0)], "nt", name="d_mix1")
            gw[("od_w_out", 0)] = matmul(s["mixin"], dh, "tn", name="d_od_out")
            do = to_heads(dmix, N_Q, F32)
            dq, dk, dv, dsink = swa_bwd(s["q"], s["k"], s["v"], sinks_b, s["o_swa"], do, s["lse"])
            small["od_sinks"] = dsink
            dqk = rope_apply(jnp.concatenate([from_heads(dq), from_heads(dk)], axis=1), cosw, sinw, True, "rope_bwd")
            dproj = jnp.concatenate([dqk, from_heads(dv)], axis=1).astype(BF16)
            gw[("od_w_in", 0)] = matmul(s["n1"], dproj, "tn", name="d_od_in")
            dn1 = matmul(dproj, wfull[("od_w_in", 0)], "nt", name="dn_mix1")
        dh, small[("norm_mix", i)] = rmsnorm_bwd(s["h0"], norm_mix[i], dn1, dh, f"norm_mix_bwd{i}")
    return dh, gw, small


_SMALL_ROWS = (("norm_mix", 0), ("norm_mix", 1), ("norm_ffn", 0), ("norm_ffn", 1),
               ("norm_ple", 0), ("norm_ple", 1), "norm_final", "misc", "loss")


def pack_small(small):
    rows = []
    for key in _SMALL_ROWS:
        if key == "misc":
            db = jnp.sum(small["ev_b_f"], axis=0)[:N_FOX]
            dsink = small["od_sinks"][:, 0, 0]
            rows.append(jnp.zeros((D_MODEL,), F32).at[:N_FOX].set(db).at[128:128 + N_Q].set(dsink))
        else:
            rows.append(jnp.sum(small[key], axis=0))
    rows += [jnp.zeros((D_MODEL,), F32)] * (SMALL_ROWS - len(rows))
    return jnp.stack(rows)


def kernel(x, p, positions, norm_mix, norm_ffn, norm_ple, norm_final, ev_w_in, ev_b_f, ev_w_out, od_w_in, od_sinks, od_w_out, ffn_w_gate, ffn_w_up, ffn_w_down, ple_w_proj, ple_w_gate, loss_target, m_norm_mix, m_norm_ffn, m_norm_ple, m_norm_final, m_ev_w_in, m_ev_b_f, m_ev_w_out, m_od_w_in, m_od_sinks, m_od_w_out, m_ffn_w_gate, m_ffn_w_up, m_ffn_w_down, m_ple_w_proj, m_ple_w_gate, v_norm_mix, v_norm_ffn, v_norm_ple, v_norm_final, v_ev_w_in, v_ev_b_f, v_ev_w_out, v_od_w_in, v_od_sinks, v_od_w_out, v_ffn_w_gate, v_ffn_w_up, v_ffn_w_down, v_ple_w_proj, v_ple_w_gate):
    local_w = dict(ev_w_in=ev_w_in, ev_w_out=ev_w_out, od_w_in=od_w_in, od_w_out=od_w_out,
                   ffn_w_gate=ffn_w_gate, ffn_w_up=ffn_w_up, ffn_w_down=ffn_w_down,
                   ple_w_proj=ple_w_proj, ple_w_gate=ple_w_gate)
    local_m = dict(ev_w_in=m_ev_w_in, ev_w_out=m_ev_w_out, od_w_in=m_od_w_in, od_w_out=m_od_w_out,
                   ffn_w_gate=m_ffn_w_gate, ffn_w_up=m_ffn_w_up, ffn_w_down=m_ffn_w_down,
                   ple_w_proj=m_ple_w_proj, ple_w_gate=m_ple_w_gate)
    local_v = dict(ev_w_in=v_ev_w_in, ev_w_out=v_ev_w_out, od_w_in=v_od_w_in, od_w_out=v_od_w_out,
                   ffn_w_gate=v_ffn_w_gate, ffn_w_up=v_ffn_w_up, ffn_w_down=v_ffn_w_down,
                   ple_w_proj=v_ple_w_proj, ple_w_gate=v_ple_w_gate)

    wfull = unpack_full(allgather_chips(pack_shards(local_w)))
    grad_x, gw, small = local_step(x[0], p[:, 0], positions[0], loss_target[0], norm_mix, norm_ffn, norm_ple,
                                   norm_final, ev_b_f, od_sinks, wfull)

    cc = lax.axis_index("c")
    g4 = pack_grads(gw).reshape(N_CHIPS, 2, PACK_ROWS_HALF, D_MODEL)
    theirs = pair_send_other_half(g4)
    mine = lax.dynamic_index_in_dim(g4, cc, axis=1, keepdims=False)
    pair_sum = add_pair(mine.reshape(N_CHIPS * PACK_ROWS_HALF, D_MODEL),
                        theirs.reshape(N_CHIPS * PACK_ROWS_HALF, D_MODEL), "add_pair")
    from_chips = exchange_chips(pair_sum.reshape(N_CHIPS, PACK_ROWS_HALF, D_MODEL))
    half_sum = sum_chips(from_chips, "sum_chips")
    g_local = unpack_local(pair_gather(half_sum).reshape(PACK_ROWS, D_MODEL))

    red = allreduce_small(pack_small(small))
    loss = 0.5 * jnp.sum(red[8]) / D_MODEL
    pad_small = lambda a: jnp.zeros((D_MODEL,), F32).at[:N_FOX].set(a[0][0]).at[128:128 + N_Q].set(a[1][0])
    stack_small = lambda a: jnp.concatenate(
        [a[0], a[1], a[2], a[3][None], pad_small(a[4:6])[None], jnp.zeros((SMALL_ROWS - 8, D_MODEL), F32)], axis=0)
    w_small = stack_small((norm_mix, norm_ffn, norm_ple, norm_final, ev_b_f, od_sinks))
    m_small = stack_small((m_norm_mix, m_norm_ffn, m_norm_ple, m_norm_final, m_ev_b_f, m_od_sinks))
    v_small = stack_small((v_norm_mix, v_norm_ffn, v_norm_ple, v_norm_final, v_ev_b_f, v_od_sinks))
    g_small = red.at[8].set(0.0)
    upd_small = (g_small,) + tuple(adamw(w_small, g_small, m_small, v_small, "adamw_small"))

    def split_small(a):
        return (a[0:2], a[2:4], a[4:6], a[6], a[7, :N_FOX][None], a[7, 128:128 + N_Q][None])

    small_out = [split_small(a) for a in upd_small]

    big_names = ("ev_w_in", "ev_w_out", "od_w_in", "od_w_out", "ffn_w_gate", "ffn_w_up", "ffn_w_down",
                 "ple_w_proj", "ple_w_gate")
    big_out = {}
    for name in big_names:
        w = local_w[name]
        layers, r, c = w.shape
        g = jnp.concatenate([g_local[(name, i)] for i in range(layers)], axis=0)
        res = adamw(w.reshape(layers * r, c), g, local_m[name].reshape(layers * r, c),
                    local_v[name].reshape(layers * r, c), f"adamw_{name}")
        big_out[name] = [a.reshape(layers, r, c) for a in (g,) + tuple(res)]

    outs = [loss, grad_x[None]]
    for kind in range(4):
        sm = small_out[kind]
        outs += [sm[0], sm[1], sm[2], sm[3],
                 big_out["ev_w_in"][kind], sm[4], big_out["ev_w_out"][kind],
                 big_out["od_w_in"][kind], sm[5], big_out["od_w_out"][kind],
                 big_out["ffn_w_gate"][kind], big_out["ffn_w_up"][kind], big_out["ffn_w_down"][kind],
                 big_out["ple_w_proj"][kind], big_out["ple_w_gate"][kind]]
    return tuple(outs)
```

```python
import jax
import jax.numpy as jnp
from jax import lax
from jax.experimental import pallas as pl
from jax.experimental.pallas import tpu as pltpu

F32 = jnp.float32
BF16 = jnp.bfloat16

D_MODEL = 1024
HEAD_DIM = 64
N_FOX = 8
N_SB = 8
FOX_W = N_FOX * HEAD_DIM
SB_W = N_SB * HEAD_DIM
EVEN_QKV = 3 * FOX_W + 3 * SB_W
GATE_PAD = 128
EVEN_IN_PAD = EVEN_QKV + GATE_PAD
N_Q = 16
N_KV = 4
GROUP = N_Q // N_KV
Q_W = N_Q * HEAD_DIM
KV_W = N_KV * HEAD_DIM
ODD_IN = Q_W + 2 * KV_W
WINDOW = 128
ROPE_THETA = 10000.0
D_FF = 2816
PLE_DIM = 256
EPS = 1e-6
NEG_INF = -1e30
SCALE = HEAD_DIM ** -0.5

ADAM_LR = 0.001
ADAM_B1 = 0.9
ADAM_B2 = 0.999
ADAM_EPS = 1e-08
ADAM_WD = 0.01
ADAM_STEP = 10

N_CHIPS = 4
VMEM_LIMIT = 48 * 1024 * 1024
MESH = pl.DeviceIdType.MESH

_PACK = (
    ("ev_w_in", 0, 1024, 770, True),
    ("ev_w_out", 0, 256, 1024, False),
    ("od_w_in", 0, 1024, 384, True),
    ("od_w_out", 0, 256, 1024, False),
    ("ffn_w_gate", 0, 1024, 704, True),
    ("ffn_w_gate", 1, 1024, 704, True),
    ("ffn_w_up", 0, 1024, 704, True),
    ("ffn_w_up", 1, 1024, 704, True),
    ("ffn_w_down", 0, 704, 1024, False),
    ("ffn_w_down", 1, 704, 1024, False),
    ("ple_w_proj", 0, 256, 256, True),
    ("ple_w_proj", 1, 256, 256, True),
    ("ple_w_gate", 0, 256, 1024, False),
    ("ple_w_gate", 1, 256, 1024, False),
)
_ROW_ALIGN = 16


def _pack_rows(r, c):
    n = r * c // D_MODEL
    return -(-n // _ROW_ALIGN) * _ROW_ALIGN


PACK_ROWS_HALF = 3328
PACK_ROWS = 2 * PACK_ROWS_HALF
assert sum(_pack_rows(r, c) for _, _, r, c, _ in _PACK) <= PACK_ROWS


def _pick(n, cands):
    for c in cands:
        if n % c == 0:
            return c
    return n


def _cparams(sem):
    return pltpu.CompilerParams(dimension_semantics=sem, vmem_limit_bytes=VMEM_LIMIT)


_DIMS = {
    "nn": (((1,), (0,)), ((), ())),
    "nt": (((1,), (1,)), ((), ())),
    "tn": (((0,), (0,)), ((), ())),
}


def matmul(a, b, mode="nn", out_dtype=F32, residual=None, name="mm"):
    if mode == "nn":
        (m, k), (k2, n) = a.shape, b.shape
    elif mode == "nt":
        (m, k), (n, k2) = a.shape, b.shape
    else:
        (k, m), (k2, n) = a.shape, b.shape
    assert k == k2, (a.shape, b.shape, mode)
    tm = _pick(m, (512, 256, 128))
    tn = _pick(n, (1024, 512, 640, 384, 256, 128))
    tk = _pick(k, (1024, 1408, 640, 512, 256, 128))
    nk = k // tk
    dims = _DIMS[mode]
    has_res = residual is not None

    def body(*refs):
        if has_res:
            a_ref, b_ref, r_ref, o_ref, acc = refs
        else:
            a_ref, b_ref, o_ref, acc = refs
        kk = pl.program_id(2)
        part = lax.dot_general(a_ref[...].astype(BF16), b_ref[...].astype(BF16), dims,
                               preferred_element_type=F32)

        def finish(r):
            if has_res:
                r = r + r_ref[...]
            o_ref[...] = r.astype(out_dtype)

        if nk == 1:
            finish(part)
            return

        @pl.when(kk == 0)
        def _():
            acc[...] = part

        @pl.when((kk > 0) & (kk < nk - 1))
        def _():
            acc[...] += part

        @pl.when(kk == nk - 1)
        def _():
            finish(acc[...] + part)

    if mode == "nn":
        a_spec = pl.BlockSpec((tm, tk), lambda i, j, kk: (i, kk))
        b_spec = pl.BlockSpec((tk, tn), lambda i, j, kk: (kk, j))
    elif mode == "nt":
        a_spec = pl.BlockSpec((tm, tk), lambda i, j, kk: (i, kk))
        b_spec = pl.BlockSpec((tn, tk), lambda i, j, kk: (j, kk))
    else:
        a_spec = pl.BlockSpec((tk, tm), lambda i, j, kk: (kk, i))
        b_spec = pl.BlockSpec((tk, tn), lambda i, j, kk: (kk, j))
    o_spec = pl.BlockSpec((tm, tn), lambda i, j, kk: (i, j))
    in_specs = [a_spec, b_spec] + ([o_spec] if has_res else [])
    args = (a, b) + ((residual,) if has_res else ())
    return pl.pallas_call(
        body, name=name, grid=(m // tm, n // tn, nk),
        in_specs=in_specs, out_specs=o_spec,
        out_shape=jax.ShapeDtypeStruct((m, n), out_dtype),
        scratch_shapes=[pltpu.VMEM((tm, tn), F32)],
        compiler_params=_cparams(("parallel", "parallel", "arbitrary")),
    )(*args)


def _fold8(v):
    r, w = v.shape
    return v.reshape(r // 8, 8, w).sum(axis=0)


def rowwise(fn, rows, bcasts, outs, accs=(), tr=256, name="rowwise", reverse=False):
    t = rows[0].shape[0]
    tr = _pick(t, (tr, 128, 64, 32, 16, 8))
    nr, nb, no, na = len(rows), len(bcasts), len(outs), len(accs)
    steps = t // tr

    def body(*refs):
        ins = [r[...] for r in refs[:nr + nb]]
        out_refs = refs[nr + nb:nr + nb + no]
        acc_refs = refs[nr + nb + no:]
        o, a = fn(*ins)
        for r, v in zip(out_refs, o):
            r[...] = v.astype(r.dtype)
        if na:
            @pl.when(pl.program_id(0) == 0)
            def _():
                for r in acc_refs:
                    r[...] = jnp.zeros_like(r)

            for r, v in zip(acc_refs, a):
                r[...] += v

    if reverse:
        ridx = lambda i: (steps - 1 - i, 0)
    else:
        ridx = lambda i: (i, 0)
    in_specs = [pl.BlockSpec((tr, x.shape[1]), ridx) for x in rows]
    in_specs += [pl.BlockSpec(x.shape, lambda i: (0, 0)) for x in bcasts]
    out_specs = [pl.BlockSpec((tr, w), ridx) for w, _ in outs]
    out_specs += [pl.BlockSpec((8, w), lambda i: (0, 0)) for w in accs]
    out_shape = [jax.ShapeDtypeStruct((t, w), dt) for w, dt in outs]
    out_shape += [jax.ShapeDtypeStruct((8, w), F32) for w in accs]
    res = pl.pallas_call(
        body, name=name, grid=(steps,), in_specs=in_specs, out_specs=out_specs, out_shape=out_shape,
        compiler_params=_cparams(("arbitrary",)),
    )(*rows, *bcasts)
    return res


def _rstd(x):
    return lax.rsqrt(jnp.mean(x * x, axis=-1, keepdims=True) + EPS)


def rmsnorm_fwd(h, g, name):
    def fn(x, gg):
        return ((x * _rstd(x)) * gg,), ()

    return rowwise(fn, [h], [g.reshape(1, -1)], [(D_MODEL, BF16)], name=name)[0]


def _rms_bwd_math(x, gg, dy):
    r = _rstd(x)
    xh = x * r
    u = dy * gg
    dx = r * (u - xh * jnp.mean(u * xh, axis=-1, keepdims=True))
    return dx, dy * xh


def rmsnorm_bwd(h, g, dn, dres, name):
    def fn(x, dy, dr, gg):
        dx, dgp = _rms_bwd_math(x, gg, dy)
        return (dr + dx,), (_fold8(dgp),)

    return rowwise(fn, [h, dn, dres], [g.reshape(1, -1)], [(D_MODEL, F32)], [D_MODEL], name=name)


def loss_head(h, g, tgt, name):
    def fn(x, tg, gg):
        y = (x * _rstd(x)) * gg
        e = y - tg
        dy = e * (1.0 / D_MODEL)
        dx, dgp = _rms_bwd_math(x, gg, dy)
        return (dx,), (_fold8(dgp), _fold8(e * e))

    return rowwise(fn, [h, tgt], [g.reshape(1, -1)], [(D_MODEL, F32)], [D_MODEL, D_MODEL], name=name)


def _sigmoid(x):
    return 1.0 / (1.0 + jnp.exp(-x))


def swiglu_fwd(gu, name):
    def fn(x):
        gg, uu = x[:, :D_FF], x[:, D_FF:]
        return ((gg * _sigmoid(gg)) * uu,), ()

    return rowwise(fn, [gu], [], [(D_FF, BF16)], name=name)[0]


def swiglu_bwd(gu, da, name):
    def fn(x, d):
        gg, uu = x[:, :D_FF], x[:, D_FF:]
        s = _sigmoid(gg)
        silu = gg * s
        dgg = d * uu * (s + silu * (1.0 - s))
        duu = d * silu
        return (jnp.concatenate([dgg, duu], axis=1),), ()

    return rowwise(fn, [gu, da], [], [(2 * D_FF, BF16)], name=name)[0]


def ple_fwd(h, pre, pp, name):
    def fn(x, a, b):
        return (x + _sigmoid(a) * b,), ()

    return rowwise(fn, [h, pre, pp], [], [(D_MODEL, F32)], name=name)[0]


def ple_bwd(dh, pre, pp, name):
    def fn(d, a, b):
        s = _sigmoid(a)
        return (d * b * s * (1.0 - s), d * s), ()

    return rowwise(fn, [dh, pre, pp], [], [(D_MODEL, BF16), (D_MODEL, BF16)], name=name)


def _rot_half_partner(x, first_half):
    w = x.shape[1]
    return jnp.where(first_half, pltpu.roll(x, w - HEAD_DIM // 2, 1), pltpu.roll(x, HEAD_DIM // 2, 1))


def rope_apply(xx, cosw, sinw, backward, name):
    width = xx.shape[1]
    reps = width // 128

    def fn(x, c, s):
        cw = jnp.tile(c, (1, reps))
        sw = jnp.tile(s, (1, reps))
        lane = lax.broadcasted_iota(jnp.int32, x.shape, 1)
        first = (lane % HEAD_DIM) < (HEAD_DIM // 2)
        if backward:
            return (x * cw + _rot_half_partner(x * sw, first),), ()
        return (x * cw + _rot_half_partner(x, first) * sw,), ()

    return rowwise(fn, [xx, cosw, sinw], [], [(width, F32)], name=name)[0]


def _log_sigmoid(x):
    return jnp.minimum(x, 0.0) - jnp.log(1.0 + jnp.exp(-jnp.abs(x)))


CUM_BLOCK = 256


def forget_cumsum(flog, bias, name):
    t = flog.shape[0]
    tb = _pick(t, (CUM_BLOCK,))

    def body(x_ref, b_ref, o_ref, carry):
        @pl.when(pl.program_id(0) == 0)
        def _():
            carry[...] = jnp.zeros_like(carry)

        lf = _log_sigmoid(x_ref[...] + b_ref[...])
        r = lax.broadcasted_iota(jnp.int32, (tb, tb), 0)
        c = lax.broadcasted_iota(jnp.int32, (tb, tb), 1)
        tri = (c <= r).astype(F32)
        cum = jnp.dot(tri, lf, preferred_element_type=F32, precision=lax.Precision.HIGHEST) + carry[...]
        o_ref[...] = cum
        carry[...] = cum[tb - 1:tb, :]

    return pl.pallas_call(
        body, name=name, grid=(t // tb,),
        in_specs=[pl.BlockSpec((tb, GATE_PAD), lambda i: (i, 0)), pl.BlockSpec((1, GATE_PAD), lambda i: (0, 0))],
        out_specs=pl.BlockSpec((tb, GATE_PAD), lambda i: (i, 0)),
        out_shape=jax.ShapeDtypeStruct((t, GATE_PAD), F32),
        scratch_shapes=[pltpu.VMEM((1, GATE_PAD), F32)],
        compiler_params=_cparams(("arbitrary",)),
    )(flog, bias)


def forget_cumsum_bwd(d_cum, flog, bias, name):
    t = flog.shape[0]
    tb = _pick(t, (CUM_BLOCK,))
    nb = t // tb

    def body(d_ref, x_ref, b_ref, o_ref, db_ref, carry):
        @pl.when(pl.program_id(0) == 0)
        def _():
            carry[...] = jnp.zeros_like(carry)
            db_ref[...] = jnp.zeros_like(db_ref)

        r = lax.broadcasted_iota(jnp.int32, (tb, tb), 0)
        c = lax.broadcasted_iota(jnp.int32, (tb, tb), 1)
        tri = (c >= r).astype(F32)
        dlf = jnp.dot(tri, d_ref[...], preferred_element_type=F32, precision=lax.Precision.HIGHEST) + carry[...]
        carry[...] = dlf[0:1, :]
        dx = dlf * (1.0 - _sigmoid(x_ref[...] + b_ref[...]))
        o_ref[...] = dx
        db_ref[...] += _fold8(dx)

    rev = lambda i: (nb - 1 - i, 0)
    return pl.pallas_call(
        body, name=name, grid=(nb,),
        in_specs=[pl.BlockSpec((tb, GATE_PAD), rev), pl.BlockSpec((tb, GATE_PAD), rev),
                  pl.BlockSpec((1, GATE_PAD), lambda i: (0, 0))],
        out_specs=[pl.BlockSpec((tb, GATE_PAD), rev), pl.BlockSpec((8, GATE_PAD), lambda i: (0, 0))],
        out_shape=[jax.ShapeDtypeStruct((t, GATE_PAD), F32), jax.ShapeDtypeStruct((8, GATE_PAD), F32)],
        scratch_shapes=[pltpu.VMEM((1, GATE_PAD), F32)],
        compiler_params=_cparams(("arbitrary",)),
    )(d_cum, flog, bias)


TQ = 512
TK = 128
AUG = 128
N_BIAS = 3


def _dot(a, b):
    return jnp.dot(a, b, preferred_element_type=F32)


def _rel(shape, d):
    return lax.broadcasted_iota(jnp.int32, shape, 0) - lax.broadcasted_iota(jnp.int32, shape, 1) + d


def _split_bf16(x):
    hi = x.astype(BF16)
    return hi, (x - hi.astype(F32)).astype(BF16)


def _tri_dot(tri, x):
    hi, lo = _split_bf16(x)
    return _dot(tri, hi) + _dot(tri, lo)


def _q_cols(width, tq):
    return pl.BlockSpec((None, width, tq), lambda h, i: (h, 0, i))


def _q_rows(width, tq):
    return pl.BlockSpec((None, tq, width), lambda h, i: (h, i, 0))


def _kv_rows(t, width):
    return pl.BlockSpec((None, t, width), lambda h, i: (h, 0, 0))


def _kv_tiles(nk):
    return pl.BlockSpec((None, nk, HEAD_DIM, TK), lambda h, i: (h, 0, 0, 0))


def _blocks(t):
    tq = TQ if t % TQ == 0 else TK
    return tq, tq // TK, t // TK


def fox_fwd(qt_aug, k_aug, vt):
    nh, _, t = qt_aug.shape
    tq, ratio, nk = _blocks(t)

    def body(q_ref, k_ref, v_ref, o_ref, lse_ref):
        i = pl.program_id(1)
        qv = q_ref[...]

        def step(first, carry, masked):
            m, l, acc = carry
            scores = []
            for u in range(ratio):
                off = pl.multiple_of((first + u) * TK, TK)
                s = _dot(k_ref[pl.ds(off, TK), :], qv)
                if masked:
                    s = jnp.where(_rel(s.shape, (first + u) * TK - i * tq) <= 0, s, NEG_INF)
                scores.append(s)
            m_new = m
            for s in scores:
                m_new = jnp.maximum(m_new, jnp.max(s, axis=0, keepdims=True))
            alpha = jnp.exp(m - m_new)
            l = alpha * l
            acc = alpha * acc
            for u, s in enumerate(scores):
                p = jnp.exp(s - m_new)
                l = l + jnp.sum(p, axis=0, keepdims=True)
                acc = acc + _dot(v_ref[first + u], p.astype(BF16))
            return m_new, l, acc

        carry = (jnp.full((1, tq), NEG_INF, F32), jnp.zeros((1, tq), F32), jnp.zeros((HEAD_DIM, tq), F32))
        carry = lax.fori_loop(0, i, lambda jj, c: step(ratio * jj, c, False), carry)
        m, l, acc = step(ratio * i, carry, True)
        o_ref[...] = acc / l
        lse_ref[...] = m + jnp.log(l)

    return pl.pallas_call(
        body, name="fox_fwd", grid=(nh, t // tq),
        in_specs=[_q_cols(AUG, tq), _kv_rows(t, AUG), _kv_tiles(nk)],
        out_specs=[_q_cols(HEAD_DIM, tq), _q_cols(1, tq)],
        out_shape=[jax.ShapeDtypeStruct((nh, HEAD_DIM, t), F32), jax.ShapeDtypeStruct((nh, 1, t), F32)],
        compiler_params=_cparams(("parallel", "arbitrary")),
    )(qt_aug, k_aug, vt)


def fox_bwd(qt_aug, q_aug, k_aug, kt, v, ot, do, dot_, lse):
    nh, _, t = qt_aug.shape
    tq, ratio, nk = _blocks(t)

    def body(qt_ref, q_ref, k_ref, kt_ref, v_ref, ot_ref, do_ref, dot_ref, lse_ref,
             dqt_ref, dk_ref, dv_ref, rs_ref):
        i = pl.program_id(1)

        @pl.when(i == 0)
        def _():
            dk_ref[...] = jnp.zeros_like(dk_ref)
            dv_ref[...] = jnp.zeros_like(dv_ref)

        qtv = qt_ref[...]
        qv = q_ref[...]
        dob = do_ref[...]
        dotb = dot_ref[...]
        delta = jnp.sum(ot_ref[...] * dotb.astype(F32), axis=0, keepdims=True)
        lse = lse_ref[...]

        def tile(j, carry, masked):
            dqt, rs = carry
            off = pl.multiple_of(j * TK, TK)
            s = _dot(k_ref[pl.ds(off, TK), :], qtv)
            p = jnp.exp(s - lse)
            if masked:
                p = jnp.where(_rel(s.shape, j * TK - i * tq) <= 0, p, 0.0)
            dp = _dot(v_ref[pl.ds(off, TK), :], dotb)
            dsb = (p * (dp - delta)).astype(BF16)
            dk_ref[pl.ds(off, TK), :] += _dot(dsb, qv)
            dv_ref[pl.ds(off, TK), :] += _dot(p.astype(BF16), dob)
            return dqt + _dot(kt_ref[j], dsb), rs + jnp.sum(dsb.astype(F32), axis=0, keepdims=True)

        def step(first, carry, masked):
            for u in range(ratio):
                carry = tile(first + u, carry, masked)
            return carry

        carry = (jnp.zeros((HEAD_DIM, tq), F32), jnp.zeros((1, tq), F32))
        carry = lax.fori_loop(0, i, lambda jj, c: step(ratio * jj, c, False), carry)
        dqt, rs = step(ratio * i, carry, True)
        dqt_ref[...] = dqt * SCALE
        rs_ref[...] = rs

    return pl.pallas_call(
        body, name="fox_bwd", grid=(nh, t // tq),
        in_specs=[_q_cols(AUG, tq), _q_rows(AUG, tq), _kv_rows(t, AUG), _kv_tiles(nk), _kv_rows(t, HEAD_DIM),
                  _q_cols(HEAD_DIM, tq), _q_rows(HEAD_DIM, tq), _q_cols(HEAD_DIM, tq), _q_cols(1, tq)],
        out_specs=[_q_cols(HEAD_DIM, tq), _kv_rows(t, AUG), _kv_rows(t, HEAD_DIM), _q_cols(1, tq)],
        out_shape=[jax.ShapeDtypeStruct((nh, HEAD_DIM, t), F32), jax.ShapeDtypeStruct((nh, t, AUG), F32),
                   jax.ShapeDtypeStruct((nh, t, HEAD_DIM), F32), jax.ShapeDtypeStruct((nh, 1, t), F32)],
        compiler_params=_cparams(("arbitrary", "arbitrary")),
    )(qt_aug, q_aug, k_aug, kt, v, ot, do, dot_, lse)


def _sb_logits(kb, qv, ok):
    z = _dot(kb, qv)
    e = jnp.exp(-jnp.abs(z))
    ll = -(jnp.maximum(z, 0.0) + jnp.log(1.0 + e))
    if ok is not None:
        ll = jnp.where(ok, ll, 0.0)
    return z, e, ll


def _tri(cmp):
    r = lax.broadcasted_iota(jnp.int32, (TK, TK), 0)
    c = lax.broadcasted_iota(jnp.int32, (TK, TK), 1)
    return cmp(r, c).astype(BF16)


def sb_fwd(qt, k, vt):
    nh, _, t = qt.shape
    tq, ratio, nk = _blocks(t)

    def body(q_ref, k_ref, v_ref, o_ref, tot_ref):
        i = pl.program_id(1)
        qv = q_ref[...]
        tri_after = _tri(lambda r, c: c > r)

        def tile(j, carry, masked):
            c_l, acc = carry
            off = pl.multiple_of(j * TK, TK)
            ok = _rel((TK, tq), j * TK - i * tq) < 0 if masked else None
            z, _, ll = _sb_logits(k_ref[pl.ds(off, TK), :], qv, ok)
            a = jnp.exp(z + ll + _tri_dot(tri_after, ll) + c_l)
            if masked:
                a = jnp.where(ok, a, 0.0)
            return c_l + jnp.sum(ll, axis=0, keepdims=True), acc + _dot(v_ref[j], a.astype(BF16))

        def step(first, carry, masked):
            for u in reversed(range(ratio)):
                carry = tile(first + u, carry, masked)
            return carry

        carry = step(ratio * i, (jnp.zeros((1, tq), F32), jnp.zeros((HEAD_DIM, tq), F32)), True)
        c_l, acc = lax.fori_loop(0, i, lambda jj, c: step(ratio * (i - 1 - jj), c, False), carry)
        o_ref[...] = acc
        tot_ref[...] = c_l

    return pl.pallas_call(
        body, name="sb_fwd", grid=(nh, t // tq),
        in_specs=[_q_cols(HEAD_DIM, tq), _kv_rows(t, HEAD_DIM), _kv_tiles(nk)],
        out_specs=[_q_cols(HEAD_DIM, tq), _q_cols(1, tq)],
        out_shape=[jax.ShapeDtypeStruct((nh, HEAD_DIM, t), F32), jax.ShapeDtypeStruct((nh, 1, t), F32)],
        compiler_params=_cparams(("parallel", "arbitrary")),
    )(qt, k, vt)


def sb_bwd(qt, q, k, kt, v, ltot, do, dot_):
    nh, _, t = qt.shape
    tq, ratio, nk = _blocks(t)

    def body(qt_ref, q_ref, k_ref, kt_ref, v_ref, tot_ref, do_ref, dot_ref, dqt_ref, dk_ref, dv_ref):
        i = pl.program_id(1)

        @pl.when(i == 0)
        def _():
            dk_ref[...] = jnp.zeros_like(dk_ref)
            dv_ref[...] = jnp.zeros_like(dv_ref)

        qtv = qt_ref[...]
        qv = q_ref[...]
        dob = do_ref[...]
        dotb = dot_ref[...]
        tri_upto = _tri(lambda r, c: c <= r)
        tri_before = _tri(lambda r, c: c < r)

        def tile(j, carry, masked):
            rest, c_w, dqt = carry
            off = pl.multiple_of(j * TK, TK)
            ok = _rel((TK, tq), j * TK - i * tq) < 0 if masked else None
            z, e, ll = _sb_logits(k_ref[pl.ds(off, TK), :], qtv, ok)
            a = jnp.exp(z + ll + (rest - _tri_dot(tri_upto, ll)))
            if masked:
                a = jnp.where(ok, a, 0.0)
            w = a * _dot(v_ref[pl.ds(off, TK), :], dotb)
            before = _tri_dot(tri_before, w) + c_w
            r = 1.0 / (1.0 + e)
            sig = jnp.where(z >= 0.0, r, e * r)
            dz = w * (1.0 - sig) - before * sig
            if masked:
                dz = jnp.where(ok, dz, 0.0)
            dzb = dz.astype(BF16)
            dk_ref[pl.ds(off, TK), :] += _dot(dzb, qv)
            dv_ref[pl.ds(off, TK), :] += _dot(a.astype(BF16), dob)
            return (rest - jnp.sum(ll, axis=0, keepdims=True), c_w + jnp.sum(w, axis=0, keepdims=True),
                    dqt + _dot(kt_ref[j], dzb))

        def step(first, carry, masked):
            for u in range(ratio):
                carry = tile(first + u, carry, masked)
            return carry

        carry = (tot_ref[...], jnp.zeros((1, tq), F32), jnp.zeros((HEAD_DIM, tq), F32))
        carry = lax.fori_loop(0, i, lambda jj, c: step(ratio * jj, c, False), carry)
        dqt_ref[...] = step(ratio * i, carry, True)[2] * SCALE

    hd = HEAD_DIM
    return pl.pallas_call(
        body, name="sb_bwd", grid=(nh, t // tq),
        in_specs=[_q_cols(hd, tq), _q_rows(hd, tq), _kv_rows(t, hd), _kv_tiles(nk), _kv_rows(t, hd),
                  _q_cols(1, tq), _q_rows(hd, tq), _q_cols(hd, tq)],
        out_specs=[_q_cols(hd, tq), _kv_rows(t, hd), _kv_rows(t, hd)],
        out_shape=[jax.ShapeDtypeStruct((nh, hd, t), F32), jax.ShapeDtypeStruct((nh, t, hd), F32),
                   jax.ShapeDtypeStruct((nh, t, hd), F32)],
        compiler_params=_cparams(("arbitrary", "arbitrary")),
    )(qt, q, k, kt, v, ltot, do, dot_)


def _swa_q_cols(width):
    return pl.BlockSpec((GROUP, width, WINDOW), lambda g, i: (g, 0, i))


def _swa_q_rows():
    return pl.BlockSpec((GROUP, WINDOW, HEAD_DIM), lambda g, i: (g, i, 0))


def _swa_kv_rows(t):
    return pl.BlockSpec((None, t, HEAD_DIM), lambda g, i: (g, 0, 0))


def _swa_kv_tiles(nk):
    return pl.BlockSpec((None, nk, HEAD_DIM, WINDOW), lambda g, i: (g, 0, 0, 0))


def _swa_sink_spec():
    return pl.BlockSpec((GROUP, 1, 128), lambda g, i: (g, 0, 0))


def _swa_window(i):
    jb = jnp.maximum(i - 1, 0)
    start = pl.multiple_of(jb * WINDOW, WINDOW)
    rel = _rel((2 * WINDOW, WINDOW), start - i * WINDOW)
    return jb, start, (rel <= 0) & (rel > -WINDOW)


def swa_fwd(qt, k, vt, sinks):
    nh, _, t = qt.shape
    nk = t // WINDOW

    def body(q_ref, k_ref, v_ref, s_ref, o_ref, lse_ref):
        i = pl.program_id(1)
        jb, start, valid = _swa_window(i)
        kwin = k_ref[pl.ds(start, 2 * WINDOW), :]
        for g in range(GROUP):
            s = jnp.where(valid, _dot(kwin, q_ref[g]), NEG_INF)
            sink = s_ref[g]
            m = jnp.maximum(jnp.max(s, axis=0, keepdims=True), sink)
            p = jnp.where(valid, jnp.exp(s - m), 0.0)
            l = jnp.sum(p, axis=0, keepdims=True) + jnp.exp(sink - m)
            pb = p.astype(BF16)
            o_ref[g] = (_dot(v_ref[jb], pb[:WINDOW]) + _dot(v_ref[jb + 1], pb[WINDOW:])) / l
            lse_ref[g] = m + jnp.log(l)

    return pl.pallas_call(
        body, name="swa_fwd", grid=(N_KV, nk),
        in_specs=[_swa_q_cols(HEAD_DIM), _swa_kv_rows(t), _swa_kv_tiles(nk), _swa_sink_spec()],
        out_specs=[_swa_q_cols(HEAD_DIM), _swa_q_cols(1)],
        out_shape=[jax.ShapeDtypeStruct((nh, HEAD_DIM, t), F32), jax.ShapeDtypeStruct((nh, 1, t), F32)],
        compiler_params=_cparams(("parallel", "arbitrary")),
    )(qt, k, vt, sinks)


def swa_bwd(qt, q, k, kt, v, sinks, ot, do, dot_, lse):
    nh, _, t = qt.shape
    nk = t // WINDOW

    def body(qt_ref, q_ref, k_ref, kt_ref, v_ref, s_ref, ot_ref, do_ref, dot_ref, lse_ref,
             dqt_ref, dk_ref, dv_ref, dsink_ref):
        i = pl.program_id(1)

        @pl.when(i == 0)
        def _():
            dk_ref[...] = jnp.zeros_like(dk_ref)
            dv_ref[...] = jnp.zeros_like(dv_ref)
            dsink_ref[...] = jnp.zeros_like(dsink_ref)

        jb, start, valid = _swa_window(i)
        kwin = k_ref[pl.ds(start, 2 * WINDOW), :]
        vwin = v_ref[pl.ds(start, 2 * WINDOW), :]
        dk_acc = jnp.zeros((2 * WINDOW, HEAD_DIM), F32)
        dv_acc = jnp.zeros((2 * WINDOW, HEAD_DIM), F32)
        for g in range(GROUP):
            lse_g = lse_ref[g]
            dotb = dot_ref[g]
            delta = jnp.sum(ot_ref[g] * dotb.astype(F32), axis=0, keepdims=True)
            p = jnp.where(valid, jnp.exp(_dot(kwin, qt_ref[g]) - lse_g), 0.0)
            dsb = (p * (_dot(vwin, dotb) - delta)).astype(BF16)
            dqt_ref[g] = (_dot(kt_ref[jb], dsb[:WINDOW]) + _dot(kt_ref[jb + 1], dsb[WINDOW:])) * SCALE
            dk_acc = dk_acc + _dot(dsb, q_ref[g])
            dv_acc = dv_acc + _dot(p.astype(BF16), do_ref[g])
            dsink_ref[g] += -jnp.exp(s_ref[g] - lse_g) * delta
        dk_ref[pl.ds(start, 2 * WINDOW), :] += dk_acc
        dv_ref[pl.ds(start, 2 * WINDOW), :] += dv_acc

    hd = HEAD_DIM
    return pl.pallas_call(
        body, name="swa_bwd", grid=(N_KV, nk),
        in_specs=[_swa_q_cols(hd), _swa_q_rows(), _swa_kv_rows(t), _swa_kv_tiles(nk), _swa_kv_rows(t),
                  _swa_sink_spec(), _swa_q_cols(hd), _swa_q_rows(), _swa_q_cols(hd), _swa_q_cols(1)],
        out_specs=[_swa_q_cols(hd), _swa_kv_rows(t), _swa_kv_rows(t), _swa_sink_spec()],
        out_shape=[jax.ShapeDtypeStruct((nh, hd, t), F32), jax.ShapeDtypeStruct((N_KV, t, hd), F32),
                   jax.ShapeDtypeStruct((N_KV, t, hd), F32), jax.ShapeDtypeStruct((nh, 1, 128), F32)],
        compiler_params=_cparams(("arbitrary", "arbitrary")),
    )(qt, q, k, kt, v, sinks, ot, do, dot_, lse)


def head_rows(a, nh):
    t = a.shape[0]
    return a.reshape(t, nh, HEAD_DIM).transpose(1, 0, 2)


def head_cols(a, nh):
    t = a.shape[0]
    return a.reshape(t, nh, HEAD_DIM).transpose(1, 2, 0)


def head_tiles(a, nh):
    t = a.shape[0]
    return a.reshape(t // TK, TK, nh, HEAD_DIM).transpose(2, 0, 3, 1)


def rows_to_flat(a):
    nh, t, _ = a.shape
    return a.transpose(1, 0, 2).reshape(t, nh * HEAD_DIM)


def cols_to_flat(a):
    nh, _, t = a.shape
    return a.transpose(2, 0, 1).reshape(t, nh * HEAD_DIM)


def fox_operands(qf, kf, cum_heads):
    t = qf.shape[0]
    nh = cum_heads.shape[0]
    qs = (qf * SCALE).astype(BF16)
    ones_t = jnp.ones((nh, N_BIAS, t), BF16)
    qt_aug = jnp.concatenate([head_cols(qs, nh), ones_t, jnp.zeros((nh, AUG - HEAD_DIM - N_BIAS, t), BF16)], axis=1)
    q_aug = jnp.concatenate([head_rows(qs, nh), jnp.ones((nh, t, 1), BF16),
                             jnp.zeros((nh, t, AUG - HEAD_DIM - 1), BF16)], axis=2)
    terms, rest = [], -cum_heads
    for _ in range(N_BIAS):
        term = lax.reduce_precision(rest, exponent_bits=8, mantissa_bits=7)
        terms.append(term.astype(BF16))
        rest = rest - term
    k_aug = jnp.concatenate([head_rows(kf.astype(BF16), nh)] + [x[:, :, None] for x in terms]
                            + [jnp.zeros((nh, t, AUG - HEAD_DIM - N_BIAS), BF16)], axis=2)
    return qt_aug, q_aug, k_aug


def adamw(w, g, m, v, name):
    cols = w.shape[1]

    def fn(ww, gg, mm, vv):
        mn = ADAM_B1 * mm + (1.0 - ADAM_B1) * gg
        vn = ADAM_B2 * vv + (1.0 - ADAM_B2) * (gg * gg)
        m_hat = mn / (1.0 - ADAM_B1 ** ADAM_STEP)
        v_hat = vn / (1.0 - ADAM_B2 ** ADAM_STEP)
        delta = -ADAM_LR * (m_hat / (jnp.sqrt(v_hat) + ADAM_EPS) + ADAM_WD * ww)
        return (delta, mn, vn), ()

    return rowwise(fn, [w, g, m, v], [], [(cols, F32)] * 3, name=name)


ANY = pl.BlockSpec(memory_space=pl.ANY)


def _my_place():
    return lax.axis_index("x"), lax.axis_index("y"), lax.axis_index("c")


def _flip(coord, bit):
    return 1 - coord if bit else coord


def allgather_chips(w):
    r, c = w.shape

    def body(w_ref, out_ref, send_sems, recv_sems, local_sem):
        x, y, cc = _my_place()
        me = 2 * x + y
        local = pltpu.make_async_copy(w_ref, out_ref.at[me], local_sem)
        local.start()
        copies = []
        for kbits in (1, 2, 3):
            peer = (_flip(x, kbits >> 1), _flip(y, kbits & 1), cc)
            cp = pltpu.make_async_remote_copy(
                src_ref=w_ref, dst_ref=out_ref.at[me], send_sem=send_sems.at[kbits - 1],
                recv_sem=recv_sems.at[kbits - 1], device_id=peer, device_id_type=MESH)
            cp.start()
            copies.append(cp)
        for cp in copies:
            cp.wait()
        local.wait()

    return pl.pallas_call(
        body, name="allgather_chips", in_specs=[ANY], out_specs=ANY,
        out_shape=jax.ShapeDtypeStruct((N_CHIPS, r, c), w.dtype),
        scratch_shapes=[pltpu.SemaphoreType.DMA((3,)), pltpu.SemaphoreType.DMA((3,)), pltpu.SemaphoreType.DMA],
    )(w)


def pair_send_other_half(g):
    n, _, rh, c = g.shape

    def body(g_ref, out_ref, send_sem, recv_sem):
        x, y, cc = _my_place()
        cp = pltpu.make_async_remote_copy(
            src_ref=g_ref.at[:, 1 - cc], dst_ref=out_ref, send_sem=send_sem, recv_sem=recv_sem,
            device_id=(x, y, 1 - cc), device_id_type=MESH)
        cp.start()
        cp.wait()

    return pl.pallas_call(
        body, name="pair_send_other_half", in_specs=[ANY], out_specs=ANY,
        out_shape=jax.ShapeDtypeStruct((n, rh, c), g.dtype),
        scratch_shapes=[pltpu.SemaphoreType.DMA, pltpu.SemaphoreType.DMA],
    )(g)


def exchange_chips(s):
    n, rh, c = s.shape

    def body(s_ref, out_ref, send_sems, recv_sems, local_sem):
        x, y, cc = _my_place()
        me = 2 * x + y
        local = pltpu.make_async_copy(s_ref.at[me], out_ref.at[me], local_sem)
        local.start()
        copies = []
        for kbits in (1, 2, 3):
            px, py = _flip(x, kbits >> 1), _flip(y, kbits & 1)
            cp = pltpu.make_async_remote_copy(
                src_ref=s_ref.at[2 * px + py], dst_ref=out_ref.at[me], send_sem=send_sems.at[kbits - 1],
                recv_sem=recv_sems.at[kbits - 1], device_id=(px, py, cc), device_id_type=MESH)
            cp.start()
            copies.append(cp)
        for cp in copies:
            cp.wait()
        local.wait()

    return pl.pallas_call(
        body, name="exchange_chips", in_specs=[ANY], out_specs=ANY,
        out_shape=jax.ShapeDtypeStruct((n, rh, c), s.dtype),
        scratch_shapes=[pltpu.SemaphoreType.DMA((3,)), pltpu.SemaphoreType.DMA((3,)), pltpu.SemaphoreType.DMA],
    )(s)


def pair_gather(tt):
    rh, c = tt.shape

    def body(t_ref, out_ref, send_sem, recv_sem, local_sem):
        x, y, cc = _my_place()
        local = pltpu.make_async_copy(t_ref, out_ref.at[cc], local_sem)
        local.start()
        cp = pltpu.make_async_remote_copy(
            src_ref=t_ref, dst_ref=out_ref.at[cc], send_sem=send_sem, recv_sem=recv_sem,
            device_id=(x, y, 1 - cc), device_id_type=MESH)
        cp.start()
        cp.wait()
        local.wait()

    return pl.pallas_call(
        body, name="pair_gather", in_specs=[ANY], out_specs=ANY,
        out_shape=jax.ShapeDtypeStruct((2, rh, c), tt.dtype),
        scratch_shapes=[pltpu.SemaphoreType.DMA, pltpu.SemaphoreType.DMA, pltpu.SemaphoreType.DMA],
    )(tt)


SMALL_ROWS = 16


def allreduce_small(v):
    r, c = v.shape
    vm = pl.BlockSpec(memory_space=pltpu.VMEM)

    def body(v_ref, out_ref, slots, send_sems, recv_sems):
        x, y, cc = _my_place()
        me = 4 * x + 2 * y + cc
        slots[me] = v_ref[...]
        copies = []
        for kbits in range(1, 8):
            peer = (_flip(x, kbits >> 2), _flip(y, (kbits >> 1) & 1), _flip(cc, kbits & 1))
            cp = pltpu.make_async_remote_copy(
                src_ref=v_ref, dst_ref=slots.at[me], send_sem=send_sems.at[kbits - 1],
                recv_sem=recv_sems.at[kbits - 1], device_id=peer, device_id_type=MESH)
            cp.start()
            copies.append(cp)
        for cp in copies:
            cp.wait()
        total = slots[0]
        for dev in range(1, 8):
            total = total + slots[dev]
        out_ref[...] = total

    return pl.pallas_call(
        body, name="allreduce_small", in_specs=[vm], out_specs=vm,
        out_shape=jax.ShapeDtypeStruct((r, c), F32),
        scratch_shapes=[pltpu.VMEM((8, r, c), F32), pltpu.SemaphoreType.DMA((7,)), pltpu.SemaphoreType.DMA((7,))],
    )(v)


def add_pair(mine, theirs, name):
    return rowwise(lambda a, b: ((a + b,), ()), [mine, theirs], [], [(mine.shape[1], BF16)], name=name)[0]


def sum_chips(r4, name):
    _, rh, c = r4.shape
    tr = _pick(rh, (256, 128, 64, 32, 16))

    def body(r_ref, o_ref):
        total = r_ref[0].astype(F32)
        for j in range(1, N_CHIPS):
            total = total + r_ref[j].astype(F32)
        o_ref[...] = total

    return pl.pallas_call(
        body, name=name, grid=(rh // tr,),
        in_specs=[pl.BlockSpec((N_CHIPS, tr, c), lambda i: (0, i, 0))],
        out_specs=pl.BlockSpec((tr, c), lambda i: (i, 0)),
        out_shape=jax.ShapeDtypeStruct((rh, c), F32),
        compiler_params=_cparams(("parallel",)),
    )(r4)


def _pad_rows(a, rows):
    return jnp.pad(a, ((0, rows - a.shape[0]), (0, 0))) if rows != a.shape[0] else a


def pack_shards(local):
    parts = []
    for name, layer, r, c, _ in _PACK:
        flat = local[name][layer].astype(BF16).reshape(r * c // D_MODEL, D_MODEL)
        parts.append(_pad_rows(flat, _pack_rows(r, c)))
    used = sum(p.shape[0] for p in parts)
    parts.append(jnp.zeros((PACK_ROWS - used, D_MODEL), BF16))
    return jnp.concatenate(parts, axis=0)


def unpack_full(gathered):
    full, off = {}, 0
    for name, layer, r, c, by_cols in _PACK:
        n = r * c // D_MODEL
        blk = gathered[:, off:off + n, :].reshape(N_CHIPS, r, c)
        if by_cols:
            full[(name, layer)] = blk.transpose(1, 0, 2).reshape(r, N_CHIPS * c)
        else:
            full[(name, layer)] = blk.reshape(N_CHIPS * r, c)
        off += _pack_rows(r, c)
    return full


def pack_grads(grads):
    parts = []
    for name, layer, r, c, by_cols in _PACK:
        g = grads[(name, layer)]
        if by_cols:
            blk = g.reshape(r, N_CHIPS, c).transpose(1, 0, 2)
        else:
            blk = g.reshape(N_CHIPS, r, c)
        flat = blk.reshape(N_CHIPS, r * c // D_MODEL, D_MODEL)
        rows = _pack_rows(r, c)
        if rows != flat.shape[1]:
            flat = jnp.pad(flat, ((0, 0), (0, rows - flat.shape[1]), (0, 0)))
        parts.append(flat)
    used = sum(p.shape[1] for p in parts)
    parts.append(jnp.zeros((N_CHIPS, PACK_ROWS - used, D_MODEL), F32))
    return jnp.concatenate(parts, axis=1)


def unpack_local(flat):
    out, off = {}, 0
    for name, layer, r, c, _ in _PACK:
        n = r * c // D_MODEL
        out[(name, layer)] = flat[off:off + n, :].reshape(r, c)
        off += _pack_rows(r, c)
    return out


def rope_tables(pos):
    half = HEAD_DIM // 2
    lane = jnp.arange(128)
    inv = ROPE_THETA ** (-(lane % half).astype(F32) / half)
    ang = pos.astype(F32)[:, None] * inv[None, :]
    sign = jnp.where((lane % HEAD_DIM) < half, -1.0, 1.0).astype(F32)
    return jnp.cos(ang), jnp.sin(ang) * sign[None, :]


def local_step(x, p, pos, tgt, norm_mix, norm_ffn, norm_ple, norm_final, ev_b_f, od_sinks, wfull):
    t = x.shape[0]
    w_in0 = wfull[("ev_w_in", 0)]
    w_in0 = jnp.concatenate([w_in0, jnp.zeros((D_MODEL, EVEN_IN_PAD - w_in0.shape[1]), w_in0.dtype)], axis=1)
    w_gu = [jnp.concatenate([wfull[("ffn_w_gate", i)], wfull[("ffn_w_up", i)]], axis=1) for i in range(2)]
    b_pad = jnp.zeros((1, GATE_PAD), F32).at[0, :N_FOX].set(ev_b_f[0])
    sinks_b = jnp.broadcast_to(od_sinks[0][:, None, None], (N_Q, 1, 128)).astype(F32)
    cosw, sinw = rope_tables(pos)

    saved = []
    h = x
    for i in range(2):
        s = {"h0": h}
        n1 = rmsnorm_fwd(h, norm_mix[i], f"norm_mix_fwd{i}")
        s["n1"] = n1
        if i == 0:
            proj = matmul(n1, w_in0, name="ev_in")
            cols = [proj[:, j * FOX_W:(j + 1) * FOX_W] for j in range(6)]
            flog = proj[:, EVEN_QKV:]
            cum = forget_cumsum(flog, b_pad, "forget_cumsum")
            fox = dict(zip(("qt", "q", "k"), fox_operands(cols[0], cols[1], cum[:, :N_FOX].T)))
            fox.update(kt=head_tiles(cols[1].astype(BF16), N_FOX), v=head_rows(cols[2].astype(BF16), N_FOX),
                       vt=head_tiles(cols[2].astype(BF16), N_FOX))
            q_sb = (cols[3] * SCALE).astype(BF16)
            sb = dict(qt=head_cols(q_sb, N_SB), q=head_rows(q_sb, N_SB), k=head_rows(cols[4].astype(BF16), N_SB),
                      kt=head_tiles(cols[4].astype(BF16), N_SB), v=head_rows(cols[5].astype(BF16), N_SB),
                      vt=head_tiles(cols[5].astype(BF16), N_SB))
            fox["ot"], fox["lse"] = fox_fwd(fox["qt"], fox["k"], fox["vt"])
            sb["ot"], sb["ltot"] = sb_fwd(sb["qt"], sb["k"], sb["vt"])
            s.update(flog=flog, fox=fox, sb=sb)
            mixin_t = jnp.concatenate([fox["ot"].reshape(FOX_W, t), sb["ot"].reshape(SB_W, t)], axis=0).astype(BF16)
            w_out = wfull[("ev_w_out", 0)]
        else:
            proj = matmul(n1, wfull[("od_w_in", 0)], name="od_in")
            qk = rope_apply(proj[:, :Q_W + KV_W], cosw, sinw, False, "rope_fwd")
            q_sc = (qk[:, :Q_W] * SCALE).astype(BF16)
            k_b = qk[:, Q_W:].astype(BF16)
            v_b = proj[:, Q_W + KV_W:].astype(BF16)
            swa = dict(qt=head_cols(q_sc, N_Q), q=head_rows(q_sc, N_Q), k=head_rows(k_b, N_KV),
                       kt=head_tiles(k_b, N_KV), v=head_rows(v_b, N_KV), vt=head_tiles(v_b, N_KV))
            swa["ot"], swa["lse"] = swa_fwd(swa["qt"], swa["k"], swa["vt"], sinks_b)
            s["swa"] = swa
            mixin_t = swa["ot"].reshape(Q_W, t).astype(BF16)
            w_out = wfull[("od_w_out", 0)]
        s["mixin_t"] = mixin_t
        h = matmul(mixin_t, w_out, "tn", residual=h, name=f"mix_out{i}")
        s["h1"] = h
        n2 = rmsnorm_fwd(h, norm_ffn[i], f"norm_ffn_fwd{i}")
        gu = matmul(n2, w_gu[i], name=f"ffn_gu{i}")
        act = swiglu_fwd(gu, f"swiglu_fwd{i}")
        s.update(n2=n2, gu=gu, act=act)
        h = matmul(act, wfull[("ffn_w_down", i)], residual=h, name=f"ffn_down{i}")
        s["h2"] = h
        n3 = rmsnorm_fwd(h, norm_ple[i], f"norm_ple_fwd{i}")
        pre = matmul(n3, wfull[("ple_w_gate", i)], name=f"ple_gate{i}")
        pp = matmul(p[i], wfull[("ple_w_proj", i)], name=f"ple_proj{i}")
        s.update(n3=n3, pre=pre, pp=pp)
        h = ple_fwd(h, pre, pp, f"ple_fwd{i}")
        saved.append(s)

    dh, dg_final, loss8 = loss_head(h, norm_final, tgt, "loss_head")
    gw = {}
    small = {"norm_final": dg_final, "loss": loss8}
    for i in (1, 0):
        s = saved[i]
        dpre, dpp = ple_bwd(dh, s["pre"], s["pp"], f"ple_bwd{i}")
        gw[("ple_w_gate", i)] = matmul(s["n3"], dpre, "tn", name=f"d_ple_gate{i}")
        gw[("ple_w_proj", i)] = matmul(p[i], dpp, "tn", name=f"d_ple_proj{i}")
        dn3 = matmul(dpre, wfull[("ple_w_gate", i)], "nt", name=f"dn_ple{i}")
        dh, small[("norm_ple", i)] = rmsnorm_bwd(s["h2"], norm_ple[i], dn3, dh, f"norm_ple_bwd{i}")

        dact = matmul(dh, wfull[("ffn_w_down", i)], "nt", name=f"d_act{i}")
        gw[("ffn_w_down", i)] = matmul(s["act"], dh, "tn", name=f"d_ffn_down{i}")
        dgu = swiglu_bwd(s["gu"], dact, f"swiglu_bwd{i}")
        dwgu = matmul(s["n2"], dgu, "tn", name=f"d_ffn_gu{i}")
        gw[("ffn_w_gate", i)] = dwgu[:, :D_FF]
        gw[("ffn_w_up", i)] = dwgu[:, D_FF:]
        dn2 = matmul(dgu, w_gu[i], "nt", name=f"dn_ffn{i}")
        dh, small[("norm_ffn", i)] = rmsnorm_bwd(s["h1"], norm_ffn[i], dn2, dh, f"norm_ffn_bwd{i}")

        if i == 0:
            dmix = matmul(dh, wfull[("ev_w_out", 0)], "nt", name="d_mix0")
            gw[("ev_w_out", 0)] = matmul(s["mixin_t"], dh, name="d_ev_out")
            fox, sb = s["fox"], s["sb"]
            dmb = dmix.astype(BF16)
            dqt_f, dk_aug, dv_f, ds_rows = fox_bwd(
                fox["qt"], fox["q"], fox["k"], fox["kt"], fox["v"], fox["ot"],
                head_rows(dmb[:, :FOX_W], N_FOX), head_cols(dmb[:, :FOX_W], N_FOX), fox["lse"])
            dqt_s, dk_s, dv_s = sb_bwd(sb["qt"], sb["q"], sb["k"], sb["kt"], sb["v"], sb["ltot"],
                                       head_rows(dmb[:, FOX_W:], N_SB), head_cols(dmb[:, FOX_W:], N_SB))
            d_cum = jnp.pad((ds_rows[:, 0, :] - dk_aug[:, :, HEAD_DIM]).T, ((0, 0), (0, GATE_PAD - N_FOX)))
            dflog, db8 = forget_cumsum_bwd(d_cum, s["flog"], b_pad, "forget_cumsum_bwd")
            small["ev_b_f"] = db8
            parts = (cols_to_flat(dqt_f), rows_to_flat(dk_aug[:, :, :HEAD_DIM]), rows_to_flat(dv_f),
                     cols_to_flat(dqt_s), rows_to_flat(dk_s), rows_to_flat(dv_s), dflog)
            dproj = jnp.concatenate([a.astype(BF16) for a in parts], axis=1)
            dw = matmul(s["n1"], dproj, "tn", name="d_ev_in")
            gw[("ev_w_in", 0)] = dw[:, :EVEN_QKV + N_FOX]
            dn1 = matmul(dproj, w_in0, "nt", name="dn_mix0")
        else:
            dmix = matmul(dh, wfull[("od_w_out", 0)], "nt", name="d_mix1")
            gw[("od_w_out", 0)] = matmul(s["mixin_t"], dh, name="d_od_out")
            swa = s["swa"]
            dmb = dmix.astype(BF16)
            dqt, dk, dv, dsink = swa_bwd(swa["qt"], swa["q"], swa["k"], swa["kt"], swa["v"], sinks_b, swa["ot"],
                                         head_rows(dmb, N_Q), head_cols(dmb, N_Q), swa["lse"])
            small["od_sinks"] = dsink
            dqk = rope_apply(jnp.concatenate([cols_to_flat(dqt), rows_to_flat(dk)], axis=1), cosw, sinw, True,
                             "rope_bwd")
            dproj = jnp.concatenate([dqk, rows_to_flat(dv)], axis=1).astype(BF16)
            gw[("od_w_in", 0)] = matmul(s["n1"], dproj, "tn", name="d_od_in")
            dn1 = matmul(dproj, wfull[("od_w_in", 0)], "nt", name="dn_mix1")
        dh, small[("norm_mix", i)] = rmsnorm_bwd(s["h0"], norm_mix[i], dn1, dh, f"norm_mix_bwd{i}")
    return dh, gw, small


_SMALL_ROWS = (("norm_mix", 0), ("norm_mix", 1), ("norm_ffn", 0), ("norm_ffn", 1),
               ("norm_ple", 0), ("norm_ple", 1), "norm_final", "misc", "loss")


def pack_small(small):
    rows = []
    for key in _SMALL_ROWS:
        if key == "misc":
            db = jnp.sum(small["ev_b_f"], axis=0)[:N_FOX]
            dsink = jnp.sum(small["od_sinks"][:, 0, :], axis=1)
            rows.append(jnp.zeros((D_MODEL,), F32).at[:N_FOX].set(db).at[128:128 + N_Q].set(dsink))
        else:
            rows.append(jnp.sum(small[key], axis=0))
    rows += [jnp.zeros((D_MODEL,), F32)] * (SMALL_ROWS - len(rows))
    return jnp.stack(rows)


def kernel(x, p, positions, norm_mix, norm_ffn, norm_ple, norm_final, ev_w_in, ev_b_f, ev_w_out, od_w_in, od_sinks, od_w_out, ffn_w_gate, ffn_w_up, ffn_w_down, ple_w_proj, ple_w_gate, loss_target, m_norm_mix, m_norm_ffn, m_norm_ple, m_norm_final, m_ev_w_in, m_ev_b_f, m_ev_w_out, m_od_w_in, m_od_sinks, m_od_w_out, m_ffn_w_gate, m_ffn_w_up, m_ffn_w_down, m_ple_w_proj, m_ple_w_gate, v_norm_mix, v_norm_ffn, v_norm_ple, v_norm_final, v_ev_w_in, v_ev_b_f, v_ev_w_out, v_od_w_in, v_od_sinks, v_od_w_out, v_ffn_w_gate, v_ffn_w_up, v_ffn_w_down, v_ple_w_proj, v_ple_w_gate):
    local_w = dict(ev_w_in=ev_w_in, ev_w_out=ev_w_out, od_w_in=od_w_in, od_w_out=od_w_out,
                   ffn_w_gate=ffn_w_gate, ffn_w_up=ffn_w_up, ffn_w_down=ffn_w_down,
                   ple_w_proj=ple_w_proj, ple_w_gate=ple_w_gate)
    local_m = dict(ev_w_in=m_ev_w_in, ev_w_out=m_ev_w_out, od_w_in=m_od_w_in, od_w_out=m_od_w_out,
                   ffn_w_gate=m_ffn_w_gate, ffn_w_up=m_ffn_w_up, ffn_w_down=m_ffn_w_down,
                   ple_w_proj=m_ple_w_proj, ple_w_gate=m_ple_w_gate)
    local_v = dict(ev_w_in=v_ev_w_in, ev_w_out=v_ev_w_out, od_w_in=v_od_w_in, od_w_out=v_od_w_out,
                   ffn_w_gate=v_ffn_w_gate, ffn_w_up=v_ffn_w_up, ffn_w_down=v_ffn_w_down,
                   ple_w_proj=v_ple_w_proj, ple_w_gate=v_ple_w_gate)

    wfull = unpack_full(allgather_chips(pack_shards(local_w)))
    grad_x, gw, small = local_step(x[0], p[:, 0], positions[0], loss_target[0], norm_mix, norm_ffn, norm_ple,
                                   norm_final, ev_b_f, od_sinks, wfull)

    cc = lax.axis_index("c")
    g4 = pack_grads(gw).reshape(N_CHIPS, 2, PACK_ROWS_HALF, D_MODEL)
    theirs = pair_send_other_half(g4)
    mine = lax.dynamic_index_in_dim(g4, cc, axis=1, keepdims=False)
    pair_sum = add_pair(mine.reshape(N_CHIPS * PACK_ROWS_HALF, D_MODEL),
                        theirs.reshape(N_CHIPS * PACK_ROWS_HALF, D_MODEL), "add_pair")
    from_chips = exchange_chips(pair_sum.reshape(N_CHIPS, PACK_ROWS_HALF, D_MODEL))
    half_sum = sum_chips(from_chips, "sum_chips")
    g_local = unpack_local(pair_gather(half_sum).reshape(PACK_ROWS, D_MODEL))

    red = allreduce_small(pack_small(small))
    loss = 0.5 * jnp.sum(red[8]) / D_MODEL
    pad_small = lambda a: jnp.zeros((D_MODEL,), F32).at[:N_FOX].set(a[0][0]).at[128:128 + N_Q].set(a[1][0])
    stack_small = lambda a: jnp.concatenate(
        [a[0], a[1], a[2], a[3][None], pad_small(a[4:6])[None], jnp.zeros((SMALL_ROWS - 8, D_MODEL), F32)], axis=0)
    w_small = stack_small((norm_mix, norm_ffn, norm_ple, norm_final, ev_b_f, od_sinks))
    m_small = stack_small((m_norm_mix, m_norm_ffn, m_norm_ple, m_norm_final, m_ev_b_f, m_od_sinks))
    v_small = stack_small((v_norm_mix, v_norm_ffn, v_norm_ple, v_norm_final, v_ev_b_f, v_od_sinks))
    g_small = red.at[8].set(0.0)
    upd_small = (g_small,) + tuple(adamw(w_small, g_small, m_small, v_small, "adamw_small"))

    def split_small(a):
        return (a[0:2], a[2:4], a[4:6], a[6], a[7, :N_FOX][None], a[7, 128:128 + N_Q][None])

    small_out = [split_small(a) for a in upd_small]

    big_names = ("ev_w_in", "ev_w_out", "od_w_in", "od_w_out", "ffn_w_gate", "ffn_w_up", "ffn_w_down",
                 "ple_w_proj", "ple_w_gate")
    big_out = {}
    for name in big_names:
        w = local_w[name]
        layers, r, c = w.shape
        g = jnp.concatenate([g_local[(name, i)] for i in range(layers)], axis=0)
        res = adamw(w.reshape(layers * r, c), g, local_m[name].reshape(layers * r, c),
                    local_v[name].reshape(layers * r, c), f"adamw_{name}")
        big_out[name] = [a.reshape(layers, r, c) for a in (g,) + tuple(res)]

    outs = [loss, grad_x[None]]
    for kind in range(4):
        sm = small_out[kind]
        outs += [sm[0], sm[1], sm[2], sm[3],
                 big_out["ev_w_in"][kind], sm[4], big_out["ev_w_out"][kind],
                 big_out["od_w_in"][kind], sm[5], big_out["od_w_out"][kind],
                 big_out["ffn_w_gate"][kind], big_out["ffn_w_up"][kind], big_out["ffn_w_down"][kind],
                 big_out["ple_w_proj"][kind], big_out["ple_w_gate"][kind]]
    return tuple(outs)
```

```python
import jax
import jax.numpy as jnp
from jax import lax
from jax.experimental import pallas as pl
from jax.experimental.pallas import tpu as pltpu

F32 = jnp.float32
BF16 = jnp.bfloat16

D_MODEL = 1024
HEAD_DIM = 64
N_FOX = 8
N_SB = 8
FOX_W = N_FOX * HEAD_DIM
SB_W = N_SB * HEAD_DIM
EVEN_QKV = 3 * FOX_W + 3 * SB_W
GATE_PAD = 128
EVEN_IN_PAD = EVEN_QKV + GATE_PAD
N_Q = 16
N_KV = 4
GROUP = N_Q // N_KV
Q_W = N_Q * HEAD_DIM
KV_W = N_KV * HEAD_DIM
ODD_IN = Q_W + 2 * KV_W
WINDOW = 128
ROPE_THETA = 10000.0
D_FF = 2816
PLE_DIM = 256
EPS = 1e-6
NEG_INF = -1e30
SCALE = HEAD_DIM ** -0.5

ADAM_LR = 0.001
ADAM_B1 = 0.9
ADAM_B2 = 0.999
ADAM_EPS = 1e-08
ADAM_WD = 0.01
ADAM_STEP = 10

N_CHIPS = 4
VMEM_LIMIT = 48 * 1024 * 1024
MESH = pl.DeviceIdType.MESH

_PACK = (
    ("ev_w_in", 0, 1024, 770, True),
    ("ev_w_out", 0, 256, 1024, False),
    ("od_w_in", 0, 1024, 384, True),
    ("od_w_out", 0, 256, 1024, False),
    ("ffn_w_gate", 0, 1024, 704, True),
    ("ffn_w_gate", 1, 1024, 704, True),
    ("ffn_w_up", 0, 1024, 704, True),
    ("ffn_w_up", 1, 1024, 704, True),
    ("ffn_w_down", 0, 704, 1024, False),
    ("ffn_w_down", 1, 704, 1024, False),
    ("ple_w_proj", 0, 256, 256, True),
    ("ple_w_proj", 1, 256, 256, True),
    ("ple_w_gate", 0, 256, 1024, False),
    ("ple_w_gate", 1, 256, 1024, False),
)
_ROW_ALIGN = 16


def _pack_rows(r, c):
    n = r * c // D_MODEL
    return -(-n // _ROW_ALIGN) * _ROW_ALIGN


PACK_ROWS_HALF = 3328
PACK_ROWS = 2 * PACK_ROWS_HALF
assert sum(_pack_rows(r, c) for _, _, r, c, _ in _PACK) <= PACK_ROWS


def _pick(n, cands):
    for c in cands:
        if n % c == 0:
            return c
    return n


def _cparams(sem):
    return pltpu.CompilerParams(dimension_semantics=sem, vmem_limit_bytes=VMEM_LIMIT)


_DIMS = {
    "nn": (((1,), (0,)), ((), ())),
    "nt": (((1,), (1,)), ((), ())),
    "tn": (((0,), (0,)), ((), ())),
}


def matmul(a, b, mode="nn", out_dtype=F32, residual=None, name="mm"):
    if mode == "nn":
        (m, k), (k2, n) = a.shape, b.shape
    elif mode == "nt":
        (m, k), (n, k2) = a.shape, b.shape
    else:
        (k, m), (k2, n) = a.shape, b.shape
    assert k == k2, (a.shape, b.shape, mode)
    tm = _pick(m, (512, 256, 128))
    tn = _pick(n, (1024, 512, 640, 384, 256, 128))
    tk = _pick(k, (1024, 1408, 640, 512, 256, 128))
    nk = k // tk
    dims = _DIMS[mode]
    has_res = residual is not None

    def body(*refs):
        if has_res:
            a_ref, b_ref, r_ref, o_ref, acc = refs
        else:
            a_ref, b_ref, o_ref, acc = refs
        kk = pl.program_id(2)
        part = lax.dot_general(a_ref[...].astype(BF16), b_ref[...].astype(BF16), dims,
                               preferred_element_type=F32)

        def finish(r):
            if has_res:
                r = r + r_ref[...]
            o_ref[...] = r.astype(out_dtype)

        if nk == 1:
            finish(part)
            return

        @pl.when(kk == 0)
        def _():
            acc[...] = part

        @pl.when((kk > 0) & (kk < nk - 1))
        def _():
            acc[...] += part

        @pl.when(kk == nk - 1)
        def _():
            finish(acc[...] + part)

    if mode == "nn":
        a_spec = pl.BlockSpec((tm, tk), lambda i, j, kk: (i, kk))
        b_spec = pl.BlockSpec((tk, tn), lambda i, j, kk: (kk, j))
    elif mode == "nt":
        a_spec = pl.BlockSpec((tm, tk), lambda i, j, kk: (i, kk))
        b_spec = pl.BlockSpec((tn, tk), lambda i, j, kk: (j, kk))
    else:
        a_spec = pl.BlockSpec((tk, tm), lambda i, j, kk: (kk, i))
        b_spec = pl.BlockSpec((tk, tn), lambda i, j, kk: (kk, j))
    o_spec = pl.BlockSpec((tm, tn), lambda i, j, kk: (i, j))
    in_specs = [a_spec, b_spec] + ([o_spec] if has_res else [])
    args = (a, b) + ((residual,) if has_res else ())
    return pl.pallas_call(
        body, name=name, grid=(m // tm, n // tn, nk),
        in_specs=in_specs, out_specs=o_spec,
        out_shape=jax.ShapeDtypeStruct((m, n), out_dtype),
        scratch_shapes=[pltpu.VMEM((tm, tn), F32)],
        compiler_params=_cparams(("parallel", "parallel", "arbitrary")),
    )(*args)


def _fold8(v):
    r, w = v.shape
    return v.reshape(r // 8, 8, w).sum(axis=0)


def rowwise(fn, rows, bcasts, outs, accs=(), tr=256, name="rowwise", reverse=False):
    t = rows[0].shape[0]
    tr = _pick(t, (tr, 128, 64, 32, 16, 8))
    nr, nb, no, na = len(rows), len(bcasts), len(outs), len(accs)
    steps = t // tr

    def body(*refs):
        ins = [r[...] for r in refs[:nr + nb]]
        out_refs = refs[nr + nb:nr + nb + no]
        acc_refs = refs[nr + nb + no:]
        o, a = fn(*ins)
        for r, v in zip(out_refs, o):
            r[...] = v.astype(r.dtype)
        if na:
            @pl.when(pl.program_id(0) == 0)
            def _():
                for r in acc_refs:
                    r[...] = jnp.zeros_like(r)

            for r, v in zip(acc_refs, a):
                r[...] += v

    if reverse:
        ridx = lambda i: (steps - 1 - i, 0)
    else:
        ridx = lambda i: (i, 0)
    in_specs = [pl.BlockSpec((tr, x.shape[1]), ridx) for x in rows]
    in_specs += [pl.BlockSpec(x.shape, lambda i: (0, 0)) for x in bcasts]
    out_specs = [pl.BlockSpec((tr, w), ridx) for w, _ in outs]
    out_specs += [pl.BlockSpec((8, w), lambda i: (0, 0)) for w in accs]
    out_shape = [jax.ShapeDtypeStruct((t, w), dt) for w, dt in outs]
    out_shape += [jax.ShapeDtypeStruct((8, w), F32) for w in accs]
    res = pl.pallas_call(
        body, name=name, grid=(steps,), in_specs=in_specs, out_specs=out_specs, out_shape=out_shape,
        compiler_params=_cparams(("arbitrary",)),
    )(*rows, *bcasts)
    return res


def _rstd(x):
    return lax.rsqrt(jnp.mean(x * x, axis=-1, keepdims=True) + EPS)


def rmsnorm_fwd(h, g, name):
    def fn(x, gg):
        return ((x * _rstd(x)) * gg,), ()

    return rowwise(fn, [h], [g.reshape(1, -1)], [(D_MODEL, BF16)], name=name)[0]


def _rms_bwd_math(x, gg, dy):
    r = _rstd(x)
    xh = x * r
    u = dy * gg
    dx = r * (u - xh * jnp.mean(u * xh, axis=-1, keepdims=True))
    return dx, dy * xh


def rmsnorm_bwd(h, g, dn, dres, name):
    def fn(x, dy, dr, gg):
        dx, dgp = _rms_bwd_math(x, gg, dy)
        return (dr + dx,), (_fold8(dgp),)

    return rowwise(fn, [h, dn, dres], [g.reshape(1, -1)], [(D_MODEL, F32)], [D_MODEL], name=name)


def loss_head(h, g, tgt, name):
    def fn(x, tg, gg):
        y = (x * _rstd(x)) * gg
        e = y - tg
        dy = e * (1.0 / D_MODEL)
        dx, dgp = _rms_bwd_math(x, gg, dy)
        return (dx,), (_fold8(dgp), _fold8(e * e))

    return rowwise(fn, [h, tgt], [g.reshape(1, -1)], [(D_MODEL, F32)], [D_MODEL, D_MODEL], name=name)


def _sigmoid(x):
    return 1.0 / (1.0 + jnp.exp(-x))


def swiglu_fwd(gu, name):
    def fn(x):
        gg, uu = x[:, :D_FF], x[:, D_FF:]
        return ((gg * _sigmoid(gg)) * uu,), ()

    return rowwise(fn, [gu], [], [(D_FF, BF16)], name=name)[0]


def swiglu_bwd(gu, da, name):
    def fn(x, d):
        gg, uu = x[:, :D_FF], x[:, D_FF:]
        s = _sigmoid(gg)
        silu = gg * s
        dgg = d * uu * (s + silu * (1.0 - s))
        duu = d * silu
        return (jnp.concatenate([dgg, duu], axis=1),), ()

    return rowwise(fn, [gu, da], [], [(2 * D_FF, BF16)], name=name)[0]


def ple_fwd(h, pre, pp, name):
    def fn(x, a, b):
        return (x + _sigmoid(a) * b,), ()

    return rowwise(fn, [h, pre, pp], [], [(D_MODEL, F32)], name=name)[0]


def ple_bwd(dh, pre, pp, name):
    def fn(d, a, b):
        s = _sigmoid(a)
        return (d * b * s * (1.0 - s), d * s), ()

    return rowwise(fn, [dh, pre, pp], [], [(D_MODEL, BF16), (D_MODEL, BF16)], name=name)


def _rot_half_partner(x, first_half):
    w = x.shape[1]
    return jnp.where(first_half, pltpu.roll(x, w - HEAD_DIM // 2, 1), pltpu.roll(x, HEAD_DIM // 2, 1))


def rope_apply(xx, cosw, sinw, backward, name):
    width = xx.shape[1]
    reps = width // 128

    def fn(x, c, s):
        cw = jnp.tile(c, (1, reps))
        sw = jnp.tile(s, (1, reps))
        lane = lax.broadcasted_iota(jnp.int32, x.shape, 1)
        first = (lane % HEAD_DIM) < (HEAD_DIM // 2)
        if backward:
            return (x * cw + _rot_half_partner(x * sw, first),), ()
        return (x * cw + _rot_half_partner(x, first) * sw,), ()

    return rowwise(fn, [xx, cosw, sinw], [], [(width, F32)], name=name)[0]


def _log_sigmoid(x):
    return jnp.minimum(x, 0.0) - jnp.log(1.0 + jnp.exp(-jnp.abs(x)))


CUM_BLOCK = 256


def forget_cumsum(flog, bias, name):
    t = flog.shape[0]
    tb = _pick(t, (CUM_BLOCK,))

    def body(x_ref, b_ref, o_ref, carry):
        @pl.when(pl.program_id(0) == 0)
        def _():
            carry[...] = jnp.zeros_like(carry)

        lf = _log_sigmoid(x_ref[...] + b_ref[...])
        r = lax.broadcasted_iota(jnp.int32, (tb, tb), 0)
        c = lax.broadcasted_iota(jnp.int32, (tb, tb), 1)
        tri = (c <= r).astype(F32)
        cum = jnp.dot(tri, lf, preferred_element_type=F32, precision=lax.Precision.HIGHEST) + carry[...]
        o_ref[...] = cum
        carry[...] = cum[tb - 1:tb, :]

    return pl.pallas_call(
        body, name=name, grid=(t // tb,),
        in_specs=[pl.BlockSpec((tb, GATE_PAD), lambda i: (i, 0)), pl.BlockSpec((1, GATE_PAD), lambda i: (0, 0))],
        out_specs=pl.BlockSpec((tb, GATE_PAD), lambda i: (i, 0)),
        out_shape=jax.ShapeDtypeStruct((t, GATE_PAD), F32),
        scratch_shapes=[pltpu.VMEM((1, GATE_PAD), F32)],
        compiler_params=_cparams(("arbitrary",)),
    )(flog, bias)


def forget_cumsum_bwd(d_cum, flog, bias, name):
    t = flog.shape[0]
    tb = _pick(t, (CUM_BLOCK,))
    nb = t // tb

    def body(d_ref, x_ref, b_ref, o_ref, db_ref, carry):
        @pl.when(pl.program_id(0) == 0)
        def _():
            carry[...] = jnp.zeros_like(carry)
            db_ref[...] = jnp.zeros_like(db_ref)

        r = lax.broadcasted_iota(jnp.int32, (tb, tb), 0)
        c = lax.broadcasted_iota(jnp.int32, (tb, tb), 1)
        tri = (c >= r).astype(F32)
        dlf = jnp.dot(tri, d_ref[...], preferred_element_type=F32, precision=lax.Precision.HIGHEST) + carry[...]
        carry[...] = dlf[0:1, :]
        dx = dlf * (1.0 - _sigmoid(x_ref[...] + b_ref[...]))
        o_ref[...] = dx
        db_ref[...] += _fold8(dx)

    rev = lambda i: (nb - 1 - i, 0)
    return pl.pallas_call(
        body, name=name, grid=(nb,),
        in_specs=[pl.BlockSpec((tb, GATE_PAD), rev), pl.BlockSpec((tb, GATE_PAD), rev),
                  pl.BlockSpec((1, GATE_PAD), lambda i: (0, 0))],
        out_specs=[pl.BlockSpec((tb, GATE_PAD), rev), pl.BlockSpec((8, GATE_PAD), lambda i: (0, 0))],
        out_shape=[jax.ShapeDtypeStruct((t, GATE_PAD), F32), jax.ShapeDtypeStruct((8, GATE_PAD), F32)],
        scratch_shapes=[pltpu.VMEM((1, GATE_PAD), F32)],
        compiler_params=_cparams(("arbitrary",)),
    )(d_cum, flog, bias)


TQ = 512
TK = 128
AUG = 128
N_BIAS = 3
SB_CUTOFF = 110.0


def _dot(a, b):
    return jnp.dot(a, b, preferred_element_type=F32)


def _rel(shape, d):
    return lax.broadcasted_iota(jnp.int32, shape, 0) - lax.broadcasted_iota(jnp.int32, shape, 1) + d


def _split_bf16(x):
    hi = x.astype(BF16)
    return hi, (x - hi.astype(F32)).astype(BF16)


def _tri_dot(tri, x):
    hi, lo = _split_bf16(x)
    return _dot(tri, hi) + _dot(tri, lo)


def _q_cols(width, tq):
    return pl.BlockSpec((None, width, tq), lambda h, i: (h, 0, i))


def _q_rows(width, tq):
    return pl.BlockSpec((None, tq, width), lambda h, i: (h, i, 0))


def _kv_rows(t, width):
    return pl.BlockSpec((None, t, width), lambda h, i: (h, 0, 0))


def _kv_tiles(nk):
    return pl.BlockSpec((None, nk, HEAD_DIM, TK), lambda h, i: (h, 0, 0, 0))


def _blocks(t):
    tq = TQ if t % TQ == 0 else TK
    return tq, tq // TK, t // TK


def fox_fwd(qt_aug, k_aug, vt):
    nh, _, t = qt_aug.shape
    tq, ratio, nk = _blocks(t)

    def body(q_ref, k_ref, v_ref, o_ref, lse_ref):
        i = pl.program_id(1)
        qv = q_ref[...]

        def step(first, carry, masked):
            m, l, acc = carry
            scores = []
            for u in range(ratio):
                off = pl.multiple_of((first + u) * TK, TK)
                s = _dot(k_ref[pl.ds(off, TK), :], qv)
                if masked:
                    s = jnp.where(_rel(s.shape, (first + u) * TK - i * tq) <= 0, s, NEG_INF)
                scores.append(s)
            m_new = m
            for s in scores:
                m_new = jnp.maximum(m_new, jnp.max(s, axis=0, keepdims=True))
            alpha = jnp.exp(m - m_new)
            l = alpha * l
            acc = alpha * acc
            for u, s in enumerate(scores):
                p = jnp.exp(s - m_new)
                l = l + jnp.sum(p, axis=0, keepdims=True)
                acc = acc + _dot(v_ref[first + u], p.astype(BF16))
            return m_new, l, acc

        carry = (jnp.full((1, tq), NEG_INF, F32), jnp.zeros((1, tq), F32), jnp.zeros((HEAD_DIM, tq), F32))
        carry = lax.fori_loop(0, i, lambda jj, c: step(ratio * jj, c, False), carry)
        m, l, acc = step(ratio * i, carry, True)
        o_ref[...] = acc / l
        lse_ref[...] = m + jnp.log(l)

    return pl.pallas_call(
        body, name="fox_fwd", grid=(nh, t // tq),
        in_specs=[_q_cols(AUG, tq), _kv_rows(t, AUG), _kv_tiles(nk)],
        out_specs=[_q_cols(HEAD_DIM, tq), _q_cols(1, tq)],
        out_shape=[jax.ShapeDtypeStruct((nh, HEAD_DIM, t), F32), jax.ShapeDtypeStruct((nh, 1, t), F32)],
        compiler_params=_cparams(("parallel", "arbitrary")),
    )(qt_aug, k_aug, vt)


def fox_bwd(qt_aug, q_aug, k_aug, kt, v, ot, do, dot_, lse):
    nh, _, t = qt_aug.shape
    tq, ratio, nk = _blocks(t)

    def body(qt_ref, q_ref, k_ref, kt_ref, v_ref, ot_ref, do_ref, dot_ref, lse_ref,
             dqt_ref, dk_ref, dv_ref, rs_ref):
        i = pl.program_id(1)

        @pl.when(i == 0)
        def _():
            dk_ref[...] = jnp.zeros_like(dk_ref)
            dv_ref[...] = jnp.zeros_like(dv_ref)

        qtv = qt_ref[...]
        qv = q_ref[...]
        dob = do_ref[...]
        dotb = dot_ref[...]
        delta = jnp.sum(ot_ref[...] * dotb.astype(F32), axis=0, keepdims=True)
        lse = lse_ref[...]

        def tile(j, carry, masked):
            dqt, rs = carry
            off = pl.multiple_of(j * TK, TK)
            s = _dot(k_ref[pl.ds(off, TK), :], qtv)
            p = jnp.exp(s - lse)
            if masked:
                p = jnp.where(_rel(s.shape, j * TK - i * tq) <= 0, p, 0.0)
            dp = _dot(v_ref[pl.ds(off, TK), :], dotb)
            dsb = (p * (dp - delta)).astype(BF16)
            dk_ref[pl.ds(off, TK), :] += _dot(dsb, qv)
            dv_ref[pl.ds(off, TK), :] += _dot(p.astype(BF16), dob)
            return dqt + _dot(kt_ref[j], dsb), rs + jnp.sum(dsb.astype(F32), axis=0, keepdims=True)

        def step(first, carry, masked):
            for u in range(ratio):
                carry = tile(first + u, carry, masked)
            return carry

        carry = (jnp.zeros((HEAD_DIM, tq), F32), jnp.zeros((1, tq), F32))
        carry = lax.fori_loop(0, i, lambda jj, c: step(ratio * jj, c, False), carry)
        dqt, rs = step(ratio * i, carry, True)
        dqt_ref[...] = dqt * SCALE
        rs_ref[...] = rs

    return pl.pallas_call(
        body, name="fox_bwd", grid=(nh, t // tq),
        in_specs=[_q_cols(AUG, tq), _q_rows(AUG, tq), _kv_rows(t, AUG), _kv_tiles(nk), _kv_rows(t, HEAD_DIM),
                  _q_cols(HEAD_DIM, tq), _q_rows(HEAD_DIM, tq), _q_cols(HEAD_DIM, tq), _q_cols(1, tq)],
        out_specs=[_q_cols(HEAD_DIM, tq), _kv_rows(t, AUG), _kv_rows(t, HEAD_DIM), _q_cols(1, tq)],
        out_shape=[jax.ShapeDtypeStruct((nh, HEAD_DIM, t), F32), jax.ShapeDtypeStruct((nh, t, AUG), F32),
                   jax.ShapeDtypeStruct((nh, t, HEAD_DIM), F32), jax.ShapeDtypeStruct((nh, 1, t), F32)],
        compiler_params=_cparams(("arbitrary", "arbitrary")),
    )(qt_aug, q_aug, k_aug, kt, v, ot, do, dot_, lse)


def _sb_logits(kb, qv, ok):
    z = _dot(kb, qv)
    e = jnp.exp(-jnp.abs(z))
    ll = -(jnp.maximum(z, 0.0) + jnp.log(1.0 + e))
    if ok is not None:
        ll = jnp.where(ok, ll, 0.0)
    return z, e, ll


def _tri(cmp):
    r = lax.broadcasted_iota(jnp.int32, (TK, TK), 0)
    c = lax.broadcasted_iota(jnp.int32, (TK, TK), 1)
    return cmp(r, c).astype(BF16)


def sb_fwd(qt, k, vt):
    nh, _, t = qt.shape
    tq, ratio, nk = _blocks(t)

    def body(q_ref, k_ref, v_ref, o_ref, tot_ref, first_ref):
        i = pl.program_id(1)
        qv = q_ref[...]
        tri_after = _tri(lambda r, c: c > r)

        def tile(j, carry, masked):
            c_l, acc = carry
            off = pl.multiple_of(j * TK, TK)
            ok = _rel((TK, tq), j * TK - i * tq) < 0 if masked else None
            z, _, ll = _sb_logits(k_ref[pl.ds(off, TK), :], qv, ok)
            a = jnp.exp(z + ll + _tri_dot(tri_after, ll) + c_l)
            if masked:
                a = jnp.where(ok, a, 0.0)
            return c_l + jnp.sum(ll, axis=0, keepdims=True), acc + _dot(v_ref[j], a.astype(BF16))

        def step(first, carry, masked):
            for u in reversed(range(ratio)):
                carry = tile(first + u, carry, masked)
            return carry

        c_l, acc = step(ratio * i, (jnp.zeros((1, tq), F32), jnp.zeros((HEAD_DIM, tq), F32)), True)

        def more(c):
            return (c[0] < i) & (jnp.max(c[1]) > -SB_CUTOFF)

        def sweep(c):
            c_l, acc = step(ratio * (i - 1 - c[0]), (c[1], c[2]), False)
            return c[0] + 1, c_l, acc

        done, c_l, acc = lax.while_loop(more, sweep, (jnp.int32(0), c_l, acc))
        o_ref[...] = acc
        tot_ref[...] = c_l
        first_ref[...] = jnp.full((1, 128), (i - done).astype(F32), F32)

    return pl.pallas_call(
        body, name="sb_fwd", grid=(nh, t // tq),
        in_specs=[_q_cols(HEAD_DIM, tq), _kv_rows(t, HEAD_DIM), _kv_tiles(nk)],
        out_specs=[_q_cols(HEAD_DIM, tq), _q_cols(1, tq), _q_cols(1, 128)],
        out_shape=[jax.ShapeDtypeStruct((nh, HEAD_DIM, t), F32), jax.ShapeDtypeStruct((nh, 1, t), F32),
                   jax.ShapeDtypeStruct((nh, 1, 128 * (t // tq)), F32)],
        compiler_params=_cparams(("parallel", "arbitrary")),
    )(qt, k, vt)


def sb_bwd(qt, q, k, kt, v, ltot, first, do, dot_):
    nh, _, t = qt.shape
    tq, ratio, nk = _blocks(t)

    def body(qt_ref, q_ref, k_ref, kt_ref, v_ref, tot_ref, first_ref, do_ref, dot_ref, dqt_ref, dk_ref, dv_ref):
        i = pl.program_id(1)
        start = jnp.max(first_ref[...]).astype(jnp.int32)

        @pl.when(i == 0)
        def _():
            dk_ref[...] = jnp.zeros_like(dk_ref)
            dv_ref[...] = jnp.zeros_like(dv_ref)

        qtv = qt_ref[...]
        qv = q_ref[...]
        dob = do_ref[...]
        dotb = dot_ref[...]
        tri_upto = _tri(lambda r, c: c <= r)
        tri_before = _tri(lambda r, c: c < r)

        def tile(j, carry, masked):
            rest, c_w, dqt = carry
            off = pl.multiple_of(j * TK, TK)
            ok = _rel((TK, tq), j * TK - i * tq) < 0 if masked else None
            z, e, ll = _sb_logits(k_ref[pl.ds(off, TK), :], qtv, ok)
            a = jnp.exp(z + ll + (rest - _tri_dot(tri_upto, ll)))
            if masked:
                a = jnp.where(ok, a, 0.0)
            w = a * _dot(v_ref[pl.ds(off, TK), :], dotb)
            before = _tri_dot(tri_before, w) + c_w
            r = 1.0 / (1.0 + e)
            sig = jnp.where(z >= 0.0, r, e * r)
            dz = w * (1.0 - sig) - before * sig
            if masked:
                dz = jnp.where(ok, dz, 0.0)
            dzb = dz.astype(BF16)
            dk_ref[pl.ds(off, TK), :] += _dot(dzb, qv)
            dv_ref[pl.ds(off, TK), :] += _dot(a.astype(BF16), dob)
            return (rest - jnp.sum(ll, axis=0, keepdims=True), c_w + jnp.sum(w, axis=0, keepdims=True),
                    dqt + _dot(kt_ref[j], dzb))

        def step(first, carry, masked):
            for u in range(ratio):
                carry = tile(first + u, carry, masked)
            return carry

        carry = (tot_ref[...], jnp.zeros((1, tq), F32), jnp.zeros((HEAD_DIM, tq), F32))
        carry = lax.fori_loop(start, i, lambda jj, c: step(ratio * jj, c, False), carry)
        dqt_ref[...] = step(ratio * i, carry, True)[2] * SCALE

    hd = HEAD_DIM
    return pl.pallas_call(
        body, name="sb_bwd", grid=(nh, t // tq),
        in_specs=[_q_cols(hd, tq), _q_rows(hd, tq), _kv_rows(t, hd), _kv_tiles(nk), _kv_rows(t, hd),
                  _q_cols(1, tq), _q_cols(1, 128), _q_rows(hd, tq), _q_cols(hd, tq)],
        out_specs=[_q_cols(hd, tq), _kv_rows(t, hd), _kv_rows(t, hd)],
        out_shape=[jax.ShapeDtypeStruct((nh, hd, t), F32), jax.ShapeDtypeStruct((nh, t, hd), F32),
                   jax.ShapeDtypeStruct((nh, t, hd), F32)],
        compiler_params=_cparams(("arbitrary", "arbitrary")),
    )(qt, q, k, kt, v, ltot, first, do, dot_)


def _swa_q_cols(width):
    return pl.BlockSpec((GROUP, width, WINDOW), lambda g, i: (g, 0, i))


def _swa_q_rows():
    return pl.BlockSpec((GROUP, WINDOW, HEAD_DIM), lambda g, i: (g, i, 0))


def _swa_kv_rows(t):
    return pl.BlockSpec((None, t, HEAD_DIM), lambda g, i: (g, 0, 0))


def _swa_kv_tiles(nk):
    return pl.BlockSpec((None, nk, HEAD_DIM, WINDOW), lambda g, i: (g, 0, 0, 0))


def _swa_sink_spec():
    return pl.BlockSpec((GROUP, 1, 128), lambda g, i: (g, 0, 0))


def _swa_window(i):
    jb = jnp.maximum(i - 1, 0)
    start = pl.multiple_of(jb * WINDOW, WINDOW)
    rel = _rel((2 * WINDOW, WINDOW), start - i * WINDOW)
    return jb, start, (rel <= 0) & (rel > -WINDOW)


def swa_fwd(qt, k, vt, sinks):
    nh, _, t = qt.shape
    nk = t // WINDOW

    def body(q_ref, k_ref, v_ref, s_ref, o_ref, lse_ref):
        i = pl.program_id(1)
        jb, start, valid = _swa_window(i)
        kwin = k_ref[pl.ds(start, 2 * WINDOW), :]
        for g in range(GROUP):
            s = jnp.where(valid, _dot(kwin, q_ref[g]), NEG_INF)
            sink = s_ref[g]
            m = jnp.maximum(jnp.max(s, axis=0, keepdims=True), sink)
            p = jnp.where(valid, jnp.exp(s - m), 0.0)
            l = jnp.sum(p, axis=0, keepdims=True) + jnp.exp(sink - m)
            pb = p.astype(BF16)
            o_ref[g] = (_dot(v_ref[jb], pb[:WINDOW]) + _dot(v_ref[jb + 1], pb[WINDOW:])) / l
            lse_ref[g] = m + jnp.log(l)

    return pl.pallas_call(
        body, name="swa_fwd", grid=(N_KV, nk),
        in_specs=[_swa_q_cols(HEAD_DIM), _swa_kv_rows(t), _swa_kv_tiles(nk), _swa_sink_spec()],
        out_specs=[_swa_q_cols(HEAD_DIM), _swa_q_cols(1)],
        out_shape=[jax.ShapeDtypeStruct((nh, HEAD_DIM, t), F32), jax.ShapeDtypeStruct((nh, 1, t), F32)],
        compiler_params=_cparams(("parallel", "arbitrary")),
    )(qt, k, vt, sinks)


def swa_bwd(qt, q, k, kt, v, sinks, ot, do, dot_, lse):
    nh, _, t = qt.shape
    nk = t // WINDOW

    def body(qt_ref, q_ref, k_ref, kt_ref, v_ref, s_ref, ot_ref, do_ref, dot_ref, lse_ref,
             dqt_ref, dk_ref, dv_ref, dsink_ref):
        i = pl.program_id(1)

        @pl.when(i == 0)
        def _():
            dk_ref[...] = jnp.zeros_like(dk_ref)
            dv_ref[...] = jnp.zeros_like(dv_ref)
            dsink_ref[...] = jnp.zeros_like(dsink_ref)

        jb, start, valid = _swa_window(i)
        kwin = k_ref[pl.ds(start, 2 * WINDOW), :]
        vwin = v_ref[pl.ds(start, 2 * WINDOW), :]
        dk_acc = jnp.zeros((2 * WINDOW, HEAD_DIM), F32)
        dv_acc = jnp.zeros((2 * WINDOW, HEAD_DIM), F32)
        for g in range(GROUP):
            lse_g = lse_ref[g]
            dotb = dot_ref[g]
            delta = jnp.sum(ot_ref[g] * dotb.astype(F32), axis=0, keepdims=True)
            p = jnp.where(valid, jnp.exp(_dot(kwin, qt_ref[g]) - lse_g), 0.0)
            dsb = (p * (_dot(vwin, dotb) - delta)).astype(BF16)
            dqt_ref[g] = (_dot(kt_ref[jb], dsb[:WINDOW]) + _dot(kt_ref[jb + 1], dsb[WINDOW:])) * SCALE
            dk_acc = dk_acc + _dot(dsb, q_ref[g])
            dv_acc = dv_acc + _dot(p.astype(BF16), do_ref[g])
            dsink_ref[g] += -jnp.exp(s_ref[g] - lse_g) * delta
        dk_ref[pl.ds(start, 2 * WINDOW), :] += dk_acc
        dv_ref[pl.ds(start, 2 * WINDOW), :] += dv_acc

    hd = HEAD_DIM
    return pl.pallas_call(
        body, name="swa_bwd", grid=(N_KV, nk),
        in_specs=[_swa_q_cols(hd), _swa_q_rows(), _swa_kv_rows(t), _swa_kv_tiles(nk), _swa_kv_rows(t),
                  _swa_sink_spec(), _swa_q_cols(hd), _swa_q_rows(), _swa_q_cols(hd), _swa_q_cols(1)],
        out_specs=[_swa_q_cols(hd), _swa_kv_rows(t), _swa_kv_rows(t), _swa_sink_spec()],
        out_shape=[jax.ShapeDtypeStruct((nh, hd, t), F32), jax.ShapeDtypeStruct((N_KV, t, hd), F32),
                   jax.ShapeDtypeStruct((N_KV, t, hd), F32), jax.ShapeDtypeStruct((nh, 1, 128), F32)],
        compiler_params=_cparams(("arbitrary", "arbitrary")),
    )(qt, q, k, kt, v, sinks, ot, do, dot_, lse)


def head_rows(a, nh):
    t = a.shape[0]
    return a.reshape(t, nh, HEAD_DIM).transpose(1, 0, 2)


def head_cols(a, nh):
    t = a.shape[0]
    return a.reshape(t, nh, HEAD_DIM).transpose(1, 2, 0)


def head_tiles(a, nh):
    t = a.shape[0]
    return a.reshape(t // TK, TK, nh, HEAD_DIM).transpose(2, 0, 3, 1)


def rows_to_flat(a):
    nh, t, _ = a.shape
    return a.transpose(1, 0, 2).reshape(t, nh * HEAD_DIM)


def cols_to_flat(a):
    nh, _, t = a.shape
    return a.transpose(2, 0, 1).reshape(t, nh * HEAD_DIM)


def fox_operands(qf, kf, cum_heads):
    t = qf.shape[0]
    nh = cum_heads.shape[0]
    qs = (qf * SCALE).astype(BF16)
    ones_t = jnp.ones((nh, N_BIAS, t), BF16)
    qt_aug = jnp.concatenate([head_cols(qs, nh), ones_t, jnp.zeros((nh, AUG - HEAD_DIM - N_BIAS, t), BF16)], axis=1)
    q_aug = jnp.concatenate([head_rows(qs, nh), jnp.ones((nh, t, 1), BF16),
                             jnp.zeros((nh, t, AUG - HEAD_DIM - 1), BF16)], axis=2)
    terms, rest = [], -cum_heads
    for _ in range(N_BIAS):
        term = lax.reduce_precision(rest, exponent_bits=8, mantissa_bits=7)
        terms.append(term.astype(BF16))
        rest = rest - term
    k_aug = jnp.concatenate([head_rows(kf.astype(BF16), nh)] + [x[:, :, None] for x in terms]
                            + [jnp.zeros((nh, t, AUG - HEAD_DIM - N_BIAS), BF16)], axis=2)
    return qt_aug, q_aug, k_aug


def adamw(w, g, m, v, name):
    cols = w.shape[1]

    def fn(ww, gg, mm, vv):
        mn = ADAM_B1 * mm + (1.0 - ADAM_B1) * gg
        vn = ADAM_B2 * vv + (1.0 - ADAM_B2) * (gg * gg)
        m_hat = mn / (1.0 - ADAM_B1 ** ADAM_STEP)
        v_hat = vn / (1.0 - ADAM_B2 ** ADAM_STEP)
        delta = -ADAM_LR * (m_hat / (jnp.sqrt(v_hat) + ADAM_EPS) + ADAM_WD * ww)
        return (delta, mn, vn), ()

    return rowwise(fn, [w, g, m, v], [], [(cols, F32)] * 3, name=name)


ANY = pl.BlockSpec(memory_space=pl.ANY)


def _my_place():
    return lax.axis_index("x"), lax.axis_index("y"), lax.axis_index("c")


def _flip(coord, bit):
    return 1 - coord if bit else coord


def allgather_chips(w):
    r, c = w.shape

    def body(w_ref, out_ref, send_sems, recv_sems, local_sem):
        x, y, cc = _my_place()
        me = 2 * x + y
        local = pltpu.make_async_copy(w_ref, out_ref.at[me], local_sem)
        local.start()
        copies = []
        for kbits in (1, 2, 3):
            peer = (_flip(x, kbits >> 1), _flip(y, kbits & 1), cc)
            cp = pltpu.make_async_remote_copy(
                src_ref=w_ref, dst_ref=out_ref.at[me], send_sem=send_sems.at[kbits - 1],
                recv_sem=recv_sems.at[kbits - 1], device_id=peer, device_id_type=MESH)
            cp.start()
            copies.append(cp)
        for cp in copies:
            cp.wait()
        local.wait()

    return pl.pallas_call(
        body, name="allgather_chips", in_specs=[ANY], out_specs=ANY,
        out_shape=jax.ShapeDtypeStruct((N_CHIPS, r, c), w.dtype),
        scratch_shapes=[pltpu.SemaphoreType.DMA((3,)), pltpu.SemaphoreType.DMA((3,)), pltpu.SemaphoreType.DMA],
    )(w)


def pair_send_other_half(g):
    n, _, rh, c = g.shape

    def body(g_ref, out_ref, send_sem, recv_sem):
        x, y, cc = _my_place()
        cp = pltpu.make_async_remote_copy(
            src_ref=g_ref.at[:, 1 - cc], dst_ref=out_ref, send_sem=send_sem, recv_sem=recv_sem,
            device_id=(x, y, 1 - cc), device_id_type=MESH)
        cp.start()
        cp.wait()

    return pl.pallas_call(
        body, name="pair_send_other_half", in_specs=[ANY], out_specs=ANY,
        out_shape=jax.ShapeDtypeStruct((n, rh, c), g.dtype),
        scratch_shapes=[pltpu.SemaphoreType.DMA, pltpu.SemaphoreType.DMA],
    )(g)


def exchange_chips(s):
    n, rh, c = s.shape

    def body(s_ref, out_ref, send_sems, recv_sems, local_sem):
        x, y, cc = _my_place()
        me = 2 * x + y
        local = pltpu.make_async_copy(s_ref.at[me], out_ref.at[me], local_sem)
        local.start()
        copies = []
        for kbits in (1, 2, 3):
            px, py = _flip(x, kbits >> 1), _flip(y, kbits & 1)
            cp = pltpu.make_async_remote_copy(
                src_ref=s_ref.at[2 * px + py], dst_ref=out_ref.at[me], send_sem=send_sems.at[kbits - 1],
                recv_sem=recv_sems.at[kbits - 1], device_id=(px, py, cc), device_id_type=MESH)
            cp.start()
            copies.append(cp)
        for cp in copies:
            cp.wait()
        local.wait()

    return pl.pallas_call(
        body, name="exchange_chips", in_specs=[ANY], out_specs=ANY,
        out_shape=jax.ShapeDtypeStruct((n, rh, c), s.dtype),
        scratch_shapes=[pltpu.SemaphoreType.DMA((3,)), pltpu.SemaphoreType.DMA((3,)), pltpu.SemaphoreType.DMA],
    )(s)


def pair_gather(tt):
    rh, c = tt.shape

    def body(t_ref, out_ref, send_sem, recv_sem, local_sem):
        x, y, cc = _my_place()
        local = pltpu.make_async_copy(t_ref, out_ref.at[cc], local_sem)
        local.start()
        cp = pltpu.make_async_remote_copy(
            src_ref=t_ref, dst_ref=out_ref.at[cc], send_sem=send_sem, recv_sem=recv_sem,
            device_id=(x, y, 1 - cc), device_id_type=MESH)
        cp.start()
        cp.wait()
        local.wait()

    return pl.pallas_call(
        body, name="pair_gather", in_specs=[ANY], out_specs=ANY,
        out_shape=jax.ShapeDtypeStruct((2, rh, c), tt.dtype),
        scratch_shapes=[pltpu.SemaphoreType.DMA, pltpu.SemaphoreType.DMA, pltpu.SemaphoreType.DMA],
    )(tt)


SMALL_ROWS = 16


def allreduce_small(v):
    r, c = v.shape
    vm = pl.BlockSpec(memory_space=pltpu.VMEM)

    def body(v_ref, out_ref, slots, send_sems, recv_sems):
        x, y, cc = _my_place()
        me = 4 * x + 2 * y + cc
        slots[me] = v_ref[...]
        copies = []
        for kbits in range(1, 8):
            peer = (_flip(x, kbits >> 2), _flip(y, (kbits >> 1) & 1), _flip(cc, kbits & 1))
            cp = pltpu.make_async_remote_copy(
                src_ref=v_ref, dst_ref=slots.at[me], send_sem=send_sems.at[kbits - 1],
                recv_sem=recv_sems.at[kbits - 1], device_id=peer, device_id_type=MESH)
            cp.start()
            copies.append(cp)
        for cp in copies:
            cp.wait()
        total = slots[0]
        for dev in range(1, 8):
            total = total + slots[dev]
        out_ref[...] = total

    return pl.pallas_call(
        body, name="allreduce_small", in_specs=[vm], out_specs=vm,
        out_shape=jax.ShapeDtypeStruct((r, c), F32),
        scratch_shapes=[pltpu.VMEM((8, r, c), F32), pltpu.SemaphoreType.DMA((7,)), pltpu.SemaphoreType.DMA((7,))],
    )(v)


def add_pair(mine, theirs, name):
    return rowwise(lambda a, b: ((a + b,), ()), [mine, theirs], [], [(mine.shape[1], BF16)], name=name)[0]


def sum_chips(r4, name):
    _, rh, c = r4.shape
    tr = _pick(rh, (256, 128, 64, 32, 16))

    def body(r_ref, o_ref):
        total = r_ref[0].astype(F32)
        for j in range(1, N_CHIPS):
            total = total + r_ref[j].astype(F32)
        o_ref[...] = total

    return pl.pallas_call(
        body, name=name, grid=(rh // tr,),
        in_specs=[pl.BlockSpec((N_CHIPS, tr, c), lambda i: (0, i, 0))],
        out_specs=pl.BlockSpec((tr, c), lambda i: (i, 0)),
        out_shape=jax.ShapeDtypeStruct((rh, c), F32),
        compiler_params=_cparams(("parallel",)),
    )(r4)


def _pad_rows(a, rows):
    return jnp.pad(a, ((0, rows - a.shape[0]), (0, 0))) if rows != a.shape[0] else a


def pack_shards(local):
    parts = []
    for name, layer, r, c, _ in _PACK:
        flat = local[name][layer].astype(BF16).reshape(r * c // D_MODEL, D_MODEL)
        parts.append(_pad_rows(flat, _pack_rows(r, c)))
    used = sum(p.shape[0] for p in parts)
    parts.append(jnp.zeros((PACK_ROWS - used, D_MODEL), BF16))
    return jnp.concatenate(parts, axis=0)


def unpack_full(gathered):
    full, off = {}, 0
    for name, layer, r, c, by_cols in _PACK:
        n = r * c // D_MODEL
        blk = gathered[:, off:off + n, :].reshape(N_CHIPS, r, c)
        if by_cols:
            full[(name, layer)] = blk.transpose(1, 0, 2).reshape(r, N_CHIPS * c)
        else:
            full[(name, layer)] = blk.reshape(N_CHIPS * r, c)
        off += _pack_rows(r, c)
    return full


def pack_grads(grads):
    parts = []
    for name, layer, r, c, by_cols in _PACK:
        g = grads[(name, layer)]
        if by_cols:
            blk = g.reshape(r, N_CHIPS, c).transpose(1, 0, 2)
        else:
            blk = g.reshape(N_CHIPS, r, c)
        flat = blk.reshape(N_CHIPS, r * c // D_MODEL, D_MODEL)
        rows = _pack_rows(r, c)
        if rows != flat.shape[1]:
            flat = jnp.pad(flat, ((0, 0), (0, rows - flat.shape[1]), (0, 0)))
        parts.append(flat)
    used = sum(p.shape[1] for p in parts)
    parts.append(jnp.zeros((N_CHIPS, PACK_ROWS - used, D_MODEL), F32))
    return jnp.concatenate(parts, axis=1)


def unpack_local(flat):
    out, off = {}, 0
    for name, layer, r, c, _ in _PACK:
        n = r * c // D_MODEL
        out[(name, layer)] = flat[off:off + n, :].reshape(r, c)
        off += _pack_rows(r, c)
    return out


def rope_tables(pos):
    half = HEAD_DIM // 2
    lane = jnp.arange(128)
    inv = ROPE_THETA ** (-(lane % half).astype(F32) / half)
    ang = pos.astype(F32)[:, None] * inv[None, :]
    sign = jnp.where((lane % HEAD_DIM) < half, -1.0, 1.0).astype(F32)
    return jnp.cos(ang), jnp.sin(ang) * sign[None, :]


def local_step(x, p, pos, tgt, norm_mix, norm_ffn, norm_ple, norm_final, ev_b_f, od_sinks, wfull):
    t = x.shape[0]
    w_in0 = wfull[("ev_w_in", 0)]
    w_in0 = jnp.concatenate([w_in0, jnp.zeros((D_MODEL, EVEN_IN_PAD - w_in0.shape[1]), w_in0.dtype)], axis=1)
    w_gu = [jnp.concatenate([wfull[("ffn_w_gate", i)], wfull[("ffn_w_up", i)]], axis=1) for i in range(2)]
    b_pad = jnp.zeros((1, GATE_PAD), F32).at[0, :N_FOX].set(ev_b_f[0])
    sinks_b = jnp.broadcast_to(od_sinks[0][:, None, None], (N_Q, 1, 128)).astype(F32)
    cosw, sinw = rope_tables(pos)

    saved = []
    h = x
    for i in range(2):
        s = {"h0": h}
        n1 = rmsnorm_fwd(h, norm_mix[i], f"norm_mix_fwd{i}")
        s["n1"] = n1
        if i == 0:
            proj = matmul(n1, w_in0, name="ev_in")
            cols = [proj[:, j * FOX_W:(j + 1) * FOX_W] for j in range(6)]
            flog = proj[:, EVEN_QKV:]
            cum = forget_cumsum(flog, b_pad, "forget_cumsum")
            fox = dict(zip(("qt", "q", "k"), fox_operands(cols[0], cols[1], cum[:, :N_FOX].T)))
            fox.update(kt=head_tiles(cols[1].astype(BF16), N_FOX), v=head_rows(cols[2].astype(BF16), N_FOX),
                       vt=head_tiles(cols[2].astype(BF16), N_FOX))
            q_sb = (cols[3] * SCALE).astype(BF16)
            sb = dict(qt=head_cols(q_sb, N_SB), q=head_rows(q_sb, N_SB), k=head_rows(cols[4].astype(BF16), N_SB),
                      kt=head_tiles(cols[4].astype(BF16), N_SB), v=head_rows(cols[5].astype(BF16), N_SB),
                      vt=head_tiles(cols[5].astype(BF16), N_SB))
            fox["ot"], fox["lse"] = fox_fwd(fox["qt"], fox["k"], fox["vt"])
            sb["ot"], sb["ltot"], sb["first"] = sb_fwd(sb["qt"], sb["k"], sb["vt"])
            s.update(flog=flog, fox=fox, sb=sb)
            mixin_t = jnp.concatenate([fox["ot"].reshape(FOX_W, t), sb["ot"].reshape(SB_W, t)], axis=0).astype(BF16)
            w_out = wfull[("ev_w_out", 0)]
        else:
            proj = matmul(n1, wfull[("od_w_in", 0)], name="od_in")
            qk = rope_apply(proj[:, :Q_W + KV_W], cosw, sinw, False, "rope_fwd")
            q_sc = (qk[:, :Q_W] * SCALE).astype(BF16)
            k_b = qk[:, Q_W:].astype(BF16)
            v_b = proj[:, Q_W + KV_W:].astype(BF16)
            swa = dict(qt=head_cols(q_sc, N_Q), q=head_rows(q_sc, N_Q), k=head_rows(k_b, N_KV),
                       kt=head_tiles(k_b, N_KV), v=head_rows(v_b, N_KV), vt=head_tiles(v_b, N_KV))
            swa["ot"], swa["lse"] = swa_fwd(swa["qt"], swa["k"], swa["vt"], sinks_b)
            s["swa"] = swa
            mixin_t = swa["ot"].reshape(Q_W, t).astype(BF16)
            w_out = wfull[("od_w_out", 0)]
        s["mixin_t"] = mixin_t
        h = matmul(mixin_t, w_out, "tn", residual=h, name=f"mix_out{i}")
        s["h1"] = h
        n2 = rmsnorm_fwd(h, norm_ffn[i], f"norm_ffn_fwd{i}")
        gu = matmul(n2, w_gu[i], name=f"ffn_gu{i}")
        act = swiglu_fwd(gu, f"swiglu_fwd{i}")
        s.update(n2=n2, gu=gu, act=act)
        h = matmul(act, wfull[("ffn_w_down", i)], residual=h, name=f"ffn_down{i}")
        s["h2"] = h
        n3 = rmsnorm_fwd(h, norm_ple[i], f"norm_ple_fwd{i}")
        pre = matmul(n3, wfull[("ple_w_gate", i)], name=f"ple_gate{i}")
        pp = matmul(p[i], wfull[("ple_w_proj", i)], name=f"ple_proj{i}")
        s.update(n3=n3, pre=pre, pp=pp)
        h = ple_fwd(h, pre, pp, f"ple_fwd{i}")
        saved.append(s)

    dh, dg_final, loss8 = loss_head(h, norm_final, tgt, "loss_head")
    gw = {}
    small = {"norm_final": dg_final, "loss": loss8}
    for i in (1, 0):
        s = saved[i]
        dpre, dpp = ple_bwd(dh, s["pre"], s["pp"], f"ple_bwd{i}")
        gw[("ple_w_gate", i)] = matmul(s["n3"], dpre, "tn", name=f"d_ple_gate{i}")
        gw[("ple_w_proj", i)] = matmul(p[i], dpp, "tn", name=f"d_ple_proj{i}")
        dn3 = matmul(dpre, wfull[("ple_w_gate", i)], "nt", name=f"dn_ple{i}")
        dh, small[("norm_ple", i)] = rmsnorm_bwd(s["h2"], norm_ple[i], dn3, dh, f"norm_ple_bwd{i}")

        dact = matmul(dh, wfull[("ffn_w_down", i)], "nt", name=f"d_act{i}")
        gw[("ffn_w_down", i)] = matmul(s["act"], dh, "tn", name=f"d_ffn_down{i}")
        dgu = swiglu_bwd(s["gu"], dact, f"swiglu_bwd{i}")
        dwgu = matmul(s["n2"], dgu, "tn", name=f"d_ffn_gu{i}")
        gw[("ffn_w_gate", i)] = dwgu[:, :D_FF]
        gw[("ffn_w_up", i)] = dwgu[:, D_FF:]
        dn2 = matmul(dgu, w_gu[i], "nt", name=f"dn_ffn{i}")
        dh, small[("norm_ffn", i)] = rmsnorm_bwd(s["h1"], norm_ffn[i], dn2, dh, f"norm_ffn_bwd{i}")

        if i == 0:
            dmix = matmul(dh, wfull[("ev_w_out", 0)], "nt", name="d_mix0")
            gw[("ev_w_out", 0)] = matmul(s["mixin_t"], dh, name="d_ev_out")
            fox, sb = s["fox"], s["sb"]
            dmb = dmix.astype(BF16)
            dqt_f, dk_aug, dv_f, ds_rows = fox_bwd(
                fox["qt"], fox["q"], fox["k"], fox["kt"], fox["v"], fox["ot"],
                head_rows(dmb[:, :FOX_W], N_FOX), head_cols(dmb[:, :FOX_W], N_FOX), fox["lse"])
            dqt_s, dk_s, dv_s = sb_bwd(sb["qt"], sb["q"], sb["k"], sb["kt"], sb["v"], sb["ltot"], sb["first"],
                                       head_rows(dmb[:, FOX_W:], N_SB), head_cols(dmb[:, FOX_W:], N_SB))
            d_cum = jnp.pad((ds_rows[:, 0, :] - dk_aug[:, :, HEAD_DIM]).T, ((0, 0), (0, GATE_PAD - N_FOX)))
            dflog, db8 = forget_cumsum_bwd(d_cum, s["flog"], b_pad, "forget_cumsum_bwd")
            small["ev_b_f"] = db8
            parts = (cols_to_flat(dqt_f), rows_to_flat(dk_aug[:, :, :HEAD_DIM]), rows_to_flat(dv_f),
                     cols_to_flat(dqt_s), rows_to_flat(dk_s), rows_to_flat(dv_s), dflog)
            dproj = jnp.concatenate([a.astype(BF16) for a in parts], axis=1)
            dw = matmul(s["n1"], dproj, "tn", name="d_ev_in")
            gw[("ev_w_in", 0)] = dw[:, :EVEN_QKV + N_FOX]
            dn1 = matmul(dproj, w_in0, "nt", name="dn_mix0")
        else:
            dmix = matmul(dh, wfull[("od_w_out", 0)], "nt", name="d_mix1")
            gw[("od_w_out", 0)] = matmul(s["mixin_t"], dh, name="d_od_out")
            swa = s["swa"]
            dmb = dmix.astype(BF16)
            dqt, dk, dv, dsink = swa_bwd(swa["qt"], swa["q"], swa["k"], swa["kt"], swa["v"], sinks_b, swa["ot"],
                                         head_rows(dmb, N_Q), head_cols(dmb, N_Q), swa["lse"])
            small["od_sinks"] = dsink
            dqk = rope_apply(jnp.concatenate([cols_to_flat(dqt), rows_to_flat(dk)], axis=1), cosw, sinw, True,
                             "rope_bwd")
            dproj = jnp.concatenate([dqk, rows_to_flat(dv)], axis=1).astype(BF16)
            gw[("od_w_in", 0)] = matmul(s["n1"], dproj, "tn", name="d_od_in")
            dn1 = matmul(dproj, wfull[("od_w_in", 0)], "nt", name="dn_mix1")
        dh, small[("norm_mix", i)] = rmsnorm_bwd(s["h0"], norm_mix[i], dn1, dh, f"norm_mix_bwd{i}")
    return dh, gw, small


_SMALL_ROWS = (("norm_mix", 0), ("norm_mix", 1), ("norm_ffn", 0), ("norm_ffn", 1),
               ("norm_ple", 0), ("norm_ple", 1), "norm_final", "misc", "loss")


def pack_small(small):
    rows = []
    for key in _SMALL_ROWS:
        if key == "misc":
            db = jnp.sum(small["ev_b_f"], axis=0)[:N_FOX]
            dsink = jnp.sum(small["od_sinks"][:, 0, :], axis=1)
            rows.append(jnp.zeros((D_MODEL,), F32).at[:N_FOX].set(db).at[128:128 + N_Q].set(dsink))
        else:
            rows.append(jnp.sum(small[key], axis=0))
    rows += [jnp.zeros((D_MODEL,), F32)] * (SMALL_ROWS - len(rows))
    return jnp.stack(rows)


def kernel(x, p, positions, norm_mix, norm_ffn, norm_ple, norm_final, ev_w_in, ev_b_f, ev_w_out, od_w_in, od_sinks, od_w_out, ffn_w_gate, ffn_w_up, ffn_w_down, ple_w_proj, ple_w_gate, loss_target, m_norm_mix, m_norm_ffn, m_norm_ple, m_norm_final, m_ev_w_in, m_ev_b_f, m_ev_w_out, m_od_w_in, m_od_sinks, m_od_w_out, m_ffn_w_gate, m_ffn_w_up, m_ffn_w_down, m_ple_w_proj, m_ple_w_gate, v_norm_mix, v_norm_ffn, v_norm_ple, v_norm_final, v_ev_w_in, v_ev_b_f, v_ev_w_out, v_od_w_in, v_od_sinks, v_od_w_out, v_ffn_w_gate, v_ffn_w_up, v_ffn_w_down, v_ple_w_proj, v_ple_w_gate):
    local_w = dict(ev_w_in=ev_w_in, ev_w_out=ev_w_out, od_w_in=od_w_in, od_w_out=od_w_out,
                   ffn_w_gate=ffn_w_gate, ffn_w_up=ffn_w_up, ffn_w_down=ffn_w_down,
                   ple_w_proj=ple_w_proj, ple_w_gate=ple_w_gate)
    local_m = dict(ev_w_in=m_ev_w_in, ev_w_out=m_ev_w_out, od_w_in=m_od_w_in, od_w_out=m_od_w_out,
                   ffn_w_gate=m_ffn_w_gate, ffn_w_up=m_ffn_w_up, ffn_w_down=m_ffn_w_down,
                   ple_w_proj=m_ple_w_proj, ple_w_gate=m_ple_w_gate)
    local_v = dict(ev_w_in=v_ev_w_in, ev_w_out=v_ev_w_out, od_w_in=v_od_w_in, od_w_out=v_od_w_out,
                   ffn_w_gate=v_ffn_w_gate, ffn_w_up=v_ffn_w_up, ffn_w_down=v_ffn_w_down,
                   ple_w_proj=v_ple_w_proj, ple_w_gate=v_ple_w_gate)

    wfull = unpack_full(allgather_chips(pack_shards(local_w)))
    grad_x, gw, small = local_step(x[0], p[:, 0], positions[0], loss_target[0], norm_mix, norm_ffn, norm_ple,
                                   norm_final, ev_b_f, od_sinks, wfull)

    cc = lax.axis_index("c")
    g4 = pack_grads(gw).reshape(N_CHIPS, 2, PACK_ROWS_HALF, D_MODEL)
    theirs = pair_send_other_half(g4)
    mine = lax.dynamic_index_in_dim(g4, cc, axis=1, keepdims=False)
    pair_sum = add_pair(mine.reshape(N_CHIPS * PACK_ROWS_HALF, D_MODEL),
                        theirs.reshape(N_CHIPS * PACK_ROWS_HALF, D_MODEL), "add_pair")
    from_chips = exchange_chips(pair_sum.reshape(N_CHIPS, PACK_ROWS_HALF, D_MODEL))
    half_sum = sum_chips(from_chips, "sum_chips")
    g_local = unpack_local(pair_gather(half_sum).reshape(PACK_ROWS, D_MODEL))

    red = allreduce_small(pack_small(small))
    loss = 0.5 * jnp.sum(red[8]) / D_MODEL
    pad_small = lambda a: jnp.zeros((D_MODEL,), F32).at[:N_FOX].set(a[0][0]).at[128:128 + N_Q].set(a[1][0])
    stack_small = lambda a: jnp.concatenate(
        [a[0], a[1], a[2], a[3][None], pad_small(a[4:6])[None], jnp.zeros((SMALL_ROWS - 8, D_MODEL), F32)], axis=0)
    w_small = stack_small((norm_mix, norm_ffn, norm_ple, norm_final, ev_b_f, od_sinks))
    m_small = stack_small((m_norm_mix, m_norm_ffn, m_norm_ple, m_norm_final, m_ev_b_f, m_od_sinks))
    v_small = stack_small((v_norm_mix, v_norm_ffn, v_norm_ple, v_norm_final, v_ev_b_f, v_od_sinks))
    g_small = red.at[8].set(0.0)
    upd_small = (g_small,) + tuple(adamw(w_small, g_small, m_small, v_small, "adamw_small"))

    def split_small(a):
        return (a[0:2], a[2:4], a[4:6], a[6], a[7, :N_FOX][None], a[7, 128:128 + N_Q][None])

    small_out = [split_small(a) for a in upd_small]

    big_names = ("ev_w_in", "ev_w_out", "od_w_in", "od_w_out", "ffn_w_gate", "ffn_w_up", "ffn_w_down",
                 "ple_w_proj", "ple_w_gate")
    big_out = {}
    for name in big_names:
        w = local_w[name]
        layers, r, c = w.shape
        g = jnp.concatenate([g_local[(name, i)] for i in range(layers)], axis=0)
        res = adamw(w.reshape(layers * r, c), g, local_m[name].reshape(layers * r, c),
                    local_v[name].reshape(layers * r, c), f"adamw_{name}")
        big_out[name] = [a.reshape(layers, r, c) for a in (g,) + tuple(res)]

    outs = [loss, grad_x[None]]
    for kind in range(4):
        sm = small_out[kind]
        outs += [sm[0], sm[1], sm[2], sm[3],
                 big_out["ev_w_in"][kind], sm[4], big_out["ev_w_out"][kind],
                 big_out["od_w_in"][kind], sm[5], big_out["od_w_out"][kind],
                 big_out["ffn_w_gate"][kind], big_out["ffn_w_up"][kind], big_out["ffn_w_down"][kind],
                 big_out["ple_w_proj"][kind], big_out["ple_w_gate"][kind]]
    return tuple(outs)
```

```python
import jax
import jax.numpy as jnp
from jax import lax
from jax.experimental import pallas as pl
from jax.experimental.pallas import tpu as pltpu

F32 = jnp.float32
BF16 = jnp.bfloat16

D_MODEL = 1024
HEAD_DIM = 64
N_FOX = 8
N_SB = 8
FOX_W = N_FOX * HEAD_DIM
SB_W = N_SB * HEAD_DIM
EVEN_QKV = 3 * FOX_W + 3 * SB_W
GATE_PAD = 128
EVEN_IN_PAD = EVEN_QKV + GATE_PAD
N_Q = 16
N_KV = 4
GROUP = N_Q // N_KV
Q_W = N_Q * HEAD_DIM
KV_W = N_KV * HEAD_DIM
ODD_IN = Q_W + 2 * KV_W
WINDOW = 128
ROPE_THETA = 10000.0
D_FF = 2816
PLE_DIM = 256
EPS = 1e-6
NEG_INF = -1e30
SCALE = HEAD_DIM ** -0.5

ADAM_LR = 0.001
ADAM_B1 = 0.9
ADAM_B2 = 0.999
ADAM_EPS = 1e-08
ADAM_WD = 0.01
ADAM_STEP = 10

N_CHIPS = 4
VMEM_LIMIT = 48 * 1024 * 1024
MESH = pl.DeviceIdType.MESH

_PACK = (
    ("ev_w_in", 0, 1024, 770, True),
    ("ev_w_out", 0, 256, 1024, False),
    ("od_w_in", 0, 1024, 384, True),
    ("od_w_out", 0, 256, 1024, False),
    ("ffn_w_gate", 0, 1024, 704, True),
    ("ffn_w_gate", 1, 1024, 704, True),
    ("ffn_w_up", 0, 1024, 704, True),
    ("ffn_w_up", 1, 1024, 704, True),
    ("ffn_w_down", 0, 704, 1024, False),
    ("ffn_w_down", 1, 704, 1024, False),
    ("ple_w_proj", 0, 256, 256, True),
    ("ple_w_proj", 1, 256, 256, True),
    ("ple_w_gate", 0, 256, 1024, False),
    ("ple_w_gate", 1, 256, 1024, False),
)
_ROW_ALIGN = 16


def _pack_rows(r, c):
    n = r * c // D_MODEL
    return -(-n // _ROW_ALIGN) * _ROW_ALIGN


PACK_ROWS_HALF = 3328
PACK_ROWS = 2 * PACK_ROWS_HALF
assert sum(_pack_rows(r, c) for _, _, r, c, _ in _PACK) <= PACK_ROWS


def _pick(n, cands):
    for c in cands:
        if n % c == 0:
            return c
    return n


def _cparams(sem):
    return pltpu.CompilerParams(dimension_semantics=sem, vmem_limit_bytes=VMEM_LIMIT)


_DIMS = {
    "nn": (((1,), (0,)), ((), ())),
    "nt": (((1,), (1,)), ((), ())),
    "tn": (((0,), (0,)), ((), ())),
}


def matmul(a, b, mode="nn", out_dtype=F32, residual=None, name="mm"):
    if mode == "nn":
        (m, k), (k2, n) = a.shape, b.shape
    elif mode == "nt":
        (m, k), (n, k2) = a.shape, b.shape
    else:
        (k, m), (k2, n) = a.shape, b.shape
    assert k == k2, (a.shape, b.shape, mode)
    tm = _pick(m, (512, 256, 128))
    tn = _pick(n, (1024, 512, 640, 384, 256, 128))
    tk = _pick(k, (1024, 1408, 640, 512, 256, 128))
    nk = k // tk
    dims = _DIMS[mode]
    has_res = residual is not None

    def body(*refs):
        if has_res:
            a_ref, b_ref, r_ref, o_ref, acc = refs
        else:
            a_ref, b_ref, o_ref, acc = refs
        kk = pl.program_id(2)
        part = lax.dot_general(a_ref[...].astype(BF16), b_ref[...].astype(BF16), dims,
                               preferred_element_type=F32)

        def finish(r):
            if has_res:
                r = r + r_ref[...]
            o_ref[...] = r.astype(out_dtype)

        if nk == 1:
            finish(part)
            return

        @pl.when(kk == 0)
        def _():
            acc[...] = part

        @pl.when((kk > 0) & (kk < nk - 1))
        def _():
            acc[...] += part

        @pl.when(kk == nk - 1)
        def _():
            finish(acc[...] + part)

    if mode == "nn":
        a_spec = pl.BlockSpec((tm, tk), lambda i, j, kk: (i, kk))
        b_spec = pl.BlockSpec((tk, tn), lambda i, j, kk: (kk, j))
    elif mode == "nt":
        a_spec = pl.BlockSpec((tm, tk), lambda i, j, kk: (i, kk))
        b_spec = pl.BlockSpec((tn, tk), lambda i, j, kk: (j, kk))
    else:
        a_spec = pl.BlockSpec((tk, tm), lambda i, j, kk: (kk, i))
        b_spec = pl.BlockSpec((tk, tn), lambda i, j, kk: (kk, j))
    o_spec = pl.BlockSpec((tm, tn), lambda i, j, kk: (i, j))
    in_specs = [a_spec, b_spec] + ([o_spec] if has_res else [])
    args = (a, b) + ((residual,) if has_res else ())
    return pl.pallas_call(
        body, name=name, grid=(m // tm, n // tn, nk),
        in_specs=in_specs, out_specs=o_spec,
        out_shape=jax.ShapeDtypeStruct((m, n), out_dtype),
        scratch_shapes=[pltpu.VMEM((tm, tn), F32)],
        compiler_params=_cparams(("parallel", "parallel", "arbitrary")),
    )(*args)


def _fold8(v):
    r, w = v.shape
    return v.reshape(r // 8, 8, w).sum(axis=0)


def rowwise(fn, rows, bcasts, outs, accs=(), tr=256, name="rowwise", reverse=False):
    t = rows[0].shape[0]
    tr = _pick(t, (tr, 128, 64, 32, 16, 8))
    nr, nb, no, na = len(rows), len(bcasts), len(outs), len(accs)
    steps = t // tr

    def body(*refs):
        ins = [r[...] for r in refs[:nr + nb]]
        out_refs = refs[nr + nb:nr + nb + no]
        acc_refs = refs[nr + nb + no:]
        o, a = fn(*ins)
        for r, v in zip(out_refs, o):
            r[...] = v.astype(r.dtype)
        if na:
            @pl.when(pl.program_id(0) == 0)
            def _():
                for r in acc_refs:
                    r[...] = jnp.zeros_like(r)

            for r, v in zip(acc_refs, a):
                r[...] += v

    if reverse:
        ridx = lambda i: (steps - 1 - i, 0)
    else:
        ridx = lambda i: (i, 0)
    in_specs = [pl.BlockSpec((tr, x.shape[1]), ridx) for x in rows]
    in_specs += [pl.BlockSpec(x.shape, lambda i: (0, 0)) for x in bcasts]
    out_specs = [pl.BlockSpec((tr, w), ridx) for w, _ in outs]
    out_specs += [pl.BlockSpec((8, w), lambda i: (0, 0)) for w in accs]
    out_shape = [jax.ShapeDtypeStruct((t, w), dt) for w, dt in outs]
    out_shape += [jax.ShapeDtypeStruct((8, w), F32) for w in accs]
    res = pl.pallas_call(
        body, name=name, grid=(steps,), in_specs=in_specs, out_specs=out_specs, out_shape=out_shape,
        compiler_params=_cparams(("arbitrary",)),
    )(*rows, *bcasts)
    return res


def _rstd(x):
    return lax.rsqrt(jnp.mean(x * x, axis=-1, keepdims=True) + EPS)


def rmsnorm_fwd(h, g, name):
    def fn(x, gg):
        return ((x * _rstd(x)) * gg,), ()

    return rowwise(fn, [h], [g.reshape(1, -1)], [(D_MODEL, BF16)], name=name)[0]


def _rms_bwd_math(x, gg, dy):
    r = _rstd(x)
    xh = x * r
    u = dy * gg
    dx = r * (u - xh * jnp.mean(u * xh, axis=-1, keepdims=True))
    return dx, dy * xh


def rmsnorm_bwd(h, g, dn, dres, name):
    def fn(x, dy, dr, gg):
        dx, dgp = _rms_bwd_math(x, gg, dy)
        return (dr + dx,), (_fold8(dgp),)

    return rowwise(fn, [h, dn, dres], [g.reshape(1, -1)], [(D_MODEL, F32)], [D_MODEL], name=name)


def loss_head(h, g, tgt, name):
    def fn(x, tg, gg):
        y = (x * _rstd(x)) * gg
        e = y - tg
        dy = e * (1.0 / D_MODEL)
        dx, dgp = _rms_bwd_math(x, gg, dy)
        return (dx,), (_fold8(dgp), _fold8(e * e))

    return rowwise(fn, [h, tgt], [g.reshape(1, -1)], [(D_MODEL, F32)], [D_MODEL, D_MODEL], name=name)


def _sigmoid(x):
    return 1.0 / (1.0 + jnp.exp(-x))


def swiglu_fwd(gu, name):
    def fn(x):
        gg, uu = x[:, :D_FF], x[:, D_FF:]
        return ((gg * _sigmoid(gg)) * uu,), ()

    return rowwise(fn, [gu], [], [(D_FF, BF16)], name=name)[0]


def swiglu_bwd(gu, da, name):
    def fn(x, d):
        gg, uu = x[:, :D_FF], x[:, D_FF:]
        s = _sigmoid(gg)
        silu = gg * s
        dgg = d * uu * (s + silu * (1.0 - s))
        duu = d * silu
        return (jnp.concatenate([dgg, duu], axis=1),), ()

    return rowwise(fn, [gu, da], [], [(2 * D_FF, BF16)], name=name)[0]


def ple_fwd(h, pre, pp, name):
    def fn(x, a, b):
        return (x + _sigmoid(a) * b,), ()

    return rowwise(fn, [h, pre, pp], [], [(D_MODEL, F32)], name=name)[0]


def ple_bwd(dh, pre, pp, name):
    def fn(d, a, b):
        s = _sigmoid(a)
        return (d * b * s * (1.0 - s), d * s), ()

    return rowwise(fn, [dh, pre, pp], [], [(D_MODEL, BF16), (D_MODEL, BF16)], name=name)


def _rot_half_partner(x, first_half):
    w = x.shape[1]
    return jnp.where(first_half, pltpu.roll(x, w - HEAD_DIM // 2, 1), pltpu.roll(x, HEAD_DIM // 2, 1))


def rope_apply(xx, cosw, sinw, backward, name):
    width = xx.shape[1]
    reps = width // 128

    def fn(x, c, s):
        cw = jnp.tile(c, (1, reps))
        sw = jnp.tile(s, (1, reps))
        lane = lax.broadcasted_iota(jnp.int32, x.shape, 1)
        first = (lane % HEAD_DIM) < (HEAD_DIM // 2)
        if backward:
            return (x * cw + _rot_half_partner(x * sw, first),), ()
        return (x * cw + _rot_half_partner(x, first) * sw,), ()

    return rowwise(fn, [xx, cosw, sinw], [], [(width, F32)], name=name)[0]


def _log_sigmoid(x):
    return jnp.minimum(x, 0.0) - jnp.log(1.0 + jnp.exp(-jnp.abs(x)))


CUM_BLOCK = 256


def forget_cumsum(flog, bias, name):
    t = flog.shape[0]
    tb = _pick(t, (CUM_BLOCK,))

    def body(x_ref, b_ref, o_ref, carry):
        @pl.when(pl.program_id(0) == 0)
        def _():
            carry[...] = jnp.zeros_like(carry)

        lf = _log_sigmoid(x_ref[...] + b_ref[...])
        r = lax.broadcasted_iota(jnp.int32, (tb, tb), 0)
        c = lax.broadcasted_iota(jnp.int32, (tb, tb), 1)
        tri = (c <= r).astype(F32)
        cum = jnp.dot(tri, lf, preferred_element_type=F32, precision=lax.Precision.HIGHEST) + carry[...]
        o_ref[...] = cum
        carry[...] = cum[tb - 1:tb, :]

    return pl.pallas_call(
        body, name=name, grid=(t // tb,),
        in_specs=[pl.BlockSpec((tb, GATE_PAD), lambda i: (i, 0)), pl.BlockSpec((1, GATE_PAD), lambda i: (0, 0))],
        out_specs=pl.BlockSpec((tb, GATE_PAD), lambda i: (i, 0)),
        out_shape=jax.ShapeDtypeStruct((t, GATE_PAD), F32),
        scratch_shapes=[pltpu.VMEM((1, GATE_PAD), F32)],
        compiler_params=_cparams(("arbitrary",)),
    )(flog, bias)


def forget_cumsum_bwd(d_cum, flog, bias, name):
    t = flog.shape[0]
    tb = _pick(t, (CUM_BLOCK,))
    nb = t // tb

    def body(d_ref, x_ref, b_ref, o_ref, db_ref, carry):
        @pl.when(pl.program_id(0) == 0)
        def _():
            carry[...] = jnp.zeros_like(carry)
            db_ref[...] = jnp.zeros_like(db_ref)

        r = lax.broadcasted_iota(jnp.int32, (tb, tb), 0)
        c = lax.broadcasted_iota(jnp.int32, (tb, tb), 1)
        tri = (c >= r).astype(F32)
        dlf = jnp.dot(tri, d_ref[...], preferred_element_type=F32, precision=lax.Precision.HIGHEST) + carry[...]
        carry[...] = dlf[0:1, :]
        dx = dlf * (1.0 - _sigmoid(x_ref[...] + b_ref[...]))
        o_ref[...] = dx
        db_ref[...] += _fold8(dx)

    rev = lambda i: (nb - 1 - i, 0)
    return pl.pallas_call(
        body, name=name, grid=(nb,),
        in_specs=[pl.BlockSpec((tb, GATE_PAD), rev), pl.BlockSpec((tb, GATE_PAD), rev),
                  pl.BlockSpec((1, GATE_PAD), lambda i: (0, 0))],
        out_specs=[pl.BlockSpec((tb, GATE_PAD), rev), pl.BlockSpec((8, GATE_PAD), lambda i: (0, 0))],
        out_shape=[jax.ShapeDtypeStruct((t, GATE_PAD), F32), jax.ShapeDtypeStruct((8, GATE_PAD), F32)],
        scratch_shapes=[pltpu.VMEM((1, GATE_PAD), F32)],
        compiler_params=_cparams(("arbitrary",)),
    )(d_cum, flog, bias)


TQ = 512
TK = 128
AUG = 128
N_BIAS = 3
SB_CUTOFF = 110.0
FOX_CUTOFF = 112.0


def _dot(a, b):
    return jnp.dot(a, b, preferred_element_type=F32)


def _rel(shape, d):
    return lax.broadcasted_iota(jnp.int32, shape, 0) - lax.broadcasted_iota(jnp.int32, shape, 1) + d


def _split_bf16(x):
    hi = x.astype(BF16)
    return hi, (x - hi.astype(F32)).astype(BF16)


def _tri_dot(tri, x):
    hi, lo = _split_bf16(x)
    return _dot(tri, hi) + _dot(tri, lo)


def _q_cols(width, tq):
    return pl.BlockSpec((None, width, tq), lambda h, i: (h, 0, i))


def _q_rows(width, tq):
    return pl.BlockSpec((None, tq, width), lambda h, i: (h, i, 0))


def _kv_rows(t, width):
    return pl.BlockSpec((None, t, width), lambda h, i: (h, 0, 0))


def _kv_tiles(nk):
    return pl.BlockSpec((None, nk, HEAD_DIM, TK), lambda h, i: (h, 0, 0, 0))


def _blocks(t):
    tq = TQ if t % TQ == 0 else TK
    return tq, tq // TK, t // TK


def fox_fwd(qt_aug, k_aug, vt, f_end, k_norm):
    nh, _, t = qt_aug.shape
    tq, ratio, nk = _blocks(t)
    lanes = tq // 128

    def body(q_ref, k_ref, v_ref, fe_ref, kn_ref, o_ref, lse_ref, first_ref):
        i = pl.program_id(1)
        qv = q_ref[...]
        qf = qv[:HEAD_DIM].astype(F32)
        reach = jnp.sqrt(jnp.sum(qf * qf, axis=0, keepdims=True)) * jnp.tile(kn_ref[...], (1, lanes))

        def step(first, carry, masked):
            m, l, acc = carry
            scores = []
            for u in range(ratio):
                off = pl.multiple_of((first + u) * TK, TK)
                s = _dot(k_ref[pl.ds(off, TK), :], qv)
                if masked:
                    s = jnp.where(_rel(s.shape, (first + u) * TK - i * tq) <= 0, s, NEG_INF)
                scores.append(s)
            m_new = m
            for s in scores:
                m_new = jnp.maximum(m_new, jnp.max(s, axis=0, keepdims=True))
            alpha = jnp.exp(m - m_new)
            l = alpha * l
            acc = alpha * acc
            for u, s in enumerate(scores):
                p = jnp.exp(s - m_new)
                l = l + jnp.sum(p, axis=0, keepdims=True)
                acc = acc + _dot(v_ref[first + u], p.astype(BF16))
            return m_new, l, acc

        init = (jnp.full((1, tq), NEG_INF, F32), jnp.zeros((1, tq), F32), jnp.zeros((HEAD_DIM, tq), F32))
        m, l, acc = step(ratio * i, init, True)

        def more(c):
            bound = reach + jnp.tile(fe_ref[jnp.maximum(i - 1 - c[0], 0)], (1, lanes))
            return (c[0] < i) & (jnp.max(bound - c[1]) > -FOX_CUTOFF)

        def sweep(c):
            return (c[0] + 1,) + step(ratio * (i - 1 - c[0]), c[1:], False)

        done, m, l, acc = lax.while_loop(more, sweep, (jnp.int32(0), m, l, acc))
        o_ref[...] = acc / l
        lse_ref[...] = m + jnp.log(l)
        first_ref[...] = jnp.full((1, 128), (i - done).astype(F32), F32)

    return pl.pallas_call(
        body, name="fox_fwd", grid=(nh, t // tq),
        in_specs=[_q_cols(AUG, tq), _kv_rows(t, AUG), _kv_tiles(nk),
                  pl.BlockSpec((None, t // tq, 1, 128), lambda h, i: (h, 0, 0, 0)),
                  pl.BlockSpec((None, 1, 128), lambda h, i: (h, 0, 0))],
        out_specs=[_q_cols(HEAD_DIM, tq), _q_cols(1, tq), _q_cols(1, 128)],
        out_shape=[jax.ShapeDtypeStruct((nh, HEAD_DIM, t), F32), jax.ShapeDtypeStruct((nh, 1, t), F32),
                   jax.ShapeDtypeStruct((nh, 1, 128 * (t // tq)), F32)],
        compiler_params=_cparams(("parallel", "arbitrary")),
    )(qt_aug, k_aug, vt, f_end, k_norm)


def fox_bwd(qt_aug, q_aug, k_aug, kt, v, ot, do, dot_, lse, first):
    nh, _, t = qt_aug.shape
    tq, ratio, nk = _blocks(t)

    def body(qt_ref, q_ref, k_ref, kt_ref, v_ref, ot_ref, do_ref, dot_ref, lse_ref, first_ref,
             dqt_ref, dk_ref, dv_ref, rs_ref):
        i = pl.program_id(1)
        start = jnp.max(first_ref[...]).astype(jnp.int32)

        @pl.when(i == 0)
        def _():
            dk_ref[...] = jnp.zeros_like(dk_ref)
            dv_ref[...] = jnp.zeros_like(dv_ref)

        qtv = qt_ref[...]
        qv = q_ref[...]
        dob = do_ref[...]
        dotb = dot_ref[...]
        delta = jnp.sum(ot_ref[...] * dotb.astype(F32), axis=0, keepdims=True)
        lse = lse_ref[...]

        def tile(j, carry, masked):
            dqt, rs = carry
            off = pl.multiple_of(j * TK, TK)
            s = _dot(k_ref[pl.ds(off, TK), :], qtv)
            p = jnp.exp(s - lse)
            if masked:
                p = jnp.where(_rel(s.shape, j * TK - i * tq) <= 0, p, 0.0)
            dp = _dot(v_ref[pl.ds(off, TK), :], dotb)
            dsb = (p * (dp - delta)).astype(BF16)
            dk_ref[pl.ds(off, TK), :] += _dot(dsb, qv)
            dv_ref[pl.ds(off, TK), :] += _dot(p.astype(BF16), dob)
            return dqt + _dot(kt_ref[j], dsb), rs + jnp.sum(dsb.astype(F32), axis=0, keepdims=True)

        def step(first, carry, masked):
            for u in range(ratio):
                carry = tile(first + u, carry, masked)
            return carry

        carry = (jnp.zeros((HEAD_DIM, tq), F32), jnp.zeros((1, tq), F32))
        carry = lax.fori_loop(start, i, lambda jj, c: step(ratio * jj, c, False), carry)
        dqt, rs = step(ratio * i, carry, True)
        dqt_ref[...] = dqt * SCALE
        rs_ref[...] = rs

    return pl.pallas_call(
        body, name="fox_bwd", grid=(nh, t // tq),
        in_specs=[_q_cols(AUG, tq), _q_rows(AUG, tq), _kv_rows(t, AUG), _kv_tiles(nk), _kv_rows(t, HEAD_DIM),
                  _q_cols(HEAD_DIM, tq), _q_rows(HEAD_DIM, tq), _q_cols(HEAD_DIM, tq), _q_cols(1, tq),
                  _q_cols(1, 128)],
        out_specs=[_q_cols(HEAD_DIM, tq), _kv_rows(t, AUG), _kv_rows(t, HEAD_DIM), _q_cols(1, tq)],
        out_shape=[jax.ShapeDtypeStruct((nh, HEAD_DIM, t), F32), jax.ShapeDtypeStruct((nh, t, AUG), F32),
                   jax.ShapeDtypeStruct((nh, t, HEAD_DIM), F32), jax.ShapeDtypeStruct((nh, 1, t), F32)],
        compiler_params=_cparams(("arbitrary", "arbitrary")),
    )(qt_aug, q_aug, k_aug, kt, v, ot, do, dot_, lse, first)


def _sb_logits(kb, qv, ok):
    z = _dot(kb, qv)
    e = jnp.exp(-jnp.abs(z))
    ll = -(jnp.maximum(z, 0.0) + jnp.log(1.0 + e))
    if ok is not None:
        ll = jnp.where(ok, ll, 0.0)
    return z, e, ll


def _tri(cmp):
    r = lax.broadcasted_iota(jnp.int32, (TK, TK), 0)
    c = lax.broadcasted_iota(jnp.int32, (TK, TK), 1)
    return cmp(r, c).astype(BF16)


def sb_fwd(qt, k, vt):
    nh, _, t = qt.shape
    tq, ratio, nk = _blocks(t)

    def body(q_ref, k_ref, v_ref, o_ref, tot_ref, first_ref):
        i = pl.program_id(1)
        qv = q_ref[...]
        tri_after = _tri(lambda r, c: c > r)

        def tile(j, carry, masked):
            c_l, acc = carry
            off = pl.multiple_of(j * TK, TK)
            ok = _rel((TK, tq), j * TK - i * tq) < 0 if masked else None
            z, _, ll = _sb_logits(k_ref[pl.ds(off, TK), :], qv, ok)
            a = jnp.exp(z + ll + _tri_dot(tri_after, ll) + c_l)
            if masked:
                a = jnp.where(ok, a, 0.0)
            return c_l + jnp.sum(ll, axis=0, keepdims=True), acc + _dot(v_ref[j], a.astype(BF16))

        def step(first, carry, masked):
            for u in reversed(range(ratio)):
                carry = tile(first + u, carry, masked)
            return carry

        c_l, acc = step(ratio * i, (jnp.zeros((1, tq), F32), jnp.zeros((HEAD_DIM, tq), F32)), True)

        def more(c):
            return (c[0] < i) & (jnp.max(c[1]) > -SB_CUTOFF)

        def sweep(c):
            c_l, acc = step(ratio * (i - 1 - c[0]), (c[1], c[2]), False)
            return c[0] + 1, c_l, acc

        done, c_l, acc = lax.while_loop(more, sweep, (jnp.int32(0), c_l, acc))
        o_ref[...] = acc
        tot_ref[...] = c_l
        first_ref[...] = jnp.full((1, 128), (i - done).astype(F32), F32)

    return pl.pallas_call(
        body, name="sb_fwd", grid=(nh, t // tq),
        in_specs=[_q_cols(HEAD_DIM, tq), _kv_rows(t, HEAD_DIM), _kv_tiles(nk)],
        out_specs=[_q_cols(HEAD_DIM, tq), _q_cols(1, tq), _q_cols(1, 128)],
        out_shape=[jax.ShapeDtypeStruct((nh, HEAD_DIM, t), F32), jax.ShapeDtypeStruct((nh, 1, t), F32),
                   jax.ShapeDtypeStruct((nh, 1, 128 * (t // tq)), F32)],
        compiler_params=_cparams(("parallel", "arbitrary")),
    )(qt, k, vt)


def sb_bwd(qt, q, k, kt, v, ltot, first, do, dot_):
    nh, _, t = qt.shape
    tq, ratio, nk = _blocks(t)

    def body(qt_ref, q_ref, k_ref, kt_ref, v_ref, tot_ref, first_ref, do_ref, dot_ref, dqt_ref, dk_ref, dv_ref):
        i = pl.program_id(1)
        start = jnp.max(first_ref[...]).astype(jnp.int32)

        @pl.when(i == 0)
        def _():
            dk_ref[...] = jnp.zeros_like(dk_ref)
            dv_ref[...] = jnp.zeros_like(dv_ref)

        qtv = qt_ref[...]
        qv = q_ref[...]
        dob = do_ref[...]
        dotb = dot_ref[...]
        tri_upto = _tri(lambda r, c: c <= r)
        tri_before = _tri(lambda r, c: c < r)

        def tile(j, carry, masked):
            rest, c_w, dqt = carry
            off = pl.multiple_of(j * TK, TK)
            ok = _rel((TK, tq), j * TK - i * tq) < 0 if masked else None
            z, e, ll = _sb_logits(k_ref[pl.ds(off, TK), :], qtv, ok)
            a = jnp.exp(z + ll + (rest - _tri_dot(tri_upto, ll)))
            if masked:
                a = jnp.where(ok, a, 0.0)
            w = a * _dot(v_ref[pl.ds(off, TK), :], dotb)
            before = _tri_dot(tri_before, w) + c_w
            r = 1.0 / (1.0 + e)
            sig = jnp.where(z >= 0.0, r, e * r)
            dz = w * (1.0 - sig) - before * sig
            if masked:
                dz = jnp.where(ok, dz, 0.0)
            dzb = dz.astype(BF16)
            dk_ref[pl.ds(off, TK), :] += _dot(dzb, qv)
            dv_ref[pl.ds(off, TK), :] += _dot(a.astype(BF16), dob)
            return (rest - jnp.sum(ll, axis=0, keepdims=True), c_w + jnp.sum(w, axis=0, keepdims=True),
                    dqt + _dot(kt_ref[j], dzb))

        def step(first, carry, masked):
            for u in range(ratio):
                carry = tile(first + u, carry, masked)
            return carry

        carry = (tot_ref[...], jnp.zeros((1, tq), F32), jnp.zeros((HEAD_DIM, tq), F32))
        carry = lax.fori_loop(start, i, lambda jj, c: step(ratio * jj, c, False), carry)
        dqt_ref[...] = step(ratio * i, carry, True)[2] * SCALE

    hd = HEAD_DIM
    return pl.pallas_call(
        body, name="sb_bwd", grid=(nh, t // tq),
        in_specs=[_q_cols(hd, tq), _q_rows(hd, tq), _kv_rows(t, hd), _kv_tiles(nk), _kv_rows(t, hd),
                  _q_cols(1, tq), _q_cols(1, 128), _q_rows(hd, tq), _q_cols(hd, tq)],
        out_specs=[_q_cols(hd, tq), _kv_rows(t, hd), _kv_rows(t, hd)],
        out_shape=[jax.ShapeDtypeStruct((nh, hd, t), F32), jax.ShapeDtypeStruct((nh, t, hd), F32),
                   jax.ShapeDtypeStruct((nh, t, hd), F32)],
        compiler_params=_cparams(("arbitrary", "arbitrary")),
    )(qt, q, k, kt, v, ltot, first, do, dot_)


def _swa_q_cols(width):
    return pl.BlockSpec((GROUP, width, WINDOW), lambda g, i: (g, 0, i))


def _swa_q_rows():
    return pl.BlockSpec((GROUP, WINDOW, HEAD_DIM), lambda g, i: (g, i, 0))


def _swa_kv_rows(t):
    return pl.BlockSpec((None, t, HEAD_DIM), lambda g, i: (g, 0, 0))


def _swa_kv_tiles(nk):
    return pl.BlockSpec((None, nk, HEAD_DIM, WINDOW), lambda g, i: (g, 0, 0, 0))


def _swa_sink_spec():
    return pl.BlockSpec((GROUP, 1, 128), lambda g, i: (g, 0, 0))


def _swa_window(i):
    jb = jnp.maximum(i - 1, 0)
    start = pl.multiple_of(jb * WINDOW, WINDOW)
    rel = _rel((2 * WINDOW, WINDOW), start - i * WINDOW)
    return jb, start, (rel <= 0) & (rel > -WINDOW)


def swa_fwd(qt, k, vt, sinks):
    nh, _, t = qt.shape
    nk = t // WINDOW

    def body(q_ref, k_ref, v_ref, s_ref, o_ref, lse_ref):
        i = pl.program_id(1)
        jb, start, valid = _swa_window(i)
        kwin = k_ref[pl.ds(start, 2 * WINDOW), :]
        for g in range(GROUP):
            s = jnp.where(valid, _dot(kwin, q_ref[g]), NEG_INF)
            sink = s_ref[g]
            m = jnp.maximum(jnp.max(s, axis=0, keepdims=True), sink)
            p = jnp.where(valid, jnp.exp(s - m), 0.0)
            l = jnp.sum(p, axis=0, keepdims=True) + jnp.exp(sink - m)
            pb = p.astype(BF16)
            o_ref[g] = (_dot(v_ref[jb], pb[:WINDOW]) + _dot(v_ref[jb + 1], pb[WINDOW:])) / l
            lse_ref[g] = m + jnp.log(l)

    return pl.pallas_call(
        body, name="swa_fwd", grid=(N_KV, nk),
        in_specs=[_swa_q_cols(HEAD_DIM), _swa_kv_rows(t), _swa_kv_tiles(nk), _swa_sink_spec()],
        out_specs=[_swa_q_cols(HEAD_DIM), _swa_q_cols(1)],
        out_shape=[jax.ShapeDtypeStruct((nh, HEAD_DIM, t), F32), jax.ShapeDtypeStruct((nh, 1, t), F32)],
        compiler_params=_cparams(("parallel", "arbitrary")),
    )(qt, k, vt, sinks)


def swa_bwd(qt, q, k, kt, v, sinks, ot, do, dot_, lse):
    nh, _, t = qt.shape
    nk = t // WINDOW

    def body(qt_ref, q_ref, k_ref, kt_ref, v_ref, s_ref, ot_ref, do_ref, dot_ref, lse_ref,
             dqt_ref, dk_ref, dv_ref, dsink_ref):
        i = pl.program_id(1)

        @pl.when(i == 0)
        def _():
            dk_ref[...] = jnp.zeros_like(dk_ref)
            dv_ref[...] = jnp.zeros_like(dv_ref)
            dsink_ref[...] = jnp.zeros_like(dsink_ref)

        jb, start, valid = _swa_window(i)
        kwin = k_ref[pl.ds(start, 2 * WINDOW), :]
        vwin = v_ref[pl.ds(start, 2 * WINDOW), :]
        dk_acc = jnp.zeros((2 * WINDOW, HEAD_DIM), F32)
        dv_acc = jnp.zeros((2 * WINDOW, HEAD_DIM), F32)
        for g in range(GROUP):
            lse_g = lse_ref[g]
            dotb = dot_ref[g]
            delta = jnp.sum(ot_ref[g] * dotb.astype(F32), axis=0, keepdims=True)
            p = jnp.where(valid, jnp.exp(_dot(kwin, qt_ref[g]) - lse_g), 0.0)
            dsb = (p * (_dot(vwin, dotb) - delta)).astype(BF16)
            dqt_ref[g] = (_dot(kt_ref[jb], dsb[:WINDOW]) + _dot(kt_ref[jb + 1], dsb[WINDOW:])) * SCALE
            dk_acc = dk_acc + _dot(dsb, q_ref[g])
            dv_acc = dv_acc + _dot(p.astype(BF16), do_ref[g])
            dsink_ref[g] += -jnp.exp(s_ref[g] - lse_g) * delta
        dk_ref[pl.ds(start, 2 * WINDOW), :] += dk_acc
        dv_ref[pl.ds(start, 2 * WINDOW), :] += dv_acc

    hd = HEAD_DIM
    return pl.pallas_call(
        body, name="swa_bwd", grid=(N_KV, nk),
        in_specs=[_swa_q_cols(hd), _swa_q_rows(), _swa_kv_rows(t), _swa_kv_tiles(nk), _swa_kv_rows(t),
                  _swa_sink_spec(), _swa_q_cols(hd), _swa_q_rows(), _swa_q_cols(hd), _swa_q_cols(1)],
        out_specs=[_swa_q_cols(hd), _swa_kv_rows(t), _swa_kv_rows(t), _swa_sink_spec()],
        out_shape=[jax.ShapeDtypeStruct((nh, hd, t), F32), jax.ShapeDtypeStruct((N_KV, t, hd), F32),
                   jax.ShapeDtypeStruct((N_KV, t, hd), F32), jax.ShapeDtypeStruct((nh, 1, 128), F32)],
        compiler_params=_cparams(("arbitrary", "arbitrary")),
    )(qt, q, k, kt, v, sinks, ot, do, dot_, lse)


def head_rows(a, nh):
    t = a.shape[0]
    return a.reshape(t, nh, HEAD_DIM).transpose(1, 0, 2)


def head_cols(a, nh):
    t = a.shape[0]
    return a.reshape(t, nh, HEAD_DIM).transpose(1, 2, 0)


def head_tiles(a, nh):
    t = a.shape[0]
    return a.reshape(t // TK, TK, nh, HEAD_DIM).transpose(2, 0, 3, 1)


def rows_to_flat(a):
    nh, t, _ = a.shape
    return a.transpose(1, 0, 2).reshape(t, nh * HEAD_DIM)


def cols_to_flat(a):
    nh, _, t = a.shape
    return a.transpose(2, 0, 1).reshape(t, nh * HEAD_DIM)


def fox_operands(qf, kf, cum_heads):
    t = qf.shape[0]
    nh = cum_heads.shape[0]
    tq = _blocks(t)[0]
    qs = (qf * SCALE).astype(BF16)
    ones_t = jnp.ones((nh, N_BIAS, t), BF16)
    qt_aug = jnp.concatenate([head_cols(qs, nh), ones_t, jnp.zeros((nh, AUG - HEAD_DIM - N_BIAS, t), BF16)], axis=1)
    own_lane = jnp.broadcast_to(jnp.eye(nh, dtype=BF16)[:, None, :], (nh, t, nh))
    q_aug = jnp.concatenate([head_rows(qs, nh), own_lane, jnp.zeros((nh, t, AUG - HEAD_DIM - nh), BF16)], axis=2)
    terms, rest = [], -cum_heads
    for _ in range(N_BIAS):
        term = lax.reduce_precision(rest, exponent_bits=8, mantissa_bits=7)
        terms.append(term)
        rest = rest - term
    k_rows = head_rows(kf.astype(BF16), nh)
    k_aug = jnp.concatenate([k_rows, jnp.stack(terms, axis=-1).astype(BF16),
                             jnp.zeros((nh, t, AUG - HEAD_DIM - N_BIAS), BF16)], axis=2)
    f_end = jnp.broadcast_to((-cum_heads)[:, tq - 1::tq, None, None], (nh, t // tq, 1, 128))
    k_sq = jnp.sum(jnp.square(k_rows.astype(F32)), axis=2)
    k_norm = jnp.broadcast_to(jnp.sqrt(jnp.max(k_sq, axis=1))[:, None, None], (nh, 1, 128))
    return qt_aug, q_aug, k_aug, f_end, k_norm


def adamw(w, g, m, v, name):
    cols = w.shape[1]

    def fn(ww, gg, mm, vv):
        mn = ADAM_B1 * mm + (1.0 - ADAM_B1) * gg
        vn = ADAM_B2 * vv + (1.0 - ADAM_B2) * (gg * gg)
        m_hat = mn / (1.0 - ADAM_B1 ** ADAM_STEP)
        v_hat = vn / (1.0 - ADAM_B2 ** ADAM_STEP)
        delta = -ADAM_LR * (m_hat / (jnp.sqrt(v_hat) + ADAM_EPS) + ADAM_WD * ww)
        return (delta, mn, vn), ()

    return rowwise(fn, [w, g, m, v], [], [(cols, F32)] * 3, name=name)


ANY = pl.BlockSpec(memory_space=pl.ANY)


def _my_place():
    return lax.axis_index("x"), lax.axis_index("y"), lax.axis_index("c")


def _flip(coord, bit):
    return 1 - coord if bit else coord


def allgather_chips(w):
    r, c = w.shape

    def body(w_ref, out_ref, send_sems, recv_sems, local_sem):
        x, y, cc = _my_place()
        me = 2 * x + y
        local = pltpu.make_async_copy(w_ref, out_ref.at[me], local_sem)
        local.start()
        copies = []
        for kbits in (1, 2, 3):
            peer = (_flip(x, kbits >> 1), _flip(y, kbits & 1), cc)
            cp = pltpu.make_async_remote_copy(
                src_ref=w_ref, dst_ref=out_ref.at[me], send_sem=send_sems.at[kbits - 1],
                recv_sem=recv_sems.at[kbits - 1], device_id=peer, device_id_type=MESH)
            cp.start()
            copies.append(cp)
        for cp in copies:
            cp.wait()
        local.wait()

    return pl.pallas_call(
        body, name="allgather_chips", in_specs=[ANY], out_specs=ANY,
        out_shape=jax.ShapeDtypeStruct((N_CHIPS, r, c), w.dtype),
        scratch_shapes=[pltpu.SemaphoreType.DMA((3,)), pltpu.SemaphoreType.DMA((3,)), pltpu.SemaphoreType.DMA],
    )(w)


def pair_send_other_half(g):
    n, _, rh, c = g.shape

    def body(g_ref, out_ref, send_sem, recv_sem):
        x, y, cc = _my_place()
        cp = pltpu.make_async_remote_copy(
            src_ref=g_ref.at[:, 1 - cc], dst_ref=out_ref, send_sem=send_sem, recv_sem=recv_sem,
            device_id=(x, y, 1 - cc), device_id_type=MESH)
        cp.start()
        cp.wait()

    return pl.pallas_call(
        body, name="pair_send_other_half", in_specs=[ANY], out_specs=ANY,
        out_shape=jax.ShapeDtypeStruct((n, rh, c), g.dtype),
        scratch_shapes=[pltpu.SemaphoreType.DMA, pltpu.SemaphoreType.DMA],
    )(g)


def exchange_chips(s):
    n, rh, c = s.shape

    def body(s_ref, out_ref, send_sems, recv_sems, local_sem):
        x, y, cc = _my_place()
        me = 2 * x + y
        local = pltpu.make_async_copy(s_ref.at[me], out_ref.at[me], local_sem)
        local.start()
        copies = []
        for kbits in (1, 2, 3):
            px, py = _flip(x, kbits >> 1), _flip(y, kbits & 1)
            cp = pltpu.make_async_remote_copy(
                src_ref=s_ref.at[2 * px + py], dst_ref=out_ref.at[me], send_sem=send_sems.at[kbits - 1],
                recv_sem=recv_sems.at[kbits - 1], device_id=(px, py, cc), device_id_type=MESH)
            cp.start()
            copies.append(cp)
        for cp in copies:
            cp.wait()
        local.wait()

    return pl.pallas_call(
        body, name="exchange_chips", in_specs=[ANY], out_specs=ANY,
        out_shape=jax.ShapeDtypeStruct((n, rh, c), s.dtype),
        scratch_shapes=[pltpu.SemaphoreType.DMA((3,)), pltpu.SemaphoreType.DMA((3,)), pltpu.SemaphoreType.DMA],
    )(s)


def pair_swap(tt):
    def body(t_ref, out_ref, send_sem, recv_sem):
        x, y, cc = _my_place()
        cp = pltpu.make_async_remote_copy(
            src_ref=t_ref, dst_ref=out_ref, send_sem=send_sem, recv_sem=recv_sem,
            device_id=(x, y, 1 - cc), device_id_type=MESH)
        cp.start()
        cp.wait()

    return pl.pallas_call(
        body, name="pair_swap", in_specs=[ANY], out_specs=ANY,
        out_shape=jax.ShapeDtypeStruct(tt.shape, tt.dtype),
        scratch_shapes=[pltpu.SemaphoreType.DMA, pltpu.SemaphoreType.DMA],
    )(tt)


SMALL_ROWS = 16


def allreduce_small(v):
    r, c = v.shape
    vm = pl.BlockSpec(memory_space=pltpu.VMEM)

    def body(v_ref, out_ref, slots, send_sems, recv_sems):
        x, y, cc = _my_place()
        me = 4 * x + 2 * y + cc
        slots[me] = v_ref[...]
        copies = []
        for kbits in range(1, 8):
            peer = (_flip(x, kbits >> 2), _flip(y, (kbits >> 1) & 1), _flip(cc, kbits & 1))
            cp = pltpu.make_async_remote_copy(
                src_ref=v_ref, dst_ref=slots.at[me], send_sem=send_sems.at[kbits - 1],
                recv_sem=recv_sems.at[kbits - 1], device_id=peer, device_id_type=MESH)
            cp.start()
            copies.append(cp)
        for cp in copies:
            cp.wait()
        total = slots[0]
        for dev in range(1, 8):
            total = total + slots[dev]
        out_ref[...] = total

    return pl.pallas_call(
        body, name="allreduce_small", in_specs=[vm], out_specs=vm,
        out_shape=jax.ShapeDtypeStruct((r, c), F32),
        scratch_shapes=[pltpu.VMEM((8, r, c), F32), pltpu.SemaphoreType.DMA((7,)), pltpu.SemaphoreType.DMA((7,))],
    )(v)


def add_pair(mine, theirs, name):
    return rowwise(lambda a, b: ((a + b,), ()), [mine, theirs], [], [(mine.shape[1], BF16)], name=name)[0]


def sum_chips(r4, name):
    _, rh, c = r4.shape
    tr = _pick(rh, (256, 128, 64, 32, 16))

    def body(r_ref, o_ref):
        total = r_ref[0].astype(F32)
        for j in range(1, N_CHIPS):
            total = total + r_ref[j].astype(F32)
        o_ref[...] = total

    return pl.pallas_call(
        body, name=name, grid=(rh // tr,),
        in_specs=[pl.BlockSpec((N_CHIPS, tr, c), lambda i: (0, i, 0))],
        out_specs=pl.BlockSpec((tr, c), lambda i: (i, 0)),
        out_shape=jax.ShapeDtypeStruct((rh, c), F32),
        compiler_params=_cparams(("parallel",)),
    )(r4)


def _pad_rows(a, rows):
    return jnp.pad(a, ((0, rows - a.shape[0]), (0, 0))) if rows != a.shape[0] else a


def pack_shards(local):
    parts = []
    for name, layer, r, c, _ in _PACK:
        flat = local[name][layer].astype(BF16).reshape(r * c // D_MODEL, D_MODEL)
        parts.append(_pad_rows(flat, _pack_rows(r, c)))
    used = sum(p.shape[0] for p in parts)
    parts.append(jnp.zeros((PACK_ROWS - used, D_MODEL), BF16))
    return jnp.concatenate(parts, axis=0)


def unpack_full(gathered):
    full, off = {}, 0
    for name, layer, r, c, by_cols in _PACK:
        n = r * c // D_MODEL
        blk = gathered[:, off:off + n, :].reshape(N_CHIPS, r, c)
        if by_cols:
            full[(name, layer)] = blk.transpose(1, 0, 2).reshape(r, N_CHIPS * c)
        else:
            full[(name, layer)] = blk.reshape(N_CHIPS * r, c)
        off += _pack_rows(r, c)
    return full


def pack_grads(grads):
    parts = []
    for name, layer, r, c, by_cols in _PACK:
        g = grads[(name, layer)]
        if by_cols:
            blk = g.reshape(r, N_CHIPS, c).transpose(1, 0, 2)
        else:
            blk = g.reshape(N_CHIPS, r, c)
        flat = blk.reshape(N_CHIPS, r * c // D_MODEL, D_MODEL)
        rows = _pack_rows(r, c)
        if rows != flat.shape[1]:
            flat = jnp.pad(flat, ((0, 0), (0, rows - flat.shape[1]), (0, 0)))
        parts.append(flat)
    used = sum(p.shape[1] for p in parts)
    parts.append(jnp.zeros((N_CHIPS, PACK_ROWS - used, D_MODEL), F32))
    return jnp.concatenate(parts, axis=1)


def unpack_local(flat):
    out, off = {}, 0
    for name, layer, r, c, _ in _PACK:
        n = r * c // D_MODEL
        out[(name, layer)] = flat[off:off + n, :].reshape(r, c)
        off += _pack_rows(r, c)
    return out


def rope_tables(pos):
    half = HEAD_DIM // 2
    lane = jnp.arange(128)
    inv = ROPE_THETA ** (-(lane % half).astype(F32) / half)
    ang = pos.astype(F32)[:, None] * inv[None, :]
    sign = jnp.where((lane % HEAD_DIM) < half, -1.0, 1.0).astype(F32)
    return jnp.cos(ang), jnp.sin(ang) * sign[None, :]


def local_step(x, p, pos, tgt, norm_mix, norm_ffn, norm_ple, norm_final, ev_b_f, od_sinks, wfull):
    t = x.shape[0]
    w_in0 = wfull[("ev_w_in", 0)]
    w_in0 = jnp.concatenate([w_in0, jnp.zeros((D_MODEL, EVEN_IN_PAD - w_in0.shape[1]), w_in0.dtype)], axis=1)
    w_gu = [jnp.concatenate([wfull[("ffn_w_gate", i)], wfull[("ffn_w_up", i)]], axis=1) for i in range(2)]
    b_pad = jnp.zeros((1, GATE_PAD), F32).at[0, :N_FOX].set(ev_b_f[0])
    sinks_b = jnp.broadcast_to(od_sinks[0][:, None, None], (N_Q, 1, 128)).astype(F32)
    cosw, sinw = rope_tables(pos)

    saved = []
    h = x
    for i in range(2):
        s = {"h0": h}
        n1 = rmsnorm_fwd(h, norm_mix[i], f"norm_mix_fwd{i}")
        s["n1"] = n1
        if i == 0:
            proj = matmul(n1, w_in0, name="ev_in")
            cols = [proj[:, j * FOX_W:(j + 1) * FOX_W] for j in range(6)]
            flog = proj[:, EVEN_QKV:]
            cum = forget_cumsum(flog, b_pad, "forget_cumsum")
            fox = dict(zip(("qt", "q", "k", "f_end", "k_norm"), fox_operands(cols[0], cols[1], cum[:, :N_FOX].T)))
            fox.update(kt=head_tiles(cols[1].astype(BF16), N_FOX), v=head_rows(cols[2].astype(BF16), N_FOX),
                       vt=head_tiles(cols[2].astype(BF16), N_FOX))
            q_sb = (cols[3] * SCALE).astype(BF16)
            sb = dict(qt=head_cols(q_sb, N_SB), q=head_rows(q_sb, N_SB), k=head_rows(cols[4].astype(BF16), N_SB),
                      kt=head_tiles(cols[4].astype(BF16), N_SB), v=head_rows(cols[5].astype(BF16), N_SB),
                      vt=head_tiles(cols[5].astype(BF16), N_SB))
            fox["ot"], fox["lse"], fox["first"] = fox_fwd(fox["qt"], fox["k"], fox["vt"], fox["f_end"], fox["k_norm"])
            sb["ot"], sb["ltot"], sb["first"] = sb_fwd(sb["qt"], sb["k"], sb["vt"])
            s.update(flog=flog, fox=fox, sb=sb)
            mixin_t = jnp.concatenate([fox["ot"].reshape(FOX_W, t), sb["ot"].reshape(SB_W, t)], axis=0).astype(BF16)
            w_out = wfull[("ev_w_out", 0)]
        else:
            proj = matmul(n1, wfull[("od_w_in", 0)], name="od_in")
            qk = rope_apply(proj[:, :Q_W + KV_W], cosw, sinw, False, "rope_fwd")
            q_sc = (qk[:, :Q_W] * SCALE).astype(BF16)
            k_b = qk[:, Q_W:].astype(BF16)
            v_b = proj[:, Q_W + KV_W:].astype(BF16)
            swa = dict(qt=head_cols(q_sc, N_Q), q=head_rows(q_sc, N_Q), k=head_rows(k_b, N_KV),
                       kt=head_tiles(k_b, N_KV), v=head_rows(v_b, N_KV), vt=head_tiles(v_b, N_KV))
            swa["ot"], swa["lse"] = swa_fwd(swa["qt"], swa["k"], swa["vt"], sinks_b)
            s["swa"] = swa
            mixin_t = swa["ot"].reshape(Q_W, t).astype(BF16)
            w_out = wfull[("od_w_out", 0)]
        s["mixin_t"] = mixin_t
        h = matmul(mixin_t, w_out, "tn", residual=h, name=f"mix_out{i}")
        s["h1"] = h
        n2 = rmsnorm_fwd(h, norm_ffn[i], f"norm_ffn_fwd{i}")
        gu = matmul(n2, w_gu[i], name=f"ffn_gu{i}")
        act = swiglu_fwd(gu, f"swiglu_fwd{i}")
        s.update(n2=n2, gu=gu, act=act)
        h = matmul(act, wfull[("ffn_w_down", i)], residual=h, name=f"ffn_down{i}")
        s["h2"] = h
        n3 = rmsnorm_fwd(h, norm_ple[i], f"norm_ple_fwd{i}")
        pre = matmul(n3, wfull[("ple_w_gate", i)], name=f"ple_gate{i}")
        pp = matmul(p[i], wfull[("ple_w_proj", i)], name=f"ple_proj{i}")
        s.update(n3=n3, pre=pre, pp=pp)
        h = ple_fwd(h, pre, pp, f"ple_fwd{i}")
        saved.append(s)

    dh, dg_final, loss8 = loss_head(h, norm_final, tgt, "loss_head")
    gw = {}
    small = {"norm_final": dg_final, "loss": loss8}
    for i in (1, 0):
        s = saved[i]
        dpre, dpp = ple_bwd(dh, s["pre"], s["pp"], f"ple_bwd{i}")
        gw[("ple_w_gate", i)] = matmul(s["n3"], dpre, "tn", name=f"d_ple_gate{i}")
        gw[("ple_w_proj", i)] = matmul(p[i], dpp, "tn", name=f"d_ple_proj{i}")
        dn3 = matmul(dpre, wfull[("ple_w_gate", i)], "nt", name=f"dn_ple{i}")
        dh, small[("norm_ple", i)] = rmsnorm_bwd(s["h2"], norm_ple[i], dn3, dh, f"norm_ple_bwd{i}")

        dact = matmul(dh, wfull[("ffn_w_down", i)], "nt", name=f"d_act{i}")
        gw[("ffn_w_down", i)] = matmul(s["act"], dh, "tn", name=f"d_ffn_down{i}")
        dgu = swiglu_bwd(s["gu"], dact, f"swiglu_bwd{i}")
        dwgu = matmul(s["n2"], dgu, "tn", name=f"d_ffn_gu{i}")
        gw[("ffn_w_gate", i)] = dwgu[:, :D_FF]
        gw[("ffn_w_up", i)] = dwgu[:, D_FF:]
        dn2 = matmul(dgu, w_gu[i], "nt", name=f"dn_ffn{i}")
        dh, small[("norm_ffn", i)] = rmsnorm_bwd(s["h1"], norm_ffn[i], dn2, dh, f"norm_ffn_bwd{i}")

        if i == 0:
            dmix = matmul(dh, wfull[("ev_w_out", 0)], "nt", name="d_mix0")
            gw[("ev_w_out", 0)] = matmul(s["mixin_t"], dh, name="d_ev_out")
            fox, sb = s["fox"], s["sb"]
            dmb = dmix.astype(BF16)
            dqt_f, dk_aug, dv_f, ds_rows = fox_bwd(
                fox["qt"], fox["q"], fox["k"], fox["kt"], fox["v"], fox["ot"],
                head_rows(dmb[:, :FOX_W], N_FOX), head_cols(dmb[:, :FOX_W], N_FOX), fox["lse"], fox["first"])
            dqt_s, dk_s, dv_s = sb_bwd(sb["qt"], sb["q"], sb["k"], sb["kt"], sb["v"], sb["ltot"], sb["first"],
                                       head_rows(dmb[:, FOX_W:], N_SB), head_cols(dmb[:, FOX_W:], N_SB))
            ds_cols = jnp.sum(dk_aug[:, :, HEAD_DIM:HEAD_DIM + N_FOX], axis=0)
            d_cum = jnp.pad(ds_rows[:, 0, :].T - ds_cols, ((0, 0), (0, GATE_PAD - N_FOX)))
            dflog, db8 = forget_cumsum_bwd(d_cum, s["flog"], b_pad, "forget_cumsum_bwd")
            small["ev_b_f"] = db8
            parts = (cols_to_flat(dqt_f), rows_to_flat(dk_aug[:, :, :HEAD_DIM]), rows_to_flat(dv_f),
                     cols_to_flat(dqt_s), rows_to_flat(dk_s), rows_to_flat(dv_s), dflog)
            dproj = jnp.concatenate([a.astype(BF16) for a in parts], axis=1)
            dw = matmul(s["n1"], dproj, "tn", name="d_ev_in")
            gw[("ev_w_in", 0)] = dw[:, :EVEN_QKV + N_FOX]
            dn1 = matmul(dproj, w_in0, "nt", name="dn_mix0")
        else:
            dmix = matmul(dh, wfull[("od_w_out", 0)], "nt", name="d_mix1")
            gw[("od_w_out", 0)] = matmul(s["mixin_t"], dh, name="d_od_out")
            swa = s["swa"]
            dmb = dmix.astype(BF16)
            dqt, dk, dv, dsink = swa_bwd(swa["qt"], swa["q"], swa["k"], swa["kt"], swa["v"], sinks_b, swa["ot"],
                                         head_rows(dmb, N_Q), head_cols(dmb, N_Q), swa["lse"])
            small["od_sinks"] = dsink
            dqk = rope_apply(jnp.concatenate([cols_to_flat(dqt), rows_to_flat(dk)], axis=1), cosw, sinw, True,
                             "rope_bwd")
            dproj = jnp.concatenate([dqk, rows_to_flat(dv)], axis=1).astype(BF16)
            gw[("od_w_in", 0)] = matmul(s["n1"], dproj, "tn", name="d_od_in")
            dn1 = matmul(dproj, wfull[("od_w_in", 0)], "nt", name="dn_mix1")
        dh, small[("norm_mix", i)] = rmsnorm_bwd(s["h0"], norm_mix[i], dn1, dh, f"norm_mix_bwd{i}")
    return dh, gw, small


_SMALL_ROWS = (("norm_mix", 0), ("norm_mix", 1), ("norm_ffn", 0), ("norm_ffn", 1),
               ("norm_ple", 0), ("norm_ple", 1), "norm_final", "misc", "loss")


def pack_small(small):
    rows = []
    for key in _SMALL_ROWS:
        if key == "misc":
            db = jnp.sum(small["ev_b_f"], axis=0)[:N_FOX]
            dsink = jnp.sum(small["od_sinks"][:, 0, :], axis=1)
            rows.append(jnp.zeros((D_MODEL,), F32).at[:N_FOX].set(db).at[128:128 + N_Q].set(dsink))
        else:
            rows.append(jnp.sum(small[key], axis=0))
    rows += [jnp.zeros((D_MODEL,), F32)] * (SMALL_ROWS - len(rows))
    return jnp.stack(rows)


def kernel(x, p, positions, norm_mix, norm_ffn, norm_ple, norm_final, ev_w_in, ev_b_f, ev_w_out, od_w_in, od_sinks, od_w_out, ffn_w_gate, ffn_w_up, ffn_w_down, ple_w_proj, ple_w_gate, loss_target, m_norm_mix, m_norm_ffn, m_norm_ple, m_norm_final, m_ev_w_in, m_ev_b_f, m_ev_w_out, m_od_w_in, m_od_sinks, m_od_w_out, m_ffn_w_gate, m_ffn_w_up, m_ffn_w_down, m_ple_w_proj, m_ple_w_gate, v_norm_mix, v_norm_ffn, v_norm_ple, v_norm_final, v_ev_w_in, v_ev_b_f, v_ev_w_out, v_od_w_in, v_od_sinks, v_od_w_out, v_ffn_w_gate, v_ffn_w_up, v_ffn_w_down, v_ple_w_proj, v_ple_w_gate):
    local_w = dict(ev_w_in=ev_w_in, ev_w_out=ev_w_out, od_w_in=od_w_in, od_w_out=od_w_out,
                   ffn_w_gate=ffn_w_gate, ffn_w_up=ffn_w_up, ffn_w_down=ffn_w_down,
                   ple_w_proj=ple_w_proj, ple_w_gate=ple_w_gate)
    local_m = dict(ev_w_in=m_ev_w_in, ev_w_out=m_ev_w_out, od_w_in=m_od_w_in, od_w_out=m_od_w_out,
                   ffn_w_gate=m_ffn_w_gate, ffn_w_up=m_ffn_w_up, ffn_w_down=m_ffn_w_down,
                   ple_w_proj=m_ple_w_proj, ple_w_gate=m_ple_w_gate)
    local_v = dict(ev_w_in=v_ev_w_in, ev_w_out=v_ev_w_out, od_w_in=v_od_w_in, od_w_out=v_od_w_out,
                   ffn_w_gate=v_ffn_w_gate, ffn_w_up=v_ffn_w_up, ffn_w_down=v_ffn_w_down,
                   ple_w_proj=v_ple_w_proj, ple_w_gate=v_ple_w_gate)

    wfull = unpack_full(allgather_chips(pack_shards(local_w)))
    grad_x, gw, small = local_step(x[0], p[:, 0], positions[0], loss_target[0], norm_mix, norm_ffn, norm_ple,
                                   norm_final, ev_b_f, od_sinks, wfull)

    cc = lax.axis_index("c")
    g4 = pack_grads(gw).reshape(N_CHIPS, 2, PACK_ROWS_HALF, D_MODEL)
    theirs = pair_send_other_half(g4)
    mine = lax.dynamic_index_in_dim(g4, cc, axis=1, keepdims=False)
    pair_sum = add_pair(mine.reshape(N_CHIPS * PACK_ROWS_HALF, D_MODEL),
                        theirs.reshape(N_CHIPS * PACK_ROWS_HALF, D_MODEL), "add_pair")
    from_chips = exchange_chips(pair_sum.reshape(N_CHIPS, PACK_ROWS_HALF, D_MODEL))
    half_sum = sum_chips(from_chips, "sum_chips")
    other_half = pair_swap(half_sum)
    low = jnp.where(cc == 0, half_sum, other_half)
    high = jnp.where(cc == 0, other_half, half_sum)
    g_local = unpack_local(jnp.concatenate([low, high], axis=0))

    red = allreduce_small(pack_small(small))
    loss = 0.5 * jnp.sum(red[8]) / D_MODEL
    pad_small = lambda a: jnp.zeros((D_MODEL,), F32).at[:N_FOX].set(a[0][0]).at[128:128 + N_Q].set(a[1][0])
    stack_small = lambda a: jnp.concatenate(
        [a[0], a[1], a[2], a[3][None], pad_small(a[4:6])[None], jnp.zeros((SMALL_ROWS - 8, D_MODEL), F32)], axis=0)
    w_small = stack_small((norm_mix, norm_ffn, norm_ple, norm_final, ev_b_f, od_sinks))
    m_small = stack_small((m_norm_mix, m_norm_ffn, m_norm_ple, m_norm_final, m_ev_b_f, m_od_sinks))
    v_small = stack_small((v_norm_mix, v_norm_ffn, v_norm_ple, v_norm_final, v_ev_b_f, v_od_sinks))
    g_small = red.at[8].set(0.0)
    upd_small = (g_small,) + tuple(adamw(w_small, g_small, m_small, v_small, "adamw_small"))

    def split_small(a):
        return (a[0:2], a[2:4], a[4:6], a[6], a[7, :N_FOX][None], a[7, 128:128 + N_Q][None])

    small_out = [split_small(a) for a in upd_small]

    big_names = ("ev_w_in", "ev_w_out", "od_w_in", "od_w_out", "ffn_w_gate", "ffn_w_up", "ffn_w_down",
                 "ple_w_proj", "ple_w_gate")
    big_out = {}
    for name in big_names:
        w = local_w[name]
        layers, r, c = w.shape
        g = jnp.concatenate([g_local[(name, i)] for i in range(layers)], axis=0)
        res = adamw(w.reshape(layers * r, c), g, local_m[name].reshape(layers * r, c),
                    local_v[name].reshape(layers * r, c), f"adamw_{name}")
        big_out[name] = [a.reshape(layers, r, c) for a in (g,) + tuple(res)]

    outs = [loss, grad_x[None]]
    for kind in range(4):
        sm = small_out[kind]
        outs += [sm[0], sm[1], sm[2], sm[3],
                 big_out["ev_w_in"][kind], sm[4], big_out["ev_w_out"][kind],
                 big_out["od_w_in"][kind], sm[5], big_out["od_w_out"][kind],
                 big_out["ffn_w_gate"][kind], big_out["ffn_w_up"][kind], big_out["ffn_w_down"][kind],
                 big_out["ple_w_proj"][kind], big_out["ple_w_gate"][kind]]
    return tuple(outs)
```

```python
import jax
import jax.numpy as jnp
from jax import lax
from jax.experimental import pallas as pl
from jax.experimental.pallas import tpu as pltpu

F32 = jnp.float32
BF16 = jnp.bfloat16

D_MODEL = 1024
HEAD_DIM = 64
N_FOX = 8
N_SB = 8
FOX_W = N_FOX * HEAD_DIM
SB_W = N_SB * HEAD_DIM
EVEN_QKV = 3 * FOX_W + 3 * SB_W
GATE_PAD = 128
EVEN_IN_PAD = EVEN_QKV + GATE_PAD
N_Q = 16
N_KV = 4
GROUP = N_Q // N_KV
Q_W = N_Q * HEAD_DIM
KV_W = N_KV * HEAD_DIM
ODD_IN = Q_W + 2 * KV_W
WINDOW = 128
ROPE_THETA = 10000.0
D_FF = 2816
PLE_DIM = 256
EPS = 1e-6
NEG_INF = -1e30
SCALE = HEAD_DIM ** -0.5

ADAM_LR = 0.001
ADAM_B1 = 0.9
ADAM_B2 = 0.999
ADAM_EPS = 1e-08
ADAM_WD = 0.01
ADAM_STEP = 10

N_CHIPS = 4
VMEM_LIMIT = 48 * 1024 * 1024
MESH = pl.DeviceIdType.MESH

_PACK = (
    ("ev_w_in", 0, 1024, 770, True),
    ("ev_w_out", 0, 256, 1024, False),
    ("od_w_in", 0, 1024, 384, True),
    ("od_w_out", 0, 256, 1024, False),
    ("ffn_w_gate", 0, 1024, 704, True),
    ("ffn_w_gate", 1, 1024, 704, True),
    ("ffn_w_up", 0, 1024, 704, True),
    ("ffn_w_up", 1, 1024, 704, True),
    ("ffn_w_down", 0, 704, 1024, False),
    ("ffn_w_down", 1, 704, 1024, False),
    ("ple_w_proj", 0, 256, 256, True),
    ("ple_w_proj", 1, 256, 256, True),
    ("ple_w_gate", 0, 256, 1024, False),
    ("ple_w_gate", 1, 256, 1024, False),
)
_ROW_ALIGN = 16


def _pack_rows(r, c):
    n = r * c // D_MODEL
    return -(-n // _ROW_ALIGN) * _ROW_ALIGN


PACK_ROWS_HALF = 3328
PACK_ROWS = 2 * PACK_ROWS_HALF
assert sum(_pack_rows(r, c) for _, _, r, c, _ in _PACK) <= PACK_ROWS


def _pick(n, cands):
    for c in cands:
        if n % c == 0:
            return c
    return n


def _cparams(sem):
    return pltpu.CompilerParams(dimension_semantics=sem, vmem_limit_bytes=VMEM_LIMIT)


_DIMS = {
    "nn": (((1,), (0,)), ((), ())),
    "nt": (((1,), (1,)), ((), ())),
    "tn": (((0,), (0,)), ((), ())),
}


def matmul(a, b, mode="nn", out_dtype=F32, residual=None, name="mm"):
    if mode == "nn":
        (m, k), (k2, n) = a.shape, b.shape
    elif mode == "nt":
        (m, k), (n, k2) = a.shape, b.shape
    else:
        (k, m), (k2, n) = a.shape, b.shape
    assert k == k2, (a.shape, b.shape, mode)
    tm = _pick(m, (1024, 1408, 512, 256, 128))
    tn = _pick(n, (1024, 1408, 512, 640, 384, 256, 128))
    tk = _pick(k, (1024, 1408, 640, 512, 256, 128))
    nk = k // tk
    dims = _DIMS[mode]
    has_res = residual is not None

    def body(*refs):
        if has_res:
            a_ref, b_ref, r_ref, o_ref, acc = refs
        else:
            a_ref, b_ref, o_ref, acc = refs
        kk = pl.program_id(2)
        part = lax.dot_general(a_ref[...].astype(BF16), b_ref[...].astype(BF16), dims,
                               preferred_element_type=F32)

        def finish(r):
            if has_res:
                r = r + r_ref[...]
            o_ref[...] = r.astype(out_dtype)

        if nk == 1:
            finish(part)
            return

        @pl.when(kk == 0)
        def _():
            acc[...] = part

        @pl.when((kk > 0) & (kk < nk - 1))
        def _():
            acc[...] += part

        @pl.when(kk == nk - 1)
        def _():
            finish(acc[...] + part)

    if mode == "nn":
        a_spec = pl.BlockSpec((tm, tk), lambda i, j, kk: (i, kk))
        b_spec = pl.BlockSpec((tk, tn), lambda i, j, kk: (kk, j))
    elif mode == "nt":
        a_spec = pl.BlockSpec((tm, tk), lambda i, j, kk: (i, kk))
        b_spec = pl.BlockSpec((tn, tk), lambda i, j, kk: (j, kk))
    else:
        a_spec = pl.BlockSpec((tk, tm), lambda i, j, kk: (kk, i))
        b_spec = pl.BlockSpec((tk, tn), lambda i, j, kk: (kk, j))
    o_spec = pl.BlockSpec((tm, tn), lambda i, j, kk: (i, j))
    in_specs = [a_spec, b_spec] + ([o_spec] if has_res else [])
    args = (a, b) + ((residual,) if has_res else ())
    return pl.pallas_call(
        body, name=name, grid=(m // tm, n // tn, nk),
        in_specs=in_specs, out_specs=o_spec,
        out_shape=jax.ShapeDtypeStruct((m, n), out_dtype),
        scratch_shapes=[pltpu.VMEM((tm, tn), F32)],
        compiler_params=_cparams(("parallel", "parallel", "arbitrary")),
    )(*args)


def _fold8(v):
    r, w = v.shape
    return v.reshape(r // 8, 8, w).sum(axis=0)


def rowwise(fn, rows, bcasts, outs, accs=(), tr=256, name="rowwise", reverse=False):
    t = rows[0].shape[0]
    tr = _pick(t, (tr, 128, 64, 32, 16, 8))
    nr, nb, no, na = len(rows), len(bcasts), len(outs), len(accs)
    steps = t // tr

    def body(*refs):
        ins = [r[...] for r in refs[:nr + nb]]
        out_refs = refs[nr + nb:nr + nb + no]
        acc_refs = refs[nr + nb + no:]
        o, a = fn(*ins)
        for r, v in zip(out_refs, o):
            r[...] = v.astype(r.dtype)
        if na:
            @pl.when(pl.program_id(0) == 0)
            def _():
                for r in acc_refs:
                    r[...] = jnp.zeros_like(r)

            for r, v in zip(acc_refs, a):
                r[...] += v

    if reverse:
        ridx = lambda i: (steps - 1 - i, 0)
    else:
        ridx = lambda i: (i, 0)
    in_specs = [pl.BlockSpec((tr, x.shape[1]), ridx) for x in rows]
    in_specs += [pl.BlockSpec(x.shape, lambda i: (0, 0)) for x in bcasts]
    out_specs = [pl.BlockSpec((tr, w), ridx) for w, _ in outs]
    out_specs += [pl.BlockSpec((8, w), lambda i: (0, 0)) for w in accs]
    out_shape = [jax.ShapeDtypeStruct((t, w), dt) for w, dt in outs]
    out_shape += [jax.ShapeDtypeStruct((8, w), F32) for w in accs]
    res = pl.pallas_call(
        body, name=name, grid=(steps,), in_specs=in_specs, out_specs=out_specs, out_shape=out_shape,
        compiler_params=_cparams(("arbitrary",)),
    )(*rows, *bcasts)
    return res


def _rstd(x):
    return lax.rsqrt(jnp.mean(x * x, axis=-1, keepdims=True) + EPS)


def rmsnorm_fwd(h, g, name):
    def fn(x, gg):
        return ((x * _rstd(x)) * gg,), ()

    return rowwise(fn, [h], [g.reshape(1, -1)], [(D_MODEL, BF16)], name=name)[0]


def _rms_bwd_math(x, gg, dy):
    r = _rstd(x)
    xh = x * r
    u = dy * gg
    dx = r * (u - xh * jnp.mean(u * xh, axis=-1, keepdims=True))
    return dx, dy * xh


def rmsnorm_bwd(h, g, dn, dres, name):
    def fn(x, dy, dr, gg):
        dx, dgp = _rms_bwd_math(x, gg, dy)
        return (dr + dx,), (_fold8(dgp),)

    return rowwise(fn, [h, dn, dres], [g.reshape(1, -1)], [(D_MODEL, F32)], [D_MODEL], name=name)


def loss_head(h, g, tgt, name):
    def fn(x, tg, gg):
        y = (x * _rstd(x)) * gg
        e = y - tg
        dy = e * (1.0 / D_MODEL)
        dx, dgp = _rms_bwd_math(x, gg, dy)
        return (dx,), (_fold8(dgp), _fold8(e * e))

    return rowwise(fn, [h, tgt], [g.reshape(1, -1)], [(D_MODEL, F32)], [D_MODEL, D_MODEL], name=name)


def _sigmoid(x):
    return 1.0 / (1.0 + jnp.exp(-x))


def swiglu_fwd(gu, name):
    def fn(x):
        gg, uu = x[:, :D_FF], x[:, D_FF:]
        return ((gg * _sigmoid(gg)) * uu,), ()

    return rowwise(fn, [gu], [], [(D_FF, BF16)], name=name)[0]


def swiglu_bwd(gu, da, name):
    def fn(x, d):
        gg, uu = x[:, :D_FF], x[:, D_FF:]
        s = _sigmoid(gg)
        silu = gg * s
        dgg = d * uu * (s + silu * (1.0 - s))
        duu = d * silu
        return (jnp.concatenate([dgg, duu], axis=1),), ()

    return rowwise(fn, [gu, da], [], [(2 * D_FF, BF16)], name=name)[0]


def ple_fwd(h, pre, pp, name):
    def fn(x, a, b):
        return (x + _sigmoid(a) * b,), ()

    return rowwise(fn, [h, pre, pp], [], [(D_MODEL, F32)], name=name)[0]


def ple_bwd(dh, pre, pp, name):
    def fn(d, a, b):
        s = _sigmoid(a)
        return (d * b * s * (1.0 - s), d * s), ()

    return rowwise(fn, [dh, pre, pp], [], [(D_MODEL, BF16), (D_MODEL, BF16)], name=name)


def _rot_half_partner(x, first_half):
    w = x.shape[1]
    return jnp.where(first_half, pltpu.roll(x, w - HEAD_DIM // 2, 1), pltpu.roll(x, HEAD_DIM // 2, 1))


def rope_apply(xx, cosw, sinw, backward, name):
    width = xx.shape[1]
    reps = width // 128

    def fn(x, c, s):
        cw = jnp.tile(c, (1, reps))
        sw = jnp.tile(s, (1, reps))
        lane = lax.broadcasted_iota(jnp.int32, x.shape, 1)
        first = (lane % HEAD_DIM) < (HEAD_DIM // 2)
        if backward:
            return (x * cw + _rot_half_partner(x * sw, first),), ()
        return (x * cw + _rot_half_partner(x, first) * sw,), ()

    return rowwise(fn, [xx, cosw, sinw], [], [(width, F32)], name=name)[0]


def _log_sigmoid(x):
    return jnp.minimum(x, 0.0) - jnp.log(1.0 + jnp.exp(-jnp.abs(x)))


CUM_BLOCK = 256


def forget_cumsum(flog, bias, name):
    t = flog.shape[0]
    tb = _pick(t, (CUM_BLOCK,))

    def body(x_ref, b_ref, o_ref, carry):
        @pl.when(pl.program_id(0) == 0)
        def _():
            carry[...] = jnp.zeros_like(carry)

        lf = _log_sigmoid(x_ref[...] + b_ref[...])
        r = lax.broadcasted_iota(jnp.int32, (tb, tb), 0)
        c = lax.broadcasted_iota(jnp.int32, (tb, tb), 1)
        tri = (c <= r).astype(F32)
        cum = jnp.dot(tri, lf, preferred_element_type=F32, precision=lax.Precision.HIGHEST) + carry[...]
        o_ref[...] = cum
        carry[...] = cum[tb - 1:tb, :]

    return pl.pallas_call(
        body, name=name, grid=(t // tb,),
        in_specs=[pl.BlockSpec((tb, GATE_PAD), lambda i: (i, 0)), pl.BlockSpec((1, GATE_PAD), lambda i: (0, 0))],
        out_specs=pl.BlockSpec((tb, GATE_PAD), lambda i: (i, 0)),
        out_shape=jax.ShapeDtypeStruct((t, GATE_PAD), F32),
        scratch_shapes=[pltpu.VMEM((1, GATE_PAD), F32)],
        compiler_params=_cparams(("arbitrary",)),
    )(flog, bias)


def forget_cumsum_bwd(d_cum, flog, bias, name):
    t = flog.shape[0]
    tb = _pick(t, (CUM_BLOCK,))
    nb = t // tb

    def body(d_ref, x_ref, b_ref, o_ref, db_ref, carry):
        @pl.when(pl.program_id(0) == 0)
        def _():
            carry[...] = jnp.zeros_like(carry)
            db_ref[...] = jnp.zeros_like(db_ref)

        r = lax.broadcasted_iota(jnp.int32, (tb, tb), 0)
        c = lax.broadcasted_iota(jnp.int32, (tb, tb), 1)
        tri = (c >= r).astype(F32)
        dlf = jnp.dot(tri, d_ref[...], preferred_element_type=F32, precision=lax.Precision.HIGHEST) + carry[...]
        carry[...] = dlf[0:1, :]
        dx = dlf * (1.0 - _sigmoid(x_ref[...] + b_ref[...]))
        o_ref[...] = dx
        db_ref[...] += _fold8(dx)

    rev = lambda i: (nb - 1 - i, 0)
    return pl.pallas_call(
        body, name=name, grid=(nb,),
        in_specs=[pl.BlockSpec((tb, GATE_PAD), rev), pl.BlockSpec((tb, GATE_PAD), rev),
                  pl.BlockSpec((1, GATE_PAD), lambda i: (0, 0))],
        out_specs=[pl.BlockSpec((tb, GATE_PAD), rev), pl.BlockSpec((8, GATE_PAD), lambda i: (0, 0))],
        out_shape=[jax.ShapeDtypeStruct((t, GATE_PAD), F32), jax.ShapeDtypeStruct((8, GATE_PAD), F32)],
        scratch_shapes=[pltpu.VMEM((1, GATE_PAD), F32)],
        compiler_params=_cparams(("arbitrary",)),
    )(d_cum, flog, bias)


TQ = 512
TK = 128
AUG = 128
N_BIAS = 3
SB_CUTOFF = 110.0
FOX_CUTOFF = 112.0


def _dot(a, b):
    return jnp.dot(a, b, preferred_element_type=F32)


def _rel(shape, d):
    return lax.broadcasted_iota(jnp.int32, shape, 0) - lax.broadcasted_iota(jnp.int32, shape, 1) + d


def _split_bf16(x):
    hi = x.astype(BF16)
    return hi, (x - hi.astype(F32)).astype(BF16)


def _tri_dot(tri, x):
    hi, lo = _split_bf16(x)
    return _dot(tri, hi) + _dot(tri, lo)


def _q_cols(width, tq):
    return pl.BlockSpec((None, width, tq), lambda h, i: (h, 0, i))


def _q_rows(width, tq):
    return pl.BlockSpec((None, tq, width), lambda h, i: (h, i, 0))


def _kv_rows(t, width):
    return pl.BlockSpec((None, t, width), lambda h, i: (h, 0, 0))


def _kv_tiles(nk):
    return pl.BlockSpec((None, nk, HEAD_DIM, TK), lambda h, i: (h, 0, 0, 0))


def _blocks(t):
    tq = TQ if t % TQ == 0 else TK
    return tq, tq // TK, t // TK


def fox_fwd(qt_aug, k_aug, vt, f_end, k_norm):
    nh, _, t = qt_aug.shape
    tq, ratio, nk = _blocks(t)
    lanes = tq // 128

    def body(q_ref, k_ref, v_ref, fe_ref, kn_ref, o_ref, lse_ref, first_ref):
        i = pl.program_id(1)
        qv = q_ref[...]
        qf = qv[:HEAD_DIM].astype(F32)
        reach = jnp.sqrt(jnp.sum(qf * qf, axis=0, keepdims=True)) * jnp.tile(kn_ref[...], (1, lanes))

        def step(first, carry, masked):
            m, l, acc = carry
            scores = []
            for u in range(ratio):
                off = pl.multiple_of((first + u) * TK, TK)
                s = _dot(k_ref[pl.ds(off, TK), :], qv)
                if masked:
                    s = jnp.where(_rel(s.shape, (first + u) * TK - i * tq) <= 0, s, NEG_INF)
                scores.append(s)
            m_new = m
            for s in scores:
                m_new = jnp.maximum(m_new, jnp.max(s, axis=0, keepdims=True))
            alpha = jnp.exp(m - m_new)
            l = alpha * l
            acc = alpha * acc
            for u, s in enumerate(scores):
                p = jnp.exp(s - m_new)
                l = l + jnp.sum(p, axis=0, keepdims=True)
                acc = acc + _dot(v_ref[first + u], p.astype(BF16))
            return m_new, l, acc

        init = (jnp.full((1, tq), NEG_INF, F32), jnp.zeros((1, tq), F32), jnp.zeros((HEAD_DIM, tq), F32))
        m, l, acc = step(ratio * i, init, True)

        def more(c):
            bound = reach + jnp.tile(fe_ref[jnp.maximum(i - 1 - c[0], 0)], (1, lanes))
            return (c[0] < i) & (jnp.max(bound - c[1]) > -FOX_CUTOFF)

        def sweep(c):
            return (c[0] + 1,) + step(ratio * (i - 1 - c[0]), c[1:], False)

        done, m, l, acc = lax.while_loop(more, sweep, (jnp.int32(0), m, l, acc))
        o_ref[...] = acc / l
        lse_ref[...] = m + jnp.log(l)
        first_ref[...] = jnp.full((1, 128), (i - done).astype(F32), F32)

    return pl.pallas_call(
        body, name="fox_fwd", grid=(nh, t // tq),
        in_specs=[_q_cols(AUG, tq), _kv_rows(t, AUG), _kv_tiles(nk),
                  pl.BlockSpec((None, t // tq, 1, 128), lambda h, i: (h, 0, 0, 0)),
                  pl.BlockSpec((None, 1, 128), lambda h, i: (h, 0, 0))],
        out_specs=[_q_cols(HEAD_DIM, tq), _q_cols(1, tq), _q_cols(1, 128)],
        out_shape=[jax.ShapeDtypeStruct((nh, HEAD_DIM, t), F32), jax.ShapeDtypeStruct((nh, 1, t), F32),
                   jax.ShapeDtypeStruct((nh, 1, 128 * (t // tq)), F32)],
        compiler_params=_cparams(("parallel", "arbitrary")),
    )(qt_aug, k_aug, vt, f_end, k_norm)


def fox_bwd(qt_aug, q_aug, k_aug, kt, v, ot, do, dot_, lse, first):
    nh, _, t = qt_aug.shape
    tq, ratio, nk = _blocks(t)

    def body(qt_ref, q_ref, k_ref, kt_ref, v_ref, ot_ref, do_ref, dot_ref, lse_ref, first_ref,
             dqt_ref, dk_ref, dv_ref, rs_ref):
        i = pl.program_id(1)
        start = jnp.max(first_ref[...]).astype(jnp.int32)

        @pl.when(i == 0)
        def _():
            dk_ref[...] = jnp.zeros_like(dk_ref)
            dv_ref[...] = jnp.zeros_like(dv_ref)

        qtv = qt_ref[...]
        qv = q_ref[...]
        dob = do_ref[...]
        dotb = dot_ref[...]
        delta = jnp.sum(ot_ref[...] * dotb.astype(F32), axis=0, keepdims=True)
        lse = lse_ref[...]

        def tile(j, carry, masked):
            dqt, rs = carry
            off = pl.multiple_of(j * TK, TK)
            s = _dot(k_ref[pl.ds(off, TK), :], qtv)
            p = jnp.exp(s - lse)
            if masked:
                p = jnp.where(_rel(s.shape, j * TK - i * tq) <= 0, p, 0.0)
            dp = _dot(v_ref[pl.ds(off, TK), :], dotb)
            dsb = (p * (dp - delta)).astype(BF16)
            dk_ref[pl.ds(off, TK), :] += _dot(dsb, qv)
            dv_ref[pl.ds(off, TK), :] += _dot(p.astype(BF16), dob)
            return dqt + _dot(kt_ref[j], dsb), rs + jnp.sum(dsb.astype(F32), axis=0, keepdims=True)

        def step(first, carry, masked):
            for u in range(ratio):
                carry = tile(first + u, carry, masked)
            return carry

        carry = (jnp.zeros((HEAD_DIM, tq), F32), jnp.zeros((1, tq), F32))
        carry = lax.fori_loop(start, i, lambda jj, c: step(ratio * jj, c, False), carry)
        dqt, rs = step(ratio * i, carry, True)
        dqt_ref[...] = dqt * SCALE
        rs_ref[...] = rs

    return pl.pallas_call(
        body, name="fox_bwd", grid=(nh, t // tq),
        in_specs=[_q_cols(AUG, tq), _q_rows(AUG, tq), _kv_rows(t, AUG), _kv_tiles(nk), _kv_rows(t, HEAD_DIM),
                  _q_cols(HEAD_DIM, tq), _q_rows(HEAD_DIM, tq), _q_cols(HEAD_DIM, tq), _q_cols(1, tq),
                  _q_cols(1, 128)],
        out_specs=[_q_cols(HEAD_DIM, tq), _kv_rows(t, AUG), _kv_rows(t, HEAD_DIM), _q_cols(1, tq)],
        out_shape=[jax.ShapeDtypeStruct((nh, HEAD_DIM, t), F32), jax.ShapeDtypeStruct((nh, t, AUG), F32),
                   jax.ShapeDtypeStruct((nh, t, HEAD_DIM), F32), jax.ShapeDtypeStruct((nh, 1, t), F32)],
        compiler_params=_cparams(("arbitrary", "arbitrary")),
    )(qt_aug, q_aug, k_aug, kt, v, ot, do, dot_, lse, first)


def _sb_logits(kb, qv, ok):
    z = _dot(kb, qv)
    e = jnp.exp(-jnp.abs(z))
    ll = -(jnp.maximum(z, 0.0) + jnp.log(1.0 + e))
    if ok is not None:
        ll = jnp.where(ok, ll, 0.0)
    return z, e, ll


def _tri(cmp):
    r = lax.broadcasted_iota(jnp.int32, (TK, TK), 0)
    c = lax.broadcasted_iota(jnp.int32, (TK, TK), 1)
    return cmp(r, c).astype(BF16)


def sb_fwd(qt, k, vt):
    nh, _, t = qt.shape
    tq, ratio, nk = _blocks(t)

    def body(q_ref, k_ref, v_ref, o_ref, tot_ref, first_ref):
        i = pl.program_id(1)
        qv = q_ref[...]
        tri_after = _tri(lambda r, c: c > r)

        def tile(j, carry, masked):
            c_l, acc = carry
            off = pl.multiple_of(j * TK, TK)
            ok = _rel((TK, tq), j * TK - i * tq) < 0 if masked else None
            z, _, ll = _sb_logits(k_ref[pl.ds(off, TK), :], qv, ok)
            a = jnp.exp(z + ll + _tri_dot(tri_after, ll) + c_l)
            if masked:
                a = jnp.where(ok, a, 0.0)
            return c_l + jnp.sum(ll, axis=0, keepdims=True), acc + _dot(v_ref[j], a.astype(BF16))

        def step(first, carry, masked):
            for u in reversed(range(ratio)):
                carry = tile(first + u, carry, masked)
            return carry

        c_l, acc = step(ratio * i, (jnp.zeros((1, tq), F32), jnp.zeros((HEAD_DIM, tq), F32)), True)

        def more(c):
            return (c[0] < i) & (jnp.max(c[1]) > -SB_CUTOFF)

        def sweep(c):
            c_l, acc = step(ratio * (i - 1 - c[0]), (c[1], c[2]), False)
            return c[0] + 1, c_l, acc

        done, c_l, acc = lax.while_loop(more, sweep, (jnp.int32(0), c_l, acc))
        o_ref[...] = acc
        tot_ref[...] = c_l
        first_ref[...] = jnp.full((1, 128), (i - done).astype(F32), F32)

    return pl.pallas_call(
        body, name="sb_fwd", grid=(nh, t // tq),
        in_specs=[_q_cols(HEAD_DIM, tq), _kv_rows(t, HEAD_DIM), _kv_tiles(nk)],
        out_specs=[_q_cols(HEAD_DIM, tq), _q_cols(1, tq), _q_cols(1, 128)],
        out_shape=[jax.ShapeDtypeStruct((nh, HEAD_DIM, t), F32), jax.ShapeDtypeStruct((nh, 1, t), F32),
                   jax.ShapeDtypeStruct((nh, 1, 128 * (t // tq)), F32)],
        compiler_params=_cparams(("parallel", "arbitrary")),
    )(qt, k, vt)


def sb_bwd(qt, q, k, kt, v, ltot, first, do, dot_):
    nh, _, t = qt.shape
    tq, ratio, nk = _blocks(t)

    def body(qt_ref, q_ref, k_ref, kt_ref, v_ref, tot_ref, first_ref, do_ref, dot_ref, dqt_ref, dk_ref, dv_ref):
        i = pl.program_id(1)
        start = jnp.max(first_ref[...]).astype(jnp.int32)

        @pl.when(i == 0)
        def _():
            dk_ref[...] = jnp.zeros_like(dk_ref)
            dv_ref[...] = jnp.zeros_like(dv_ref)

        qtv = qt_ref[...]
        qv = q_ref[...]
        dob = do_ref[...]
        dotb = dot_ref[...]
        tri_upto = _tri(lambda r, c: c <= r)
        tri_before = _tri(lambda r, c: c < r)

        def tile(j, carry, masked):
            rest, c_w, dqt = carry
            off = pl.multiple_of(j * TK, TK)
            ok = _rel((TK, tq), j * TK - i * tq) < 0 if masked else None
            z, e, ll = _sb_logits(k_ref[pl.ds(off, TK), :], qtv, ok)
            a = jnp.exp(z + ll + (rest - _tri_dot(tri_upto, ll)))
            if masked:
                a = jnp.where(ok, a, 0.0)
            w = a * _dot(v_ref[pl.ds(off, TK), :], dotb)
            before = _tri_dot(tri_before, w) + c_w
            r = 1.0 / (1.0 + e)
            sig = jnp.where(z >= 0.0, r, e * r)
            dz = w * (1.0 - sig) - before * sig
            if masked:
                dz = jnp.where(ok, dz, 0.0)
            dzb = dz.astype(BF16)
            dk_ref[pl.ds(off, TK), :] += _dot(dzb, qv)
            dv_ref[pl.ds(off, TK), :] += _dot(a.astype(BF16), dob)
            return (rest - jnp.sum(ll, axis=0, keepdims=True), c_w + jnp.sum(w, axis=0, keepdims=True),
                    dqt + _dot(kt_ref[j], dzb))

        def step(first, carry, masked):
            for u in range(ratio):
                carry = tile(first + u, carry, masked)
            return carry

        carry = (tot_ref[...], jnp.zeros((1, tq), F32), jnp.zeros((HEAD_DIM, tq), F32))
        carry = lax.fori_loop(start, i, lambda jj, c: step(ratio * jj, c, False), carry)
        dqt_ref[...] = step(ratio * i, carry, True)[2] * SCALE

    hd = HEAD_DIM
    return pl.pallas_call(
        body, name="sb_bwd", grid=(nh, t // tq),
        in_specs=[_q_cols(hd, tq), _q_rows(hd, tq), _kv_rows(t, hd), _kv_tiles(nk), _kv_rows(t, hd),
                  _q_cols(1, tq), _q_cols(1, 128), _q_rows(hd, tq), _q_cols(hd, tq)],
        out_specs=[_q_cols(hd, tq), _kv_rows(t, hd), _kv_rows(t, hd)],
        out_shape=[jax.ShapeDtypeStruct((nh, hd, t), F32), jax.ShapeDtypeStruct((nh, t, hd), F32),
                   jax.ShapeDtypeStruct((nh, t, hd), F32)],
        compiler_params=_cparams(("arbitrary", "arbitrary")),
    )(qt, q, k, kt, v, ltot, first, do, dot_)


def _swa_q_cols(width):
    return pl.BlockSpec((GROUP, width, WINDOW), lambda g, i: (g, 0, i))


def _swa_q_rows():
    return pl.BlockSpec((GROUP, WINDOW, HEAD_DIM), lambda g, i: (g, i, 0))


def _swa_kv_rows(t):
    return pl.BlockSpec((None, t, HEAD_DIM), lambda g, i: (g, 0, 0))


def _swa_kv_tiles(nk):
    return pl.BlockSpec((None, nk, HEAD_DIM, WINDOW), lambda g, i: (g, 0, 0, 0))


def _swa_sink_spec():
    return pl.BlockSpec((GROUP, 1, 128), lambda g, i: (g, 0, 0))


def _swa_window(i):
    jb = jnp.maximum(i - 1, 0)
    start = pl.multiple_of(jb * WINDOW, WINDOW)
    rel = _rel((2 * WINDOW, WINDOW), start - i * WINDOW)
    return jb, start, (rel <= 0) & (rel > -WINDOW)


def swa_fwd(qt, k, vt, sinks):
    nh, _, t = qt.shape
    nk = t // WINDOW

    def body(q_ref, k_ref, v_ref, s_ref, o_ref, lse_ref):
        i = pl.program_id(1)
        jb, start, valid = _swa_window(i)
        kwin = k_ref[pl.ds(start, 2 * WINDOW), :]
        for g in range(GROUP):
            s = jnp.where(valid, _dot(kwin, q_ref[g]), NEG_INF)
            sink = s_ref[g]
            m = jnp.maximum(jnp.max(s, axis=0, keepdims=True), sink)
            p = jnp.where(valid, jnp.exp(s - m), 0.0)
            l = jnp.sum(p, axis=0, keepdims=True) + jnp.exp(sink - m)
            pb = p.astype(BF16)
            o_ref[g] = (_dot(v_ref[jb], pb[:WINDOW]) + _dot(v_ref[jb + 1], pb[WINDOW:])) / l
            lse_ref[g] = m + jnp.log(l)

    return pl.pallas_call(
        body, name="swa_fwd", grid=(N_KV, nk),
        in_specs=[_swa_q_cols(HEAD_DIM), _swa_kv_rows(t), _swa_kv_tiles(nk), _swa_sink_spec()],
        out_specs=[_swa_q_cols(HEAD_DIM), _swa_q_cols(1)],
        out_shape=[jax.ShapeDtypeStruct((nh, HEAD_DIM, t), F32), jax.ShapeDtypeStruct((nh, 1, t), F32)],
        compiler_params=_cparams(("parallel", "arbitrary")),
    )(qt, k, vt, sinks)


def swa_bwd(qt, q, k, kt, v, sinks, ot, do, dot_, lse):
    nh, _, t = qt.shape
    nk = t // WINDOW

    def body(qt_ref, q_ref, k_ref, kt_ref, v_ref, s_ref, ot_ref, do_ref, dot_ref, lse_ref,
             dqt_ref, dk_ref, dv_ref, dsink_ref):
        i = pl.program_id(1)

        @pl.when(i == 0)
        def _():
            dk_ref[...] = jnp.zeros_like(dk_ref)
            dv_ref[...] = jnp.zeros_like(dv_ref)
            dsink_ref[...] = jnp.zeros_like(dsink_ref)

        jb, start, valid = _swa_window(i)
        kwin = k_ref[pl.ds(start, 2 * WINDOW), :]
        vwin = v_ref[pl.ds(start, 2 * WINDOW), :]
        dk_acc = jnp.zeros((2 * WINDOW, HEAD_DIM), F32)
        dv_acc = jnp.zeros((2 * WINDOW, HEAD_DIM), F32)
        for g in range(GROUP):
            lse_g = lse_ref[g]
            dotb = dot_ref[g]
            delta = jnp.sum(ot_ref[g] * dotb.astype(F32), axis=0, keepdims=True)
            p = jnp.where(valid, jnp.exp(_dot(kwin, qt_ref[g]) - lse_g), 0.0)
            dsb = (p * (_dot(vwin, dotb) - delta)).astype(BF16)
            dqt_ref[g] = (_dot(kt_ref[jb], dsb[:WINDOW]) + _dot(kt_ref[jb + 1], dsb[WINDOW:])) * SCALE
            dk_acc = dk_acc + _dot(dsb, q_ref[g])
            dv_acc = dv_acc + _dot(p.astype(BF16), do_ref[g])
            dsink_ref[g] += -jnp.exp(s_ref[g] - lse_g) * delta
        dk_ref[pl.ds(start, 2 * WINDOW), :] += dk_acc
        dv_ref[pl.ds(start, 2 * WINDOW), :] += dv_acc

    hd = HEAD_DIM
    return pl.pallas_call(
        body, name="swa_bwd", grid=(N_KV, nk),
        in_specs=[_swa_q_cols(hd), _swa_q_rows(), _swa_kv_rows(t), _swa_kv_tiles(nk), _swa_kv_rows(t),
                  _swa_sink_spec(), _swa_q_cols(hd), _swa_q_rows(), _swa_q_cols(hd), _swa_q_cols(1)],
        out_specs=[_swa_q_cols(hd), _swa_kv_rows(t), _swa_kv_rows(t), _swa_sink_spec()],
        out_shape=[jax.ShapeDtypeStruct((nh, hd, t), F32), jax.ShapeDtypeStruct((N_KV, t, hd), F32),
                   jax.ShapeDtypeStruct((N_KV, t, hd), F32), jax.ShapeDtypeStruct((nh, 1, 128), F32)],
        compiler_params=_cparams(("arbitrary", "arbitrary")),
    )(qt, q, k, kt, v, sinks, ot, do, dot_, lse)


def head_rows(a, nh):
    t = a.shape[0]
    return a.reshape(t, nh, HEAD_DIM).transpose(1, 0, 2)


def head_cols(a, nh):
    t = a.shape[0]
    return a.reshape(t, nh, HEAD_DIM).transpose(1, 2, 0)


def head_tiles(a, nh):
    t = a.shape[0]
    return a.reshape(t // TK, TK, nh, HEAD_DIM).transpose(2, 0, 3, 1)


def rows_to_flat(a):
    nh, t, _ = a.shape
    return a.transpose(1, 0, 2).reshape(t, nh * HEAD_DIM)


def cols_to_flat(a):
    nh, _, t = a.shape
    return a.transpose(2, 0, 1).reshape(t, nh * HEAD_DIM)


def fox_operands(qf, kf, cum_heads):
    t = qf.shape[0]
    nh = cum_heads.shape[0]
    tq = _blocks(t)[0]
    qs = (qf * SCALE).astype(BF16)
    ones_t = jnp.ones((nh, N_BIAS, t), BF16)
    qt_aug = jnp.concatenate([head_cols(qs, nh), ones_t, jnp.zeros((nh, AUG - HEAD_DIM - N_BIAS, t), BF16)], axis=1)
    own_lane = jnp.broadcast_to(jnp.eye(nh, dtype=BF16)[:, None, :], (nh, t, nh))
    q_aug = jnp.concatenate([head_rows(qs, nh), own_lane, jnp.zeros((nh, t, AUG - HEAD_DIM - nh), BF16)], axis=2)
    terms, rest = [], -cum_heads
    for _ in range(N_BIAS):
        term = lax.reduce_precision(rest, exponent_bits=8, mantissa_bits=7)
        terms.append(term)
        rest = rest - term
    k_rows = head_rows(kf.astype(BF16), nh)
    k_aug = jnp.concatenate([k_rows, jnp.stack(terms, axis=-1).astype(BF16),
                             jnp.zeros((nh, t, AUG - HEAD_DIM - N_BIAS), BF16)], axis=2)
    f_end = jnp.broadcast_to((-cum_heads)[:, tq - 1::tq, None, None], (nh, t // tq, 1, 128))
    k_sq = jnp.sum(jnp.square(k_rows.astype(F32)), axis=2)
    k_norm = jnp.broadcast_to(jnp.sqrt(jnp.max(k_sq, axis=1))[:, None, None], (nh, 1, 128))
    return qt_aug, q_aug, k_aug, f_end, k_norm


def adamw(w, g, m, v, name):
    cols = w.shape[1]

    def fn(ww, gg, mm, vv):
        mn = ADAM_B1 * mm + (1.0 - ADAM_B1) * gg
        vn = ADAM_B2 * vv + (1.0 - ADAM_B2) * (gg * gg)
        m_hat = mn / (1.0 - ADAM_B1 ** ADAM_STEP)
        v_hat = vn / (1.0 - ADAM_B2 ** ADAM_STEP)
        delta = -ADAM_LR * (m_hat / (jnp.sqrt(v_hat) + ADAM_EPS) + ADAM_WD * ww)
        return (delta, mn, vn), ()

    return rowwise(fn, [w, g, m, v], [], [(cols, F32)] * 3, name=name)


ANY = pl.BlockSpec(memory_space=pl.ANY)


def _my_place():
    return lax.axis_index("x"), lax.axis_index("y"), lax.axis_index("c")


def _flip(coord, bit):
    return 1 - coord if bit else coord


def allgather_chips(w):
    r, c = w.shape
    rh = r // 2

    def body(w_ref, out_ref, send_sems, recv_sems, local_sem):
        x, y, cc = _my_place()
        me = 2 * x + y
        sibling = (x, y, 1 - cc)
        chips = [(_flip(x, kbits >> 1), _flip(y, kbits & 1)) for kbits in (1, 2, 3)]

        def half(chip, hc):
            return out_ref.at[chip, pl.ds(pl.multiple_of(hc * rh, 16), rh)]

        def copy(k, src, dst, to):
            return pltpu.make_async_remote_copy(src_ref=src, dst_ref=dst, send_sem=send_sems.at[k],
                                                recv_sem=recv_sems.at[k], device_id=to, device_id_type=MESH)

        local = pltpu.make_async_copy(w_ref, out_ref.at[me], local_sem)
        local.start()
        my_half = w_ref.at[pl.ds(pl.multiple_of(cc * rh, 16), rh)]
        first = [copy(j, my_half, half(me, cc), (px, py, cc)) for j, (px, py) in enumerate(chips)]
        for cp in first:
            cp.start()
        passed = []
        for j, (px, py) in enumerate(chips):
            landed = half(2 * px + py, cc)
            copy(j, my_half, landed, (px, py, cc)).wait_recv()
            fwd = copy(3 + j, landed, landed, sibling)
            fwd.start()
            passed.append(fwd)
        for j, (px, py) in enumerate(chips):
            theirs = half(2 * px + py, 1 - cc)
            copy(3 + j, theirs, theirs, sibling).wait_recv()
        for cp in first + passed:
            cp.wait_send()
        local.wait()

    return pl.pallas_call(
        body, name="allgather_chips", in_specs=[ANY], out_specs=ANY,
        out_shape=jax.ShapeDtypeStruct((N_CHIPS, r, c), w.dtype),
        scratch_shapes=[pltpu.SemaphoreType.DMA((6,)), pltpu.SemaphoreType.DMA((6,)), pltpu.SemaphoreType.DMA],
    )(w)


def pair_send_other_half(g):
    n, _, rh, c = g.shape

    def body(g_ref, out_ref, send_sem, recv_sem):
        x, y, cc = _my_place()
        cp = pltpu.make_async_remote_copy(
            src_ref=g_ref.at[:, 1 - cc], dst_ref=out_ref, send_sem=send_sem, recv_sem=recv_sem,
            device_id=(x, y, 1 - cc), device_id_type=MESH)
        cp.start()
        cp.wait()

    return pl.pallas_call(
        body, name="pair_send_other_half", in_specs=[ANY], out_specs=ANY,
        out_shape=jax.ShapeDtypeStruct((n, rh, c), g.dtype),
        scratch_shapes=[pltpu.SemaphoreType.DMA, pltpu.SemaphoreType.DMA],
    )(g)


def exchange_chips(s):
    n, rh, c = s.shape

    def body(s_ref, out_ref, send_sems, recv_sems, local_sem):
        x, y, cc = _my_place()
        me = 2 * x + y
        local = pltpu.make_async_copy(s_ref.at[me], out_ref.at[me], local_sem)
        local.start()
        copies = []
        for kbits in (1, 2, 3):
            px, py = _flip(x, kbits >> 1), _flip(y, kbits & 1)
            cp = pltpu.make_async_remote_copy(
                src_ref=s_ref.at[2 * px + py], dst_ref=out_ref.at[me], send_sem=send_sems.at[kbits - 1],
                recv_sem=recv_sems.at[kbits - 1], device_id=(px, py, cc), device_id_type=MESH)
            cp.start()
            copies.append(cp)
        for cp in copies:
            cp.wait()
        local.wait()

    return pl.pallas_call(
        body, name="exchange_chips", in_specs=[ANY], out_specs=ANY,
        out_shape=jax.ShapeDtypeStruct((n, rh, c), s.dtype),
        scratch_shapes=[pltpu.SemaphoreType.DMA((3,)), pltpu.SemaphoreType.DMA((3,)), pltpu.SemaphoreType.DMA],
    )(s)


def pair_swap(tt):
    def body(t_ref, out_ref, send_sem, recv_sem):
        x, y, cc = _my_place()
        cp = pltpu.make_async_remote_copy(
            src_ref=t_ref, dst_ref=out_ref, send_sem=send_sem, recv_sem=recv_sem,
            device_id=(x, y, 1 - cc), device_id_type=MESH)
        cp.start()
        cp.wait()

    return pl.pallas_call(
        body, name="pair_swap", in_specs=[ANY], out_specs=ANY,
        out_shape=jax.ShapeDtypeStruct(tt.shape, tt.dtype),
        scratch_shapes=[pltpu.SemaphoreType.DMA, pltpu.SemaphoreType.DMA],
    )(tt)


SMALL_ROWS = 16


def allreduce_small(v):
    r, c = v.shape
    vm = pl.BlockSpec(memory_space=pltpu.VMEM)

    def body(v_ref, out_ref, slots, send_sems, recv_sems):
        x, y, cc = _my_place()
        me = 4 * x + 2 * y + cc
        slots[me] = v_ref[...]
        copies = []
        for kbits in range(1, 8):
            peer = (_flip(x, kbits >> 2), _flip(y, (kbits >> 1) & 1), _flip(cc, kbits & 1))
            cp = pltpu.make_async_remote_copy(
                src_ref=v_ref, dst_ref=slots.at[me], send_sem=send_sems.at[kbits - 1],
                recv_sem=recv_sems.at[kbits - 1], device_id=peer, device_id_type=MESH)
            cp.start()
            copies.append(cp)
        for cp in copies:
            cp.wait()
        total = slots[0]
        for dev in range(1, 8):
            total = total + slots[dev]
        out_ref[...] = total

    return pl.pallas_call(
        body, name="allreduce_small", in_specs=[vm], out_specs=vm,
        out_shape=jax.ShapeDtypeStruct((r, c), F32),
        scratch_shapes=[pltpu.VMEM((8, r, c), F32), pltpu.SemaphoreType.DMA((7,)), pltpu.SemaphoreType.DMA((7,))],
    )(v)


def add_pair(mine, theirs, name):
    return rowwise(lambda a, b: ((a + b,), ()), [mine, theirs], [], [(mine.shape[1], BF16)], name=name)[0]


def sum_chips(r4, name):
    _, rh, c = r4.shape
    tr = _pick(rh, (256, 128, 64, 32, 16))

    def body(r_ref, o_ref):
        total = r_ref[0].astype(F32)
        for j in range(1, N_CHIPS):
            total = total + r_ref[j].astype(F32)
        o_ref[...] = total

    return pl.pallas_call(
        body, name=name, grid=(rh // tr,),
        in_specs=[pl.BlockSpec((N_CHIPS, tr, c), lambda i: (0, i, 0))],
        out_specs=pl.BlockSpec((tr, c), lambda i: (i, 0)),
        out_shape=jax.ShapeDtypeStruct((rh, c), F32),
        compiler_params=_cparams(("parallel",)),
    )(r4)


def _pad_rows(a, rows):
    return jnp.pad(a, ((0, rows - a.shape[0]), (0, 0))) if rows != a.shape[0] else a


def pack_shards(local):
    parts = []
    for name, layer, r, c, _ in _PACK:
        flat = local[name][layer].astype(BF16).reshape(r * c // D_MODEL, D_MODEL)
        parts.append(_pad_rows(flat, _pack_rows(r, c)))
    used = sum(p.shape[0] for p in parts)
    parts.append(jnp.zeros((PACK_ROWS - used, D_MODEL), BF16))
    return jnp.concatenate(parts, axis=0)


def unpack_full(gathered):
    full, off = {}, 0
    for name, layer, r, c, by_cols in _PACK:
        n = r * c // D_MODEL
        blk = gathered[:, off:off + n, :].reshape(N_CHIPS, r, c)
        if by_cols:
            full[(name, layer)] = blk.transpose(1, 0, 2).reshape(r, N_CHIPS * c)
        else:
            full[(name, layer)] = blk.reshape(N_CHIPS * r, c)
        off += _pack_rows(r, c)
    return full


def pack_grads(grads):
    parts = []
    for name, layer, r, c, by_cols in _PACK:
        g = grads[(name, layer)]
        if by_cols:
            blk = g.reshape(r, N_CHIPS, c).transpose(1, 0, 2)
        else:
            blk = g.reshape(N_CHIPS, r, c)
        flat = blk.reshape(N_CHIPS, r * c // D_MODEL, D_MODEL)
        rows = _pack_rows(r, c)
        if rows != flat.shape[1]:
            flat = jnp.pad(flat, ((0, 0), (0, rows - flat.shape[1]), (0, 0)))
        parts.append(flat)
    used = sum(p.shape[1] for p in parts)
    parts.append(jnp.zeros((N_CHIPS, PACK_ROWS - used, D_MODEL), F32))
    return jnp.concatenate(parts, axis=1)


def unpack_local(flat):
    out, off = {}, 0
    for name, layer, r, c, _ in _PACK:
        n = r * c // D_MODEL
        out[(name, layer)] = flat[off:off + n, :].reshape(r, c)
        off += _pack_rows(r, c)
    return out


def rope_tables(pos):
    half = HEAD_DIM // 2
    lane = jnp.arange(128)
    inv = ROPE_THETA ** (-(lane % half).astype(F32) / half)
    ang = pos.astype(F32)[:, None] * inv[None, :]
    sign = jnp.where((lane % HEAD_DIM) < half, -1.0, 1.0).astype(F32)
    return jnp.cos(ang), jnp.sin(ang) * sign[None, :]


def local_step(x, p, pos, tgt, norm_mix, norm_ffn, norm_ple, norm_final, ev_b_f, od_sinks, wfull):
    t = x.shape[0]
    w_in0 = wfull[("ev_w_in", 0)]
    w_in0 = jnp.concatenate([w_in0, jnp.zeros((D_MODEL, EVEN_IN_PAD - w_in0.shape[1]), w_in0.dtype)], axis=1)
    w_gu = [jnp.concatenate([wfull[("ffn_w_gate", i)], wfull[("ffn_w_up", i)]], axis=1) for i in range(2)]
    b_pad = jnp.zeros((1, GATE_PAD), F32).at[0, :N_FOX].set(ev_b_f[0])
    sinks_b = jnp.broadcast_to(od_sinks[0][:, None, None], (N_Q, 1, 128)).astype(F32)
    cosw, sinw = rope_tables(pos)

    saved = []
    h = x
    for i in range(2):
        s = {"h0": h}
        n1 = rmsnorm_fwd(h, norm_mix[i], f"norm_mix_fwd{i}")
        s["n1"] = n1
        if i == 0:
            proj = matmul(n1, w_in0, name="ev_in")
            cols = [proj[:, j * FOX_W:(j + 1) * FOX_W] for j in range(6)]
            flog = proj[:, EVEN_QKV:]
            cum = forget_cumsum(flog, b_pad, "forget_cumsum")
            fox = dict(zip(("qt", "q", "k", "f_end", "k_norm"), fox_operands(cols[0], cols[1], cum[:, :N_FOX].T)))
            fox.update(kt=head_tiles(cols[1].astype(BF16), N_FOX), v=head_rows(cols[2].astype(BF16), N_FOX),
                       vt=head_tiles(cols[2].astype(BF16), N_FOX))
            q_sb = (cols[3] * SCALE).astype(BF16)
            sb = dict(qt=head_cols(q_sb, N_SB), q=head_rows(q_sb, N_SB), k=head_rows(cols[4].astype(BF16), N_SB),
                      kt=head_tiles(cols[4].astype(BF16), N_SB), v=head_rows(cols[5].astype(BF16), N_SB),
                      vt=head_tiles(cols[5].astype(BF16), N_SB))
            fox["ot"], fox["lse"], fox["first"] = fox_fwd(fox["qt"], fox["k"], fox["vt"], fox["f_end"], fox["k_norm"])
            sb["ot"], sb["ltot"], sb["first"] = sb_fwd(sb["qt"], sb["k"], sb["vt"])
            s.update(flog=flog, fox=fox, sb=sb)
            mixin_t = jnp.concatenate([fox["ot"].reshape(FOX_W, t), sb["ot"].reshape(SB_W, t)], axis=0).astype(BF16)
            w_out = wfull[("ev_w_out", 0)]
        else:
            proj = matmul(n1, wfull[("od_w_in", 0)], name="od_in")
            qk = rope_apply(proj[:, :Q_W + KV_W], cosw, sinw, False, "rope_fwd")
            q_sc = (qk[:, :Q_W] * SCALE).astype(BF16)
            k_b = qk[:, Q_W:].astype(BF16)
            v_b = proj[:, Q_W + KV_W:].astype(BF16)
            swa = dict(qt=head_cols(q_sc, N_Q), q=head_rows(q_sc, N_Q), k=head_rows(k_b, N_KV),
                       kt=head_tiles(k_b, N_KV), v=head_rows(v_b, N_KV), vt=head_tiles(v_b, N_KV))
            swa["ot"], swa["lse"] = swa_fwd(swa["qt"], swa["k"], swa["vt"], sinks_b)
            s["swa"] = swa
            mixin_t = swa["ot"].reshape(Q_W, t).astype(BF16)
            w_out = wfull[("od_w_out", 0)]
        s["mixin_t"] = mixin_t
        h = matmul(mixin_t, w_out, "tn", residual=h, name=f"mix_out{i}")
        s["h1"] = h
        n2 = rmsnorm_fwd(h, norm_ffn[i], f"norm_ffn_fwd{i}")
        gu = matmul(n2, w_gu[i], name=f"ffn_gu{i}")
        act = swiglu_fwd(gu, f"swiglu_fwd{i}")
        s.update(n2=n2, gu=gu, act=act)
        h = matmul(act, wfull[("ffn_w_down", i)], residual=h, name=f"ffn_down{i}")
        s["h2"] = h
        n3 = rmsnorm_fwd(h, norm_ple[i], f"norm_ple_fwd{i}")
        pre = matmul(n3, wfull[("ple_w_gate", i)], name=f"ple_gate{i}")
        pp = matmul(p[i], wfull[("ple_w_proj", i)], name=f"ple_proj{i}")
        s.update(n3=n3, pre=pre, pp=pp)
        h = ple_fwd(h, pre, pp, f"ple_fwd{i}")
        saved.append(s)

    dh, dg_final, loss8 = loss_head(h, norm_final, tgt, "loss_head")
    gw = {}
    small = {"norm_final": dg_final, "loss": loss8}
    for i in (1, 0):
        s = saved[i]
        dpre, dpp = ple_bwd(dh, s["pre"], s["pp"], f"ple_bwd{i}")
        gw[("ple_w_gate", i)] = matmul(s["n3"], dpre, "tn", name=f"d_ple_gate{i}")
        gw[("ple_w_proj", i)] = matmul(p[i], dpp, "tn", name=f"d_ple_proj{i}")
        dn3 = matmul(dpre, wfull[("ple_w_gate", i)], "nt", name=f"dn_ple{i}")
        dh, small[("norm_ple", i)] = rmsnorm_bwd(s["h2"], norm_ple[i], dn3, dh, f"norm_ple_bwd{i}")

        dact = matmul(dh, wfull[("ffn_w_down", i)], "nt", name=f"d_act{i}")
        gw[("ffn_w_down", i)] = matmul(s["act"], dh, "tn", name=f"d_ffn_down{i}")
        dgu = swiglu_bwd(s["gu"], dact, f"swiglu_bwd{i}")
        dwgu = matmul(s["n2"], dgu, "tn", name=f"d_ffn_gu{i}")
        gw[("ffn_w_gate", i)] = dwgu[:, :D_FF]
        gw[("ffn_w_up", i)] = dwgu[:, D_FF:]
        dn2 = matmul(dgu, w_gu[i], "nt", name=f"dn_ffn{i}")
        dh, small[("norm_ffn", i)] = rmsnorm_bwd(s["h1"], norm_ffn[i], dn2, dh, f"norm_ffn_bwd{i}")

        if i == 0:
            dmix = matmul(dh, wfull[("ev_w_out", 0)], "nt", name="d_mix0")
            gw[("ev_w_out", 0)] = matmul(s["mixin_t"], dh, name="d_ev_out")
            fox, sb = s["fox"], s["sb"]
            dmb = dmix.astype(BF16)
            dqt_f, dk_aug, dv_f, ds_rows = fox_bwd(
                fox["qt"], fox["q"], fox["k"], fox["kt"], fox["v"], fox["ot"],
                head_rows(dmb[:, :FOX_W], N_FOX), head_cols(dmb[:, :FOX_W], N_FOX), fox["lse"], fox["first"])
            dqt_s, dk_s, dv_s = sb_bwd(sb["qt"], sb["q"], sb["k"], sb["kt"], sb["v"], sb["ltot"], sb["first"],
                                       head_rows(dmb[:, FOX_W:], N_SB), head_cols(dmb[:, FOX_W:], N_SB))
            ds_cols = jnp.sum(dk_aug[:, :, HEAD_DIM:HEAD_DIM + N_FOX], axis=0)
            d_cum = jnp.pad(ds_rows[:, 0, :].T - ds_cols, ((0, 0), (0, GATE_PAD - N_FOX)))
            dflog, db8 = forget_cumsum_bwd(d_cum, s["flog"], b_pad, "forget_cumsum_bwd")
            small["ev_b_f"] = db8
            parts = (cols_to_flat(dqt_f), rows_to_flat(dk_aug[:, :, :HEAD_DIM]), rows_to_flat(dv_f),
                     cols_to_flat(dqt_s), rows_to_flat(dk_s), rows_to_flat(dv_s), dflog)
            dproj = jnp.concatenate([a.astype(BF16) for a in parts], axis=1)
            dw = matmul(s["n1"], dproj, "tn", name="d_ev_in")
            gw[("ev_w_in", 0)] = dw[:, :EVEN_QKV + N_FOX]
            dn1 = matmul(dproj, w_in0, "nt", name="dn_mix0")
        else:
            dmix = matmul(dh, wfull[("od_w_out", 0)], "nt", name="d_mix1")
            gw[("od_w_out", 0)] = matmul(s["mixin_t"], dh, name="d_od_out")
            swa = s["swa"]
            dmb = dmix.astype(BF16)
            dqt, dk, dv, dsink = swa_bwd(swa["qt"], swa["q"], swa["k"], swa["kt"], swa["v"], sinks_b, swa["ot"],
                                         head_rows(dmb, N_Q), head_cols(dmb, N_Q), swa["lse"])
            small["od_sinks"] = dsink
            dqk = rope_apply(jnp.concatenate([cols_to_flat(dqt), rows_to_flat(dk)], axis=1), cosw, sinw, True,
                             "rope_bwd")
            dproj = jnp.concatenate([dqk, rows_to_flat(dv)], axis=1).astype(BF16)
            gw[("od_w_in", 0)] = matmul(s["n1"], dproj, "tn", name="d_od_in")
            dn1 = matmul(dproj, wfull[("od_w_in", 0)], "nt", name="dn_mix1")
        dh, small[("norm_mix", i)] = rmsnorm_bwd(s["h0"], norm_mix[i], dn1, dh, f"norm_mix_bwd{i}")
    return dh, gw, small


_SMALL_ROWS = (("norm_mix", 0), ("norm_mix", 1), ("norm_ffn", 0), ("norm_ffn", 1),
               ("norm_ple", 0), ("norm_ple", 1), "norm_final", "misc", "loss")


def pack_small(small):
    rows = []
    for key in _SMALL_ROWS:
        if key == "misc":
            db = jnp.sum(small["ev_b_f"], axis=0)[:N_FOX]
            dsink = jnp.sum(small["od_sinks"][:, 0, :], axis=1)
            rows.append(jnp.zeros((D_MODEL,), F32).at[:N_FOX].set(db).at[128:128 + N_Q].set(dsink))
        else:
            rows.append(jnp.sum(small[key], axis=0))
    rows += [jnp.zeros((D_MODEL,), F32)] * (SMALL_ROWS - len(rows))
    return jnp.stack(rows)


def kernel(x, p, positions, norm_mix, norm_ffn, norm_ple, norm_final, ev_w_in, ev_b_f, ev_w_out, od_w_in, od_sinks, od_w_out, ffn_w_gate, ffn_w_up, ffn_w_down, ple_w_proj, ple_w_gate, loss_target, m_norm_mix, m_norm_ffn, m_norm_ple, m_norm_final, m_ev_w_in, m_ev_b_f, m_ev_w_out, m_od_w_in, m_od_sinks, m_od_w_out, m_ffn_w_gate, m_ffn_w_up, m_ffn_w_down, m_ple_w_proj, m_ple_w_gate, v_norm_mix, v_norm_ffn, v_norm_ple, v_norm_final, v_ev_w_in, v_ev_b_f, v_ev_w_out, v_od_w_in, v_od_sinks, v_od_w_out, v_ffn_w_gate, v_ffn_w_up, v_ffn_w_down, v_ple_w_proj, v_ple_w_gate):
    local_w = dict(ev_w_in=ev_w_in, ev_w_out=ev_w_out, od_w_in=od_w_in, od_w_out=od_w_out,
                   ffn_w_gate=ffn_w_gate, ffn_w_up=ffn_w_up, ffn_w_down=ffn_w_down,
                   ple_w_proj=ple_w_proj, ple_w_gate=ple_w_gate)
    local_m = dict(ev_w_in=m_ev_w_in, ev_w_out=m_ev_w_out, od_w_in=m_od_w_in, od_w_out=m_od_w_out,
                   ffn_w_gate=m_ffn_w_gate, ffn_w_up=m_ffn_w_up, ffn_w_down=m_ffn_w_down,
                   ple_w_proj=m_ple_w_proj, ple_w_gate=m_ple_w_gate)
    local_v = dict(ev_w_in=v_ev_w_in, ev_w_out=v_ev_w_out, od_w_in=v_od_w_in, od_w_out=v_od_w_out,
                   ffn_w_gate=v_ffn_w_gate, ffn_w_up=v_ffn_w_up, ffn_w_down=v_ffn_w_down,
                   ple_w_proj=v_ple_w_proj, ple_w_gate=v_ple_w_gate)

    wfull = unpack_full(allgather_chips(pack_shards(local_w)))
    grad_x, gw, small = local_step(x[0], p[:, 0], positions[0], loss_target[0], norm_mix, norm_ffn, norm_ple,
                                   norm_final, ev_b_f, od_sinks, wfull)

    cc = lax.axis_index("c")
    g4 = pack_grads(gw).reshape(N_CHIPS, 2, PACK_ROWS_HALF, D_MODEL)
    theirs = pair_send_other_half(g4)
    mine = lax.dynamic_index_in_dim(g4, cc, axis=1, keepdims=False)
    pair_sum = add_pair(mine.reshape(N_CHIPS * PACK_ROWS_HALF, D_MODEL),
                        theirs.reshape(N_CHIPS * PACK_ROWS_HALF, D_MODEL), "add_pair")
    from_chips = exchange_chips(pair_sum.reshape(N_CHIPS, PACK_ROWS_HALF, D_MODEL))
    half_sum = sum_chips(from_chips, "sum_chips")
    other_half = pair_swap(half_sum)
    low = jnp.where(cc == 0, half_sum, other_half)
    high = jnp.where(cc == 0, other_half, half_sum)
    g_local = unpack_local(jnp.concatenate([low, high], axis=0))

    red = allreduce_small(pack_small(small))
    loss = 0.5 * jnp.sum(red[8]) / D_MODEL
    pad_small = lambda a: jnp.zeros((D_MODEL,), F32).at[:N_FOX].set(a[0][0]).at[128:128 + N_Q].set(a[1][0])
    stack_small = lambda a: jnp.concatenate(
        [a[0], a[1], a[2], a[3][None], pad_small(a[4:6])[None], jnp.zeros((SMALL_ROWS - 8, D_MODEL), F32)], axis=0)
    w_small = stack_small((norm_mix, norm_ffn, norm_ple, norm_final, ev_b_f, od_sinks))
    m_small = stack_small((m_norm_mix, m_norm_ffn, m_norm_ple, m_norm_final, m_ev_b_f, m_od_sinks))
    v_small = stack_small((v_norm_mix, v_norm_ffn, v_norm_ple, v_norm_final, v_ev_b_f, v_od_sinks))
    g_small = red.at[8].set(0.0)
    upd_small = (g_small,) + tuple(adamw(w_small, g_small, m_small, v_small, "adamw_small"))

    def split_small(a):
        return (a[0:2], a[2:4], a[4:6], a[6], a[7, :N_FOX][None], a[7, 128:128 + N_Q][None])

    small_out = [split_small(a) for a in upd_small]

    big_names = ("ev_w_in", "ev_w_out", "od_w_in", "od_w_out", "ffn_w_gate", "ffn_w_up", "ffn_w_down",
                 "ple_w_proj", "ple_w_gate")
    big_out = {}
    for name in big_names:
        w = local_w[name]
        layers, r, c = w.shape
        g = jnp.concatenate([g_local[(name, i)] for i in range(layers)], axis=0)
        res = adamw(w.reshape(layers * r, c), g, local_m[name].reshape(layers * r, c),
                    local_v[name].reshape(layers * r, c), f"adamw_{name}")
        big_out[name] = [a.reshape(layers, r, c) for a in (g,) + tuple(res)]

    outs = [loss, grad_x[None]]
    for kind in range(4):
        sm = small_out[kind]
        outs += [sm[0], sm[1], sm[2], sm[3],
                 big_out["ev_w_in"][kind], sm[4], big_out["ev_w_out"][kind],
                 big_out["od_w_in"][kind], sm[5], big_out["od_w_out"][kind],
                 big_out["ffn_w_gate"][kind], big_out["ffn_w_up"][kind], big_out["ffn_w_down"][kind],
                 big_out["ple_w_proj"][kind], big_out["ple_w_gate"][kind]]
    return tuple(outs)
```

```python
import jax
import jax.numpy as jnp
from jax import lax
from jax.experimental import pallas as pl
from jax.experimental.pallas import tpu as pltpu

F32 = jnp.float32
BF16 = jnp.bfloat16

D_MODEL = 1024
HEAD_DIM = 64
N_FOX = 8
N_SB = 8
FOX_W = N_FOX * HEAD_DIM
SB_W = N_SB * HEAD_DIM
EVEN_QKV = 3 * FOX_W + 3 * SB_W
GATE_PAD = 128
EVEN_IN_PAD = EVEN_QKV + GATE_PAD
N_Q = 16
N_KV = 4
GROUP = N_Q // N_KV
Q_W = N_Q * HEAD_DIM
KV_W = N_KV * HEAD_DIM
ODD_IN = Q_W + 2 * KV_W
WINDOW = 128
ROPE_THETA = 10000.0
D_FF = 2816
PLE_DIM = 256
EPS = 1e-6
NEG_INF = -1e30
SCALE = HEAD_DIM ** -0.5

ADAM_LR = 0.001
ADAM_B1 = 0.9
ADAM_B2 = 0.999
ADAM_EPS = 1e-08
ADAM_WD = 0.01
ADAM_STEP = 10

N_CHIPS = 4
VMEM_LIMIT = 48 * 1024 * 1024
MESH = pl.DeviceIdType.MESH

_PACK = (
    ("ev_w_in", 0, 1024, 770, True),
    ("ev_w_out", 0, 256, 1024, False),
    ("od_w_in", 0, 1024, 384, True),
    ("od_w_out", 0, 256, 1024, False),
    ("ffn_w_gate", 0, 1024, 704, True),
    ("ffn_w_gate", 1, 1024, 704, True),
    ("ffn_w_up", 0, 1024, 704, True),
    ("ffn_w_up", 1, 1024, 704, True),
    ("ffn_w_down", 0, 704, 1024, False),
    ("ffn_w_down", 1, 704, 1024, False),
    ("ple_w_proj", 0, 256, 256, True),
    ("ple_w_proj", 1, 256, 256, True),
    ("ple_w_gate", 0, 256, 1024, False),
    ("ple_w_gate", 1, 256, 1024, False),
)
_ROW_ALIGN = 16


def _pack_rows(r, c):
    n = r * c // D_MODEL
    return -(-n // _ROW_ALIGN) * _ROW_ALIGN


PACK_ROWS_HALF = 3328
PACK_ROWS = 2 * PACK_ROWS_HALF
assert sum(_pack_rows(r, c) for _, _, r, c, _ in _PACK) <= PACK_ROWS


def _pick(n, cands):
    for c in cands:
        if n % c == 0:
            return c
    return n


def _cparams(sem):
    return pltpu.CompilerParams(dimension_semantics=sem, vmem_limit_bytes=VMEM_LIMIT)


_DIMS = {
    "nn": (((1,), (0,)), ((), ())),
    "nt": (((1,), (1,)), ((), ())),
    "tn": (((0,), (0,)), ((), ())),
}


def matmul(a, b, mode="nn", out_dtype=F32, residual=None, name="mm"):
    if mode == "nn":
        (m, k), (k2, n) = a.shape, b.shape
    elif mode == "nt":
        (m, k), (n, k2) = a.shape, b.shape
    else:
        (k, m), (k2, n) = a.shape, b.shape
    assert k == k2, (a.shape, b.shape, mode)
    tm = _pick(m, (1024, 1408, 512, 256, 128))
    tn = _pick(n, (1024, 1408, 512, 640, 384, 256, 128))
    tk = _pick(k, (1024, 1408, 640, 512, 256, 128))
    nk = k // tk
    dims = _DIMS[mode]
    has_res = residual is not None

    def body(*refs):
        if has_res:
            a_ref, b_ref, r_ref, o_ref, acc = refs
        else:
            a_ref, b_ref, o_ref, acc = refs
        kk = pl.program_id(2)
        part = lax.dot_general(a_ref[...].astype(BF16), b_ref[...].astype(BF16), dims,
                               preferred_element_type=F32)

        def finish(r):
            if has_res:
                r = r + r_ref[...]
            o_ref[...] = r.astype(out_dtype)

        if nk == 1:
            finish(part)
            return

        @pl.when(kk == 0)
        def _():
            acc[...] = part

        @pl.when((kk > 0) & (kk < nk - 1))
        def _():
            acc[...] += part

        @pl.when(kk == nk - 1)
        def _():
            finish(acc[...] + part)

    if mode == "nn":
        a_spec = pl.BlockSpec((tm, tk), lambda i, j, kk: (i, kk))
        b_spec = pl.BlockSpec((tk, tn), lambda i, j, kk: (kk, j))
    elif mode == "nt":
        a_spec = pl.BlockSpec((tm, tk), lambda i, j, kk: (i, kk))
        b_spec = pl.BlockSpec((tn, tk), lambda i, j, kk: (j, kk))
    else:
        a_spec = pl.BlockSpec((tk, tm), lambda i, j, kk: (kk, i))
        b_spec = pl.BlockSpec((tk, tn), lambda i, j, kk: (kk, j))
    o_spec = pl.BlockSpec((tm, tn), lambda i, j, kk: (i, j))
    in_specs = [a_spec, b_spec] + ([o_spec] if has_res else [])
    args = (a, b) + ((residual,) if has_res else ())
    return pl.pallas_call(
        body, name=name, grid=(m // tm, n // tn, nk),
        in_specs=in_specs, out_specs=o_spec,
        out_shape=jax.ShapeDtypeStruct((m, n), out_dtype),
        scratch_shapes=[pltpu.VMEM((tm, tn), F32)],
        compiler_params=_cparams(("parallel", "parallel", "arbitrary")),
    )(*args)


def _fold8(v):
    r, w = v.shape
    return v.reshape(r // 8, 8, w).sum(axis=0)


def rowwise(fn, rows, bcasts, outs, accs=(), tr=256, name="rowwise", reverse=False):
    t = rows[0].shape[0]
    tr = _pick(t, (tr, 128, 64, 32, 16, 8))
    nr, nb, no, na = len(rows), len(bcasts), len(outs), len(accs)
    steps = t // tr

    def body(*refs):
        ins = [r[...] for r in refs[:nr + nb]]
        out_refs = refs[nr + nb:nr + nb + no]
        acc_refs = refs[nr + nb + no:]
        o, a = fn(*ins)
        for r, v in zip(out_refs, o):
            r[...] = v.astype(r.dtype)
        if na:
            @pl.when(pl.program_id(0) == 0)
            def _():
                for r in acc_refs:
                    r[...] = jnp.zeros_like(r)

            for r, v in zip(acc_refs, a):
                r[...] += v

    if reverse:
        ridx = lambda i: (steps - 1 - i, 0)
    else:
        ridx = lambda i: (i, 0)
    in_specs = [pl.BlockSpec((tr, x.shape[1]), ridx) for x in rows]
    in_specs += [pl.BlockSpec(x.shape, lambda i: (0, 0)) for x in bcasts]
    out_specs = [pl.BlockSpec((tr, w), ridx) for w, _ in outs]
    out_specs += [pl.BlockSpec((8, w), lambda i: (0, 0)) for w in accs]
    out_shape = [jax.ShapeDtypeStruct((t, w), dt) for w, dt in outs]
    out_shape += [jax.ShapeDtypeStruct((8, w), F32) for w in accs]
    res = pl.pallas_call(
        body, name=name, grid=(steps,), in_specs=in_specs, out_specs=out_specs, out_shape=out_shape,
        compiler_params=_cparams(("arbitrary",)),
    )(*rows, *bcasts)
    return res


def _rstd(x):
    return lax.rsqrt(jnp.mean(x * x, axis=-1, keepdims=True) + EPS)


def rmsnorm_fwd(h, g, name):
    def fn(x, gg):
        return ((x * _rstd(x)) * gg,), ()

    return rowwise(fn, [h], [g.reshape(1, -1)], [(D_MODEL, BF16)], name=name)[0]


def _rms_bwd_math(x, gg, dy):
    r = _rstd(x)
    xh = x * r
    u = dy * gg
    dx = r * (u - xh * jnp.mean(u * xh, axis=-1, keepdims=True))
    return dx, dy * xh


def rmsnorm_bwd(h, g, dn, dres, name):
    def fn(x, dy, dr, gg):
        dx, dgp = _rms_bwd_math(x, gg, dy)
        return (dr + dx,), (_fold8(dgp),)

    return rowwise(fn, [h, dn, dres], [g.reshape(1, -1)], [(D_MODEL, F32)], [D_MODEL], name=name)


def loss_head(h, g, tgt, name):
    def fn(x, tg, gg):
        y = (x * _rstd(x)) * gg
        e = y - tg
        dy = e * (1.0 / D_MODEL)
        dx, dgp = _rms_bwd_math(x, gg, dy)
        return (dx,), (_fold8(dgp), _fold8(e * e))

    return rowwise(fn, [h, tgt], [g.reshape(1, -1)], [(D_MODEL, F32)], [D_MODEL, D_MODEL], name=name)


def _sigmoid(x):
    return 1.0 / (1.0 + jnp.exp(-x))


def swiglu_fwd(gu, name):
    def fn(x):
        gg, uu = x[:, :D_FF], x[:, D_FF:]
        return ((gg * _sigmoid(gg)) * uu,), ()

    return rowwise(fn, [gu], [], [(D_FF, BF16)], name=name)[0]


def swiglu_bwd(gu, da, name):
    def fn(x, d):
        gg, uu = x[:, :D_FF], x[:, D_FF:]
        s = _sigmoid(gg)
        silu = gg * s
        dgg = d * uu * (s + silu * (1.0 - s))
        duu = d * silu
        return (jnp.concatenate([dgg, duu], axis=1),), ()

    return rowwise(fn, [gu, da], [], [(2 * D_FF, BF16)], name=name)[0]


def ple_fwd(h, pre, pp, name):
    def fn(x, a, b):
        return (x + _sigmoid(a) * b,), ()

    return rowwise(fn, [h, pre, pp], [], [(D_MODEL, F32)], name=name)[0]


def ple_bwd(dh, pre, pp, name):
    def fn(d, a, b):
        s = _sigmoid(a)
        return (d * b * s * (1.0 - s), d * s), ()

    return rowwise(fn, [dh, pre, pp], [], [(D_MODEL, BF16), (D_MODEL, BF16)], name=name)


def _rot_half_partner(x, first_half):
    w = x.shape[1]
    return jnp.where(first_half, pltpu.roll(x, w - HEAD_DIM // 2, 1), pltpu.roll(x, HEAD_DIM // 2, 1))


def rope_apply(xx, cosw, sinw, backward, name):
    width = xx.shape[1]
    reps = width // 128

    def fn(x, c, s):
        cw = jnp.tile(c, (1, reps))
        sw = jnp.tile(s, (1, reps))
        lane = lax.broadcasted_iota(jnp.int32, x.shape, 1)
        first = (lane % HEAD_DIM) < (HEAD_DIM // 2)
        if backward:
            return (x * cw + _rot_half_partner(x * sw, first),), ()
        return (x * cw + _rot_half_partner(x, first) * sw,), ()

    return rowwise(fn, [xx, cosw, sinw], [], [(width, F32)], name=name)[0]


def _log_sigmoid(x):
    return jnp.minimum(x, 0.0) - jnp.log(1.0 + jnp.exp(-jnp.abs(x)))


CUM_BLOCK = 256


def forget_cumsum(flog, bias, name):
    t = flog.shape[0]
    tb = _pick(t, (CUM_BLOCK,))

    def body(x_ref, b_ref, o_ref, carry):
        @pl.when(pl.program_id(0) == 0)
        def _():
            carry[...] = jnp.zeros_like(carry)

        lf = _log_sigmoid(x_ref[...] + b_ref[...])
        r = lax.broadcasted_iota(jnp.int32, (tb, tb), 0)
        c = lax.broadcasted_iota(jnp.int32, (tb, tb), 1)
        tri = (c <= r).astype(F32)
        cum = jnp.dot(tri, lf, preferred_element_type=F32, precision=lax.Precision.HIGHEST) + carry[...]
        o_ref[...] = cum
        carry[...] = cum[tb - 1:tb, :]

    return pl.pallas_call(
        body, name=name, grid=(t // tb,),
        in_specs=[pl.BlockSpec((tb, GATE_PAD), lambda i: (i, 0)), pl.BlockSpec((1, GATE_PAD), lambda i: (0, 0))],
        out_specs=pl.BlockSpec((tb, GATE_PAD), lambda i: (i, 0)),
        out_shape=jax.ShapeDtypeStruct((t, GATE_PAD), F32),
        scratch_shapes=[pltpu.VMEM((1, GATE_PAD), F32)],
        compiler_params=_cparams(("arbitrary",)),
    )(flog, bias)


def forget_cumsum_bwd(d_cum, flog, bias, name):
    t = flog.shape[0]
    tb = _pick(t, (CUM_BLOCK,))
    nb = t // tb

    def body(d_ref, x_ref, b_ref, o_ref, db_ref, carry):
        @pl.when(pl.program_id(0) == 0)
        def _():
            carry[...] = jnp.zeros_like(carry)
            db_ref[...] = jnp.zeros_like(db_ref)

        r = lax.broadcasted_iota(jnp.int32, (tb, tb), 0)
        c = lax.broadcasted_iota(jnp.int32, (tb, tb), 1)
        tri = (c >= r).astype(F32)
        dlf = jnp.dot(tri, d_ref[...], preferred_element_type=F32, precision=lax.Precision.HIGHEST) + carry[...]
        carry[...] = dlf[0:1, :]
        dx = dlf * (1.0 - _sigmoid(x_ref[...] + b_ref[...]))
        o_ref[...] = dx
        db_ref[...] += _fold8(dx)

    rev = lambda i: (nb - 1 - i, 0)
    return pl.pallas_call(
        body, name=name, grid=(nb,),
        in_specs=[pl.BlockSpec((tb, GATE_PAD), rev), pl.BlockSpec((tb, GATE_PAD), rev),
                  pl.BlockSpec((1, GATE_PAD), lambda i: (0, 0))],
        out_specs=[pl.BlockSpec((tb, GATE_PAD), rev), pl.BlockSpec((8, GATE_PAD), lambda i: (0, 0))],
        out_shape=[jax.ShapeDtypeStruct((t, GATE_PAD), F32), jax.ShapeDtypeStruct((8, GATE_PAD), F32)],
        scratch_shapes=[pltpu.VMEM((1, GATE_PAD), F32)],
        compiler_params=_cparams(("arbitrary",)),
    )(d_cum, flog, bias)


TQ = 512
TK = 128
AUG = 128
N_BIAS = 3
SB_CUTOFF = 110.0
FOX_CUTOFF = 112.0


def _dot(a, b):
    return jnp.dot(a, b, preferred_element_type=F32)


def _rel(shape, d):
    return lax.broadcasted_iota(jnp.int32, shape, 0) - lax.broadcasted_iota(jnp.int32, shape, 1) + d


def _put(x, lo, part):
    return part if lo == 0 else jnp.concatenate([x[:, :lo], part], axis=1)


def _split_bf16(x):
    hi = x.astype(BF16)
    return hi, (x - hi.astype(F32)).astype(BF16)


def _tri_dot(tri, x):
    hi, lo = _split_bf16(x)
    return _dot(tri, hi) + _dot(tri, lo)


def _q_cols(width, tq):
    return pl.BlockSpec((None, width, tq), lambda h, i: (h, 0, i))


def _q_rows(width, tq):
    return pl.BlockSpec((None, tq, width), lambda h, i: (h, i, 0))


def _kv_rows(t, width):
    return pl.BlockSpec((None, t, width), lambda h, i: (h, 0, 0))


def _kv_tiles(nk):
    return pl.BlockSpec((None, nk, HEAD_DIM, TK), lambda h, i: (h, 0, 0, 0))


def _blocks(t):
    tq = TQ if t % TQ == 0 else TK
    return tq, tq // TK, t // TK


def fox_fwd(qt_aug, k_aug, vt, f_end, k_norm):
    nh, _, t = qt_aug.shape
    tq, ratio, nk = _blocks(t)
    lanes = tq // 128

    def body(q_ref, k_ref, v_ref, fe_ref, kn_ref, o_ref, lse_ref, first_ref):
        i = pl.program_id(1)
        qv = q_ref[...]
        qf = qv[:HEAD_DIM].astype(F32)
        reach = jnp.sqrt(jnp.sum(qf * qf, axis=0, keepdims=True)) * jnp.tile(kn_ref[...], (1, lanes))

        def step(first, carry, masked):
            m, l, acc = carry
            scores = []
            for u in range(ratio):
                lo = u * TK if masked else 0
                off = pl.multiple_of((first + u) * TK, TK)
                s = _dot(k_ref[pl.ds(off, TK), :], qv[:, lo:])
                if masked:
                    s = jnp.where(_rel(s.shape, 0) <= 0, s, NEG_INF)
                scores.append((lo, s))
            m_new = m
            for lo, s in scores:
                m_new = _put(m_new, lo, jnp.maximum(m_new[:, lo:], jnp.max(s, axis=0, keepdims=True)))
            alpha = jnp.exp(m - m_new)
            l = alpha * l
            acc = alpha * acc
            for u, (lo, s) in enumerate(scores):
                p = jnp.exp(s - m_new[:, lo:])
                l = _put(l, lo, l[:, lo:] + jnp.sum(p, axis=0, keepdims=True))
                acc = _put(acc, lo, acc[:, lo:] + _dot(v_ref[first + u], p.astype(BF16)))
            return m_new, l, acc

        init = (jnp.full((1, tq), NEG_INF, F32), jnp.zeros((1, tq), F32), jnp.zeros((HEAD_DIM, tq), F32))
        m, l, acc = step(ratio * i, init, True)

        def more(c):
            bound = reach + jnp.tile(fe_ref[jnp.maximum(i - 1 - c[0], 0)], (1, lanes))
            return (c[0] < i) & (jnp.max(bound - c[1]) > -FOX_CUTOFF)

        def sweep(c):
            return (c[0] + 1,) + step(ratio * (i - 1 - c[0]), c[1:], False)

        done, m, l, acc = lax.while_loop(more, sweep, (jnp.int32(0), m, l, acc))
        o_ref[...] = acc / l
        lse_ref[...] = m + jnp.log(l)
        first_ref[...] = jnp.full((1, 128), (i - done).astype(F32), F32)

    return pl.pallas_call(
        body, name="fox_fwd", grid=(nh, t // tq),
        in_specs=[_q_cols(AUG, tq), _kv_rows(t, AUG), _kv_tiles(nk),
                  pl.BlockSpec((None, t // tq, 1, 128), lambda h, i: (h, 0, 0, 0)),
                  pl.BlockSpec((None, 1, 128), lambda h, i: (h, 0, 0))],
        out_specs=[_q_cols(HEAD_DIM, tq), _q_cols(1, tq), _q_cols(1, 128)],
        out_shape=[jax.ShapeDtypeStruct((nh, HEAD_DIM, t), F32), jax.ShapeDtypeStruct((nh, 1, t), F32),
                   jax.ShapeDtypeStruct((nh, 1, 128 * (t // tq)), F32)],
        compiler_params=_cparams(("parallel", "arbitrary")),
    )(qt_aug, k_aug, vt, f_end, k_norm)


def fox_bwd(qt_aug, q_aug, k_aug, kt, v, ot, do, dot_, lse, first):
    nh, _, t = qt_aug.shape
    tq, ratio, nk = _blocks(t)

    def body(qt_ref, q_ref, k_ref, kt_ref, v_ref, ot_ref, do_ref, dot_ref, lse_ref, first_ref,
             dqt_ref, dk_ref, dv_ref, rs_ref):
        i = pl.program_id(1)
        start = jnp.max(first_ref[...]).astype(jnp.int32)

        @pl.when(i == 0)
        def _():
            dk_ref[...] = jnp.zeros_like(dk_ref)
            dv_ref[...] = jnp.zeros_like(dv_ref)

        qtv = qt_ref[...]
        qv = q_ref[...]
        dob = do_ref[...]
        dotb = dot_ref[...]
        delta = jnp.sum(ot_ref[...] * dotb.astype(F32), axis=0, keepdims=True)
        lse = lse_ref[...]

        def tile(j, carry, lo):
            dqt, rs = carry
            masked = lo is not None
            lo = lo or 0
            off = pl.multiple_of(j * TK, TK)
            s = _dot(k_ref[pl.ds(off, TK), :], qtv[:, lo:])
            p = jnp.exp(s - lse[:, lo:])
            if masked:
                p = jnp.where(_rel(s.shape, 0) <= 0, p, 0.0)
            dp = _dot(v_ref[pl.ds(off, TK), :], dotb[:, lo:])
            dsb = (p * (dp - delta[:, lo:])).astype(BF16)
            dk_ref[pl.ds(off, TK), :] += _dot(dsb, qv[lo:, :])
            dv_ref[pl.ds(off, TK), :] += _dot(p.astype(BF16), dob[lo:, :])
            return (_put(dqt, lo, dqt[:, lo:] + _dot(kt_ref[j], dsb)),
                    _put(rs, lo, rs[:, lo:] + jnp.sum(dsb.astype(F32), axis=0, keepdims=True)))

        def step(first, carry, masked):
            for u in range(ratio):
                carry = tile(first + u, carry, u * TK if masked else None)
            return carry

        carry = (jnp.zeros((HEAD_DIM, tq), F32), jnp.zeros((1, tq), F32))
        carry = lax.fori_loop(start, i, lambda jj, c: step(ratio * jj, c, False), carry)
        dqt, rs = step(ratio * i, carry, True)
        dqt_ref[...] = dqt * SCALE
        rs_ref[...] = rs

    return pl.pallas_call(
        body, name="fox_bwd", grid=(nh, t // tq),
        in_specs=[_q_cols(AUG, tq), _q_rows(AUG, tq), _kv_rows(t, AUG), _kv_tiles(nk), _kv_rows(t, HEAD_DIM),
                  _q_cols(HEAD_DIM, tq), _q_rows(HEAD_DIM, tq), _q_cols(HEAD_DIM, tq), _q_cols(1, tq),
                  _q_cols(1, 128)],
        out_specs=[_q_cols(HEAD_DIM, tq), _kv_rows(t, AUG), _kv_rows(t, HEAD_DIM), _q_cols(1, tq)],
        out_shape=[jax.ShapeDtypeStruct((nh, HEAD_DIM, t), F32), jax.ShapeDtypeStruct((nh, t, AUG), F32),
                   jax.ShapeDtypeStruct((nh, t, HEAD_DIM), F32), jax.ShapeDtypeStruct((nh, 1, t), F32)],
        compiler_params=_cparams(("arbitrary", "arbitrary")),
    )(qt_aug, q_aug, k_aug, kt, v, ot, do, dot_, lse, first)


def _sb_logits(kb, qv, ok):
    z = _dot(kb, qv)
    e = jnp.exp(-jnp.abs(z))
    ll = -(jnp.maximum(z, 0.0) + jnp.log(1.0 + e))
    if ok is not None:
        ll = jnp.where(ok, ll, 0.0)
    return z, e, ll


def _tri(cmp):
    r = lax.broadcasted_iota(jnp.int32, (TK, TK), 0)
    c = lax.broadcasted_iota(jnp.int32, (TK, TK), 1)
    return cmp(r, c).astype(BF16)


def sb_fwd(qt, k, vt):
    nh, _, t = qt.shape
    tq, ratio, nk = _blocks(t)

    def body(q_ref, k_ref, v_ref, o_ref, tot_ref, first_ref):
        i = pl.program_id(1)
        qv = q_ref[...]
        tri_after = _tri(lambda r, c: c > r)

        def tile(j, carry, lo):
            c_l, acc = carry
            masked = lo is not None
            lo = lo or 0
            off = pl.multiple_of(j * TK, TK)
            ok = _rel((TK, tq - lo), 0) < 0 if masked else None
            z, _, ll = _sb_logits(k_ref[pl.ds(off, TK), :], qv[:, lo:], ok)
            a = jnp.exp(z + ll + _tri_dot(tri_after, ll) + c_l[:, lo:])
            if masked:
                a = jnp.where(ok, a, 0.0)
            return (_put(c_l, lo, c_l[:, lo:] + jnp.sum(ll, axis=0, keepdims=True)),
                    _put(acc, lo, acc[:, lo:] + _dot(v_ref[j], a.astype(BF16))))

        def step(first, carry, masked):
            for u in reversed(range(ratio)):
                carry = tile(first + u, carry, u * TK if masked else None)
            return carry

        c_l, acc = step(ratio * i, (jnp.zeros((1, tq), F32), jnp.zeros((HEAD_DIM, tq), F32)), True)

        def more(c):
            return (c[0] < i) & (jnp.max(c[1]) > -SB_CUTOFF)

        def sweep(c):
            c_l, acc = step(ratio * (i - 1 - c[0]), (c[1], c[2]), False)
            return c[0] + 1, c_l, acc

        done, c_l, acc = lax.while_loop(more, sweep, (jnp.int32(0), c_l, acc))
        o_ref[...] = acc
        tot_ref[...] = c_l
        first_ref[...] = jnp.full((1, 128), (i - done).astype(F32), F32)

    return pl.pallas_call(
        body, name="sb_fwd", grid=(nh, t // tq),
        in_specs=[_q_cols(HEAD_DIM, tq), _kv_rows(t, HEAD_DIM), _kv_tiles(nk)],
        out_specs=[_q_cols(HEAD_DIM, tq), _q_cols(1, tq), _q_cols(1, 128)],
        out_shape=[jax.ShapeDtypeStruct((nh, HEAD_DIM, t), F32), jax.ShapeDtypeStruct((nh, 1, t), F32),
                   jax.ShapeDtypeStruct((nh, 1, 128 * (t // tq)), F32)],
        compiler_params=_cparams(("parallel", "arbitrary")),
    )(qt, k, vt)


def sb_bwd(qt, q, k, kt, v, ltot, first, do, dot_):
    nh, _, t = qt.shape
    tq, ratio, nk = _blocks(t)

    def body(qt_ref, q_ref, k_ref, kt_ref, v_ref, tot_ref, first_ref, do_ref, dot_ref, dqt_ref, dk_ref, dv_ref):
        i = pl.program_id(1)
        start = jnp.max(first_ref[...]).astype(jnp.int32)

        @pl.when(i == 0)
        def _():
            dk_ref[...] = jnp.zeros_like(dk_ref)
            dv_ref[...] = jnp.zeros_like(dv_ref)

        qtv = qt_ref[...]
        qv = q_ref[...]
        dob = do_ref[...]
        dotb = dot_ref[...]
        tri_upto = _tri(lambda r, c: c <= r)
        tri_before = _tri(lambda r, c: c < r)

        def tile(j, carry, lo):
            rest, c_w, dqt = carry
            masked = lo is not None
            lo = lo or 0
            off = pl.multiple_of(j * TK, TK)
            ok = _rel((TK, tq - lo), 0) < 0 if masked else None
            z, e, ll = _sb_logits(k_ref[pl.ds(off, TK), :], qtv[:, lo:], ok)
            a = jnp.exp(z + ll + (rest[:, lo:] - _tri_dot(tri_upto, ll)))
            if masked:
                a = jnp.where(ok, a, 0.0)
            w = a * _dot(v_ref[pl.ds(off, TK), :], dotb[:, lo:])
            before = _tri_dot(tri_before, w) + c_w[:, lo:]
            r = 1.0 / (1.0 + e)
            sig = jnp.where(z >= 0.0, r, e * r)
            dz = w * (1.0 - sig) - before * sig
            if masked:
                dz = jnp.where(ok, dz, 0.0)
            dzb = dz.astype(BF16)
            dk_ref[pl.ds(off, TK), :] += _dot(dzb, qv[lo:, :])
            dv_ref[pl.ds(off, TK), :] += _dot(a.astype(BF16), dob[lo:, :])
            return (_put(rest, lo, rest[:, lo:] - jnp.sum(ll, axis=0, keepdims=True)),
                    _put(c_w, lo, c_w[:, lo:] + jnp.sum(w, axis=0, keepdims=True)),
                    _put(dqt, lo, dqt[:, lo:] + _dot(kt_ref[j], dzb)))

        def step(first, carry, masked):
            for u in range(ratio):
                carry = tile(first + u, carry, u * TK if masked else None)
            return carry

        carry = (tot_ref[...], jnp.zeros((1, tq), F32), jnp.zeros((HEAD_DIM, tq), F32))
        carry = lax.fori_loop(start, i, lambda jj, c: step(ratio * jj, c, False), carry)
        dqt_ref[...] = step(ratio * i, carry, True)[2] * SCALE

    hd = HEAD_DIM
    return pl.pallas_call(
        body, name="sb_bwd", grid=(nh, t // tq),
        in_specs=[_q_cols(hd, tq), _q_rows(hd, tq), _kv_rows(t, hd), _kv_tiles(nk), _kv_rows(t, hd),
                  _q_cols(1, tq), _q_cols(1, 128), _q_rows(hd, tq), _q_cols(hd, tq)],
        out_specs=[_q_cols(hd, tq), _kv_rows(t, hd), _kv_rows(t, hd)],
        out_shape=[jax.ShapeDtypeStruct((nh, hd, t), F32), jax.ShapeDtypeStruct((nh, t, hd), F32),
                   jax.ShapeDtypeStruct((nh, t, hd), F32)],
        compiler_params=_cparams(("arbitrary", "arbitrary")),
    )(qt, q, k, kt, v, ltot, first, do, dot_)


def _swa_q_cols(width):
    return pl.BlockSpec((GROUP, width, WINDOW), lambda g, i: (g, 0, i))


def _swa_q_rows():
    return pl.BlockSpec((GROUP, WINDOW, HEAD_DIM), lambda g, i: (g, i, 0))


def _swa_kv_rows(t):
    return pl.BlockSpec((None, t, HEAD_DIM), lambda g, i: (g, 0, 0))


def _swa_kv_tiles(nk):
    return pl.BlockSpec((None, nk, HEAD_DIM, WINDOW), lambda g, i: (g, 0, 0, 0))


def _swa_sink_spec():
    return pl.BlockSpec((GROUP, 1, 128), lambda g, i: (g, 0, 0))


def _lane_cat(parts):
    return jnp.concatenate(parts, axis=1)


def _swa_window(i):
    jb = jnp.maximum(i - 1, 0)
    start = pl.multiple_of(jb * WINDOW, WINDOW)
    shape = (2 * WINDOW, GROUP * WINDOW)
    query = lax.broadcasted_iota(jnp.int32, shape, 1) % WINDOW
    rel = lax.broadcasted_iota(jnp.int32, shape, 0) - query + (start - i * WINDOW)
    return jb, start, (rel <= 0) & (rel > -WINDOW)


def swa_fwd(qt, k, vt, sinks):
    nh, _, t = qt.shape
    nk = t // WINDOW

    def body(q_ref, k_ref, v_ref, s_ref, o_ref, lse_ref):
        i = pl.program_id(1)
        jb, start, valid = _swa_window(i)
        qv = _lane_cat([q_ref[g] for g in range(GROUP)])
        sink = _lane_cat([s_ref[g] for g in range(GROUP)])
        s = jnp.where(valid, _dot(k_ref[pl.ds(start, 2 * WINDOW), :], qv), NEG_INF)
        m = jnp.maximum(jnp.max(s, axis=0, keepdims=True), sink)
        p = jnp.where(valid, jnp.exp(s - m), 0.0)
        l = jnp.sum(p, axis=0, keepdims=True) + jnp.exp(sink - m)
        pb = p.astype(BF16)
        o = (_dot(v_ref[jb], pb[:WINDOW]) + _dot(v_ref[jb + 1], pb[WINDOW:])) / l
        lse = m + jnp.log(l)
        for g in range(GROUP):
            o_ref[g] = o[:, g * WINDOW:(g + 1) * WINDOW]
            lse_ref[g] = lse[:, g * WINDOW:(g + 1) * WINDOW]

    return pl.pallas_call(
        body, name="swa_fwd", grid=(N_KV, nk),
        in_specs=[_swa_q_cols(HEAD_DIM), _swa_kv_rows(t), _swa_kv_tiles(nk), _swa_sink_spec()],
        out_specs=[_swa_q_cols(HEAD_DIM), _swa_q_cols(1)],
        out_shape=[jax.ShapeDtypeStruct((nh, HEAD_DIM, t), F32), jax.ShapeDtypeStruct((nh, 1, t), F32)],
        compiler_params=_cparams(("parallel", "arbitrary")),
    )(qt, k, vt, sinks)


def swa_bwd(qt, q, k, kt, v, sinks, ot, do, dot_, lse):
    nh, _, t = qt.shape
    nk = t // WINDOW

    def body(qt_ref, q_ref, k_ref, kt_ref, v_ref, s_ref, ot_ref, do_ref, dot_ref, lse_ref,
             dqt_ref, dk_ref, dv_ref, dsink_ref):
        i = pl.program_id(1)

        @pl.when(i == 0)
        def _():
            dk_ref[...] = jnp.zeros_like(dk_ref)
            dv_ref[...] = jnp.zeros_like(dv_ref)
            dsink_ref[...] = jnp.zeros_like(dsink_ref)

        jb, start, valid = _swa_window(i)
        heads = range(GROUP)
        qtv = _lane_cat([qt_ref[g] for g in heads])
        dotb = _lane_cat([dot_ref[g] for g in heads])
        lse = _lane_cat([lse_ref[g] for g in heads])
        sink = _lane_cat([s_ref[g] for g in heads])
        otv = _lane_cat([ot_ref[g] for g in heads])
        q_rows = jnp.concatenate([q_ref[g] for g in heads], axis=0)
        do_rows = jnp.concatenate([do_ref[g] for g in heads], axis=0)
        delta = jnp.sum(otv * dotb.astype(F32), axis=0, keepdims=True)
        p = jnp.where(valid, jnp.exp(_dot(k_ref[pl.ds(start, 2 * WINDOW), :], qtv) - lse), 0.0)
        dsb = (p * (_dot(v_ref[pl.ds(start, 2 * WINDOW), :], dotb) - delta)).astype(BF16)
        dqt = (_dot(kt_ref[jb], dsb[:WINDOW]) + _dot(kt_ref[jb + 1], dsb[WINDOW:])) * SCALE
        dsink = -jnp.exp(sink - lse) * delta
        for g in heads:
            dqt_ref[g] = dqt[:, g * WINDOW:(g + 1) * WINDOW]
            dsink_ref[g] += dsink[:, g * WINDOW:(g + 1) * WINDOW]
        dk_ref[pl.ds(start, 2 * WINDOW), :] += _dot(dsb, q_rows)
        dv_ref[pl.ds(start, 2 * WINDOW), :] += _dot(p.astype(BF16), do_rows)

    hd = HEAD_DIM
    return pl.pallas_call(
        body, name="swa_bwd", grid=(N_KV, nk),
        in_specs=[_swa_q_cols(hd), _swa_q_rows(), _swa_kv_rows(t), _swa_kv_tiles(nk), _swa_kv_rows(t),
                  _swa_sink_spec(), _swa_q_cols(hd), _swa_q_rows(), _swa_q_cols(hd), _swa_q_cols(1)],
        out_specs=[_swa_q_cols(hd), _swa_kv_rows(t), _swa_kv_rows(t), _swa_sink_spec()],
        out_shape=[jax.ShapeDtypeStruct((nh, hd, t), F32), jax.ShapeDtypeStruct((N_KV, t, hd), F32),
                   jax.ShapeDtypeStruct((N_KV, t, hd), F32), jax.ShapeDtypeStruct((nh, 1, 128), F32)],
        compiler_params=_cparams(("arbitrary", "arbitrary")),
    )(qt, q, k, kt, v, sinks, ot, do, dot_, lse)


def head_rows(a, nh):
    t = a.shape[0]
    return a.reshape(t, nh, HEAD_DIM).transpose(1, 0, 2)


def head_cols(a, nh):
    t = a.shape[0]
    return a.reshape(t, nh, HEAD_DIM).transpose(1, 2, 0)


def head_tiles(a, nh):
    t = a.shape[0]
    return a.reshape(t // TK, TK, nh, HEAD_DIM).transpose(2, 0, 3, 1)


def rows_to_flat(a):
    nh, t, _ = a.shape
    return a.transpose(1, 0, 2).reshape(t, nh * HEAD_DIM)


def cols_to_flat(a):
    nh, _, t = a.shape
    return a.transpose(2, 0, 1).reshape(t, nh * HEAD_DIM)


def fox_keys(proj, cum, name):
    t = proj.shape[0]
    tr = _pick(t, (256, 128))
    width = (1 + N_BIAS) * 128

    def body(k_ref, c_ref, o_ref):
        kb = k_ref[...].astype(BF16)
        terms, rest = [], -c_ref[...]
        for _ in range(N_BIAS):
            term = rest.astype(BF16)
            terms.append(term)
            rest = rest - term.astype(F32)
        row = lax.broadcasted_iota(jnp.int32, (width, AUG), 0)
        col = lax.broadcasted_iota(jnp.int32, (width, AUG), 1)
        for h in range(N_FOX):
            pair = kb[:, 128 * (h // 2):128 * (h // 2 + 1)]
            place = (row < 128) & (col < HEAD_DIM) & (row - HEAD_DIM * (h % 2) == col)
            for b in range(N_BIAS):
                place = place | ((row == 128 * (1 + b) + h) & (col == HEAD_DIM + b))
            src = jnp.concatenate([pair] + terms, axis=1)
            o_ref[h] = _dot(src, place.astype(BF16)).astype(BF16)

    return pl.pallas_call(
        body, name=name, grid=(t // tr,),
        in_specs=[pl.BlockSpec((tr, FOX_W), lambda i: (i, 1)), pl.BlockSpec((tr, GATE_PAD), lambda i: (i, 0))],
        out_specs=pl.BlockSpec((N_FOX, tr, AUG), lambda i: (0, i, 0)),
        out_shape=jax.ShapeDtypeStruct((N_FOX, t, AUG), BF16),
        compiler_params=_cparams(("parallel",)),
    )(proj, cum)


def fox_operands(qf, kf, cum_heads):
    t = qf.shape[0]
    nh = cum_heads.shape[0]
    tq = _blocks(t)[0]
    qs = (qf * SCALE).astype(BF16)
    ones_t = jnp.ones((nh, N_BIAS, t), BF16)
    qt_aug = jnp.concatenate([head_cols(qs, nh), ones_t, jnp.zeros((nh, AUG - HEAD_DIM - N_BIAS, t), BF16)], axis=1)
    own_lane = jnp.broadcast_to(jnp.eye(nh, dtype=BF16)[:, None, :], (nh, t, nh))
    q_aug = jnp.concatenate([head_rows(qs, nh), own_lane, jnp.zeros((nh, t, AUG - HEAD_DIM - nh), BF16)], axis=2)
    f_end = jnp.broadcast_to((-cum_heads)[:, tq - 1::tq, None, None], (nh, t // tq, 1, 128))
    k_sq = jnp.sum(jnp.square(kf).reshape(t, nh, HEAD_DIM), axis=2)
    k_norm = jnp.broadcast_to(1.01 * jnp.sqrt(jnp.max(k_sq, axis=0))[:, None, None], (nh, 1, 128))
    return qt_aug, q_aug, f_end, k_norm


def adamw(w, g, m, v, name):
    cols = w.shape[1]

    def fn(ww, gg, mm, vv):
        mn = ADAM_B1 * mm + (1.0 - ADAM_B1) * gg
        vn = ADAM_B2 * vv + (1.0 - ADAM_B2) * (gg * gg)
        m_hat = mn / (1.0 - ADAM_B1 ** ADAM_STEP)
        v_hat = vn / (1.0 - ADAM_B2 ** ADAM_STEP)
        delta = -ADAM_LR * (m_hat / (jnp.sqrt(v_hat) + ADAM_EPS) + ADAM_WD * ww)
        return (delta, mn, vn), ()

    return rowwise(fn, [w, g, m, v], [], [(cols, F32)] * 3, name=name)


ANY = pl.BlockSpec(memory_space=pl.ANY)


def _my_place():
    return lax.axis_index("x"), lax.axis_index("y"), lax.axis_index("c")


def _flip(coord, bit):
    return 1 - coord if bit else coord


def allgather_chips(w):
    r, c = w.shape
    rh = r // 2

    def body(w_ref, out_ref, send_sems, recv_sems, local_sem):
        x, y, cc = _my_place()
        me = 2 * x + y
        sibling = (x, y, 1 - cc)
        chips = [(_flip(x, kbits >> 1), _flip(y, kbits & 1)) for kbits in (1, 2, 3)]

        def half(chip, hc):
            return out_ref.at[chip, pl.ds(pl.multiple_of(hc * rh, 16), rh)]

        def copy(k, src, dst, to):
            return pltpu.make_async_remote_copy(src_ref=src, dst_ref=dst, send_sem=send_sems.at[k],
                                                recv_sem=recv_sems.at[k], device_id=to, device_id_type=MESH)

        local = pltpu.make_async_copy(w_ref, out_ref.at[me], local_sem)
        local.start()
        my_half = w_ref.at[pl.ds(pl.multiple_of(cc * rh, 16), rh)]
        first = [copy(j, my_half, half(me, cc), (px, py, cc)) for j, (px, py) in enumerate(chips)]
        for cp in first:
            cp.start()
        passed = []
        for j, (px, py) in enumerate(chips):
            landed = half(2 * px + py, cc)
            copy(j, my_half, landed, (px, py, cc)).wait_recv()
            fwd = copy(3 + j, landed, landed, sibling)
            fwd.start()
            passed.append(fwd)
        for j, (px, py) in enumerate(chips):
            theirs = half(2 * px + py, 1 - cc)
            copy(3 + j, theirs, theirs, sibling).wait_recv()
        for cp in first + passed:
            cp.wait_send()
        local.wait()

    return pl.pallas_call(
        body, name="allgather_chips", in_specs=[ANY], out_specs=ANY,
        out_shape=jax.ShapeDtypeStruct((N_CHIPS, r, c), w.dtype),
        scratch_shapes=[pltpu.SemaphoreType.DMA((6,)), pltpu.SemaphoreType.DMA((6,)), pltpu.SemaphoreType.DMA],
    )(w)


def pair_send_other_half(g):
    n, _, rh, c = g.shape

    def body(g_ref, out_ref, send_sem, recv_sem):
        x, y, cc = _my_place()
        cp = pltpu.make_async_remote_copy(
            src_ref=g_ref.at[:, 1 - cc], dst_ref=out_ref, send_sem=send_sem, recv_sem=recv_sem,
            device_id=(x, y, 1 - cc), device_id_type=MESH)
        cp.start()
        cp.wait()

    return pl.pallas_call(
        body, name="pair_send_other_half", in_specs=[ANY], out_specs=ANY,
        out_shape=jax.ShapeDtypeStruct((n, rh, c), g.dtype),
        scratch_shapes=[pltpu.SemaphoreType.DMA, pltpu.SemaphoreType.DMA],
    )(g)


def exchange_chips(s):
    n, rh, c = s.shape

    def body(s_ref, out_ref, send_sems, recv_sems, local_sem):
        x, y, cc = _my_place()
        me = 2 * x + y
        local = pltpu.make_async_copy(s_ref.at[me], out_ref.at[me], local_sem)
        local.start()
        copies = []
        for kbits in (1, 2, 3):
            px, py = _flip(x, kbits >> 1), _flip(y, kbits & 1)
            cp = pltpu.make_async_remote_copy(
                src_ref=s_ref.at[2 * px + py], dst_ref=out_ref.at[me], send_sem=send_sems.at[kbits - 1],
                recv_sem=recv_sems.at[kbits - 1], device_id=(px, py, cc), device_id_type=MESH)
            cp.start()
            copies.append(cp)
        for cp in copies:
            cp.wait()
        local.wait()

    return pl.pallas_call(
        body, name="exchange_chips", in_specs=[ANY], out_specs=ANY,
        out_shape=jax.ShapeDtypeStruct((n, rh, c), s.dtype),
        scratch_shapes=[pltpu.SemaphoreType.DMA((3,)), pltpu.SemaphoreType.DMA((3,)), pltpu.SemaphoreType.DMA],
    )(s)


def pair_swap(tt):
    def body(t_ref, out_ref, send_sem, recv_sem):
        x, y, cc = _my_place()
        cp = pltpu.make_async_remote_copy(
            src_ref=t_ref, dst_ref=out_ref, send_sem=send_sem, recv_sem=recv_sem,
            device_id=(x, y, 1 - cc), device_id_type=MESH)
        cp.start()
        cp.wait()

    return pl.pallas_call(
        body, name="pair_swap", in_specs=[ANY], out_specs=ANY,
        out_shape=jax.ShapeDtypeStruct(tt.shape, tt.dtype),
        scratch_shapes=[pltpu.SemaphoreType.DMA, pltpu.SemaphoreType.DMA],
    )(tt)


SMALL_ROWS = 16


def allreduce_small(v):
    r, c = v.shape
    vm = pl.BlockSpec(memory_space=pltpu.VMEM)

    def body(v_ref, out_ref, slots, send_sems, recv_sems):
        x, y, cc = _my_place()
        me = 4 * x + 2 * y + cc
        slots[me] = v_ref[...]
        copies = []
        for kbits in range(1, 8):
            peer = (_flip(x, kbits >> 2), _flip(y, (kbits >> 1) & 1), _flip(cc, kbits & 1))
            cp = pltpu.make_async_remote_copy(
                src_ref=v_ref, dst_ref=slots.at[me], send_sem=send_sems.at[kbits - 1],
                recv_sem=recv_sems.at[kbits - 1], device_id=peer, device_id_type=MESH)
            cp.start()
            copies.append(cp)
        for cp in copies:
            cp.wait()
        total = slots[0]
        for dev in range(1, 8):
            total = total + slots[dev]
        out_ref[...] = total

    return pl.pallas_call(
        body, name="allreduce_small", in_specs=[vm], out_specs=vm,
        out_shape=jax.ShapeDtypeStruct((r, c), F32),
        scratch_shapes=[pltpu.VMEM((8, r, c), F32), pltpu.SemaphoreType.DMA((7,)), pltpu.SemaphoreType.DMA((7,))],
    )(v)


def add_pair(mine, theirs, name):
    return rowwise(lambda a, b: ((a + b,), ()), [mine, theirs], [], [(mine.shape[1], BF16)], name=name)[0]


def sum_chips(r4, name):
    _, rh, c = r4.shape
    tr = _pick(rh, (256, 128, 64, 32, 16))

    def body(r_ref, o_ref):
        total = r_ref[0].astype(F32)
        for j in range(1, N_CHIPS):
            total = total + r_ref[j].astype(F32)
        o_ref[...] = total

    return pl.pallas_call(
        body, name=name, grid=(rh // tr,),
        in_specs=[pl.BlockSpec((N_CHIPS, tr, c), lambda i: (0, i, 0))],
        out_specs=pl.BlockSpec((tr, c), lambda i: (i, 0)),
        out_shape=jax.ShapeDtypeStruct((rh, c), F32),
        compiler_params=_cparams(("parallel",)),
    )(r4)


def _pad_rows(a, rows):
    return jnp.pad(a, ((0, rows - a.shape[0]), (0, 0))) if rows != a.shape[0] else a


def pack_shards(local):
    parts = []
    for name, layer, r, c, _ in _PACK:
        flat = local[name][layer].astype(BF16).reshape(r * c // D_MODEL, D_MODEL)
        parts.append(_pad_rows(flat, _pack_rows(r, c)))
    used = sum(p.shape[0] for p in parts)
    parts.append(jnp.zeros((PACK_ROWS - used, D_MODEL), BF16))
    return jnp.concatenate(parts, axis=0)


def unpack_full(gathered):
    full, off = {}, 0
    for name, layer, r, c, by_cols in _PACK:
        n = r * c // D_MODEL
        blk = gathered[:, off:off + n, :].reshape(N_CHIPS, r, c)
        if by_cols:
            full[(name, layer)] = blk.transpose(1, 0, 2).reshape(r, N_CHIPS * c)
        else:
            full[(name, layer)] = blk.reshape(N_CHIPS * r, c)
        off += _pack_rows(r, c)
    return full


def pack_grads(grads):
    parts = []
    for name, layer, r, c, by_cols in _PACK:
        g = grads[(name, layer)]
        if by_cols:
            blk = g.reshape(r, N_CHIPS, c).transpose(1, 0, 2)
        else:
            blk = g.reshape(N_CHIPS, r, c)
        flat = blk.reshape(N_CHIPS, r * c // D_MODEL, D_MODEL)
        rows = _pack_rows(r, c)
        if rows != flat.shape[1]:
            flat = jnp.pad(flat, ((0, 0), (0, rows - flat.shape[1]), (0, 0)))
        parts.append(flat)
    used = sum(p.shape[1] for p in parts)
    parts.append(jnp.zeros((N_CHIPS, PACK_ROWS - used, D_MODEL), F32))
    return jnp.concatenate(parts, axis=1)


def unpack_local(flat):
    out, off = {}, 0
    for name, layer, r, c, _ in _PACK:
        n = r * c // D_MODEL
        out[(name, layer)] = flat[off:off + n, :].reshape(r, c)
        off += _pack_rows(r, c)
    return out


def rope_tables(pos):
    half = HEAD_DIM // 2
    lane = jnp.arange(128)
    inv = ROPE_THETA ** (-(lane % half).astype(F32) / half)
    ang = pos.astype(F32)[:, None] * inv[None, :]
    sign = jnp.where((lane % HEAD_DIM) < half, -1.0, 1.0).astype(F32)
    return jnp.cos(ang), jnp.sin(ang) * sign[None, :]


def local_step(x, p, pos, tgt, norm_mix, norm_ffn, norm_ple, norm_final, ev_b_f, od_sinks, wfull):
    t = x.shape[0]
    w_in0 = wfull[("ev_w_in", 0)]
    w_in0 = jnp.concatenate([w_in0, jnp.zeros((D_MODEL, EVEN_IN_PAD - w_in0.shape[1]), w_in0.dtype)], axis=1)
    w_gu = [jnp.concatenate([wfull[("ffn_w_gate", i)], wfull[("ffn_w_up", i)]], axis=1) for i in range(2)]
    b_pad = jnp.zeros((1, GATE_PAD), F32).at[0, :N_FOX].set(ev_b_f[0])
    sinks_b = jnp.broadcast_to(od_sinks[0][:, None, None], (N_Q, 1, 128)).astype(F32)
    cosw, sinw = rope_tables(pos)

    saved = []
    h = x
    for i in range(2):
        s = {"h0": h}
        n1 = rmsnorm_fwd(h, norm_mix[i], f"norm_mix_fwd{i}")
        s["n1"] = n1
        if i == 0:
            proj = matmul(n1, w_in0, name="ev_in")
            cols = [proj[:, j * FOX_W:(j + 1) * FOX_W] for j in range(6)]
            flog = proj[:, EVEN_QKV:]
            cum = forget_cumsum(flog, b_pad, "forget_cumsum")
            fox = dict(zip(("qt", "q", "f_end", "k_norm"), fox_operands(cols[0], cols[1], cum[:, :N_FOX].T)))
            fox["k"] = fox_keys(proj, cum, "fox_keys")
            fox.update(kt=head_tiles(cols[1].astype(BF16), N_FOX), v=head_rows(cols[2].astype(BF16), N_FOX),
                       vt=head_tiles(cols[2].astype(BF16), N_FOX))
            q_sb = (cols[3] * SCALE).astype(BF16)
            sb = dict(qt=head_cols(q_sb, N_SB), q=head_rows(q_sb, N_SB), k=head_rows(cols[4].astype(BF16), N_SB),
                      kt=head_tiles(cols[4].astype(BF16), N_SB), v=head_rows(cols[5].astype(BF16), N_SB),
                      vt=head_tiles(cols[5].astype(BF16), N_SB))
            fox["ot"], fox["lse"], fox["first"] = fox_fwd(fox["qt"], fox["k"], fox["vt"], fox["f_end"], fox["k_norm"])
            sb["ot"], sb["ltot"], sb["first"] = sb_fwd(sb["qt"], sb["k"], sb["vt"])
            s.update(flog=flog, fox=fox, sb=sb)
            mixin_t = jnp.concatenate([fox["ot"].reshape(FOX_W, t), sb["ot"].reshape(SB_W, t)], axis=0).astype(BF16)
            w_out = wfull[("ev_w_out", 0)]
        else:
            proj = matmul(n1, wfull[("od_w_in", 0)], name="od_in")
            qk = rope_apply(proj[:, :Q_W + KV_W], cosw, sinw, False, "rope_fwd")
            q_sc = (qk[:, :Q_W] * SCALE).astype(BF16)
            k_b = qk[:, Q_W:].astype(BF16)
            v_b = proj[:, Q_W + KV_W:].astype(BF16)
            swa = dict(qt=head_cols(q_sc, N_Q), q=head_rows(q_sc, N_Q), k=head_rows(k_b, N_KV),
                       kt=head_tiles(k_b, N_KV), v=head_rows(v_b, N_KV), vt=head_tiles(v_b, N_KV))
            swa["ot"], swa["lse"] = swa_fwd(swa["qt"], swa["k"], swa["vt"], sinks_b)
            s["swa"] = swa
            mixin_t = swa["ot"].reshape(Q_W, t).astype(BF16)
            w_out = wfull[("od_w_out", 0)]
        s["mixin_t"] = mixin_t
        h = matmul(mixin_t, w_out, "tn", residual=h, name=f"mix_out{i}")
        s["h1"] = h
        n2 = rmsnorm_fwd(h, norm_ffn[i], f"norm_ffn_fwd{i}")
        gu = matmul(n2, w_gu[i], name=f"ffn_gu{i}")
        act = swiglu_fwd(gu, f"swiglu_fwd{i}")
        s.update(n2=n2, gu=gu, act=act)
        h = matmul(act, wfull[("ffn_w_down", i)], residual=h, name=f"ffn_down{i}")
        s["h2"] = h
        n3 = rmsnorm_fwd(h, norm_ple[i], f"norm_ple_fwd{i}")
        pre = matmul(n3, wfull[("ple_w_gate", i)], name=f"ple_gate{i}")
        pp = matmul(p[i], wfull[("ple_w_proj", i)], name=f"ple_proj{i}")
        s.update(n3=n3, pre=pre, pp=pp)
        h = ple_fwd(h, pre, pp, f"ple_fwd{i}")
        saved.append(s)

    dh, dg_final, loss8 = loss_head(h, norm_final, tgt, "loss_head")
    gw = {}
    small = {"norm_final": dg_final, "loss": loss8}
    for i in (1, 0):
        s = saved[i]
        dpre, dpp = ple_bwd(dh, s["pre"], s["pp"], f"ple_bwd{i}")
        gw[("ple_w_gate", i)] = matmul(s["n3"], dpre, "tn", name=f"d_ple_gate{i}")
        gw[("ple_w_proj", i)] = matmul(p[i], dpp, "tn", name=f"d_ple_proj{i}")
        dn3 = matmul(dpre, wfull[("ple_w_gate", i)], "nt", name=f"dn_ple{i}")
        dh, small[("norm_ple", i)] = rmsnorm_bwd(s["h2"], norm_ple[i], dn3, dh, f"norm_ple_bwd{i}")

        dact = matmul(dh, wfull[("ffn_w_down", i)], "nt", name=f"d_act{i}")
        gw[("ffn_w_down", i)] = matmul(s["act"], dh, "tn", name=f"d_ffn_down{i}")
        dgu = swiglu_bwd(s["gu"], dact, f"swiglu_bwd{i}")
        dwgu = matmul(s["n2"], dgu, "tn", name=f"d_ffn_gu{i}")
        gw[("ffn_w_gate", i)] = dwgu[:, :D_FF]
        gw[("ffn_w_up", i)] = dwgu[:, D_FF:]
        dn2 = matmul(dgu, w_gu[i], "nt", name=f"dn_ffn{i}")
        dh, small[("norm_ffn", i)] = rmsnorm_bwd(s["h1"], norm_ffn[i], dn2, dh, f"norm_ffn_bwd{i}")

        if i == 0:
            dmix = matmul(dh, wfull[("ev_w_out", 0)], "nt", name="d_mix0")
            gw[("ev_w_out", 0)] = matmul(s["mixin_t"], dh, name="d_ev_out")
            fox, sb = s["fox"], s["sb"]
            dmb = dmix.astype(BF16)
            dqt_f, dk_aug, dv_f, ds_rows = fox_bwd(
                fox["qt"], fox["q"], fox["k"], fox["kt"], fox["v"], fox["ot"],
                head_rows(dmb[:, :FOX_W], N_FOX), head_cols(dmb[:, :FOX_W], N_FOX), fox["lse"], fox["first"])
            dqt_s, dk_s, dv_s = sb_bwd(sb["qt"], sb["q"], sb["k"], sb["kt"], sb["v"], sb["ltot"], sb["first"],
                                       head_rows(dmb[:, FOX_W:], N_SB), head_cols(dmb[:, FOX_W:], N_SB))
            ds_cols = jnp.sum(dk_aug[:, :, HEAD_DIM:HEAD_DIM + N_FOX], axis=0)
            d_cum = jnp.pad(ds_rows[:, 0, :].T - ds_cols, ((0, 0), (0, GATE_PAD - N_FOX)))
            dflog, db8 = forget_cumsum_bwd(d_cum, s["flog"], b_pad, "forget_cumsum_bwd")
            small["ev_b_f"] = db8
            parts = (cols_to_flat(dqt_f), rows_to_flat(dk_aug[:, :, :HEAD_DIM]), rows_to_flat(dv_f),
                     cols_to_flat(dqt_s), rows_to_flat(dk_s), rows_to_flat(dv_s), dflog)
            dproj = jnp.concatenate([a.astype(BF16) for a in parts], axis=1)
            dw = matmul(s["n1"], dproj, "tn", name="d_ev_in")
            gw[("ev_w_in", 0)] = dw[:, :EVEN_QKV + N_FOX]
            dn1 = matmul(dproj, w_in0, "nt", name="dn_mix0")
        else:
            dmix = matmul(dh, wfull[("od_w_out", 0)], "nt", name="d_mix1")
            gw[("od_w_out", 0)] = matmul(s["mixin_t"], dh, name="d_od_out")
            swa = s["swa"]
            dmb = dmix.astype(BF16)
            dqt, dk, dv, dsink = swa_bwd(swa["qt"], swa["q"], swa["k"], swa["kt"], swa["v"], sinks_b, swa["ot"],
                                         head_rows(dmb, N_Q), head_cols(dmb, N_Q), swa["lse"])
            small["od_sinks"] = dsink
            dqk = rope_apply(jnp.concatenate([cols_to_flat(dqt), rows_to_flat(dk)], axis=1), cosw, sinw, True,
                             "rope_bwd")
            dproj = jnp.concatenate([dqk, rows_to_flat(dv)], axis=1).astype(BF16)
            gw[("od_w_in", 0)] = matmul(s["n1"], dproj, "tn", name="d_od_in")
            dn1 = matmul(dproj, wfull[("od_w_in", 0)], "nt", name="dn_mix1")
        dh, small[("norm_mix", i)] = rmsnorm_bwd(s["h0"], norm_mix[i], dn1, dh, f"norm_mix_bwd{i}")
    return dh, gw, small


_SMALL_ROWS = (("norm_mix", 0), ("norm_mix", 1), ("norm_ffn", 0), ("norm_ffn", 1),
               ("norm_ple", 0), ("norm_ple", 1), "norm_final", "misc", "loss")


def pack_small(small):
    rows = []
    for key in _SMALL_ROWS:
        if key == "misc":
            db = jnp.sum(small["ev_b_f"], axis=0)[:N_FOX]
            dsink = jnp.sum(small["od_sinks"][:, 0, :], axis=1)
            rows.append(jnp.zeros((D_MODEL,), F32).at[:N_FOX].set(db).at[128:128 + N_Q].set(dsink))
        else:
            rows.append(jnp.sum(small[key], axis=0))
    rows += [jnp.zeros((D_MODEL,), F32)] * (SMALL_ROWS - len(rows))
    return jnp.stack(rows)


def kernel(x, p, positions, norm_mix, norm_ffn, norm_ple, norm_final, ev_w_in, ev_b_f, ev_w_out, od_w_in, od_sinks, od_w_out, ffn_w_gate, ffn_w_up, ffn_w_down, ple_w_proj, ple_w_gate, loss_target, m_norm_mix, m_norm_ffn, m_norm_ple, m_norm_final, m_ev_w_in, m_ev_b_f, m_ev_w_out, m_od_w_in, m_od_sinks, m_od_w_out, m_ffn_w_gate, m_ffn_w_up, m_ffn_w_down, m_ple_w_proj, m_ple_w_gate, v_norm_mix, v_norm_ffn, v_norm_ple, v_norm_final, v_ev_w_in, v_ev_b_f, v_ev_w_out, v_od_w_in, v_od_sinks, v_od_w_out, v_ffn_w_gate, v_ffn_w_up, v_ffn_w_down, v_ple_w_proj, v_ple_w_gate):
    local_w = dict(ev_w_in=ev_w_in, ev_w_out=ev_w_out, od_w_in=od_w_in, od_w_out=od_w_out,
                   ffn_w_gate=ffn_w_gate, ffn_w_up=ffn_w_up, ffn_w_down=ffn_w_down,
                   ple_w_proj=ple_w_proj, ple_w_gate=ple_w_gate)
    local_m = dict(ev_w_in=m_ev_w_in, ev_w_out=m_ev_w_out, od_w_in=m_od_w_in, od_w_out=m_od_w_out,
                   ffn_w_gate=m_ffn_w_gate, ffn_w_up=m_ffn_w_up, ffn_w_down=m_ffn_w_down,
                   ple_w_proj=m_ple_w_proj, ple_w_gate=m_ple_w_gate)
    local_v = dict(ev_w_in=v_ev_w_in, ev_w_out=v_ev_w_out, od_w_in=v_od_w_in, od_w_out=v_od_w_out,
                   ffn_w_gate=v_ffn_w_gate, ffn_w_up=v_ffn_w_up, ffn_w_down=v_ffn_w_down,
                   ple_w_proj=v_ple_w_proj, ple_w_gate=v_ple_w_gate)

    wfull = unpack_full(allgather_chips(pack_shards(local_w)))
    grad_x, gw, small = local_step(x[0], p[:, 0], positions[0], loss_target[0], norm_mix, norm_ffn, norm_ple,
                                   norm_final, ev_b_f, od_sinks, wfull)

    cc = lax.axis_index("c")
    g4 = pack_grads(gw).reshape(N_CHIPS, 2, PACK_ROWS_HALF, D_MODEL)
    theirs = pair_send_other_half(g4)
    mine = lax.dynamic_index_in_dim(g4, cc, axis=1, keepdims=False)
    pair_sum = add_pair(mine.reshape(N_CHIPS * PACK_ROWS_HALF, D_MODEL),
                        theirs.reshape(N_CHIPS * PACK_ROWS_HALF, D_MODEL), "add_pair")
    from_chips = exchange_chips(pair_sum.reshape(N_CHIPS, PACK_ROWS_HALF, D_MODEL))
    half_sum = sum_chips(from_chips, "sum_chips")
    other_half = pair_swap(half_sum)
    low = jnp.where(cc == 0, half_sum, other_half)
    high = jnp.where(cc == 0, other_half, half_sum)
    g_local = unpack_local(jnp.concatenate([low, high], axis=0))

    red = allreduce_small(pack_small(small))
    loss = 0.5 * jnp.sum(red[8]) / D_MODEL
    pad_small = lambda a: jnp.zeros((D_MODEL,), F32).at[:N_FOX].set(a[0][0]).at[128:128 + N_Q].set(a[1][0])
    stack_small = lambda a: jnp.concatenate(
        [a[0], a[1], a[2], a[3][None], pad_small(a[4:6])[None], jnp.zeros((SMALL_ROWS - 8, D_MODEL), F32)], axis=0)
    w_small = stack_small((norm_mix, norm_ffn, norm_ple, norm_final, ev_b_f, od_sinks))
    m_small = stack_small((m_norm_mix, m_norm_ffn, m_norm_ple, m_norm_final, m_ev_b_f, m_od_sinks))
    v_small = stack_small((v_norm_mix, v_norm_ffn, v_norm_ple, v_norm_final, v_ev_b_f, v_od_sinks))
    g_small = red.at[8].set(0.0)
    upd_small = (g_small,) + tuple(adamw(w_small, g_small, m_small, v_small, "adamw_small"))

    def split_small(a):
        return (a[0:2], a[2:4], a[4:6], a[6], a[7, :N_FOX][None], a[7, 128:128 + N_Q][None])

    small_out = [split_small(a) for a in upd_small]

    big_names = ("ev_w_in", "ev_w_out", "od_w_in", "od_w_out", "ffn_w_gate", "ffn_w_up", "ffn_w_down",
                 "ple_w_proj", "ple_w_gate")
    big_out = {}
    for name in big_names:
        w = local_w[name]
        layers, r, c = w.shape
        g = jnp.concatenate([g_local[(name, i)] for i in range(layers)], axis=0)
        res = adamw(w.reshape(layers * r, c), g, local_m[name].reshape(layers * r, c),
                    local_v[name].reshape(layers * r, c), f"adamw_{name}")
        big_out[name] = [a.reshape(layers, r, c) for a in (g,) + tuple(res)]

    outs = [loss, grad_x[None]]
    for kind in range(4):
        sm = small_out[kind]
        outs += [sm[0], sm[1], sm[2], sm[3],
                 big_out["ev_w_in"][kind], sm[4], big_out["ev_w_out"][kind],
                 big_out["od_w_in"][kind], sm[5], big_out["od_w_out"][kind],
                 big_out["ffn_w_gate"][kind], big_out["ffn_w_up"][kind], big_out["ffn_w_down"][kind],
                 big_out["ple_w_proj"][kind], big_out["ple_w_gate"][kind]]
    return tuple(outs)
```

```python
import jax
import jax.numpy as jnp
from jax import lax
from jax.experimental import pallas as pl
from jax.experimental.pallas import tpu as pltpu

F32 = jnp.float32
BF16 = jnp.bfloat16

D_MODEL = 1024
HEAD_DIM = 64
N_FOX = 8
N_SB = 8
FOX_W = N_FOX * HEAD_DIM
SB_W = N_SB * HEAD_DIM
EVEN_QKV = 3 * FOX_W + 3 * SB_W
GATE_PAD = 128
EVEN_IN_PAD = EVEN_QKV + GATE_PAD
N_Q = 16
N_KV = 4
GROUP = N_Q // N_KV
Q_W = N_Q * HEAD_DIM
KV_W = N_KV * HEAD_DIM
ODD_IN = Q_W + 2 * KV_W
WINDOW = 128
ROPE_THETA = 10000.0
D_FF = 2816
PLE_DIM = 256
EPS = 1e-6
NEG_INF = -1e30
SCALE = HEAD_DIM ** -0.5

ADAM_LR = 0.001
ADAM_B1 = 0.9
ADAM_B2 = 0.999
ADAM_EPS = 1e-08
ADAM_WD = 0.01
ADAM_STEP = 10

N_CHIPS = 4
VMEM_LIMIT = 48 * 1024 * 1024
MESH = pl.DeviceIdType.MESH

_PACK = (
    ("ev_w_in", 0, 1024, 770, True),
    ("ev_w_out", 0, 256, 1024, False),
    ("od_w_in", 0, 1024, 384, True),
    ("od_w_out", 0, 256, 1024, False),
    ("ffn_w_gate", 0, 1024, 704, True),
    ("ffn_w_gate", 1, 1024, 704, True),
    ("ffn_w_up", 0, 1024, 704, True),
    ("ffn_w_up", 1, 1024, 704, True),
    ("ffn_w_down", 0, 704, 1024, False),
    ("ffn_w_down", 1, 704, 1024, False),
    ("ple_w_proj", 0, 256, 256, True),
    ("ple_w_proj", 1, 256, 256, True),
    ("ple_w_gate", 0, 256, 1024, False),
    ("ple_w_gate", 1, 256, 1024, False),
)
_ROW_ALIGN = 16


def _pack_rows(r, c):
    n = r * c // D_MODEL
    return -(-n // _ROW_ALIGN) * _ROW_ALIGN


PACK_ROWS_HALF = 3328
PACK_ROWS = 2 * PACK_ROWS_HALF
assert sum(_pack_rows(r, c) for _, _, r, c, _ in _PACK) <= PACK_ROWS


def _pick(n, cands):
    for c in cands:
        if n % c == 0:
            return c
    return n


def _cparams(sem):
    return pltpu.CompilerParams(dimension_semantics=sem, vmem_limit_bytes=VMEM_LIMIT)


_DIMS = {
    "nn": (((1,), (0,)), ((), ())),
    "nt": (((1,), (1,)), ((), ())),
    "tn": (((0,), (0,)), ((), ())),
}


def matmul(a, b, mode="nn", out_dtype=F32, residual=None, name="mm"):
    if mode == "nn":
        (m, k), (k2, n) = a.shape, b.shape
    elif mode == "nt":
        (m, k), (n, k2) = a.shape, b.shape
    else:
        (k, m), (k2, n) = a.shape, b.shape
    assert k == k2, (a.shape, b.shape, mode)
    tm = _pick(m, (1024, 1408, 512, 256, 128))
    tn = _pick(n, (1024, 1408, 512, 640, 384, 256, 128))
    tk = _pick(k, (1024, 1408, 640, 512, 256, 128))
    nk = k // tk
    dims = _DIMS[mode]
    has_res = residual is not None

    def body(*refs):
        if has_res:
            a_ref, b_ref, r_ref, o_ref, acc = refs
        else:
            a_ref, b_ref, o_ref, acc = refs
        kk = pl.program_id(2)
        part = lax.dot_general(a_ref[...].astype(BF16), b_ref[...].astype(BF16), dims,
                               preferred_element_type=F32)

        def finish(r):
            if has_res:
                r = r + r_ref[...]
            o_ref[...] = r.astype(out_dtype)

        if nk == 1:
            finish(part)
            return

        @pl.when(kk == 0)
        def _():
            acc[...] = part

        @pl.when((kk > 0) & (kk < nk - 1))
        def _():
            acc[...] += part

        @pl.when(kk == nk - 1)
        def _():
            finish(acc[...] + part)

    if mode == "nn":
        a_spec = pl.BlockSpec((tm, tk), lambda i, j, kk: (i, kk))
        b_spec = pl.BlockSpec((tk, tn), lambda i, j, kk: (kk, j))
    elif mode == "nt":
        a_spec = pl.BlockSpec((tm, tk), lambda i, j, kk: (i, kk))
        b_spec = pl.BlockSpec((tn, tk), lambda i, j, kk: (j, kk))
    else:
        a_spec = pl.BlockSpec((tk, tm), lambda i, j, kk: (kk, i))
        b_spec = pl.BlockSpec((tk, tn), lambda i, j, kk: (kk, j))
    o_spec = pl.BlockSpec((tm, tn), lambda i, j, kk: (i, j))
    in_specs = [a_spec, b_spec] + ([o_spec] if has_res else [])
    args = (a, b) + ((residual,) if has_res else ())
    return pl.pallas_call(
        body, name=name, grid=(m // tm, n // tn, nk),
        in_specs=in_specs, out_specs=o_spec,
        out_shape=jax.ShapeDtypeStruct((m, n), out_dtype),
        scratch_shapes=[pltpu.VMEM((tm, tn), F32)],
        compiler_params=_cparams(("parallel", "parallel", "arbitrary")),
    )(*args)


def _fold8(v):
    r, w = v.shape
    return v.reshape(r // 8, 8, w).sum(axis=0)


ROW_BLOCK_BYTES = 12 * 1024 * 1024


def rowwise(fn, rows, bcasts, outs, accs=(), name="rowwise", reverse=False):
    t = rows[0].shape[0]
    row_bytes = sum(x.shape[1] * x.dtype.itemsize for x in rows) + sum(w * jnp.dtype(dt).itemsize for w, dt in outs)
    tr = _pick(t, tuple(c for c in (512, 256, 128, 64, 32, 16, 8) if c * row_bytes <= ROW_BLOCK_BYTES or c == 8))
    nr, nb, no, na = len(rows), len(bcasts), len(outs), len(accs)
    steps = t // tr

    def body(*refs):
        ins = [r[...] for r in refs[:nr + nb]]
        out_refs = refs[nr + nb:nr + nb + no]
        acc_refs = refs[nr + nb + no:]
        o, a = fn(*ins)
        for r, v in zip(out_refs, o):
            r[...] = v.astype(r.dtype)
        if na:
            @pl.when(pl.program_id(0) == 0)
            def _():
                for r in acc_refs:
                    r[...] = jnp.zeros_like(r)

            for r, v in zip(acc_refs, a):
                r[...] += v

    if reverse:
        ridx = lambda i: (steps - 1 - i, 0)
    else:
        ridx = lambda i: (i, 0)
    in_specs = [pl.BlockSpec((tr, x.shape[1]), ridx) for x in rows]
    in_specs += [pl.BlockSpec(x.shape, lambda i: (0, 0)) for x in bcasts]
    out_specs = [pl.BlockSpec((tr, w), ridx) for w, _ in outs]
    out_specs += [pl.BlockSpec((8, w), lambda i: (0, 0)) for w in accs]
    out_shape = [jax.ShapeDtypeStruct((t, w), dt) for w, dt in outs]
    out_shape += [jax.ShapeDtypeStruct((8, w), F32) for w in accs]
    res = pl.pallas_call(
        body, name=name, grid=(steps,), in_specs=in_specs, out_specs=out_specs, out_shape=out_shape,
        compiler_params=_cparams(("arbitrary",)),
    )(*rows, *bcasts)
    return res


def _rstd(x):
    return lax.rsqrt(jnp.mean(x * x, axis=-1, keepdims=True) + EPS)


def rmsnorm_fwd(h, g, name):
    def fn(x, gg):
        return ((x * _rstd(x)) * gg,), ()

    return rowwise(fn, [h], [g.reshape(1, -1)], [(D_MODEL, BF16)], name=name)[0]


def _rms_bwd_math(x, gg, dy):
    r = _rstd(x)
    xh = x * r
    u = dy * gg
    dx = r * (u - xh * jnp.mean(u * xh, axis=-1, keepdims=True))
    return dx, dy * xh


def rmsnorm_bwd(h, g, dn, dres, name):
    def fn(x, dy, dr, gg):
        dx, dgp = _rms_bwd_math(x, gg, dy)
        return (dr + dx,), (_fold8(dgp),)

    return rowwise(fn, [h, dn, dres], [g.reshape(1, -1)], [(D_MODEL, F32)], [D_MODEL], name=name)


def loss_head(h, g, tgt, name):
    def fn(x, tg, gg):
        y = (x * _rstd(x)) * gg
        e = y - tg
        dy = e * (1.0 / D_MODEL)
        dx, dgp = _rms_bwd_math(x, gg, dy)
        return (dx,), (_fold8(dgp), _fold8(e * e))

    return rowwise(fn, [h, tgt], [g.reshape(1, -1)], [(D_MODEL, F32)], [D_MODEL, D_MODEL], name=name)


def _sigmoid(x):
    return 1.0 / (1.0 + jnp.exp(-x))


FFN_TILE = 256


def ffn_gate_up(n2, w_gate, w_up, name):
    t = n2.shape[0]
    tm = _pick(t, (1024, 512, 256, 128))

    def body(a_ref, wg_ref, wu_ref, g_ref, u_ref, act_ref):
        a = a_ref[...]
        g = _dot(a, wg_ref[...])
        u = _dot(a, wu_ref[...])
        g_ref[...] = g
        u_ref[...] = u
        act_ref[...] = ((g * _sigmoid(g)) * u).astype(BF16)

    w_spec = pl.BlockSpec((D_MODEL, FFN_TILE), lambda i, j: (0, j))
    o_spec = pl.BlockSpec((tm, FFN_TILE), lambda i, j: (i, j))
    return pl.pallas_call(
        body, name=name, grid=(t // tm, D_FF // FFN_TILE),
        in_specs=[pl.BlockSpec((tm, D_MODEL), lambda i, j: (i, 0)), w_spec, w_spec],
        out_specs=[o_spec, o_spec, o_spec],
        out_shape=[jax.ShapeDtypeStruct((t, D_FF), F32), jax.ShapeDtypeStruct((t, D_FF), F32),
                   jax.ShapeDtypeStruct((t, D_FF), BF16)],
        compiler_params=_cparams(("parallel", "parallel")),
    )(n2, w_gate, w_up)


def ffn_gate_up_bwd(dh, w_down, gate, up, name):
    t = dh.shape[0]
    tm = _pick(t, (1024, 512, 256, 128))

    def body(dh_ref, wd_ref, g_ref, u_ref, dg_ref, du_ref, dh_bf):
        @pl.when(pl.program_id(1) == 0)
        def _():
            dh_bf[...] = dh_ref[...].astype(BF16)

        d = lax.dot_general(dh_bf[...], wd_ref[...], _DIMS["nt"], preferred_element_type=F32)
        g = g_ref[...]
        s = _sigmoid(g)
        silu = g * s
        dg_ref[...] = (d * u_ref[...] * (s + silu * (1.0 - s))).astype(BF16)
        du_ref[...] = (d * silu).astype(BF16)

    o_spec = pl.BlockSpec((tm, FFN_TILE), lambda i, j: (i, j))
    return pl.pallas_call(
        body, name=name, grid=(t // tm, D_FF // FFN_TILE),
        in_specs=[pl.BlockSpec((tm, D_MODEL), lambda i, j: (i, 0)),
                  pl.BlockSpec((FFN_TILE, D_MODEL), lambda i, j: (j, 0)), o_spec, o_spec],
        out_specs=[o_spec, o_spec],
        out_shape=[jax.ShapeDtypeStruct((t, D_FF), BF16)] * 2,
        scratch_shapes=[pltpu.VMEM((tm, D_MODEL), BF16)],
        compiler_params=_cparams(("parallel", "arbitrary")),
    )(dh, w_down, gate, up)


def ple_fwd(h, pre, pp, name):
    def fn(x, a, b):
        return (x + _sigmoid(a) * b,), ()

    return rowwise(fn, [h, pre, pp], [], [(D_MODEL, F32)], name=name)[0]


def ple_bwd(dh, pre, pp, name):
    def fn(d, a, b):
        s = _sigmoid(a)
        return (d * b * s * (1.0 - s), d * s), ()

    return rowwise(fn, [dh, pre, pp], [], [(D_MODEL, BF16), (D_MODEL, BF16)], name=name)


def _rot_half_partner(x, first_half):
    w = x.shape[1]
    return jnp.where(first_half, pltpu.roll(x, w - HEAD_DIM // 2, 1), pltpu.roll(x, HEAD_DIM // 2, 1))


def rope_apply(xx, cosw, sinw, backward, name):
    width = xx.shape[1]
    reps = width // 128

    def fn(x, c, s):
        cw = jnp.tile(c, (1, reps))
        sw = jnp.tile(s, (1, reps))
        lane = lax.broadcasted_iota(jnp.int32, x.shape, 1)
        first = (lane % HEAD_DIM) < (HEAD_DIM // 2)
        if backward:
            return (x * cw + _rot_half_partner(x * sw, first),), ()
        return (x * cw + _rot_half_partner(x, first) * sw,), ()

    return rowwise(fn, [xx, cosw, sinw], [], [(width, F32)], name=name)[0]


def _log_sigmoid(x):
    return jnp.minimum(x, 0.0) - jnp.log(1.0 + jnp.exp(-jnp.abs(x)))


CUM_BLOCK = 256


def forget_cumsum(flog, bias, name):
    t = flog.shape[0]
    tb = _pick(t, (CUM_BLOCK,))

    def body(x_ref, b_ref, o_ref, carry):
        @pl.when(pl.program_id(0) == 0)
        def _():
            carry[...] = jnp.zeros_like(carry)

        lf = _log_sigmoid(x_ref[...] + b_ref[...])
        r = lax.broadcasted_iota(jnp.int32, (tb, tb), 0)
        c = lax.broadcasted_iota(jnp.int32, (tb, tb), 1)
        tri = (c <= r).astype(F32)
        cum = jnp.dot(tri, lf, preferred_element_type=F32, precision=lax.Precision.HIGHEST) + carry[...]
        o_ref[...] = cum
        carry[...] = cum[tb - 1:tb, :]

    return pl.pallas_call(
        body, name=name, grid=(t // tb,),
        in_specs=[pl.BlockSpec((tb, GATE_PAD), lambda i: (i, 0)), pl.BlockSpec((1, GATE_PAD), lambda i: (0, 0))],
        out_specs=pl.BlockSpec((tb, GATE_PAD), lambda i: (i, 0)),
        out_shape=jax.ShapeDtypeStruct((t, GATE_PAD), F32),
        scratch_shapes=[pltpu.VMEM((1, GATE_PAD), F32)],
        compiler_params=_cparams(("arbitrary",)),
    )(flog, bias)


def forget_cumsum_bwd(d_cum, flog, bias, name):
    t = flog.shape[0]
    tb = _pick(t, (CUM_BLOCK,))
    nb = t // tb

    def body(d_ref, x_ref, b_ref, o_ref, db_ref, carry):
        @pl.when(pl.program_id(0) == 0)
        def _():
            carry[...] = jnp.zeros_like(carry)
            db_ref[...] = jnp.zeros_like(db_ref)

        r = lax.broadcasted_iota(jnp.int32, (tb, tb), 0)
        c = lax.broadcasted_iota(jnp.int32, (tb, tb), 1)
        tri = (c >= r).astype(F32)
        dlf = jnp.dot(tri, d_ref[...], preferred_element_type=F32, precision=lax.Precision.HIGHEST) + carry[...]
        carry[...] = dlf[0:1, :]
        dx = dlf * (1.0 - _sigmoid(x_ref[...] + b_ref[...]))
        o_ref[...] = dx
        db_ref[...] += _fold8(dx)

    rev = lambda i: (nb - 1 - i, 0)
    return pl.pallas_call(
        body, name=name, grid=(nb,),
        in_specs=[pl.BlockSpec((tb, GATE_PAD), rev), pl.BlockSpec((tb, GATE_PAD), rev),
                  pl.BlockSpec((1, GATE_PAD), lambda i: (0, 0))],
        out_specs=[pl.BlockSpec((tb, GATE_PAD), rev), pl.BlockSpec((8, GATE_PAD), lambda i: (0, 0))],
        out_shape=[jax.ShapeDtypeStruct((t, GATE_PAD), F32), jax.ShapeDtypeStruct((8, GATE_PAD), F32)],
        scratch_shapes=[pltpu.VMEM((1, GATE_PAD), F32)],
        compiler_params=_cparams(("arbitrary",)),
    )(d_cum, flog, bias)


TQ = 512
TK = 128
AUG = 128
N_BIAS = 3
SB_CUTOFF = 110.0
FOX_CUTOFF = 112.0


def _dot(a, b):
    return jnp.dot(a, b, preferred_element_type=F32)


def _rel(shape, d):
    return lax.broadcasted_iota(jnp.int32, shape, 0) - lax.broadcasted_iota(jnp.int32, shape, 1) + d


def _put(x, lo, part):
    return part if lo == 0 else jnp.concatenate([x[:, :lo], part], axis=1)


def _split_bf16(x):
    hi = x.astype(BF16)
    return hi, (x - hi.astype(F32)).astype(BF16)


def _tri_dot(tri2, x):
    hi, lo = _split_bf16(x)
    return _dot(tri2, jnp.concatenate([hi, lo], axis=0))


def _q_cols(width, tq):
    return pl.BlockSpec((None, width, tq), lambda h, i: (h, 0, i))


def _q_rows(width, tq):
    return pl.BlockSpec((None, tq, width), lambda h, i: (h, i, 0))


def _kv_rows(t, width):
    return pl.BlockSpec((None, t, width), lambda h, i: (h, 0, 0))


def _kv_tiles(nk):
    return pl.BlockSpec((None, nk, HEAD_DIM, TK), lambda h, i: (h, 0, 0, 0))


def _blocks(t):
    tq = TQ if t % TQ == 0 else TK
    return tq, tq // TK, t // TK


def fox_fwd(qt_aug, k_aug, vt, f_end, k_norm):
    nh, _, t = qt_aug.shape
    tq, ratio, nk = _blocks(t)
    lanes = tq // 128

    def body(q_ref, k_ref, v_ref, fe_ref, kn_ref, o_ref, lse_ref, first_ref):
        i = pl.program_id(1)
        qv = q_ref[...]
        qf = qv[:HEAD_DIM].astype(F32)
        reach = jnp.sqrt(jnp.sum(qf * qf, axis=0, keepdims=True)) * jnp.tile(kn_ref[...], (1, lanes))

        def step(first, carry, masked):
            m, l, acc = carry
            scores = []
            for u in range(ratio):
                lo = u * TK if masked else 0
                off = pl.multiple_of((first + u) * TK, TK)
                s = _dot(k_ref[pl.ds(off, TK), :], qv[:, lo:])
                if masked:
                    s = jnp.where(_rel(s.shape, 0) <= 0, s, NEG_INF)
                scores.append((lo, s))
            m_new = m
            for lo, s in scores:
                m_new = _put(m_new, lo, jnp.maximum(m_new[:, lo:], jnp.max(s, axis=0, keepdims=True)))
            alpha = jnp.exp(m - m_new)
            l = alpha * l
            acc = alpha * acc
            for u, (lo, s) in enumerate(scores):
                p = jnp.exp(s - m_new[:, lo:])
                l = _put(l, lo, l[:, lo:] + jnp.sum(p, axis=0, keepdims=True))
                acc = _put(acc, lo, acc[:, lo:] + _dot(v_ref[first + u], p.astype(BF16)))
            return m_new, l, acc

        init = (jnp.full((1, tq), NEG_INF, F32), jnp.zeros((1, tq), F32), jnp.zeros((HEAD_DIM, tq), F32))
        m, l, acc = step(ratio * i, init, True)

        def more(c):
            bound = reach + jnp.tile(fe_ref[jnp.maximum(i - 1 - c[0], 0)], (1, lanes))
            return (c[0] < i) & (jnp.max(bound - c[1]) > -FOX_CUTOFF)

        def sweep(c):
            return (c[0] + 1,) + step(ratio * (i - 1 - c[0]), c[1:], False)

        done, m, l, acc = lax.while_loop(more, sweep, (jnp.int32(0), m, l, acc))
        o_ref[...] = acc / l
        lse_ref[...] = m + jnp.log(l)
        first_ref[...] = jnp.full((1, 128), (i - done).astype(F32), F32)

    return pl.pallas_call(
        body, name="fox_fwd", grid=(nh, t // tq),
        in_specs=[_q_cols(AUG, tq), _kv_rows(t, AUG), _kv_tiles(nk),
                  pl.BlockSpec((None, t // tq, 1, 128), lambda h, i: (h, 0, 0, 0)),
                  pl.BlockSpec((None, 1, 128), lambda h, i: (h, 0, 0))],
        out_specs=[_q_cols(HEAD_DIM, tq), _q_cols(1, tq), _q_cols(1, 128)],
        out_shape=[jax.ShapeDtypeStruct((nh, HEAD_DIM, t), F32), jax.ShapeDtypeStruct((nh, 1, t), F32),
                   jax.ShapeDtypeStruct((nh, 1, 128 * (t // tq)), F32)],
        compiler_params=_cparams(("parallel", "arbitrary")),
    )(qt_aug, k_aug, vt, f_end, k_norm)


def fox_bwd(qt_aug, q_aug, k_aug, kt, v, ot, do, dot_, lse, first):
    nh, _, t = qt_aug.shape
    tq, ratio, nk = _blocks(t)

    def body(qt_ref, q_ref, k_ref, kt_ref, v_ref, ot_ref, do_ref, dot_ref, lse_ref, first_ref,
             dqt_ref, dk_ref, dv_ref, rs_ref):
        i = pl.program_id(1)
        start = jnp.max(first_ref[...]).astype(jnp.int32)

        @pl.when(i == 0)
        def _():
            dk_ref[...] = jnp.zeros_like(dk_ref)
            dv_ref[...] = jnp.zeros_like(dv_ref)

        qtv = qt_ref[...]
        qv = q_ref[...]
        dob = do_ref[...]
        dotb = dot_ref[...]
        delta = jnp.sum(ot_ref[...] * dotb.astype(F32), axis=0, keepdims=True)
        lse = lse_ref[...]

        def tile(j, carry, lo):
            dqt, rs = carry
            masked = lo is not None
            lo = lo or 0
            off = pl.multiple_of(j * TK, TK)
            s = _dot(k_ref[pl.ds(off, TK), :], qtv[:, lo:])
            p = jnp.exp(s - lse[:, lo:])
            if masked:
                p = jnp.where(_rel(s.shape, 0) <= 0, p, 0.0)
            dp = _dot(v_ref[pl.ds(off, TK), :], dotb[:, lo:])
            dsb = (p * (dp - delta[:, lo:])).astype(BF16)
            dk_ref[pl.ds(off, TK), :] += _dot(dsb, qv[lo:, :])
            dv_ref[pl.ds(off, TK), :] += _dot(p.astype(BF16), dob[lo:, :])
            return (_put(dqt, lo, dqt[:, lo:] + _dot(kt_ref[j], dsb)),
                    _put(rs, lo, rs[:, lo:] + jnp.sum(dsb.astype(F32), axis=0, keepdims=True)))

        def step(first, carry, masked):
            for u in range(ratio):
                carry = tile(first + u, carry, u * TK if masked else None)
            return carry

        carry = (jnp.zeros((HEAD_DIM, tq), F32), jnp.zeros((1, tq), F32))
        carry = lax.fori_loop(start, i, lambda jj, c: step(ratio * jj, c, False), carry)
        dqt, rs = step(ratio * i, carry, True)
        dqt_ref[...] = dqt * SCALE
        rs_ref[...] = rs

    return pl.pallas_call(
        body, name="fox_bwd", grid=(nh, t // tq),
        in_specs=[_q_cols(AUG, tq), _q_rows(AUG, tq), _kv_rows(t, AUG), _kv_tiles(nk), _kv_rows(t, HEAD_DIM),
                  _q_cols(HEAD_DIM, tq), _q_rows(HEAD_DIM, tq), _q_cols(HEAD_DIM, tq), _q_cols(1, tq),
                  _q_cols(1, 128)],
        out_specs=[_q_cols(HEAD_DIM, tq), _kv_rows(t, AUG), _kv_rows(t, HEAD_DIM), _q_cols(1, tq)],
        out_shape=[jax.ShapeDtypeStruct((nh, HEAD_DIM, t), F32), jax.ShapeDtypeStruct((nh, t, AUG), F32),
                   jax.ShapeDtypeStruct((nh, t, HEAD_DIM), F32), jax.ShapeDtypeStruct((nh, 1, t), F32)],
        compiler_params=_cparams(("arbitrary", "arbitrary")),
    )(qt_aug, q_aug, k_aug, kt, v, ot, do, dot_, lse, first)


def _sb_logits(kb, qv, ok):
    z = _dot(kb, qv)
    e = jnp.exp(-jnp.abs(z))
    ll = -(jnp.maximum(z, 0.0) + jnp.log(1.0 + e))
    if ok is not None:
        ll = jnp.where(ok, ll, 0.0)
    return z, e, ll


def _tri(cmp):
    r = lax.broadcasted_iota(jnp.int32, (TK, 2 * TK), 0)
    c = lax.broadcasted_iota(jnp.int32, (TK, 2 * TK), 1) % TK
    return cmp(r, c).astype(BF16)


def sb_fwd(qt, k, vt):
    nh, _, t = qt.shape
    tq, ratio, nk = _blocks(t)

    def body(q_ref, k_ref, v_ref, o_ref, tot_ref, first_ref):
        i = pl.program_id(1)
        qv = q_ref[...]
        tri_after = _tri(lambda r, c: c > r)

        def tile(j, carry, lo):
            c_l, acc = carry
            masked = lo is not None
            lo = lo or 0
            off = pl.multiple_of(j * TK, TK)
            ok = _rel((TK, tq - lo), 0) < 0 if masked else None
            z, _, ll = _sb_logits(k_ref[pl.ds(off, TK), :], qv[:, lo:], ok)
            a = jnp.exp(z + ll + _tri_dot(tri_after, ll) + c_l[:, lo:])
            if masked:
                a = jnp.where(ok, a, 0.0)
            return (_put(c_l, lo, c_l[:, lo:] + jnp.sum(ll, axis=0, keepdims=True)),
                    _put(acc, lo, acc[:, lo:] + _dot(v_ref[j], a.astype(BF16))))

        def step(first, carry, masked):
            for u in reversed(range(ratio)):
                carry = tile(first + u, carry, u * TK if masked else None)
            return carry

        c_l, acc = step(ratio * i, (jnp.zeros((1, tq), F32), jnp.zeros((HEAD_DIM, tq), F32)), True)

        def more(c):
            return (c[0] < i) & (jnp.max(c[1]) > -SB_CUTOFF)

        def sweep(c):
            c_l, acc = step(ratio * (i - 1 - c[0]), (c[1], c[2]), False)
            return c[0] + 1, c_l, acc

        done, c_l, acc = lax.while_loop(more, sweep, (jnp.int32(0), c_l, acc))
        o_ref[...] = acc
        tot_ref[...] = c_l
        first_ref[...] = jnp.full((1, 128), (i - done).astype(F32), F32)

    return pl.pallas_call(
        body, name="sb_fwd", grid=(nh, t // tq),
        in_specs=[_q_cols(HEAD_DIM, tq), _kv_rows(t, HEAD_DIM), _kv_tiles(nk)],
        out_specs=[_q_cols(HEAD_DIM, tq), _q_cols(1, tq), _q_cols(1, 128)],
        out_shape=[jax.ShapeDtypeStruct((nh, HEAD_DIM, t), F32), jax.ShapeDtypeStruct((nh, 1, t), F32),
                   jax.ShapeDtypeStruct((nh, 1, 128 * (t // tq)), F32)],
        compiler_params=_cparams(("parallel", "arbitrary")),
    )(qt, k, vt)


def sb_bwd(qt, q, k, kt, v, ltot, first, do, dot_):
    nh, _, t = qt.shape
    tq, ratio, nk = _blocks(t)

    def body(qt_ref, q_ref, k_ref, kt_ref, v_ref, tot_ref, first_ref, do_ref, dot_ref, dqt_ref, dk_ref, dv_ref):
        i = pl.program_id(1)
        start = jnp.max(first_ref[...]).astype(jnp.int32)

        @pl.when(i == 0)
        def _():
            dk_ref[...] = jnp.zeros_like(dk_ref)
            dv_ref[...] = jnp.zeros_like(dv_ref)

        qtv = qt_ref[...]
        qv = q_ref[...]
        dob = do_ref[...]
        dotb = dot_ref[...]
        tri_upto = _tri(lambda r, c: c <= r)
        tri_before = _tri(lambda r, c: c < r)

        def tile(j, carry, lo):
            rest, c_w, dqt = carry
            masked = lo is not None
            lo = lo or 0
            off = pl.multiple_of(j * TK, TK)
            ok = _rel((TK, tq - lo), 0) < 0 if masked else None
            z, e, ll = _sb_logits(k_ref[pl.ds(off, TK), :], qtv[:, lo:], ok)
            a = jnp.exp(z + ll + (rest[:, lo:] - _tri_dot(tri_upto, ll)))
            if masked:
                a = jnp.where(ok, a, 0.0)
            w = a * _dot(v_ref[pl.ds(off, TK), :], dotb[:, lo:])
            before = _tri_dot(tri_before, w) + c_w[:, lo:]
            r = 1.0 / (1.0 + e)
            sig = jnp.where(z >= 0.0, r, e * r)
            dz = w * (1.0 - sig) - before * sig
            if masked:
                dz = jnp.where(ok, dz, 0.0)
            dzb = dz.astype(BF16)
            dk_ref[pl.ds(off, TK), :] += _dot(dzb, qv[lo:, :])
            dv_ref[pl.ds(off, TK), :] += _dot(a.astype(BF16), dob[lo:, :])
            return (_put(rest, lo, rest[:, lo:] - jnp.sum(ll, axis=0, keepdims=True)),
                    _put(c_w, lo, c_w[:, lo:] + jnp.sum(w, axis=0, keepdims=True)),
                    _put(dqt, lo, dqt[:, lo:] + _dot(kt_ref[j], dzb)))

        def step(first, carry, masked):
            for u in range(ratio):
                carry = tile(first + u, carry, u * TK if masked else None)
            return carry

        carry = (tot_ref[...], jnp.zeros((1, tq), F32), jnp.zeros((HEAD_DIM, tq), F32))
        carry = lax.fori_loop(start, i, lambda jj, c: step(ratio * jj, c, False), carry)
        dqt_ref[...] = step(ratio * i, carry, True)[2] * SCALE

    hd = HEAD_DIM
    return pl.pallas_call(
        body, name="sb_bwd", grid=(nh, t // tq),
        in_specs=[_q_cols(hd, tq), _q_rows(hd, tq), _kv_rows(t, hd), _kv_tiles(nk), _kv_rows(t, hd),
                  _q_cols(1, tq), _q_cols(1, 128), _q_rows(hd, tq), _q_cols(hd, tq)],
        out_specs=[_q_cols(hd, tq), _kv_rows(t, hd), _kv_rows(t, hd)],
        out_shape=[jax.ShapeDtypeStruct((nh, hd, t), F32), jax.ShapeDtypeStruct((nh, t, hd), F32),
                   jax.ShapeDtypeStruct((nh, t, hd), F32)],
        compiler_params=_cparams(("arbitrary", "arbitrary")),
    )(qt, q, k, kt, v, ltot, first, do, dot_)


def _swa_q_cols(width):
    return pl.BlockSpec((GROUP, width, WINDOW), lambda g, i: (g, 0, i))


def _swa_q_rows():
    return pl.BlockSpec((GROUP, WINDOW, HEAD_DIM), lambda g, i: (g, i, 0))


def _swa_kv_rows(t):
    return pl.BlockSpec((None, t, HEAD_DIM), lambda g, i: (g, 0, 0))


def _swa_kv_tiles(nk):
    return pl.BlockSpec((None, nk, HEAD_DIM, WINDOW), lambda g, i: (g, 0, 0, 0))


def _swa_sink_spec():
    return pl.BlockSpec((GROUP, 1, 128), lambda g, i: (g, 0, 0))


def _lane_cat(parts):
    return jnp.concatenate(parts, axis=1)


def _swa_window(i):
    jb = jnp.maximum(i - 1, 0)
    start = pl.multiple_of(jb * WINDOW, WINDOW)
    shape = (2 * WINDOW, GROUP * WINDOW)
    query = lax.broadcasted_iota(jnp.int32, shape, 1) % WINDOW
    rel = lax.broadcasted_iota(jnp.int32, shape, 0) - query + (start - i * WINDOW)
    return jb, start, (rel <= 0) & (rel > -WINDOW)


def swa_fwd(qt, k, vt, sinks):
    nh, _, t = qt.shape
    nk = t // WINDOW

    def body(q_ref, k_ref, v_ref, s_ref, o_ref, lse_ref):
        i = pl.program_id(1)
        jb, start, valid = _swa_window(i)
        qv = _lane_cat([q_ref[g] for g in range(GROUP)])
        sink = _lane_cat([s_ref[g] for g in range(GROUP)])
        s = jnp.where(valid, _dot(k_ref[pl.ds(start, 2 * WINDOW), :], qv), NEG_INF)
        m = jnp.maximum(jnp.max(s, axis=0, keepdims=True), sink)
        p = jnp.where(valid, jnp.exp(s - m), 0.0)
        l = jnp.sum(p, axis=0, keepdims=True) + jnp.exp(sink - m)
        pb = p.astype(BF16)
        o = (_dot(v_ref[jb], pb[:WINDOW]) + _dot(v_ref[jb + 1], pb[WINDOW:])) / l
        lse = m + jnp.log(l)
        for g in range(GROUP):
            o_ref[g] = o[:, g * WINDOW:(g + 1) * WINDOW]
            lse_ref[g] = lse[:, g * WINDOW:(g + 1) * WINDOW]

    return pl.pallas_call(
        body, name="swa_fwd", grid=(N_KV, nk),
        in_specs=[_swa_q_cols(HEAD_DIM), _swa_kv_rows(t), _swa_kv_tiles(nk), _swa_sink_spec()],
        out_specs=[_swa_q_cols(HEAD_DIM), _swa_q_cols(1)],
        out_shape=[jax.ShapeDtypeStruct((nh, HEAD_DIM, t), F32), jax.ShapeDtypeStruct((nh, 1, t), F32)],
        compiler_params=_cparams(("parallel", "arbitrary")),
    )(qt, k, vt, sinks)


def swa_bwd(qt, q, k, kt, v, sinks, ot, do, dot_, lse):
    nh, _, t = qt.shape
    nk = t // WINDOW

    def body(qt_ref, q_ref, k_ref, kt_ref, v_ref, s_ref, ot_ref, do_ref, dot_ref, lse_ref,
             dqt_ref, dk_ref, dv_ref, dsink_ref):
        i = pl.program_id(1)

        @pl.when(i == 0)
        def _():
            dk_ref[...] = jnp.zeros_like(dk_ref)
            dv_ref[...] = jnp.zeros_like(dv_ref)
            dsink_ref[...] = jnp.zeros_like(dsink_ref)

        jb, start, valid = _swa_window(i)
        heads = range(GROUP)
        qtv = _lane_cat([qt_ref[g] for g in heads])
        dotb = _lane_cat([dot_ref[g] for g in heads])
        lse = _lane_cat([lse_ref[g] for g in heads])
        sink = _lane_cat([s_ref[g] for g in heads])
        otv = _lane_cat([ot_ref[g] for g in heads])
        q_rows = jnp.concatenate([q_ref[g] for g in heads], axis=0)
        do_rows = jnp.concatenate([do_ref[g] for g in heads], axis=0)
        delta = jnp.sum(otv * dotb.astype(F32), axis=0, keepdims=True)
        p = jnp.where(valid, jnp.exp(_dot(k_ref[pl.ds(start, 2 * WINDOW), :], qtv) - lse), 0.0)
        dsb = (p * (_dot(v_ref[pl.ds(start, 2 * WINDOW), :], dotb) - delta)).astype(BF16)
        dqt = (_dot(kt_ref[jb], dsb[:WINDOW]) + _dot(kt_ref[jb + 1], dsb[WINDOW:])) * SCALE
        dsink = -jnp.exp(sink - lse) * delta
        for g in heads:
            dqt_ref[g] = dqt[:, g * WINDOW:(g + 1) * WINDOW]
            dsink_ref[g] += dsink[:, g * WINDOW:(g + 1) * WINDOW]
        dk_ref[pl.ds(start, 2 * WINDOW), :] += _dot(dsb, q_rows)
        dv_ref[pl.ds(start, 2 * WINDOW), :] += _dot(p.astype(BF16), do_rows)

    hd = HEAD_DIM
    return pl.pallas_call(
        body, name="swa_bwd", grid=(N_KV, nk),
        in_specs=[_swa_q_cols(hd), _swa_q_rows(), _swa_kv_rows(t), _swa_kv_tiles(nk), _swa_kv_rows(t),
                  _swa_sink_spec(), _swa_q_cols(hd), _swa_q_rows(), _swa_q_cols(hd), _swa_q_cols(1)],
        out_specs=[_swa_q_cols(hd), _swa_kv_rows(t), _swa_kv_rows(t), _swa_sink_spec()],
        out_shape=[jax.ShapeDtypeStruct((nh, hd, t), F32), jax.ShapeDtypeStruct((N_KV, t, hd), F32),
                   jax.ShapeDtypeStruct((N_KV, t, hd), F32), jax.ShapeDtypeStruct((nh, 1, 128), F32)],
        compiler_params=_cparams(("arbitrary", "arbitrary")),
    )(qt, q, k, kt, v, sinks, ot, do, dot_, lse)


def head_rows(a, nh):
    t = a.shape[0]
    return a.reshape(t, nh, HEAD_DIM).transpose(1, 0, 2)


def head_cols(a, nh):
    t = a.shape[0]
    return a.reshape(t, nh, HEAD_DIM).transpose(1, 2, 0)


def head_tiles(a, nh):
    t = a.shape[0]
    return a.reshape(t // TK, TK, nh, HEAD_DIM).transpose(2, 0, 3, 1)


def rows_to_flat(a):
    nh, t, _ = a.shape
    return a.transpose(1, 0, 2).reshape(t, nh * HEAD_DIM)


def cols_to_flat(a):
    nh, _, t = a.shape
    return a.transpose(2, 0, 1).reshape(t, nh * HEAD_DIM)


def fox_keys(proj, cum, name):
    t = proj.shape[0]
    tr = _pick(t, (256, 128))
    width = (1 + N_BIAS) * 128

    def body(k_ref, c_ref, o_ref):
        kb = k_ref[...].astype(BF16)
        terms, rest = [], -c_ref[...]
        for _ in range(N_BIAS):
            term = rest.astype(BF16)
            terms.append(term)
            rest = rest - term.astype(F32)
        row = lax.broadcasted_iota(jnp.int32, (width, AUG), 0)
        col = lax.broadcasted_iota(jnp.int32, (width, AUG), 1)
        for h in range(N_FOX):
            pair = kb[:, 128 * (h // 2):128 * (h // 2 + 1)]
            place = (row < 128) & (col < HEAD_DIM) & (row - HEAD_DIM * (h % 2) == col)
            for b in range(N_BIAS):
                place = place | ((row == 128 * (1 + b) + h) & (col == HEAD_DIM + b))
            src = jnp.concatenate([pair] + terms, axis=1)
            o_ref[h] = _dot(src, place.astype(BF16)).astype(BF16)

    return pl.pallas_call(
        body, name=name, grid=(t // tr,),
        in_specs=[pl.BlockSpec((tr, FOX_W), lambda i: (i, 1)), pl.BlockSpec((tr, GATE_PAD), lambda i: (i, 0))],
        out_specs=pl.BlockSpec((N_FOX, tr, AUG), lambda i: (0, i, 0)),
        out_shape=jax.ShapeDtypeStruct((N_FOX, t, AUG), BF16),
        compiler_params=_cparams(("parallel",)),
    )(proj, cum)


def fox_operands(qf, kf, cum_heads):
    t = qf.shape[0]
    nh = cum_heads.shape[0]
    tq = _blocks(t)[0]
    qs = (qf * SCALE).astype(BF16)
    ones_t = jnp.ones((nh, N_BIAS, t), BF16)
    qt_aug = jnp.concatenate([head_cols(qs, nh), ones_t, jnp.zeros((nh, AUG - HEAD_DIM - N_BIAS, t), BF16)], axis=1)
    own_lane = jnp.broadcast_to(jnp.eye(nh, dtype=BF16)[:, None, :], (nh, t, nh))
    q_aug = jnp.concatenate([head_rows(qs, nh), own_lane, jnp.zeros((nh, t, AUG - HEAD_DIM - nh), BF16)], axis=2)
    f_end = jnp.broadcast_to((-cum_heads)[:, tq - 1::tq, None, None], (nh, t // tq, 1, 128))
    k_sq = jnp.sum(jnp.square(kf).reshape(t, nh, HEAD_DIM), axis=2)
    k_norm = jnp.broadcast_to(1.01 * jnp.sqrt(jnp.max(k_sq, axis=0))[:, None, None], (nh, 1, 128))
    return qt_aug, q_aug, f_end, k_norm


def adamw(w, g, m, v, name):
    cols = w.shape[1]

    def fn(ww, gg, mm, vv):
        mn = ADAM_B1 * mm + (1.0 - ADAM_B1) * gg
        vn = ADAM_B2 * vv + (1.0 - ADAM_B2) * (gg * gg)
        m_hat = mn / (1.0 - ADAM_B1 ** ADAM_STEP)
        v_hat = vn / (1.0 - ADAM_B2 ** ADAM_STEP)
        delta = -ADAM_LR * (m_hat / (jnp.sqrt(v_hat) + ADAM_EPS) + ADAM_WD * ww)
        return (delta, mn, vn), ()

    return rowwise(fn, [w, g, m, v], [], [(cols, F32)] * 3, name=name)


ANY = pl.BlockSpec(memory_space=pl.ANY)


def _my_place():
    return lax.axis_index("x"), lax.axis_index("y"), lax.axis_index("c")


def _flip(coord, bit):
    return 1 - coord if bit else coord


def allgather_chips(w):
    r, c = w.shape
    rh = r // 2

    def body(w_ref, out_ref, send_sems, recv_sems, local_sem):
        x, y, cc = _my_place()
        me = 2 * x + y
        sibling = (x, y, 1 - cc)
        chips = [(_flip(x, kbits >> 1), _flip(y, kbits & 1)) for kbits in (1, 2, 3)]

        def half(chip, hc):
            return out_ref.at[chip, pl.ds(pl.multiple_of(hc * rh, 16), rh)]

        def copy(k, src, dst, to):
            return pltpu.make_async_remote_copy(src_ref=src, dst_ref=dst, send_sem=send_sems.at[k],
                                                recv_sem=recv_sems.at[k], device_id=to, device_id_type=MESH)

        local = pltpu.make_async_copy(w_ref, out_ref.at[me], local_sem)
        local.start()
        my_half = w_ref.at[pl.ds(pl.multiple_of(cc * rh, 16), rh)]
        first = [copy(j, my_half, half(me, cc), (px, py, cc)) for j, (px, py) in enumerate(chips)]
        for cp in first:
            cp.start()
        passed = []
        for j, (px, py) in enumerate(chips):
            landed = half(2 * px + py, cc)
            copy(j, my_half, landed, (px, py, cc)).wait_recv()
            fwd = copy(3 + j, landed, landed, sibling)
            fwd.start()
            passed.append(fwd)
        for j, (px, py) in enumerate(chips):
            theirs = half(2 * px + py, 1 - cc)
            copy(3 + j, theirs, theirs, sibling).wait_recv()
        for cp in first + passed:
            cp.wait_send()
        local.wait()

    return pl.pallas_call(
        body, name="allgather_chips", in_specs=[ANY], out_specs=ANY,
        out_shape=jax.ShapeDtypeStruct((N_CHIPS, r, c), w.dtype),
        scratch_shapes=[pltpu.SemaphoreType.DMA((6,)), pltpu.SemaphoreType.DMA((6,)), pltpu.SemaphoreType.DMA],
    )(w)


def pair_send_other_half(g):
    n, _, rh, c = g.shape

    def body(g_ref, out_ref, send_sem, recv_sem):
        x, y, cc = _my_place()
        cp = pltpu.make_async_remote_copy(
            src_ref=g_ref.at[:, 1 - cc], dst_ref=out_ref, send_sem=send_sem, recv_sem=recv_sem,
            device_id=(x, y, 1 - cc), device_id_type=MESH)
        cp.start()
        cp.wait()

    return pl.pallas_call(
        body, name="pair_send_other_half", in_specs=[ANY], out_specs=ANY,
        out_shape=jax.ShapeDtypeStruct((n, rh, c), g.dtype),
        scratch_shapes=[pltpu.SemaphoreType.DMA, pltpu.SemaphoreType.DMA],
    )(g)


def exchange_chips(s):
    n, rh, c = s.shape

    def body(s_ref, out_ref, send_sems, recv_sems, local_sem):
        x, y, cc = _my_place()
        me = 2 * x + y
        local = pltpu.make_async_copy(s_ref.at[me], out_ref.at[me], local_sem)
        local.start()
        copies = []
        for kbits in (1, 2, 3):
            px, py = _flip(x, kbits >> 1), _flip(y, kbits & 1)
            cp = pltpu.make_async_remote_copy(
                src_ref=s_ref.at[2 * px + py], dst_ref=out_ref.at[me], send_sem=send_sems.at[kbits - 1],
                recv_sem=recv_sems.at[kbits - 1], device_id=(px, py, cc), device_id_type=MESH)
            cp.start()
            copies.append(cp)
        for cp in copies:
            cp.wait()
        local.wait()

    return pl.pallas_call(
        body, name="exchange_chips", in_specs=[ANY], out_specs=ANY,
        out_shape=jax.ShapeDtypeStruct((n, rh, c), s.dtype),
        scratch_shapes=[pltpu.SemaphoreType.DMA((3,)), pltpu.SemaphoreType.DMA((3,)), pltpu.SemaphoreType.DMA],
    )(s)


def pair_swap(tt):
    def body(t_ref, out_ref, send_sem, recv_sem):
        x, y, cc = _my_place()
        cp = pltpu.make_async_remote_copy(
            src_ref=t_ref, dst_ref=out_ref, send_sem=send_sem, recv_sem=recv_sem,
            device_id=(x, y, 1 - cc), device_id_type=MESH)
        cp.start()
        cp.wait()

    return pl.pallas_call(
        body, name="pair_swap", in_specs=[ANY], out_specs=ANY,
        out_shape=jax.ShapeDtypeStruct(tt.shape, tt.dtype),
        scratch_shapes=[pltpu.SemaphoreType.DMA, pltpu.SemaphoreType.DMA],
    )(tt)


SMALL_ROWS = 16


def allreduce_small(v):
    r, c = v.shape
    vm = pl.BlockSpec(memory_space=pltpu.VMEM)

    def body(v_ref, out_ref, slots, send_sems, recv_sems):
        x, y, cc = _my_place()
        me = 4 * x + 2 * y + cc
        slots[me] = v_ref[...]
        copies = []
        for kbits in range(1, 8):
            peer = (_flip(x, kbits >> 2), _flip(y, (kbits >> 1) & 1), _flip(cc, kbits & 1))
            cp = pltpu.make_async_remote_copy(
                src_ref=v_ref, dst_ref=slots.at[me], send_sem=send_sems.at[kbits - 1],
                recv_sem=recv_sems.at[kbits - 1], device_id=peer, device_id_type=MESH)
            cp.start()
            copies.append(cp)
        for cp in copies:
            cp.wait()
        total = slots[0]
        for dev in range(1, 8):
            total = total + slots[dev]
        out_ref[...] = total

    return pl.pallas_call(
        body, name="allreduce_small", in_specs=[vm], out_specs=vm,
        out_shape=jax.ShapeDtypeStruct((r, c), F32),
        scratch_shapes=[pltpu.VMEM((8, r, c), F32), pltpu.SemaphoreType.DMA((7,)), pltpu.SemaphoreType.DMA((7,))],
    )(v)


def add_pair(mine, theirs, name):
    return rowwise(lambda a, b: ((a + b,), ()), [mine, theirs], [], [(mine.shape[1], BF16)], name=name)[0]


def sum_chips(r4, name):
    _, rh, c = r4.shape
    tr = _pick(rh, (256, 128, 64, 32, 16))

    def body(r_ref, o_ref):
        total = r_ref[0].astype(F32)
        for j in range(1, N_CHIPS):
            total = total + r_ref[j].astype(F32)
        o_ref[...] = total

    return pl.pallas_call(
        body, name=name, grid=(rh // tr,),
        in_specs=[pl.BlockSpec((N_CHIPS, tr, c), lambda i: (0, i, 0))],
        out_specs=pl.BlockSpec((tr, c), lambda i: (i, 0)),
        out_shape=jax.ShapeDtypeStruct((rh, c), F32),
        compiler_params=_cparams(("parallel",)),
    )(r4)


def _pad_rows(a, rows):
    return jnp.pad(a, ((0, rows - a.shape[0]), (0, 0))) if rows != a.shape[0] else a


def pack_shards(local):
    parts = []
    for name, layer, r, c, _ in _PACK:
        flat = local[name][layer].astype(BF16).reshape(r * c // D_MODEL, D_MODEL)
        parts.append(_pad_rows(flat, _pack_rows(r, c)))
    used = sum(p.shape[0] for p in parts)
    parts.append(jnp.zeros((PACK_ROWS - used, D_MODEL), BF16))
    return jnp.concatenate(parts, axis=0)


def unpack_full(gathered):
    full, off = {}, 0
    for name, layer, r, c, by_cols in _PACK:
        n = r * c // D_MODEL
        blk = gathered[:, off:off + n, :].reshape(N_CHIPS, r, c)
        if by_cols:
            full[(name, layer)] = blk.transpose(1, 0, 2).reshape(r, N_CHIPS * c)
        else:
            full[(name, layer)] = blk.reshape(N_CHIPS * r, c)
        off += _pack_rows(r, c)
    return full


def pack_grads(grads):
    parts = []
    for name, layer, r, c, by_cols in _PACK:
        g = grads[(name, layer)]
        if by_cols:
            blk = g.reshape(r, N_CHIPS, c).transpose(1, 0, 2)
        else:
            blk = g.reshape(N_CHIPS, r, c)
        flat = blk.reshape(N_CHIPS, r * c // D_MODEL, D_MODEL)
        rows = _pack_rows(r, c)
        if rows != flat.shape[1]:
            flat = jnp.pad(flat, ((0, 0), (0, rows - flat.shape[1]), (0, 0)))
        parts.append(flat)
    used = sum(p.shape[1] for p in parts)
    parts.append(jnp.zeros((N_CHIPS, PACK_ROWS - used, D_MODEL), F32))
    return jnp.concatenate(parts, axis=1)


def unpack_local(flat):
    out, off = {}, 0
    for name, layer, r, c, _ in _PACK:
        n = r * c // D_MODEL
        out[(name, layer)] = flat[off:off + n, :].reshape(r, c)
        off += _pack_rows(r, c)
    return out


def rope_tables(pos):
    half = HEAD_DIM // 2
    lane = jnp.arange(128)
    inv = ROPE_THETA ** (-(lane % half).astype(F32) / half)
    ang = pos.astype(F32)[:, None] * inv[None, :]
    sign = jnp.where((lane % HEAD_DIM) < half, -1.0, 1.0).astype(F32)
    return jnp.cos(ang), jnp.sin(ang) * sign[None, :]


def local_step(x, p, pos, tgt, norm_mix, norm_ffn, norm_ple, norm_final, ev_b_f, od_sinks, wfull):
    t = x.shape[0]
    w_in0 = wfull[("ev_w_in", 0)]
    w_in0 = jnp.concatenate([w_in0, jnp.zeros((D_MODEL, EVEN_IN_PAD - w_in0.shape[1]), w_in0.dtype)], axis=1)
    b_pad = jnp.zeros((1, GATE_PAD), F32).at[0, :N_FOX].set(ev_b_f[0])
    sinks_b = jnp.broadcast_to(od_sinks[0][:, None, None], (N_Q, 1, 128)).astype(F32)
    cosw, sinw = rope_tables(pos)

    saved = []
    h = x
    for i in range(2):
        s = {"h0": h}
        n1 = rmsnorm_fwd(h, norm_mix[i], f"norm_mix_fwd{i}")
        s["n1"] = n1
        if i == 0:
            proj = matmul(n1, w_in0, name="ev_in")
            cols = [proj[:, j * FOX_W:(j + 1) * FOX_W] for j in range(6)]
            flog = proj[:, EVEN_QKV:]
            cum = forget_cumsum(flog, b_pad, "forget_cumsum")
            fox = dict(zip(("qt", "q", "f_end", "k_norm"), fox_operands(cols[0], cols[1], cum[:, :N_FOX].T)))
            fox["k"] = fox_keys(proj, cum, "fox_keys")
            fox.update(kt=head_tiles(cols[1].astype(BF16), N_FOX), v=head_rows(cols[2].astype(BF16), N_FOX),
                       vt=head_tiles(cols[2].astype(BF16), N_FOX))
            q_sb = (cols[3] * SCALE).astype(BF16)
            sb = dict(qt=head_cols(q_sb, N_SB), q=head_rows(q_sb, N_SB), k=head_rows(cols[4].astype(BF16), N_SB),
                      kt=head_tiles(cols[4].astype(BF16), N_SB), v=head_rows(cols[5].astype(BF16), N_SB),
                      vt=head_tiles(cols[5].astype(BF16), N_SB))
            fox["ot"], fox["lse"], fox["first"] = fox_fwd(fox["qt"], fox["k"], fox["vt"], fox["f_end"], fox["k_norm"])
            sb["ot"], sb["ltot"], sb["first"] = sb_fwd(sb["qt"], sb["k"], sb["vt"])
            s.update(flog=flog, fox=fox, sb=sb)
            mixin_t = jnp.concatenate([fox["ot"].reshape(FOX_W, t), sb["ot"].reshape(SB_W, t)], axis=0).astype(BF16)
            w_out = wfull[("ev_w_out", 0)]
        else:
            proj = matmul(n1, wfull[("od_w_in", 0)], name="od_in")
            qk = rope_apply(proj[:, :Q_W + KV_W], cosw, sinw, False, "rope_fwd")
            q_sc = (qk[:, :Q_W] * SCALE).astype(BF16)
            k_b = qk[:, Q_W:].astype(BF16)
            v_b = proj[:, Q_W + KV_W:].astype(BF16)
            swa = dict(qt=head_cols(q_sc, N_Q), q=head_rows(q_sc, N_Q), k=head_rows(k_b, N_KV),
                       kt=head_tiles(k_b, N_KV), v=head_rows(v_b, N_KV), vt=head_tiles(v_b, N_KV))
            swa["ot"], swa["lse"] = swa_fwd(swa["qt"], swa["k"], swa["vt"], sinks_b)
            s["swa"] = swa
            mixin_t = swa["ot"].reshape(Q_W, t).astype(BF16)
            w_out = wfull[("od_w_out", 0)]
        s["mixin_t"] = mixin_t
        h = matmul(mixin_t, w_out, "tn", residual=h, name=f"mix_out{i}")
        s["h1"] = h
        n2 = rmsnorm_fwd(h, norm_ffn[i], f"norm_ffn_fwd{i}")
        gate, up, act = ffn_gate_up(n2, wfull[("ffn_w_gate", i)], wfull[("ffn_w_up", i)], f"ffn_gate_up{i}")
        s.update(n2=n2, gate=gate, up=up, act=act)
        h = matmul(act, wfull[("ffn_w_down", i)], residual=h, name=f"ffn_down{i}")
        s["h2"] = h
        n3 = rmsnorm_fwd(h, norm_ple[i], f"norm_ple_fwd{i}")
        pre = matmul(n3, wfull[("ple_w_gate", i)], name=f"ple_gate{i}")
        pp = matmul(p[i], wfull[("ple_w_proj", i)], name=f"ple_proj{i}")
        s.update(n3=n3, pre=pre, pp=pp)
        h = ple_fwd(h, pre, pp, f"ple_fwd{i}")
        saved.append(s)

    dh, dg_final, loss8 = loss_head(h, norm_final, tgt, "loss_head")
    gw = {}
    small = {"norm_final": dg_final, "loss": loss8}
    for i in (1, 0):
        s = saved[i]
        dpre, dpp = ple_bwd(dh, s["pre"], s["pp"], f"ple_bwd{i}")
        gw[("ple_w_gate", i)] = matmul(s["n3"], dpre, "tn", name=f"d_ple_gate{i}")
        gw[("ple_w_proj", i)] = matmul(p[i], dpp, "tn", name=f"d_ple_proj{i}")
        dn3 = matmul(dpre, wfull[("ple_w_gate", i)], "nt", name=f"dn_ple{i}")
        dh, small[("norm_ple", i)] = rmsnorm_bwd(s["h2"], norm_ple[i], dn3, dh, f"norm_ple_bwd{i}")

        dgate, dup = ffn_gate_up_bwd(dh, wfull[("ffn_w_down", i)], s["gate"], s["up"], f"ffn_gate_up_bwd{i}")
        gw[("ffn_w_down", i)] = matmul(s["act"], dh, "tn", name=f"d_ffn_down{i}")
        gw[("ffn_w_gate", i)] = matmul(s["n2"], dgate, "tn", name=f"d_ffn_gate{i}")
        gw[("ffn_w_up", i)] = matmul(s["n2"], dup, "tn", name=f"d_ffn_up{i}")
        dn2 = matmul(dgate, wfull[("ffn_w_gate", i)], "nt", name=f"dn_ffn_gate{i}")
        dn2 = matmul(dup, wfull[("ffn_w_up", i)], "nt", residual=dn2, name=f"dn_ffn_up{i}")
        dh, small[("norm_ffn", i)] = rmsnorm_bwd(s["h1"], norm_ffn[i], dn2, dh, f"norm_ffn_bwd{i}")

        if i == 0:
            dmb = matmul(dh, wfull[("ev_w_out", 0)], "nt", out_dtype=BF16, name="d_mix0")
            gw[("ev_w_out", 0)] = matmul(s["mixin_t"], dh, name="d_ev_out")
            fox, sb = s["fox"], s["sb"]
            dqt_f, dk_aug, dv_f, ds_rows = fox_bwd(
                fox["qt"], fox["q"], fox["k"], fox["kt"], fox["v"], fox["ot"],
                head_rows(dmb[:, :FOX_W], N_FOX), head_cols(dmb[:, :FOX_W], N_FOX), fox["lse"], fox["first"])
            dqt_s, dk_s, dv_s = sb_bwd(sb["qt"], sb["q"], sb["k"], sb["kt"], sb["v"], sb["ltot"], sb["first"],
                                       head_rows(dmb[:, FOX_W:], N_SB), head_cols(dmb[:, FOX_W:], N_SB))
            ds_cols = jnp.sum(dk_aug[:, :, HEAD_DIM:HEAD_DIM + N_FOX], axis=0)
            d_cum = jnp.pad(ds_rows[:, 0, :].T - ds_cols, ((0, 0), (0, GATE_PAD - N_FOX)))
            dflog, db8 = forget_cumsum_bwd(d_cum, s["flog"], b_pad, "forget_cumsum_bwd")
            small["ev_b_f"] = db8
            parts = (cols_to_flat(dqt_f), rows_to_flat(dk_aug[:, :, :HEAD_DIM]), rows_to_flat(dv_f),
                     cols_to_flat(dqt_s), rows_to_flat(dk_s), rows_to_flat(dv_s), dflog)
            dproj = jnp.concatenate([a.astype(BF16) for a in parts], axis=1)
            dw = matmul(s["n1"], dproj, "tn", name="d_ev_in")
            gw[("ev_w_in", 0)] = dw[:, :EVEN_QKV + N_FOX]
            dn1 = matmul(dproj, w_in0, "nt", name="dn_mix0")
        else:
            dmb = matmul(dh, wfull[("od_w_out", 0)], "nt", out_dtype=BF16, name="d_mix1")
            gw[("od_w_out", 0)] = matmul(s["mixin_t"], dh, name="d_od_out")
            swa = s["swa"]
            dqt, dk, dv, dsink = swa_bwd(swa["qt"], swa["q"], swa["k"], swa["kt"], swa["v"], sinks_b, swa["ot"],
                                         head_rows(dmb, N_Q), head_cols(dmb, N_Q), swa["lse"])
            small["od_sinks"] = dsink
            dqk = rope_apply(jnp.concatenate([cols_to_flat(dqt), rows_to_flat(dk)], axis=1), cosw, sinw, True,
                             "rope_bwd")
            dproj = jnp.concatenate([dqk, rows_to_flat(dv)], axis=1).astype(BF16)
            gw[("od_w_in", 0)] = matmul(s["n1"], dproj, "tn", name="d_od_in")
            dn1 = matmul(dproj, wfull[("od_w_in", 0)], "nt", name="dn_mix1")
        dh, small[("norm_mix", i)] = rmsnorm_bwd(s["h0"], norm_mix[i], dn1, dh, f"norm_mix_bwd{i}")
    return dh, gw, small


_SMALL_ROWS = (("norm_mix", 0), ("norm_mix", 1), ("norm_ffn", 0), ("norm_ffn", 1),
               ("norm_ple", 0), ("norm_ple", 1), "norm_final", "misc", "loss")


def pack_small(small):
    rows = []
    for key in _SMALL_ROWS:
        if key == "misc":
            db = jnp.sum(small["ev_b_f"], axis=0)[:N_FOX]
            dsink = jnp.sum(small["od_sinks"][:, 0, :], axis=1)
            rows.append(jnp.zeros((D_MODEL,), F32).at[:N_FOX].set(db).at[128:128 + N_Q].set(dsink))
        else:
            rows.append(jnp.sum(small[key], axis=0))
    rows += [jnp.zeros((D_MODEL,), F32)] * (SMALL_ROWS - len(rows))
    return jnp.stack(rows)


def kernel(x, p, positions, norm_mix, norm_ffn, norm_ple, norm_final, ev_w_in, ev_b_f, ev_w_out, od_w_in, od_sinks, od_w_out, ffn_w_gate, ffn_w_up, ffn_w_down, ple_w_proj, ple_w_gate, loss_target, m_norm_mix, m_norm_ffn, m_norm_ple, m_norm_final, m_ev_w_in, m_ev_b_f, m_ev_w_out, m_od_w_in, m_od_sinks, m_od_w_out, m_ffn_w_gate, m_ffn_w_up, m_ffn_w_down, m_ple_w_proj, m_ple_w_gate, v_norm_mix, v_norm_ffn, v_norm_ple, v_norm_final, v_ev_w_in, v_ev_b_f, v_ev_w_out, v_od_w_in, v_od_sinks, v_od_w_out, v_ffn_w_gate, v_ffn_w_up, v_ffn_w_down, v_ple_w_proj, v_ple_w_gate):
    local_w = dict(ev_w_in=ev_w_in, ev_w_out=ev_w_out, od_w_in=od_w_in, od_w_out=od_w_out,
                   ffn_w_gate=ffn_w_gate, ffn_w_up=ffn_w_up, ffn_w_down=ffn_w_down,
                   ple_w_proj=ple_w_proj, ple_w_gate=ple_w_gate)
    local_m = dict(ev_w_in=m_ev_w_in, ev_w_out=m_ev_w_out, od_w_in=m_od_w_in, od_w_out=m_od_w_out,
                   ffn_w_gate=m_ffn_w_gate, ffn_w_up=m_ffn_w_up, ffn_w_down=m_ffn_w_down,
                   ple_w_proj=m_ple_w_proj, ple_w_gate=m_ple_w_gate)
    local_v = dict(ev_w_in=v_ev_w_in, ev_w_out=v_ev_w_out, od_w_in=v_od_w_in, od_w_out=v_od_w_out,
                   ffn_w_gate=v_ffn_w_gate, ffn_w_up=v_ffn_w_up, ffn_w_down=v_ffn_w_down,
                   ple_w_proj=v_ple_w_proj, ple_w_gate=v_ple_w_gate)

    wfull = unpack_full(allgather_chips(pack_shards(local_w)))
    grad_x, gw, small = local_step(x[0], p[:, 0], positions[0], loss_target[0], norm_mix, norm_ffn, norm_ple,
                                   norm_final, ev_b_f, od_sinks, wfull)

    cc = lax.axis_index("c")
    g4 = pack_grads(gw).reshape(N_CHIPS, 2, PACK_ROWS_HALF, D_MODEL)
    theirs = pair_send_other_half(g4)
    mine = lax.dynamic_index_in_dim(g4, cc, axis=1, keepdims=False)
    pair_sum = add_pair(mine.reshape(N_CHIPS * PACK_ROWS_HALF, D_MODEL),
                        theirs.reshape(N_CHIPS * PACK_ROWS_HALF, D_MODEL), "add_pair")
    from_chips = exchange_chips(pair_sum.reshape(N_CHIPS, PACK_ROWS_HALF, D_MODEL))
    half_sum = sum_chips(from_chips, "sum_chips")
    other_half = pair_swap(half_sum)
    low = jnp.where(cc == 0, half_sum, other_half)
    high = jnp.where(cc == 0, other_half, half_sum)
    g_local = unpack_local(jnp.concatenate([low, high], axis=0))

    red = allreduce_small(pack_small(small))
    loss = 0.5 * jnp.sum(red[8]) / D_MODEL
    pad_small = lambda a: jnp.zeros((D_MODEL,), F32).at[:N_FOX].set(a[0][0]).at[128:128 + N_Q].set(a[1][0])
    stack_small = lambda a: jnp.concatenate(
        [a[0], a[1], a[2], a[3][None], pad_small(a[4:6])[None], jnp.zeros((SMALL_ROWS - 8, D_MODEL), F32)], axis=0)
    w_small = stack_small((norm_mix, norm_ffn, norm_ple, norm_final, ev_b_f, od_sinks))
    m_small = stack_small((m_norm_mix, m_norm_ffn, m_norm_ple, m_norm_final, m_ev_b_f, m_od_sinks))
    v_small = stack_small((v_norm_mix, v_norm_ffn, v_norm_ple, v_norm_final, v_ev_b_f, v_od_sinks))
    g_small = red.at[8].set(0.0)
    upd_small = (g_small,) + tuple(adamw(w_small, g_small, m_small, v_small, "adamw_small"))

    def split_small(a):
        return (a[0:2], a[2:4], a[4:6], a[6], a[7, :N_FOX][None], a[7, 128:128 + N_Q][None])

    small_out = [split_small(a) for a in upd_small]

    big_names = ("ev_w_in", "ev_w_out", "od_w_in", "od_w_out", "ffn_w_gate", "ffn_w_up", "ffn_w_down",
                 "ple_w_proj", "ple_w_gate")
    big_out = {}
    for name in big_names:
        w = local_w[name]
        layers, r, c = w.shape
        g = jnp.concatenate([g_local[(name, i)] for i in range(layers)], axis=0)
        res = adamw(w.reshape(layers * r, c), g, local_m[name].reshape(layers * r, c),
                    local_v[name].reshape(layers * r, c), f"adamw_{name}")
        big_out[name] = [a.reshape(layers, r, c) for a in (g,) + tuple(res)]

    outs = [loss, grad_x[None]]
    for kind in range(4):
        sm = small_out[kind]
        outs += [sm[0], sm[1], sm[2], sm[3],
                 big_out["ev_w_in"][kind], sm[4], big_out["ev_w_out"][kind],
                 big_out["od_w_in"][kind], sm[5], big_out["od_w_out"][kind],
                 big_out["ffn_w_gate"][kind], big_out["ffn_w_up"][kind], big_out["ffn_w_down"][kind],
                 big_out["ple_w_proj"][kind], big_out["ple_w_gate"][kind]]
    return tuple(outs)
```

```python
import jax
import jax.numpy as jnp
from jax import lax
from jax.experimental import pallas as pl
from jax.experimental.pallas import tpu as pltpu

F32 = jnp.float32
BF16 = jnp.bfloat16

D_MODEL = 1024
HEAD_DIM = 64
N_FOX = 8
N_SB = 8
FOX_W = N_FOX * HEAD_DIM
SB_W = N_SB * HEAD_DIM
EVEN_QKV = 3 * FOX_W + 3 * SB_W
GATE_PAD = 128
EVEN_IN_PAD = EVEN_QKV + GATE_PAD
N_Q = 16
N_KV = 4
GROUP = N_Q // N_KV
Q_W = N_Q * HEAD_DIM
KV_W = N_KV * HEAD_DIM
ODD_IN = Q_W + 2 * KV_W
WINDOW = 128
ROPE_THETA = 10000.0
D_FF = 2816
PLE_DIM = 256
EPS = 1e-6
NEG_INF = -1e30
SCALE = HEAD_DIM ** -0.5

ADAM_LR = 0.001
ADAM_B1 = 0.9
ADAM_B2 = 0.999
ADAM_EPS = 1e-08
ADAM_WD = 0.01
ADAM_STEP = 10

N_CHIPS = 4
VMEM_LIMIT = 48 * 1024 * 1024
MESH = pl.DeviceIdType.MESH

_PACK = (
    ("ev_w_in", 0, 1024, 770, True),
    ("ev_w_out", 0, 256, 1024, False),
    ("od_w_in", 0, 1024, 384, True),
    ("od_w_out", 0, 256, 1024, False),
    ("ffn_w_gate", 0, 1024, 704, True),
    ("ffn_w_gate", 1, 1024, 704, True),
    ("ffn_w_up", 0, 1024, 704, True),
    ("ffn_w_up", 1, 1024, 704, True),
    ("ffn_w_down", 0, 704, 1024, False),
    ("ffn_w_down", 1, 704, 1024, False),
    ("ple_w_proj", 0, 256, 256, True),
    ("ple_w_proj", 1, 256, 256, True),
    ("ple_w_gate", 0, 256, 1024, False),
    ("ple_w_gate", 1, 256, 1024, False),
)
_ROW_ALIGN = 16


def _pack_rows(r, c):
    n = r * c // D_MODEL
    return -(-n // _ROW_ALIGN) * _ROW_ALIGN


PACK_ROWS_HALF = 3328
PACK_ROWS = 2 * PACK_ROWS_HALF
assert sum(_pack_rows(r, c) for _, _, r, c, _ in _PACK) <= PACK_ROWS


def _pick(n, cands):
    for c in cands:
        if n % c == 0:
            return c
    return n


def _cparams(sem):
    return pltpu.CompilerParams(dimension_semantics=sem, vmem_limit_bytes=VMEM_LIMIT)


_DIMS = {
    "nn": (((1,), (0,)), ((), ())),
    "nt": (((1,), (1,)), ((), ())),
    "tn": (((0,), (0,)), ((), ())),
}


def matmul(a, b, mode="nn", out_dtype=F32, residual=None, name="mm"):
    if mode == "nn":
        (m, k), (k2, n) = a.shape, b.shape
    elif mode == "nt":
        (m, k), (n, k2) = a.shape, b.shape
    else:
        (k, m), (k2, n) = a.shape, b.shape
    assert k == k2, (a.shape, b.shape, mode)
    tm = _pick(m, (1024, 1408, 512, 256, 128))
    tn = _pick(n, (1024, 1408, 512, 640, 384, 256, 128))
    tk = _pick(k, (1024, 1408, 640, 512, 256, 128))
    nk = k // tk
    dims = _DIMS[mode]
    has_res = residual is not None

    def body(*refs):
        if has_res:
            a_ref, b_ref, r_ref, o_ref, acc = refs
        else:
            a_ref, b_ref, o_ref, acc = refs
        kk = pl.program_id(2)
        part = lax.dot_general(a_ref[...].astype(BF16), b_ref[...].astype(BF16), dims,
                               preferred_element_type=F32)

        def finish(r):
            if has_res:
                r = r + r_ref[...]
            o_ref[...] = r.astype(out_dtype)

        if nk == 1:
            finish(part)
            return

        @pl.when(kk == 0)
        def _():
            acc[...] = part

        @pl.when((kk > 0) & (kk < nk - 1))
        def _():
            acc[...] += part

        @pl.when(kk == nk - 1)
        def _():
            finish(acc[...] + part)

    if mode == "nn":
        a_spec = pl.BlockSpec((tm, tk), lambda i, j, kk: (i, kk))
        b_spec = pl.BlockSpec((tk, tn), lambda i, j, kk: (kk, j))
    elif mode == "nt":
        a_spec = pl.BlockSpec((tm, tk), lambda i, j, kk: (i, kk))
        b_spec = pl.BlockSpec((tn, tk), lambda i, j, kk: (j, kk))
    else:
        a_spec = pl.BlockSpec((tk, tm), lambda i, j, kk: (kk, i))
        b_spec = pl.BlockSpec((tk, tn), lambda i, j, kk: (kk, j))
    o_spec = pl.BlockSpec((tm, tn), lambda i, j, kk: (i, j))
    in_specs = [a_spec, b_spec] + ([o_spec] if has_res else [])
    args = (a, b) + ((residual,) if has_res else ())
    return pl.pallas_call(
        body, name=name, grid=(m // tm, n // tn, nk),
        in_specs=in_specs, out_specs=o_spec,
        out_shape=jax.ShapeDtypeStruct((m, n), out_dtype),
        scratch_shapes=[pltpu.VMEM((tm, tn), F32)],
        compiler_params=_cparams(("parallel", "parallel", "arbitrary")),
    )(*args)


def _fold8(v):
    r, w = v.shape
    return v.reshape(r // 8, 8, w).sum(axis=0)


ROW_BLOCK_BYTES = 12 * 1024 * 1024


def rowwise(fn, rows, bcasts, outs, accs=(), name="rowwise", reverse=False):
    t = rows[0].shape[0]
    row_bytes = sum(x.shape[1] * x.dtype.itemsize for x in rows) + sum(w * jnp.dtype(dt).itemsize for w, dt in outs)
    tr = _pick(t, tuple(c for c in (512, 256, 128, 64, 32, 16, 8) if c * row_bytes <= ROW_BLOCK_BYTES or c == 8))
    nr, nb, no, na = len(rows), len(bcasts), len(outs), len(accs)
    steps = t // tr

    def body(*refs):
        ins = [r[...] for r in refs[:nr + nb]]
        out_refs = refs[nr + nb:nr + nb + no]
        acc_refs = refs[nr + nb + no:]
        o, a = fn(*ins)
        for r, v in zip(out_refs, o):
            r[...] = v.astype(r.dtype)
        if na:
            @pl.when(pl.program_id(0) == 0)
            def _():
                for r in acc_refs:
                    r[...] = jnp.zeros_like(r)

            for r, v in zip(acc_refs, a):
                r[...] += v

    if reverse:
        ridx = lambda i: (steps - 1 - i, 0)
    else:
        ridx = lambda i: (i, 0)
    in_specs = [pl.BlockSpec((tr, x.shape[1]), ridx) for x in rows]
    in_specs += [pl.BlockSpec(x.shape, lambda i: (0, 0)) for x in bcasts]
    out_specs = [pl.BlockSpec((tr, w), ridx) for w, _ in outs]
    out_specs += [pl.BlockSpec((8, w), lambda i: (0, 0)) for w in accs]
    out_shape = [jax.ShapeDtypeStruct((t, w), dt) for w, dt in outs]
    out_shape += [jax.ShapeDtypeStruct((8, w), F32) for w in accs]
    res = pl.pallas_call(
        body, name=name, grid=(steps,), in_specs=in_specs, out_specs=out_specs, out_shape=out_shape,
        compiler_params=_cparams(("arbitrary",)),
    )(*rows, *bcasts)
    return res


def _rstd(x):
    return lax.rsqrt(jnp.mean(x * x, axis=-1, keepdims=True) + EPS)


def rmsnorm_fwd(h, g, name):
    def fn(x, gg):
        return ((x * _rstd(x)) * gg,), ()

    return rowwise(fn, [h], [g.reshape(1, -1)], [(D_MODEL, BF16)], name=name)[0]


def _rms_bwd_math(x, gg, dy):
    r = _rstd(x)
    xh = x * r
    u = dy * gg
    dx = r * (u - xh * jnp.mean(u * xh, axis=-1, keepdims=True))
    return dx, dy * xh


def rmsnorm_bwd(h, g, dn, dres, name):
    def fn(x, dy, dr, gg):
        dx, dgp = _rms_bwd_math(x, gg, dy)
        return (dr + dx,), (_fold8(dgp),)

    return rowwise(fn, [h, dn, dres], [g.reshape(1, -1)], [(D_MODEL, F32)], [D_MODEL], name=name)


def loss_head(h, g, tgt, name):
    def fn(x, tg, gg):
        y = (x * _rstd(x)) * gg
        e = y - tg
        dy = e * (1.0 / D_MODEL)
        dx, dgp = _rms_bwd_math(x, gg, dy)
        return (dx,), (_fold8(dgp), _fold8(e * e))

    return rowwise(fn, [h, tgt], [g.reshape(1, -1)], [(D_MODEL, F32)], [D_MODEL, D_MODEL], name=name)


def _sigmoid(x):
    return 1.0 / (1.0 + jnp.exp(-x))


FFN_TILE = 256


def ffn_gate_up(n2, w_gate, w_up, name):
    t = n2.shape[0]
    tm = _pick(t, (1024, 512, 256, 128))

    def body(a_ref, wg_ref, wu_ref, g_ref, u_ref, act_ref):
        a = a_ref[...]
        g = _dot(a, wg_ref[...])
        u = _dot(a, wu_ref[...])
        g_ref[...] = g
        u_ref[...] = u
        act_ref[...] = ((g * _sigmoid(g)) * u).astype(BF16)

    w_spec = pl.BlockSpec((D_MODEL, FFN_TILE), lambda i, j: (0, j))
    o_spec = pl.BlockSpec((tm, FFN_TILE), lambda i, j: (i, j))
    return pl.pallas_call(
        body, name=name, grid=(t // tm, D_FF // FFN_TILE),
        in_specs=[pl.BlockSpec((tm, D_MODEL), lambda i, j: (i, 0)), w_spec, w_spec],
        out_specs=[o_spec, o_spec, o_spec],
        out_shape=[jax.ShapeDtypeStruct((t, D_FF), F32), jax.ShapeDtypeStruct((t, D_FF), F32),
                   jax.ShapeDtypeStruct((t, D_FF), BF16)],
        compiler_params=_cparams(("parallel", "parallel")),
    )(n2, w_gate, w_up)


def ffn_gate_up_bwd(dh, w_down, gate, up, name):
    t = dh.shape[0]
    tm = _pick(t, (1024, 512, 256, 128))

    def body(dh_ref, wd_ref, g_ref, u_ref, dg_ref, du_ref, dh_bf):
        @pl.when(pl.program_id(1) == 0)
        def _():
            dh_bf[...] = dh_ref[...].astype(BF16)

        d = lax.dot_general(dh_bf[...], wd_ref[...], _DIMS["nt"], preferred_element_type=F32)
        g = g_ref[...]
        s = _sigmoid(g)
        silu = g * s
        dg_ref[...] = (d * u_ref[...] * (s + silu * (1.0 - s))).astype(BF16)
        du_ref[...] = (d * silu).astype(BF16)

    o_spec = pl.BlockSpec((tm, FFN_TILE), lambda i, j: (i, j))
    return pl.pallas_call(
        body, name=name, grid=(t // tm, D_FF // FFN_TILE),
        in_specs=[pl.BlockSpec((tm, D_MODEL), lambda i, j: (i, 0)),
                  pl.BlockSpec((FFN_TILE, D_MODEL), lambda i, j: (j, 0)), o_spec, o_spec],
        out_specs=[o_spec, o_spec],
        out_shape=[jax.ShapeDtypeStruct((t, D_FF), BF16)] * 2,
        scratch_shapes=[pltpu.VMEM((tm, D_MODEL), BF16)],
        compiler_params=_cparams(("parallel", "arbitrary")),
    )(dh, w_down, gate, up)


def ple_fwd(h, pre, pp, name):
    def fn(x, a, b):
        return (x + _sigmoid(a) * b,), ()

    return rowwise(fn, [h, pre, pp], [], [(D_MODEL, F32)], name=name)[0]


def ple_bwd(dh, pre, pp, name):
    def fn(d, a, b):
        s = _sigmoid(a)
        return (d * b * s * (1.0 - s), d * s), ()

    return rowwise(fn, [dh, pre, pp], [], [(D_MODEL, BF16), (D_MODEL, BF16)], name=name)


def _rot_half_partner(x, first_half):
    w = x.shape[1]
    return jnp.where(first_half, pltpu.roll(x, w - HEAD_DIM // 2, 1), pltpu.roll(x, HEAD_DIM // 2, 1))


def rope_apply(xx, cosw, sinw, backward, name):
    width = xx.shape[1]
    reps = width // 128

    def fn(x, c, s):
        cw = jnp.tile(c, (1, reps))
        sw = jnp.tile(s, (1, reps))
        lane = lax.broadcasted_iota(jnp.int32, x.shape, 1)
        first = (lane % HEAD_DIM) < (HEAD_DIM // 2)
        if backward:
            return (x * cw + _rot_half_partner(x * sw, first),), ()
        return (x * cw + _rot_half_partner(x, first) * sw,), ()

    return rowwise(fn, [xx, cosw, sinw], [], [(width, F32)], name=name)[0]


def _log_sigmoid(x):
    return jnp.minimum(x, 0.0) - jnp.log(1.0 + jnp.exp(-jnp.abs(x)))


CUM_BLOCK = 256


def forget_cumsum(flog, bias, name):
    t = flog.shape[0]
    tb = _pick(t, (CUM_BLOCK,))

    def body(x_ref, b_ref, o_ref, carry):
        @pl.when(pl.program_id(0) == 0)
        def _():
            carry[...] = jnp.zeros_like(carry)

        lf = _log_sigmoid(x_ref[...] + b_ref[...])
        r = lax.broadcasted_iota(jnp.int32, (tb, tb), 0)
        c = lax.broadcasted_iota(jnp.int32, (tb, tb), 1)
        tri = (c <= r).astype(F32)
        cum = jnp.dot(tri, lf, preferred_element_type=F32, precision=lax.Precision.HIGHEST) + carry[...]
        o_ref[...] = cum
        carry[...] = cum[tb - 1:tb, :]

    return pl.pallas_call(
        body, name=name, grid=(t // tb,),
        in_specs=[pl.BlockSpec((tb, GATE_PAD), lambda i: (i, 0)), pl.BlockSpec((1, GATE_PAD), lambda i: (0, 0))],
        out_specs=pl.BlockSpec((tb, GATE_PAD), lambda i: (i, 0)),
        out_shape=jax.ShapeDtypeStruct((t, GATE_PAD), F32),
        scratch_shapes=[pltpu.VMEM((1, GATE_PAD), F32)],
        compiler_params=_cparams(("arbitrary",)),
    )(flog, bias)


def forget_cumsum_bwd(d_cum, flog, bias, name):
    t = flog.shape[0]
    tb = _pick(t, (CUM_BLOCK,))
    nb = t // tb

    def body(d_ref, x_ref, b_ref, o_ref, db_ref, carry):
        @pl.when(pl.program_id(0) == 0)
        def _():
            carry[...] = jnp.zeros_like(carry)
            db_ref[...] = jnp.zeros_like(db_ref)

        r = lax.broadcasted_iota(jnp.int32, (tb, tb), 0)
        c = lax.broadcasted_iota(jnp.int32, (tb, tb), 1)
        tri = (c >= r).astype(F32)
        dlf = jnp.dot(tri, d_ref[...], preferred_element_type=F32, precision=lax.Precision.HIGHEST) + carry[...]
        carry[...] = dlf[0:1, :]
        dx = dlf * (1.0 - _sigmoid(x_ref[...] + b_ref[...]))
        o_ref[...] = dx
        db_ref[...] += _fold8(dx)

    rev = lambda i: (nb - 1 - i, 0)
    return pl.pallas_call(
        body, name=name, grid=(nb,),
        in_specs=[pl.BlockSpec((tb, GATE_PAD), rev), pl.BlockSpec((tb, GATE_PAD), rev),
                  pl.BlockSpec((1, GATE_PAD), lambda i: (0, 0))],
        out_specs=[pl.BlockSpec((tb, GATE_PAD), rev), pl.BlockSpec((8, GATE_PAD), lambda i: (0, 0))],
        out_shape=[jax.ShapeDtypeStruct((t, GATE_PAD), F32), jax.ShapeDtypeStruct((8, GATE_PAD), F32)],
        scratch_shapes=[pltpu.VMEM((1, GATE_PAD), F32)],
        compiler_params=_cparams(("arbitrary",)),
    )(d_cum, flog, bias)


TQ = 512
TK = 128
AUG = 128
N_BIAS = 3
SB_CUTOFF = 110.0
FOX_CUTOFF = 112.0


def _dot(a, b):
    return jnp.dot(a, b, preferred_element_type=F32)


def _rel(shape, d):
    return lax.broadcasted_iota(jnp.int32, shape, 0) - lax.broadcasted_iota(jnp.int32, shape, 1) + d


def _put(x, lo, part, hi=None):
    hi = x.shape[1] if hi is None else hi
    pieces = ([x[:, :lo]] if lo else []) + [part] + ([x[:, hi:]] if hi < x.shape[1] else [])
    return part if len(pieces) == 1 else jnp.concatenate(pieces, axis=1)


def _stop_code(first, narrow):
    return first.astype(F32) + jnp.where(narrow, 0.5, 0.0)


def _read_stop_code(code_ref):
    code = jnp.max(code_ref[...])
    first = code.astype(jnp.int32)
    return first, (code - first.astype(F32)) > 0.25


def _split_bf16(x):
    hi = x.astype(BF16)
    return hi, (x - hi.astype(F32)).astype(BF16)


def _tri_dot(tri2, x):
    hi, lo = _split_bf16(x)
    return _dot(tri2, jnp.concatenate([hi, lo], axis=0))


def _q_cols(width, tq):
    return pl.BlockSpec((None, width, tq), lambda h, i: (h, 0, i))


def _q_rows(width, tq):
    return pl.BlockSpec((None, tq, width), lambda h, i: (h, i, 0))


def _kv_rows(t, width):
    return pl.BlockSpec((None, t, width), lambda h, i: (h, 0, 0))


def _kv_tiles(nk):
    return pl.BlockSpec((None, nk, HEAD_DIM, TK), lambda h, i: (h, 0, 0, 0))


def _blocks(t):
    tq = TQ if t % TQ == 0 else TK
    return tq, tq // TK, t // TK


def fox_fwd(qt_aug, k_aug, vt, f_end, k_norm):
    nh, _, t = qt_aug.shape
    tq, ratio, nk = _blocks(t)
    lanes = tq // 128

    def body(q_ref, k_ref, v_ref, fe_ref, kn_ref, o_ref, lse_ref, first_ref):
        i = pl.program_id(1)
        qv = q_ref[...]
        qf = qv[:HEAD_DIM].astype(F32)
        reach = jnp.sqrt(jnp.sum(qf * qf, axis=0, keepdims=True)) * jnp.tile(kn_ref[...], (1, lanes))

        def step(first, carry, masked, width=tq):
            m, l, acc = carry
            scores = []
            for u in range(ratio):
                lo, hi = (u * TK, tq) if masked else (0, width)
                off = pl.multiple_of((first + u) * TK, TK)
                s = _dot(k_ref[pl.ds(off, TK), :], qv[:, lo:hi])
                if masked:
                    s = jnp.where(_rel(s.shape, 0) <= 0, s, NEG_INF)
                scores.append((lo, hi, s))
            m_new = m
            for lo, hi, s in scores:
                m_new = _put(m_new, lo, jnp.maximum(m_new[:, lo:hi], jnp.max(s, axis=0, keepdims=True)), hi)
            alpha = jnp.exp(m - m_new)
            l = alpha * l
            acc = alpha * acc
            for u, (lo, hi, s) in enumerate(scores):
                p = jnp.exp(s - m_new[:, lo:hi])
                l = _put(l, lo, l[:, lo:hi] + jnp.sum(p, axis=0, keepdims=True), hi)
                acc = _put(acc, lo, acc[:, lo:hi] + _dot(v_ref[first + u], p.astype(BF16)), hi)
            return m_new, l, acc

        def slack(done, m):
            return reach + jnp.tile(fe_ref[jnp.maximum(i - 1 - done, 0)], (1, lanes)) - m

        def sweep_down(width, go, state):
            def more(c):
                return go & (c[0] < i) & (jnp.max(slack(c[0], c[1])[:, :width]) > -FOX_CUTOFF)

            def sweep(c):
                return (c[0] + 1,) + step(ratio * (i - 1 - c[0]), c[1:], False, width)

            return lax.while_loop(more, sweep, state)

        init = (jnp.full((1, tq), NEG_INF, F32), jnp.zeros((1, tq), F32), jnp.zeros((HEAD_DIM, tq), F32))
        state = (jnp.int32(0),) + step(ratio * i, init, True)
        if ratio > 1:
            narrow = jnp.max(slack(0, state[1])[:, tq // 2:]) <= -FOX_CUTOFF
            state = sweep_down(tq // 2, narrow, state)
            state = sweep_down(tq, jnp.logical_not(narrow), state)
        else:
            narrow = False
            state = sweep_down(tq, True, state)
        done, m, l, acc = state
        o_ref[...] = acc / l
        lse_ref[...] = m + jnp.log(l)
        first_ref[...] = jnp.full((1, 128), _stop_code(i - done, narrow), F32)

    return pl.pallas_call(
        body, name="fox_fwd", grid=(nh, t // tq),
        in_specs=[_q_cols(AUG, tq), _kv_rows(t, AUG), _kv_tiles(nk),
                  pl.BlockSpec((None, t // tq, 1, 128), lambda h, i: (h, 0, 0, 0)),
                  pl.BlockSpec((None, 1, 128), lambda h, i: (h, 0, 0))],
        out_specs=[_q_cols(HEAD_DIM, tq), _q_cols(1, tq), _q_cols(1, 128)],
        out_shape=[jax.ShapeDtypeStruct((nh, HEAD_DIM, t), F32), jax.ShapeDtypeStruct((nh, 1, t), F32),
                   jax.ShapeDtypeStruct((nh, 1, 128 * (t // tq)), F32)],
        compiler_params=_cparams(("parallel", "arbitrary")),
    )(qt_aug, k_aug, vt, f_end, k_norm)


def fox_bwd(qt_aug, q_aug, k_aug, kt, v, ot, do, dot_, lse, first):
    nh, _, t = qt_aug.shape
    tq, ratio, nk = _blocks(t)

    def body(qt_ref, q_ref, k_ref, kt_ref, v_ref, ot_ref, do_ref, dot_ref, lse_ref, first_ref,
             dqt_ref, dk_ref, dv_ref, rs_ref):
        i = pl.program_id(1)
        start, narrow = _read_stop_code(first_ref)

        @pl.when(i == 0)
        def _():
            dk_ref[...] = jnp.zeros_like(dk_ref)
            dv_ref[...] = jnp.zeros_like(dv_ref)

        qtv = qt_ref[...]
        qv = q_ref[...]
        dob = do_ref[...]
        dotb = dot_ref[...]
        delta = jnp.sum(ot_ref[...] * dotb.astype(F32), axis=0, keepdims=True)
        lse = lse_ref[...]

        def tile(j, carry, lo, hi, masked):
            dqt, rs = carry
            off = pl.multiple_of(j * TK, TK)
            s = _dot(k_ref[pl.ds(off, TK), :], qtv[:, lo:hi])
            p = jnp.exp(s - lse[:, lo:hi])
            if masked:
                p = jnp.where(_rel(s.shape, 0) <= 0, p, 0.0)
            dp = _dot(v_ref[pl.ds(off, TK), :], dotb[:, lo:hi])
            dsb = (p * (dp - delta[:, lo:hi])).astype(BF16)
            dk_ref[pl.ds(off, TK), :] += _dot(dsb, qv[lo:hi, :])
            dv_ref[pl.ds(off, TK), :] += _dot(p.astype(BF16), dob[lo:hi, :])
            return (_put(dqt, lo, dqt[:, lo:hi] + _dot(kt_ref[j], dsb), hi),
                    _put(rs, lo, rs[:, lo:hi] + jnp.sum(dsb.astype(F32), axis=0, keepdims=True), hi))

        def step(first, carry, masked, width=tq):
            for u in range(ratio):
                lo, hi = (u * TK, tq) if masked else (0, width)
                carry = tile(first + u, carry, lo, hi, masked)
            return carry

        carry = (jnp.zeros((HEAD_DIM, tq), F32), jnp.zeros((1, tq), F32))
        if ratio > 1:
            carry = lax.fori_loop(start, jnp.where(narrow, i, start),
                                  lambda jj, c: step(ratio * jj, c, False, tq // 2), carry)
            carry = lax.fori_loop(start, jnp.where(narrow, start, i), lambda jj, c: step(ratio * jj, c, False), carry)
        else:
            carry = lax.fori_loop(start, i, lambda jj, c: step(ratio * jj, c, False), carry)
        dqt, rs = step(ratio * i, carry, True)
        dqt_ref[...] = dqt * SCALE
        rs_ref[...] = rs

    return pl.pallas_call(
        body, name="fox_bwd", grid=(nh, t // tq),
        in_specs=[_q_cols(AUG, tq), _q_rows(AUG, tq), _kv_rows(t, AUG), _kv_tiles(nk), _kv_rows(t, HEAD_DIM),
                  _q_cols(HEAD_DIM, tq), _q_rows(HEAD_DIM, tq), _q_cols(HEAD_DIM, tq), _q_cols(1, tq),
                  _q_cols(1, 128)],
        out_specs=[_q_cols(HEAD_DIM, tq), _kv_rows(t, AUG), _kv_rows(t, HEAD_DIM), _q_cols(1, tq)],
        out_shape=[jax.ShapeDtypeStruct((nh, HEAD_DIM, t), F32), jax.ShapeDtypeStruct((nh, t, AUG), F32),
                   jax.ShapeDtypeStruct((nh, t, HEAD_DIM), F32), jax.ShapeDtypeStruct((nh, 1, t), F32)],
        compiler_params=_cparams(("arbitrary", "arbitrary")),
    )(qt_aug, q_aug, k_aug, kt, v, ot, do, dot_, lse, first)


def _sb_logits(kb, qv, ok):
    z = _dot(kb, qv)
    e = jnp.exp(-jnp.abs(z))
    ll = -(jnp.maximum(z, 0.0) + jnp.log(1.0 + e))
    if ok is not None:
        ll = jnp.where(ok, ll, 0.0)
    return z, e, ll


def _tri(cmp):
    r = lax.broadcasted_iota(jnp.int32, (TK, 2 * TK), 0)
    c = lax.broadcasted_iota(jnp.int32, (TK, 2 * TK), 1) % TK
    return cmp(r, c).astype(BF16)


def sb_fwd(qt, k, vt):
    nh, _, t = qt.shape
    tq, ratio, nk = _blocks(t)

    def body(q_ref, k_ref, v_ref, o_ref, tot_ref, first_ref):
        i = pl.program_id(1)
        qv = q_ref[...]
        tri_after = _tri(lambda r, c: c > r)

        def tile(j, carry, lo, hi, masked):
            c_l, acc = carry
            off = pl.multiple_of(j * TK, TK)
            ok = _rel((TK, hi - lo), 0) < 0 if masked else None
            z, _, ll = _sb_logits(k_ref[pl.ds(off, TK), :], qv[:, lo:hi], ok)
            a = jnp.exp(z + ll + _tri_dot(tri_after, ll) + c_l[:, lo:hi])
            if masked:
                a = jnp.where(ok, a, 0.0)
            return (_put(c_l, lo, c_l[:, lo:hi] + jnp.sum(ll, axis=0, keepdims=True), hi),
                    _put(acc, lo, acc[:, lo:hi] + _dot(v_ref[j], a.astype(BF16)), hi))

        def step(first, carry, masked, width=tq):
            for u in reversed(range(ratio)):
                lo, hi = (u * TK, tq) if masked else (0, width)
                carry = tile(first + u, carry, lo, hi, masked)
            return carry

        def sweep_down(width, go, state):
            def more(c):
                return go & (c[0] < i) & (jnp.max(c[1][:, :width]) > -SB_CUTOFF)

            def sweep(c):
                return (c[0] + 1,) + step(ratio * (i - 1 - c[0]), c[1:], False, width)

            return lax.while_loop(more, sweep, state)

        state = (jnp.int32(0),) + step(ratio * i, (jnp.zeros((1, tq), F32), jnp.zeros((HEAD_DIM, tq), F32)), True)
        if ratio > 1:
            narrow = jnp.max(state[1][:, tq // 2:]) <= -SB_CUTOFF
            state = sweep_down(tq // 2, narrow, state)
            state = sweep_down(tq, jnp.logical_not(narrow), state)
        else:
            narrow = False
            state = sweep_down(tq, True, state)
        done, c_l, acc = state
        o_ref[...] = acc
        tot_ref[...] = c_l
        first_ref[...] = jnp.full((1, 128), _stop_code(i - done, narrow), F32)

    return pl.pallas_call(
        body, name="sb_fwd", grid=(nh, t // tq),
        in_specs=[_q_cols(HEAD_DIM, tq), _kv_rows(t, HEAD_DIM), _kv_tiles(nk)],
        out_specs=[_q_cols(HEAD_DIM, tq), _q_cols(1, tq), _q_cols(1, 128)],
        out_shape=[jax.ShapeDtypeStruct((nh, HEAD_DIM, t), F32), jax.ShapeDtypeStruct((nh, 1, t), F32),
                   jax.ShapeDtypeStruct((nh, 1, 128 * (t // tq)), F32)],
        compiler_params=_cparams(("parallel", "arbitrary")),
    )(qt, k, vt)


def sb_bwd(qt, q, k, kt, v, ltot, first, do, dot_):
    nh, _, t = qt.shape
    tq, ratio, nk = _blocks(t)

    def body(qt_ref, q_ref, k_ref, kt_ref, v_ref, tot_ref, first_ref, do_ref, dot_ref, dqt_ref, dk_ref, dv_ref):
        i = pl.program_id(1)
        start, narrow = _read_stop_code(first_ref)

        @pl.when(i == 0)
        def _():
            dk_ref[...] = jnp.zeros_like(dk_ref)
            dv_ref[...] = jnp.zeros_like(dv_ref)

        qtv = qt_ref[...]
        qv = q_ref[...]
        dob = do_ref[...]
        dotb = dot_ref[...]
        tri_upto = _tri(lambda r, c: c <= r)
        tri_before = _tri(lambda r, c: c < r)

        def tile(j, carry, lo, hi, masked):
            rest, c_w, dqt = carry
            off = pl.multiple_of(j * TK, TK)
            ok = _rel((TK, hi - lo), 0) < 0 if masked else None
            z, e, ll = _sb_logits(k_ref[pl.ds(off, TK), :], qtv[:, lo:hi], ok)
            a = jnp.exp(z + ll + (rest[:, lo:hi] - _tri_dot(tri_upto, ll)))
            if masked:
                a = jnp.where(ok, a, 0.0)
            w = a * _dot(v_ref[pl.ds(off, TK), :], dotb[:, lo:hi])
            before = _tri_dot(tri_before, w) + c_w[:, lo:hi]
            r = 1.0 / (1.0 + e)
            sig = jnp.where(z >= 0.0, r, e * r)
            dz = w - (w + before) * sig
            if masked:
                dz = jnp.where(ok, dz, 0.0)
            dzb = dz.astype(BF16)
            dk_ref[pl.ds(off, TK), :] += _dot(dzb, qv[lo:hi, :])
            dv_ref[pl.ds(off, TK), :] += _dot(a.astype(BF16), dob[lo:hi, :])
            return (_put(rest, lo, rest[:, lo:hi] - jnp.sum(ll, axis=0, keepdims=True), hi),
                    _put(c_w, lo, c_w[:, lo:hi] + jnp.sum(w, axis=0, keepdims=True), hi),
                    _put(dqt, lo, dqt[:, lo:hi] + _dot(kt_ref[j], dzb), hi))

        def step(first, carry, masked, width=tq):
            for u in range(ratio):
                lo, hi = (u * TK, tq) if masked else (0, width)
                carry = tile(first + u, carry, lo, hi, masked)
            return carry

        carry = (tot_ref[...], jnp.zeros((1, tq), F32), jnp.zeros((HEAD_DIM, tq), F32))
        if ratio > 1:
            carry = lax.fori_loop(start, jnp.where(narrow, i, start),
                                  lambda jj, c: step(ratio * jj, c, False, tq // 2), carry)
            carry = lax.fori_loop(start, jnp.where(narrow, start, i), lambda jj, c: step(ratio * jj, c, False), carry)
        else:
            carry = lax.fori_loop(start, i, lambda jj, c: step(ratio * jj, c, False), carry)
        dqt_ref[...] = step(ratio * i, carry, True)[2] * SCALE

    hd = HEAD_DIM
    return pl.pallas_call(
        body, name="sb_bwd", grid=(nh, t // tq),
        in_specs=[_q_cols(hd, tq), _q_rows(hd, tq), _kv_rows(t, hd), _kv_tiles(nk), _kv_rows(t, hd),
                  _q_cols(1, tq), _q_cols(1, 128), _q_rows(hd, tq), _q_cols(hd, tq)],
        out_specs=[_q_cols(hd, tq), _kv_rows(t, hd), _kv_rows(t, hd)],
        out_shape=[jax.ShapeDtypeStruct((nh, hd, t), F32), jax.ShapeDtypeStruct((nh, t, hd), F32),
                   jax.ShapeDtypeStruct((nh, t, hd), F32)],
        compiler_params=_cparams(("arbitrary", "arbitrary")),
    )(qt, q, k, kt, v, ltot, first, do, dot_)


def _swa_q_cols(width):
    return pl.BlockSpec((GROUP, width, WINDOW), lambda g, i: (g, 0, i))


def _swa_q_rows():
    return pl.BlockSpec((GROUP, WINDOW, HEAD_DIM), lambda g, i: (g, i, 0))


def _swa_kv_rows(t):
    return pl.BlockSpec((None, t, HEAD_DIM), lambda g, i: (g, 0, 0))


def _swa_kv_tiles(nk):
    return pl.BlockSpec((None, nk, HEAD_DIM, WINDOW), lambda g, i: (g, 0, 0, 0))


def _swa_sink_spec():
    return pl.BlockSpec((GROUP, 1, 128), lambda g, i: (g, 0, 0))


def _lane_cat(parts):
    return jnp.concatenate(parts, axis=1)


def _swa_window(i):
    jb = jnp.maximum(i - 1, 0)
    start = pl.multiple_of(jb * WINDOW, WINDOW)
    shape = (2 * WINDOW, GROUP * WINDOW)
    query = lax.broadcasted_iota(jnp.int32, shape, 1) % WINDOW
    rel = lax.broadcasted_iota(jnp.int32, shape, 0) - query + (start - i * WINDOW)
    return jb, start, (rel <= 0) & (rel > -WINDOW)


def swa_fwd(qt, k, vt, sinks):
    nh, _, t = qt.shape
    nk = t // WINDOW

    def body(q_ref, k_ref, v_ref, s_ref, o_ref, lse_ref):
        i = pl.program_id(1)
        jb, start, valid = _swa_window(i)
        qv = _lane_cat([q_ref[g] for g in range(GROUP)])
        sink = _lane_cat([s_ref[g] for g in range(GROUP)])
        s = jnp.where(valid, _dot(k_ref[pl.ds(start, 2 * WINDOW), :], qv), NEG_INF)
        m = jnp.maximum(jnp.max(s, axis=0, keepdims=True), sink)
        p = jnp.where(valid, jnp.exp(s - m), 0.0)
        l = jnp.sum(p, axis=0, keepdims=True) + jnp.exp(sink - m)
        pb = p.astype(BF16)
        o = (_dot(v_ref[jb], pb[:WINDOW]) + _dot(v_ref[jb + 1], pb[WINDOW:])) / l
        lse = m + jnp.log(l)
        for g in range(GROUP):
            o_ref[g] = o[:, g * WINDOW:(g + 1) * WINDOW]
            lse_ref[g] = lse[:, g * WINDOW:(g + 1) * WINDOW]

    return pl.pallas_call(
        body, name="swa_fwd", grid=(N_KV, nk),
        in_specs=[_swa_q_cols(HEAD_DIM), _swa_kv_rows(t), _swa_kv_tiles(nk), _swa_sink_spec()],
        out_specs=[_swa_q_cols(HEAD_DIM), _swa_q_cols(1)],
        out_shape=[jax.ShapeDtypeStruct((nh, HEAD_DIM, t), F32), jax.ShapeDtypeStruct((nh, 1, t), F32)],
        compiler_params=_cparams(("parallel", "arbitrary")),
    )(qt, k, vt, sinks)


def swa_bwd(qt, q, k, kt, v, sinks, ot, do, dot_, lse):
    nh, _, t = qt.shape
    nk = t // WINDOW

    def body(qt_ref, q_ref, k_ref, kt_ref, v_ref, s_ref, ot_ref, do_ref, dot_ref, lse_ref,
             dqt_ref, dk_ref, dv_ref, dsink_ref):
        i = pl.program_id(1)

        @pl.when(i == 0)
        def _():
            dk_ref[...] = jnp.zeros_like(dk_ref)
            dv_ref[...] = jnp.zeros_like(dv_ref)
            dsink_ref[...] = jnp.zeros_like(dsink_ref)

        jb, start, valid = _swa_window(i)
        heads = range(GROUP)
        qtv = _lane_cat([qt_ref[g] for g in heads])
        dotb = _lane_cat([dot_ref[g] for g in heads])
        lse = _lane_cat([lse_ref[g] for g in heads])
        sink = _lane_cat([s_ref[g] for g in heads])
        otv = _lane_cat([ot_ref[g] for g in heads])
        q_rows = jnp.concatenate([q_ref[g] for g in heads], axis=0)
        do_rows = jnp.concatenate([do_ref[g] for g in heads], axis=0)
        delta = jnp.sum(otv * dotb.astype(F32), axis=0, keepdims=True)
        p = jnp.where(valid, jnp.exp(_dot(k_ref[pl.ds(start, 2 * WINDOW), :], qtv) - lse), 0.0)
        dsb = (p * (_dot(v_ref[pl.ds(start, 2 * WINDOW), :], dotb) - delta)).astype(BF16)
        dqt = (_dot(kt_ref[jb], dsb[:WINDOW]) + _dot(kt_ref[jb + 1], dsb[WINDOW:])) * SCALE
        dsink = -jnp.exp(sink - lse) * delta
        for g in heads:
            dqt_ref[g] = dqt[:, g * WINDOW:(g + 1) * WINDOW]
            dsink_ref[g] += dsink[:, g * WINDOW:(g + 1) * WINDOW]
        dk_ref[pl.ds(start, 2 * WINDOW), :] += _dot(dsb, q_rows)
        dv_ref[pl.ds(start, 2 * WINDOW), :] += _dot(p.astype(BF16), do_rows)

    hd = HEAD_DIM
    return pl.pallas_call(
        body, name="swa_bwd", grid=(N_KV, nk),
        in_specs=[_swa_q_cols(hd), _swa_q_rows(), _swa_kv_rows(t), _swa_kv_tiles(nk), _swa_kv_rows(t),
                  _swa_sink_spec(), _swa_q_cols(hd), _swa_q_rows(), _swa_q_cols(hd), _swa_q_cols(1)],
        out_specs=[_swa_q_cols(hd), _swa_kv_rows(t), _swa_kv_rows(t), _swa_sink_spec()],
        out_shape=[jax.ShapeDtypeStruct((nh, hd, t), F32), jax.ShapeDtypeStruct((N_KV, t, hd), F32),
                   jax.ShapeDtypeStruct((N_KV, t, hd), F32), jax.ShapeDtypeStruct((nh, 1, 128), F32)],
        compiler_params=_cparams(("arbitrary", "arbitrary")),
    )(qt, q, k, kt, v, sinks, ot, do, dot_, lse)


def head_rows(a, nh):
    t = a.shape[0]
    return a.reshape(t, nh, HEAD_DIM).transpose(1, 0, 2)


def head_cols(a, nh):
    t = a.shape[0]
    return a.reshape(t, nh, HEAD_DIM).transpose(1, 2, 0)


def head_tiles(a, nh):
    t = a.shape[0]
    return a.reshape(t // TK, TK, nh, HEAD_DIM).transpose(2, 0, 3, 1)


def rows_to_flat(a):
    nh, t, _ = a.shape
    return a.transpose(1, 0, 2).reshape(t, nh * HEAD_DIM)


def cols_to_flat(a):
    nh, _, t = a.shape
    return a.transpose(2, 0, 1).reshape(t, nh * HEAD_DIM)


def fox_keys(proj, cum, name):
    t = proj.shape[0]
    tr = _pick(t, (256, 128))
    width = (1 + N_BIAS) * 128

    def body(k_ref, c_ref, o_ref):
        kb = k_ref[...].astype(BF16)
        terms, rest = [], -c_ref[...]
        for _ in range(N_BIAS):
            term = rest.astype(BF16)
            terms.append(term)
            rest = rest - term.astype(F32)
        row = lax.broadcasted_iota(jnp.int32, (width, AUG), 0)
        col = lax.broadcasted_iota(jnp.int32, (width, AUG), 1)
        for h in range(N_FOX):
            pair = kb[:, 128 * (h // 2):128 * (h // 2 + 1)]
            place = (row < 128) & (col < HEAD_DIM) & (row - HEAD_DIM * (h % 2) == col)
            for b in range(N_BIAS):
                place = place | ((row == 128 * (1 + b) + h) & (col == HEAD_DIM + b))
            src = jnp.concatenate([pair] + terms, axis=1)
            o_ref[h] = _dot(src, place.astype(BF16)).astype(BF16)

    return pl.pallas_call(
        body, name=name, grid=(t // tr,),
        in_specs=[pl.BlockSpec((tr, FOX_W), lambda i: (i, 1)), pl.BlockSpec((tr, GATE_PAD), lambda i: (i, 0))],
        out_specs=pl.BlockSpec((N_FOX, tr, AUG), lambda i: (0, i, 0)),
        out_shape=jax.ShapeDtypeStruct((N_FOX, t, AUG), BF16),
        compiler_params=_cparams(("parallel",)),
    )(proj, cum)


def fox_operands(qf, kf, cum_heads):
    t = qf.shape[0]
    nh = cum_heads.shape[0]
    tq = _blocks(t)[0]
    qs = (qf * SCALE).astype(BF16)
    ones_t = jnp.ones((nh, N_BIAS, t), BF16)
    qt_aug = jnp.concatenate([head_cols(qs, nh), ones_t, jnp.zeros((nh, AUG - HEAD_DIM - N_BIAS, t), BF16)], axis=1)
    own_lane = jnp.broadcast_to(jnp.eye(nh, dtype=BF16)[:, None, :], (nh, t, nh))
    q_aug = jnp.concatenate([head_rows(qs, nh), own_lane, jnp.zeros((nh, t, AUG - HEAD_DIM - nh), BF16)], axis=2)
    f_end = jnp.broadcast_to((-cum_heads)[:, tq - 1::tq, None, None], (nh, t // tq, 1, 128))
    k_sq = jnp.sum(jnp.square(kf).reshape(t, nh, HEAD_DIM), axis=2)
    k_norm = jnp.broadcast_to(1.01 * jnp.sqrt(jnp.max(k_sq, axis=0))[:, None, None], (nh, 1, 128))
    return qt_aug, q_aug, f_end, k_norm


def adamw(w, g, m, v, name):
    cols = w.shape[1]

    def fn(ww, gg, mm, vv):
        mn = ADAM_B1 * mm + (1.0 - ADAM_B1) * gg
        vn = ADAM_B2 * vv + (1.0 - ADAM_B2) * (gg * gg)
        m_hat = mn / (1.0 - ADAM_B1 ** ADAM_STEP)
        v_hat = vn / (1.0 - ADAM_B2 ** ADAM_STEP)
        delta = -ADAM_LR * (m_hat / (jnp.sqrt(v_hat) + ADAM_EPS) + ADAM_WD * ww)
        return (delta, mn, vn), ()

    return rowwise(fn, [w, g, m, v], [], [(cols, F32)] * 3, name=name)


ANY = pl.BlockSpec(memory_space=pl.ANY)


def _my_place():
    return lax.axis_index("x"), lax.axis_index("y"), lax.axis_index("c")


def _flip(coord, bit):
    return 1 - coord if bit else coord


def allgather_chips(w):
    r, c = w.shape
    rh = r // 2

    def body(w_ref, out_ref, send_sems, recv_sems, local_sem):
        x, y, cc = _my_place()
        me = 2 * x + y
        sibling = (x, y, 1 - cc)
        chips = [(_flip(x, kbits >> 1), _flip(y, kbits & 1)) for kbits in (1, 2, 3)]

        def half(chip, hc):
            return out_ref.at[chip, pl.ds(pl.multiple_of(hc * rh, 16), rh)]

        def copy(k, src, dst, to):
            return pltpu.make_async_remote_copy(src_ref=src, dst_ref=dst, send_sem=send_sems.at[k],
                                                recv_sem=recv_sems.at[k], device_id=to, device_id_type=MESH)

        local = pltpu.make_async_copy(w_ref, out_ref.at[me], local_sem)
        local.start()
        my_half = w_ref.at[pl.ds(pl.multiple_of(cc * rh, 16), rh)]
        first = [copy(j, my_half, half(me, cc), (px, py, cc)) for j, (px, py) in enumerate(chips)]
        for cp in first:
            cp.start()
        passed = []
        for j, (px, py) in enumerate(chips):
            landed = half(2 * px + py, cc)
            copy(j, my_half, landed, (px, py, cc)).wait_recv()
            fwd = copy(3 + j, landed, landed, sibling)
            fwd.start()
            passed.append(fwd)
        for j, (px, py) in enumerate(chips):
            theirs = half(2 * px + py, 1 - cc)
            copy(3 + j, theirs, theirs, sibling).wait_recv()
        for cp in first + passed:
            cp.wait_send()
        local.wait()

    return pl.pallas_call(
        body, name="allgather_chips", in_specs=[ANY], out_specs=ANY,
        out_shape=jax.ShapeDtypeStruct((N_CHIPS, r, c), w.dtype),
        scratch_shapes=[pltpu.SemaphoreType.DMA((6,)), pltpu.SemaphoreType.DMA((6,)), pltpu.SemaphoreType.DMA],
    )(w)


def pair_send_other_half(g):
    n, _, rh, c = g.shape

    def body(g_ref, out_ref, send_sem, recv_sem):
        x, y, cc = _my_place()
        cp = pltpu.make_async_remote_copy(
            src_ref=g_ref.at[:, 1 - cc], dst_ref=out_ref, send_sem=send_sem, recv_sem=recv_sem,
            device_id=(x, y, 1 - cc), device_id_type=MESH)
        cp.start()
        cp.wait()

    return pl.pallas_call(
        body, name="pair_send_other_half", in_specs=[ANY], out_specs=ANY,
        out_shape=jax.ShapeDtypeStruct((n, rh, c), g.dtype),
        scratch_shapes=[pltpu.SemaphoreType.DMA, pltpu.SemaphoreType.DMA],
    )(g)


def exchange_chips(s):
    n, rh, c = s.shape

    def body(s_ref, out_ref, send_sems, recv_sems, local_sem):
        x, y, cc = _my_place()
        me = 2 * x + y
        local = pltpu.make_async_copy(s_ref.at[me], out_ref.at[me], local_sem)
        local.start()
        copies = []
        for kbits in (1, 2, 3):
            px, py = _flip(x, kbits >> 1), _flip(y, kbits & 1)
            cp = pltpu.make_async_remote_copy(
                src_ref=s_ref.at[2 * px + py], dst_ref=out_ref.at[me], send_sem=send_sems.at[kbits - 1],
                recv_sem=recv_sems.at[kbits - 1], device_id=(px, py, cc), device_id_type=MESH)
            cp.start()
            copies.append(cp)
        for cp in copies:
            cp.wait()
        local.wait()

    return pl.pallas_call(
        body, name="exchange_chips", in_specs=[ANY], out_specs=ANY,
        out_shape=jax.ShapeDtypeStruct((n, rh, c), s.dtype),
        scratch_shapes=[pltpu.SemaphoreType.DMA((3,)), pltpu.SemaphoreType.DMA((3,)), pltpu.SemaphoreType.DMA],
    )(s)


def pair_swap(tt):
    def body(t_ref, out_ref, send_sem, recv_sem):
        x, y, cc = _my_place()
        cp = pltpu.make_async_remote_copy(
            src_ref=t_ref, dst_ref=out_ref, send_sem=send_sem, recv_sem=recv_sem,
            device_id=(x, y, 1 - cc), device_id_type=MESH)
        cp.start()
        cp.wait()

    return pl.pallas_call(
        body, name="pair_swap", in_specs=[ANY], out_specs=ANY,
        out_shape=jax.ShapeDtypeStruct(tt.shape, tt.dtype),
        scratch_shapes=[pltpu.SemaphoreType.DMA, pltpu.SemaphoreType.DMA],
    )(tt)


SMALL_ROWS = 16


def allreduce_small(v):
    r, c = v.shape
    vm = pl.BlockSpec(memory_space=pltpu.VMEM)

    def body(v_ref, out_ref, slots, send_sems, recv_sems):
        x, y, cc = _my_place()
        me = 4 * x + 2 * y + cc
        slots[me] = v_ref[...]
        copies = []
        for kbits in range(1, 8):
            peer = (_flip(x, kbits >> 2), _flip(y, (kbits >> 1) & 1), _flip(cc, kbits & 1))
            cp = pltpu.make_async_remote_copy(
                src_ref=v_ref, dst_ref=slots.at[me], send_sem=send_sems.at[kbits - 1],
                recv_sem=recv_sems.at[kbits - 1], device_id=peer, device_id_type=MESH)
            cp.start()
            copies.append(cp)
        for cp in copies:
            cp.wait()
        total = slots[0]
        for dev in range(1, 8):
            total = total + slots[dev]
        out_ref[...] = total

    return pl.pallas_call(
        body, name="allreduce_small", in_specs=[vm], out_specs=vm,
        out_shape=jax.ShapeDtypeStruct((r, c), F32),
        scratch_shapes=[pltpu.VMEM((8, r, c), F32), pltpu.SemaphoreType.DMA((7,)), pltpu.SemaphoreType.DMA((7,))],
    )(v)


def add_pair(mine, theirs, name):
    return rowwise(lambda a, b: ((a + b,), ()), [mine, theirs], [], [(mine.shape[1], BF16)], name=name)[0]


def sum_chips(r4, name):
    _, rh, c = r4.shape
    tr = _pick(rh, (256, 128, 64, 32, 16))

    def body(r_ref, o_ref):
        total = r_ref[0].astype(F32)
        for j in range(1, N_CHIPS):
            total = total + r_ref[j].astype(F32)
        o_ref[...] = total

    return pl.pallas_call(
        body, name=name, grid=(rh // tr,),
        in_specs=[pl.BlockSpec((N_CHIPS, tr, c), lambda i: (0, i, 0))],
        out_specs=pl.BlockSpec((tr, c), lambda i: (i, 0)),
        out_shape=jax.ShapeDtypeStruct((rh, c), F32),
        compiler_params=_cparams(("parallel",)),
    )(r4)


def _pad_rows(a, rows):
    return jnp.pad(a, ((0, rows - a.shape[0]), (0, 0))) if rows != a.shape[0] else a


def pack_shards(local):
    parts = []
    for name, layer, r, c, _ in _PACK:
        flat = local[name][layer].astype(BF16).reshape(r * c // D_MODEL, D_MODEL)
        parts.append(_pad_rows(flat, _pack_rows(r, c)))
    used = sum(p.shape[0] for p in parts)
    parts.append(jnp.zeros((PACK_ROWS - used, D_MODEL), BF16))
    return jnp.concatenate(parts, axis=0)


def unpack_full(gathered):
    full, off = {}, 0
    for name, layer, r, c, by_cols in _PACK:
        n = r * c // D_MODEL
        blk = gathered[:, off:off + n, :].reshape(N_CHIPS, r, c)
        if by_cols:
            full[(name, layer)] = blk.transpose(1, 0, 2).reshape(r, N_CHIPS * c)
        else:
            full[(name, layer)] = blk.reshape(N_CHIPS * r, c)
        off += _pack_rows(r, c)
    return full


def pack_grads(grads):
    parts = []
    for name, layer, r, c, by_cols in _PACK:
        g = grads[(name, layer)]
        if by_cols:
            blk = g.reshape(r, N_CHIPS, c).transpose(1, 0, 2)
        else:
            blk = g.reshape(N_CHIPS, r, c)
        flat = blk.reshape(N_CHIPS, r * c // D_MODEL, D_MODEL)
        rows = _pack_rows(r, c)
        if rows != flat.shape[1]:
            flat = jnp.pad(flat, ((0, 0), (0, rows - flat.shape[1]), (0, 0)))
        parts.append(flat)
    used = sum(p.shape[1] for p in parts)
    parts.append(jnp.zeros((N_CHIPS, PACK_ROWS - used, D_MODEL), F32))
    return jnp.concatenate(parts, axis=1)


def unpack_local(flat):
    out, off = {}, 0
    for name, layer, r, c, _ in _PACK:
        n = r * c // D_MODEL
        out[(name, layer)] = flat[off:off + n, :].reshape(r, c)
        off += _pack_rows(r, c)
    return out


def rope_tables(pos):
    half = HEAD_DIM // 2
    lane = jnp.arange(128)
    inv = ROPE_THETA ** (-(lane % half).astype(F32) / half)
    ang = pos.astype(F32)[:, None] * inv[None, :]
    sign = jnp.where((lane % HEAD_DIM) < half, -1.0, 1.0).astype(F32)
    return jnp.cos(ang), jnp.sin(ang) * sign[None, :]


def local_step(x, p, pos, tgt, norm_mix, norm_ffn, norm_ple, norm_final, ev_b_f, od_sinks, wfull):
    t = x.shape[0]
    w_in0 = wfull[("ev_w_in", 0)]
    w_in0 = jnp.concatenate([w_in0, jnp.zeros((D_MODEL, EVEN_IN_PAD - w_in0.shape[1]), w_in0.dtype)], axis=1)
    b_pad = jnp.zeros((1, GATE_PAD), F32).at[0, :N_FOX].set(ev_b_f[0])
    sinks_b = jnp.broadcast_to(od_sinks[0][:, None, None], (N_Q, 1, 128)).astype(F32)
    cosw, sinw = rope_tables(pos)

    saved = []
    h = x
    for i in range(2):
        s = {"h0": h}
        n1 = rmsnorm_fwd(h, norm_mix[i], f"norm_mix_fwd{i}")
        s["n1"] = n1
        if i == 0:
            proj = matmul(n1, w_in0, name="ev_in")
            cols = [proj[:, j * FOX_W:(j + 1) * FOX_W] for j in range(6)]
            flog = proj[:, EVEN_QKV:]
            cum = forget_cumsum(flog, b_pad, "forget_cumsum")
            fox = dict(zip(("qt", "q", "f_end", "k_norm"), fox_operands(cols[0], cols[1], cum[:, :N_FOX].T)))
            fox["k"] = fox_keys(proj, cum, "fox_keys")
            fox.update(kt=head_tiles(cols[1].astype(BF16), N_FOX), v=head_rows(cols[2].astype(BF16), N_FOX),
                       vt=head_tiles(cols[2].astype(BF16), N_FOX))
            q_sb = (cols[3] * SCALE).astype(BF16)
            sb = dict(qt=head_cols(q_sb, N_SB), q=head_rows(q_sb, N_SB), k=head_rows(cols[4].astype(BF16), N_SB),
                      kt=head_tiles(cols[4].astype(BF16), N_SB), v=head_rows(cols[5].astype(BF16), N_SB),
                      vt=head_tiles(cols[5].astype(BF16), N_SB))
            fox["ot"], fox["lse"], fox["first"] = fox_fwd(fox["qt"], fox["k"], fox["vt"], fox["f_end"], fox["k_norm"])
            sb["ot"], sb["ltot"], sb["first"] = sb_fwd(sb["qt"], sb["k"], sb["vt"])
            s.update(flog=flog, fox=fox, sb=sb)
            mixin_t = jnp.concatenate([fox["ot"].reshape(FOX_W, t), sb["ot"].reshape(SB_W, t)], axis=0).astype(BF16)
            w_out = wfull[("ev_w_out", 0)]
        else:
            proj = matmul(n1, wfull[("od_w_in", 0)], name="od_in")
            qk = rope_apply(proj[:, :Q_W + KV_W], cosw, sinw, False, "rope_fwd")
            q_sc = (qk[:, :Q_W] * SCALE).astype(BF16)
            k_b = qk[:, Q_W:].astype(BF16)
            v_b = proj[:, Q_W + KV_W:].astype(BF16)
            swa = dict(qt=head_cols(q_sc, N_Q), q=head_rows(q_sc, N_Q), k=head_rows(k_b, N_KV),
                       kt=head_tiles(k_b, N_KV), v=head_rows(v_b, N_KV), vt=head_tiles(v_b, N_KV))
            swa["ot"], swa["lse"] = swa_fwd(swa["qt"], swa["k"], swa["vt"], sinks_b)
            s["swa"] = swa
            mixin_t = swa["ot"].reshape(Q_W, t).astype(BF16)
            w_out = wfull[("od_w_out", 0)]
        s["mixin_t"] = mixin_t
        h = matmul(mixin_t, w_out, "tn", residual=h, name=f"mix_out{i}")
        s["h1"] = h
        n2 = rmsnorm_fwd(h, norm_ffn[i], f"norm_ffn_fwd{i}")
        gate, up, act = ffn_gate_up(n2, wfull[("ffn_w_gate", i)], wfull[("ffn_w_up", i)], f"ffn_gate_up{i}")
        s.update(n2=n2, gate=gate, up=up, act=act)
        h = matmul(act, wfull[("ffn_w_down", i)], residual=h, name=f"ffn_down{i}")
        s["h2"] = h
        n3 = rmsnorm_fwd(h, norm_ple[i], f"norm_ple_fwd{i}")
        pre = matmul(n3, wfull[("ple_w_gate", i)], name=f"ple_gate{i}")
        pp = matmul(p[i], wfull[("ple_w_proj", i)], name=f"ple_proj{i}")
        s.update(n3=n3, pre=pre, pp=pp)
        h = ple_fwd(h, pre, pp, f"ple_fwd{i}")
        saved.append(s)

    dh, dg_final, loss8 = loss_head(h, norm_final, tgt, "loss_head")
    gw = {}
    small = {"norm_final": dg_final, "loss": loss8}
    for i in (1, 0):
        s = saved[i]
        dpre, dpp = ple_bwd(dh, s["pre"], s["pp"], f"ple_bwd{i}")
        gw[("ple_w_gate", i)] = matmul(s["n3"], dpre, "tn", name=f"d_ple_gate{i}")
        gw[("ple_w_proj", i)] = matmul(p[i], dpp, "tn", name=f"d_ple_proj{i}")
        dn3 = matmul(dpre, wfull[("ple_w_gate", i)], "nt", name=f"dn_ple{i}")
        dh, small[("norm_ple", i)] = rmsnorm_bwd(s["h2"], norm_ple[i], dn3, dh, f"norm_ple_bwd{i}")

        dgate, dup = ffn_gate_up_bwd(dh, wfull[("ffn_w_down", i)], s["gate"], s["up"], f"ffn_gate_up_bwd{i}")
        gw[("ffn_w_down", i)] = matmul(s["act"], dh, "tn", name=f"d_ffn_down{i}")
        gw[("ffn_w_gate", i)] = matmul(s["n2"], dgate, "tn", name=f"d_ffn_gate{i}")
        gw[("ffn_w_up", i)] = matmul(s["n2"], dup, "tn", name=f"d_ffn_up{i}")
        dn2 = matmul(dgate, wfull[("ffn_w_gate", i)], "nt", name=f"dn_ffn_gate{i}")
        dn2 = matmul(dup, wfull[("ffn_w_up", i)], "nt", residual=dn2, name=f"dn_ffn_up{i}")
        dh, small[("norm_ffn", i)] = rmsnorm_bwd(s["h1"], norm_ffn[i], dn2, dh, f"norm_ffn_bwd{i}")

        if i == 0:
            dmb = matmul(dh, wfull[("ev_w_out", 0)], "nt", out_dtype=BF16, name="d_mix0")
            gw[("ev_w_out", 0)] = matmul(s["mixin_t"], dh, name="d_ev_out")
            fox, sb = s["fox"], s["sb"]
            dqt_f, dk_aug, dv_f, ds_rows = fox_bwd(
                fox["qt"], fox["q"], fox["k"], fox["kt"], fox["v"], fox["ot"],
                head_rows(dmb[:, :FOX_W], N_FOX), head_cols(dmb[:, :FOX_W], N_FOX), fox["lse"], fox["first"])
            dqt_s, dk_s, dv_s = sb_bwd(sb["qt"], sb["q"], sb["k"], sb["kt"], sb["v"], sb["ltot"], sb["first"],
                                       head_rows(dmb[:, FOX_W:], N_SB), head_cols(dmb[:, FOX_W:], N_SB))
            ds_cols = jnp.sum(dk_aug[:, :, HEAD_DIM:HEAD_DIM + N_FOX], axis=0)
            d_cum = jnp.pad(ds_rows[:, 0, :].T - ds_cols, ((0, 0), (0, GATE_PAD - N_FOX)))
            dflog, db8 = forget_cumsum_bwd(d_cum, s["flog"], b_pad, "forget_cumsum_bwd")
            small["ev_b_f"] = db8
            parts = (cols_to_flat(dqt_f), rows_to_flat(dk_aug[:, :, :HEAD_DIM]), rows_to_flat(dv_f),
                     cols_to_flat(dqt_s), rows_to_flat(dk_s), rows_to_flat(dv_s), dflog)
            dproj = jnp.concatenate([a.astype(BF16) for a in parts], axis=1)
            dw = matmul(s["n1"], dproj, "tn", name="d_ev_in")
            gw[("ev_w_in", 0)] = dw[:, :EVEN_QKV + N_FOX]
            dn1 = matmul(dproj, w_in0, "nt", name="dn_mix0")
        else:
            dmb = matmul(dh, wfull[("od_w_out", 0)], "nt", out_dtype=BF16, name="d_mix1")
            gw[("od_w_out", 0)] = matmul(s["mixin_t"], dh, name="d_od_out")
            swa = s["swa"]
            dqt, dk, dv, dsink = swa_bwd(swa["qt"], swa["q"], swa["k"], swa["kt"], swa["v"], sinks_b, swa["ot"],
                                         head_rows(dmb, N_Q), head_cols(dmb, N_Q), swa["lse"])
            small["od_sinks"] = dsink
            dqk = rope_apply(jnp.concatenate([cols_to_flat(dqt), rows_to_flat(dk)], axis=1), cosw, sinw, True,
                             "rope_bwd")
            dproj = jnp.concatenate([dqk, rows_to_flat(dv)], axis=1).astype(BF16)
            gw[("od_w_in", 0)] = matmul(s["n1"], dproj, "tn", name="d_od_in")
            dn1 = matmul(dproj, wfull[("od_w_in", 0)], "nt", name="dn_mix1")
        dh, small[("norm_mix", i)] = rmsnorm_bwd(s["h0"], norm_mix[i], dn1, dh, f"norm_mix_bwd{i}")
    return dh, gw, small


_SMALL_ROWS = (("norm_mix", 0), ("norm_mix", 1), ("norm_ffn", 0), ("norm_ffn", 1),
               ("norm_ple", 0), ("norm_ple", 1), "norm_final", "misc", "loss")


def pack_small(small):
    rows = []
    for key in _SMALL_ROWS:
        if key == "misc":
            db = jnp.sum(small["ev_b_f"], axis=0)[:N_FOX]
            dsink = jnp.sum(small["od_sinks"][:, 0, :], axis=1)
            rows.append(jnp.zeros((D_MODEL,), F32).at[:N_FOX].set(db).at[128:128 + N_Q].set(dsink))
        else:
            rows.append(jnp.sum(small[key], axis=0))
    rows += [jnp.zeros((D_MODEL,), F32)] * (SMALL_ROWS - len(rows))
    return jnp.stack(rows)


def kernel(x, p, positions, norm_mix, norm_ffn, norm_ple, norm_final, ev_w_in, ev_b_f, ev_w_out, od_w_in, od_sinks, od_w_out, ffn_w_gate, ffn_w_up, ffn_w_down, ple_w_proj, ple_w_gate, loss_target, m_norm_mix, m_norm_ffn, m_norm_ple, m_norm_final, m_ev_w_in, m_ev_b_f, m_ev_w_out, m_od_w_in, m_od_sinks, m_od_w_out, m_ffn_w_gate, m_ffn_w_up, m_ffn_w_down, m_ple_w_proj, m_ple_w_gate, v_norm_mix, v_norm_ffn, v_norm_ple, v_norm_final, v_ev_w_in, v_ev_b_f, v_ev_w_out, v_od_w_in, v_od_sinks, v_od_w_out, v_ffn_w_gate, v_ffn_w_up, v_ffn_w_down, v_ple_w_proj, v_ple_w_gate):
    local_w = dict(ev_w_in=ev_w_in, ev_w_out=ev_w_out, od_w_in=od_w_in, od_w_out=od_w_out,
                   ffn_w_gate=ffn_w_gate, ffn_w_up=ffn_w_up, ffn_w_down=ffn_w_down,
                   ple_w_proj=ple_w_proj, ple_w_gate=ple_w_gate)
    local_m = dict(ev_w_in=m_ev_w_in, ev_w_out=m_ev_w_out, od_w_in=m_od_w_in, od_w_out=m_od_w_out,
                   ffn_w_gate=m_ffn_w_gate, ffn_w_up=m_ffn_w_up, ffn_w_down=m_ffn_w_down,
                   ple_w_proj=m_ple_w_proj, ple_w_gate=m_ple_w_gate)
    local_v = dict(ev_w_in=v_ev_w_in, ev_w_out=v_ev_w_out, od_w_in=v_od_w_in, od_w_out=v_od_w_out,
                   ffn_w_gate=v_ffn_w_gate, ffn_w_up=v_ffn_w_up, ffn_w_down=v_ffn_w_down,
                   ple_w_proj=v_ple_w_proj, ple_w_gate=v_ple_w_gate)

    wfull = unpack_full(allgather_chips(pack_shards(local_w)))
    grad_x, gw, small = local_step(x[0], p[:, 0], positions[0], loss_target[0], norm_mix, norm_ffn, norm_ple,
                                   norm_final, ev_b_f, od_sinks, wfull)

    cc = lax.axis_index("c")
    g4 = pack_grads(gw).reshape(N_CHIPS, 2, PACK_ROWS_HALF, D_MODEL)
    theirs = pair_send_other_half(g4)
    mine = lax.dynamic_index_in_dim(g4, cc, axis=1, keepdims=False)
    pair_sum = add_pair(mine.reshape(N_CHIPS * PACK_ROWS_HALF, D_MODEL),
                        theirs.reshape(N_CHIPS * PACK_ROWS_HALF, D_MODEL), "add_pair")
    from_chips = exchange_chips(pair_sum.reshape(N_CHIPS, PACK_ROWS_HALF, D_MODEL))
    half_sum = sum_chips(from_chips, "sum_chips")
    other_half = pair_swap(half_sum)
    low = jnp.where(cc == 0, half_sum, other_half)
    high = jnp.where(cc == 0, other_half, half_sum)
    g_local = unpack_local(jnp.concatenate([low, high], axis=0))

    red = allreduce_small(pack_small(small))
    loss = 0.5 * jnp.sum(red[8]) / D_MODEL
    pad_small = lambda a: jnp.zeros((D_MODEL,), F32).at[:N_FOX].set(a[0][0]).at[128:128 + N_Q].set(a[1][0])
    stack_small = lambda a: jnp.concatenate(
        [a[0], a[1], a[2], a[3][None], pad_small(a[4:6])[None], jnp.zeros((SMALL_ROWS - 8, D_MODEL), F32)], axis=0)
    w_small = stack_small((norm_mix, norm_ffn, norm_ple, norm_final, ev_b_f, od_sinks))
    m_small = stack_small((m_norm_mix, m_norm_ffn, m_norm_ple, m_norm_final, m_ev_b_f, m_od_sinks))
    v_small = stack_small((v_norm_mix, v_norm_ffn, v_norm_ple, v_norm_final, v_ev_b_f, v_od_sinks))
    g_small = red.at[8].set(0.0)
    upd_small = (g_small,) + tuple(adamw(w_small, g_small, m_small, v_small, "adamw_small"))

    def split_small(a):
        return (a[0:2], a[2:4], a[4:6], a[6], a[7, :N_FOX][None], a[7, 128:128 + N_Q][None])

    small_out = [split_small(a) for a in upd_small]

    big_names = ("ev_w_in", "ev_w_out", "od_w_in", "od_w_out", "ffn_w_gate", "ffn_w_up", "ffn_w_down",
                 "ple_w_proj", "ple_w_gate")
    big_out = {}
    for name in big_names:
        w = local_w[name]
        layers, r, c = w.shape
        g = jnp.concatenate([g_local[(name, i)] for i in range(layers)], axis=0)
        res = adamw(w.reshape(layers * r, c), g, local_m[name].reshape(layers * r, c),
                    local_v[name].reshape(layers * r, c), f"adamw_{name}")
        big_out[name] = [a.reshape(layers, r, c) for a in (g,) + tuple(res)]

    outs = [loss, grad_x[None]]
    for kind in range(4):
        sm = small_out[kind]
        outs += [sm[0], sm[1], sm[2], sm[3],
                 big_out["ev_w_in"][kind], sm[4], big_out["ev_w_out"][kind],
                 big_out["od_w_in"][kind], sm[5], big_out["od_w_out"][kind],
                 big_out["ffn_w_gate"][kind], big_out["ffn_w_up"][kind], big_out["ffn_w_down"][kind],
                 big_out["ple_w_proj"][kind], big_out["ple_w_gate"][kind]]
    return tuple(outs)
```

```python
import jax
import jax.numpy as jnp
from jax import lax
from jax.experimental import pallas as pl
from jax.experimental.pallas import tpu as pltpu

F32 = jnp.float32
BF16 = jnp.bfloat16

D_MODEL = 1024
HEAD_DIM = 64
N_FOX = 8
N_SB = 8
FOX_W = N_FOX * HEAD_DIM
SB_W = N_SB * HEAD_DIM
EVEN_QKV = 3 * FOX_W + 3 * SB_W
GATE_PAD = 128
EVEN_IN_PAD = EVEN_QKV + GATE_PAD
N_Q = 16
N_KV = 4
GROUP = N_Q // N_KV
Q_W = N_Q * HEAD_DIM
KV_W = N_KV * HEAD_DIM
ODD_IN = Q_W + 2 * KV_W
WINDOW = 128
ROPE_THETA = 10000.0
D_FF = 2816
PLE_DIM = 256
EPS = 1e-6
NEG_INF = -1e30
SCALE = HEAD_DIM ** -0.5

ADAM_LR = 0.001
ADAM_B1 = 0.9
ADAM_B2 = 0.999
ADAM_EPS = 1e-08
ADAM_WD = 0.01
ADAM_STEP = 10

N_CHIPS = 4
VMEM_LIMIT = 48 * 1024 * 1024
MESH = pl.DeviceIdType.MESH

_PACK = (
    ("ev_w_in", 0, 1024, 770, True),
    ("ev_w_out", 0, 256, 1024, False),
    ("od_w_in", 0, 1024, 384, True),
    ("od_w_out", 0, 256, 1024, False),
    ("ffn_w_gate", 0, 1024, 704, True),
    ("ffn_w_gate", 1, 1024, 704, True),
    ("ffn_w_up", 0, 1024, 704, True),
    ("ffn_w_up", 1, 1024, 704, True),
    ("ffn_w_down", 0, 704, 1024, False),
    ("ffn_w_down", 1, 704, 1024, False),
    ("ple_w_proj", 0, 256, 256, True),
    ("ple_w_proj", 1, 256, 256, True),
    ("ple_w_gate", 0, 256, 1024, False),
    ("ple_w_gate", 1, 256, 1024, False),
)
_ROW_ALIGN = 16


def _transposed(r, by_cols):
    return by_cols and r == D_MODEL


def _pack_rows(r, c):
    n = r * c // D_MODEL
    return -(-n // _ROW_ALIGN) * _ROW_ALIGN


PACK_ROWS_HALF = 3328
PACK_ROWS = 2 * PACK_ROWS_HALF
assert sum(_pack_rows(r, c) for _, _, r, c, _ in _PACK) <= PACK_ROWS


def _pick(n, cands):
    for c in cands:
        if n % c == 0:
            return c
    return n


def _cparams(sem):
    return pltpu.CompilerParams(dimension_semantics=sem, vmem_limit_bytes=VMEM_LIMIT)


_DIMS = {
    "nn": (((1,), (0,)), ((), ())),
    "nt": (((1,), (1,)), ((), ())),
    "tn": (((0,), (0,)), ((), ())),
}


def matmul(a, b, mode="nn", out_dtype=F32, residual=None, name="mm"):
    if mode == "nn":
        (m, k), (k2, n) = a.shape, b.shape
    elif mode == "nt":
        (m, k), (n, k2) = a.shape, b.shape
    else:
        (k, m), (k2, n) = a.shape, b.shape
    assert k == k2, (a.shape, b.shape, mode)
    tm = _pick(m, (1024, 1408, 640, 512, 256, 128))
    tn = _pick(n, (1024, 1408, 512, 640, 384, 256, 128))
    tk = _pick(k, (1024, 1408, 640, 512, 256, 128))
    nk = k // tk
    dims = _DIMS[mode]
    has_res = residual is not None

    def body(*refs):
        if has_res:
            a_ref, b_ref, r_ref, o_ref, acc = refs
        else:
            a_ref, b_ref, o_ref, acc = refs
        kk = pl.program_id(2)
        part = lax.dot_general(a_ref[...].astype(BF16), b_ref[...].astype(BF16), dims,
                               preferred_element_type=F32)

        def finish(r):
            if has_res:
                r = r + r_ref[...]
            o_ref[...] = r.astype(out_dtype)

        if nk == 1:
            finish(part)
            return

        @pl.when(kk == 0)
        def _():
            acc[...] = part

        @pl.when((kk > 0) & (kk < nk - 1))
        def _():
            acc[...] += part

        @pl.when(kk == nk - 1)
        def _():
            finish(acc[...] + part)

    if mode == "nn":
        a_spec = pl.BlockSpec((tm, tk), lambda i, j, kk: (i, kk))
        b_spec = pl.BlockSpec((tk, tn), lambda i, j, kk: (kk, j))
    elif mode == "nt":
        a_spec = pl.BlockSpec((tm, tk), lambda i, j, kk: (i, kk))
        b_spec = pl.BlockSpec((tn, tk), lambda i, j, kk: (j, kk))
    else:
        a_spec = pl.BlockSpec((tk, tm), lambda i, j, kk: (kk, i))
        b_spec = pl.BlockSpec((tk, tn), lambda i, j, kk: (kk, j))
    o_spec = pl.BlockSpec((tm, tn), lambda i, j, kk: (i, j))
    in_specs = [a_spec, b_spec] + ([o_spec] if has_res else [])
    args = (a, b) + ((residual,) if has_res else ())
    return pl.pallas_call(
        body, name=name, grid=(m // tm, n // tn, nk),
        in_specs=in_specs, out_specs=o_spec,
        out_shape=jax.ShapeDtypeStruct((m, n), out_dtype),
        scratch_shapes=[pltpu.VMEM((tm, tn), F32)],
        compiler_params=_cparams(("parallel", "parallel", "arbitrary")),
    )(*args)


def _fold8(v):
    r, w = v.shape
    return v.reshape(r // 8, 8, w).sum(axis=0)


ROW_BLOCK_BYTES = 12 * 1024 * 1024


def rowwise(fn, rows, bcasts, outs, accs=(), name="rowwise", reverse=False):
    t = rows[0].shape[0]
    row_bytes = sum(x.shape[1] * x.dtype.itemsize for x in rows) + sum(w * jnp.dtype(dt).itemsize for w, dt in outs)
    tr = _pick(t, tuple(c for c in (512, 256, 128, 64, 32, 16, 8) if c * row_bytes <= ROW_BLOCK_BYTES or c == 8))
    nr, nb, no, na = len(rows), len(bcasts), len(outs), len(accs)
    steps = t // tr

    def body(*refs):
        ins = [r[...] for r in refs[:nr + nb]]
        out_refs = refs[nr + nb:nr + nb + no]
        acc_refs = refs[nr + nb + no:]
        o, a = fn(*ins)
        for r, v in zip(out_refs, o):
            r[...] = v.astype(r.dtype)
        if na:
            @pl.when(pl.program_id(0) == 0)
            def _():
                for r in acc_refs:
                    r[...] = jnp.zeros_like(r)

            for r, v in zip(acc_refs, a):
                r[...] += v

    if reverse:
        ridx = lambda i: (steps - 1 - i, 0)
    else:
        ridx = lambda i: (i, 0)
    in_specs = [pl.BlockSpec((tr, x.shape[1]), ridx) for x in rows]
    in_specs += [pl.BlockSpec(x.shape, lambda i: (0, 0)) for x in bcasts]
    out_specs = [pl.BlockSpec((tr, w), ridx) for w, _ in outs]
    out_specs += [pl.BlockSpec((8, w), lambda i: (0, 0)) for w in accs]
    out_shape = [jax.ShapeDtypeStruct((t, w), dt) for w, dt in outs]
    out_shape += [jax.ShapeDtypeStruct((8, w), F32) for w in accs]
    res = pl.pallas_call(
        body, name=name, grid=(steps,), in_specs=in_specs, out_specs=out_specs, out_shape=out_shape,
        compiler_params=_cparams(("arbitrary",)),
    )(*rows, *bcasts)
    return res


def _rstd(x):
    return lax.rsqrt(jnp.mean(x * x, axis=-1, keepdims=True) + EPS)


def rmsnorm_fwd(h, g, name):
    def fn(x, gg):
        return ((x * _rstd(x)) * gg,), ()

    return rowwise(fn, [h], [g.reshape(1, -1)], [(D_MODEL, BF16)], name=name)[0]


def _rms_bwd_math(x, gg, dy):
    r = _rstd(x)
    xh = x * r
    u = dy * gg
    dx = r * (u - xh * jnp.mean(u * xh, axis=-1, keepdims=True))
    return dx, dy * xh


def rmsnorm_bwd(h, g, dn, dres, name):
    def fn(x, dy, dr, gg):
        dx, dgp = _rms_bwd_math(x, gg, dy)
        return (dr + dx,), (_fold8(dgp),)

    return rowwise(fn, [h, dn, dres], [g.reshape(1, -1)], [(D_MODEL, F32)], [D_MODEL], name=name)


def loss_head(h, g, tgt, name):
    def fn(x, tg, gg):
        y = (x * _rstd(x)) * gg
        e = y - tg
        dy = e * (1.0 / D_MODEL)
        dx, dgp = _rms_bwd_math(x, gg, dy)
        return (dx,), (_fold8(dgp), _fold8(e * e))

    return rowwise(fn, [h, tgt], [g.reshape(1, -1)], [(D_MODEL, F32)], [D_MODEL, D_MODEL], name=name)


def _sigmoid(x):
    return 1.0 / (1.0 + jnp.exp(-x))


FFN_TILE = 256


def ffn_gate_up(n2, w_gate, w_up, name):
    t = n2.shape[0]
    tm = _pick(t, (1024, 512, 256, 128))

    def body(a_ref, wg_ref, wu_ref, g_ref, u_ref, act_ref):
        a = a_ref[...]
        g = lax.dot_general(a, wg_ref[...], _DIMS["nt"], preferred_element_type=F32)
        u = lax.dot_general(a, wu_ref[...], _DIMS["nt"], preferred_element_type=F32)
        g_ref[...] = g
        u_ref[...] = u
        act_ref[...] = ((g * _sigmoid(g)) * u).astype(BF16)

    w_spec = pl.BlockSpec((FFN_TILE, D_MODEL), lambda i, j: (j, 0))
    o_spec = pl.BlockSpec((tm, FFN_TILE), lambda i, j: (i, j))
    return pl.pallas_call(
        body, name=name, grid=(t // tm, D_FF // FFN_TILE),
        in_specs=[pl.BlockSpec((tm, D_MODEL), lambda i, j: (i, 0)), w_spec, w_spec],
        out_specs=[o_spec, o_spec, o_spec],
        out_shape=[jax.ShapeDtypeStruct((t, D_FF), F32), jax.ShapeDtypeStruct((t, D_FF), F32),
                   jax.ShapeDtypeStruct((t, D_FF), BF16)],
        compiler_params=_cparams(("parallel", "parallel")),
    )(n2, w_gate, w_up)


def ffn_gate_up_bwd(dh, w_down, gate, up, name):
    t = dh.shape[0]
    tm = _pick(t, (1024, 512, 256, 128))

    def body(dh_ref, wd_ref, g_ref, u_ref, dg_ref, du_ref, dh_bf):
        @pl.when(pl.program_id(1) == 0)
        def _():
            dh_bf[...] = dh_ref[...].astype(BF16)

        d = lax.dot_general(dh_bf[...], wd_ref[...], _DIMS["nt"], preferred_element_type=F32)
        g = g_ref[...]
        s = _sigmoid(g)
        silu = g * s
        dg_ref[...] = (d * u_ref[...] * (s + silu * (1.0 - s))).astype(BF16)
        du_ref[...] = (d * silu).astype(BF16)

    o_spec = pl.BlockSpec((tm, FFN_TILE), lambda i, j: (i, j))
    return pl.pallas_call(
        body, name=name, grid=(t // tm, D_FF // FFN_TILE),
        in_specs=[pl.BlockSpec((tm, D_MODEL), lambda i, j: (i, 0)),
                  pl.BlockSpec((FFN_TILE, D_MODEL), lambda i, j: (j, 0)), o_spec, o_spec],
        out_specs=[o_spec, o_spec],
        out_shape=[jax.ShapeDtypeStruct((t, D_FF), BF16)] * 2,
        scratch_shapes=[pltpu.VMEM((tm, D_MODEL), BF16)],
        compiler_params=_cparams(("parallel", "arbitrary")),
    )(dh, w_down, gate, up)


def ple_fwd(h, pre, pp, name):
    def fn(x, a, b):
        return (x + _sigmoid(a) * b,), ()

    return rowwise(fn, [h, pre, pp], [], [(D_MODEL, F32)], name=name)[0]


def ple_bwd(dh, pre, pp, name):
    def fn(d, a, b):
        s = _sigmoid(a)
        return (d * b * s * (1.0 - s), d * s), ()

    return rowwise(fn, [dh, pre, pp], [], [(D_MODEL, BF16), (D_MODEL, BF16)], name=name)


def _rot_half_partner(x, first_half):
    w = x.shape[1]
    return jnp.where(first_half, pltpu.roll(x, w - HEAD_DIM // 2, 1), pltpu.roll(x, HEAD_DIM // 2, 1))


def rope_apply(xx, cosw, sinw, backward, name):
    width = xx.shape[1]
    reps = width // 128

    def fn(x, c, s):
        cw = jnp.tile(c, (1, reps))
        sw = jnp.tile(s, (1, reps))
        lane = lax.broadcasted_iota(jnp.int32, x.shape, 1)
        first = (lane % HEAD_DIM) < (HEAD_DIM // 2)
        if backward:
            return (x * cw + _rot_half_partner(x * sw, first),), ()
        return (x * cw + _rot_half_partner(x, first) * sw,), ()

    return rowwise(fn, [xx, cosw, sinw], [], [(width, F32)], name=name)[0]


def _log_sigmoid(x):
    return jnp.minimum(x, 0.0) - jnp.log(1.0 + jnp.exp(-jnp.abs(x)))


CUM_BLOCK = 256


def forget_cumsum(flog, bias, name):
    t = flog.shape[0]
    tb = _pick(t, (CUM_BLOCK,))

    def body(x_ref, b_ref, o_ref, carry):
        @pl.when(pl.program_id(0) == 0)
        def _():
            carry[...] = jnp.zeros_like(carry)

        lf = _log_sigmoid(x_ref[...] + b_ref[...])
        r = lax.broadcasted_iota(jnp.int32, (tb, tb), 0)
        c = lax.broadcasted_iota(jnp.int32, (tb, tb), 1)
        tri = (c <= r).astype(F32)
        cum = jnp.dot(tri, lf, preferred_element_type=F32, precision=lax.Precision.HIGHEST) + carry[...]
        o_ref[...] = cum
        carry[...] = cum[tb - 1:tb, :]

    return pl.pallas_call(
        body, name=name, grid=(t // tb,),
        in_specs=[pl.BlockSpec((tb, GATE_PAD), lambda i: (i, 0)), pl.BlockSpec((1, GATE_PAD), lambda i: (0, 0))],
        out_specs=pl.BlockSpec((tb, GATE_PAD), lambda i: (i, 0)),
        out_shape=jax.ShapeDtypeStruct((t, GATE_PAD), F32),
        scratch_shapes=[pltpu.VMEM((1, GATE_PAD), F32)],
        compiler_params=_cparams(("arbitrary",)),
    )(flog, bias)


def forget_cumsum_bwd(d_cum, flog, bias, name):
    t = flog.shape[0]
    tb = _pick(t, (CUM_BLOCK,))
    nb = t // tb

    def body(d_ref, x_ref, b_ref, o_ref, db_ref, carry):
        @pl.when(pl.program_id(0) == 0)
        def _():
            carry[...] = jnp.zeros_like(carry)
            db_ref[...] = jnp.zeros_like(db_ref)

        r = lax.broadcasted_iota(jnp.int32, (tb, tb), 0)
        c = lax.broadcasted_iota(jnp.int32, (tb, tb), 1)
        tri = (c >= r).astype(F32)
        dlf = jnp.dot(tri, d_ref[...], preferred_element_type=F32, precision=lax.Precision.HIGHEST) + carry[...]
        carry[...] = dlf[0:1, :]
        dx = dlf * (1.0 - _sigmoid(x_ref[...] + b_ref[...]))
        o_ref[...] = dx
        db_ref[...] += _fold8(dx)

    rev = lambda i: (nb - 1 - i, 0)
    return pl.pallas_call(
        body, name=name, grid=(nb,),
        in_specs=[pl.BlockSpec((tb, GATE_PAD), rev), pl.BlockSpec((tb, GATE_PAD), rev),
                  pl.BlockSpec((1, GATE_PAD), lambda i: (0, 0))],
        out_specs=[pl.BlockSpec((tb, GATE_PAD), rev), pl.BlockSpec((8, GATE_PAD), lambda i: (0, 0))],
        out_shape=[jax.ShapeDtypeStruct((t, GATE_PAD), F32), jax.ShapeDtypeStruct((8, GATE_PAD), F32)],
        scratch_shapes=[pltpu.VMEM((1, GATE_PAD), F32)],
        compiler_params=_cparams(("arbitrary",)),
    )(d_cum, flog, bias)


TQ = 512
TK = 128
AUG = 128
N_BIAS = 3
SB_CUTOFF = 110.0
FOX_CUTOFF = 112.0


def _dot(a, b):
    return jnp.dot(a, b, preferred_element_type=F32)


def _rel(shape, d):
    return lax.broadcasted_iota(jnp.int32, shape, 0) - lax.broadcasted_iota(jnp.int32, shape, 1) + d


def _put(x, lo, part, hi=None):
    hi = x.shape[1] if hi is None else hi
    pieces = ([x[:, :lo]] if lo else []) + [part] + ([x[:, hi:]] if hi < x.shape[1] else [])
    return part if len(pieces) == 1 else jnp.concatenate(pieces, axis=1)


def _stop_code(first, narrow):
    return first.astype(F32) + jnp.where(narrow, 0.5, 0.0)


def _read_stop_code(code_ref):
    code = jnp.max(code_ref[...])
    first = code.astype(jnp.int32)
    return first, (code - first.astype(F32)) > 0.25


def _split_bf16(x):
    hi = x.astype(BF16)
    return hi, (x - hi.astype(F32)).astype(BF16)


def _tri_dot(tri2, x):
    hi, lo = _split_bf16(x)
    return _dot(tri2, jnp.concatenate([hi, lo], axis=0))


def _q_cols(width, tq):
    return pl.BlockSpec((None, width, tq), lambda h, i: (h, 0, i))


def _q_rows(width, tq):
    return pl.BlockSpec((None, tq, width), lambda h, i: (h, i, 0))


def _kv_rows(t, width):
    return pl.BlockSpec((None, t, width), lambda h, i: (h, 0, 0))


def _kv_tiles(nk):
    return pl.BlockSpec((None, nk, HEAD_DIM, TK), lambda h, i: (h, 0, 0, 0))


def _blocks(t):
    tq = TQ if t % TQ == 0 else TK
    return tq, tq // TK, t // TK


def fox_fwd(qt_aug, k_aug, vt, f_end, k_norm):
    nh, _, t = qt_aug.shape
    tq, ratio, nk = _blocks(t)
    lanes = tq // 128

    def body(q_ref, k_ref, v_ref, fe_ref, kn_ref, o_ref, lse_ref, first_ref):
        i = pl.program_id(1)
        qv = q_ref[...]
        qf = qv[:HEAD_DIM].astype(F32)
        reach = jnp.sqrt(jnp.sum(qf * qf, axis=0, keepdims=True)) * jnp.tile(kn_ref[...], (1, lanes))

        def step(first, carry, masked, width=tq):
            m, l, acc = carry
            scores = []
            for u in range(ratio):
                lo, hi = (u * TK, tq) if masked else (0, width)
                off = pl.multiple_of((first + u) * TK, TK)
                s = _dot(k_ref[pl.ds(off, TK), :], qv[:, lo:hi])
                if masked:
                    s = jnp.where(_rel(s.shape, 0) <= 0, s, NEG_INF)
                scores.append((lo, hi, s))
            m_new = m
            for lo, hi, s in scores:
                m_new = _put(m_new, lo, jnp.maximum(m_new[:, lo:hi], jnp.max(s, axis=0, keepdims=True)), hi)
            alpha = jnp.exp(m - m_new)
            l = alpha * l
            acc = alpha * acc
            for u, (lo, hi, s) in enumerate(scores):
                p = jnp.exp(s - m_new[:, lo:hi])
                l = _put(l, lo, l[:, lo:hi] + jnp.sum(p, axis=0, keepdims=True), hi)
                acc = _put(acc, lo, acc[:, lo:hi] + _dot(v_ref[first + u], p.astype(BF16)), hi)
            return m_new, l, acc

        def slack(done, m):
            return reach + jnp.tile(fe_ref[jnp.maximum(i - 1 - done, 0)], (1, lanes)) - m

        def sweep_down(width, go, state):
            def more(c):
                return go & (c[0] < i) & (jnp.max(slack(c[0], c[1])[:, :width]) > -FOX_CUTOFF)

            def sweep(c):
                return (c[0] + 1,) + step(ratio * (i - 1 - c[0]), c[1:], False, width)

            return lax.while_loop(more, sweep, state)

        init = (jnp.full((1, tq), NEG_INF, F32), jnp.zeros((1, tq), F32), jnp.zeros((HEAD_DIM, tq), F32))
        state = (jnp.int32(0),) + step(ratio * i, init, True)
        if ratio > 1:
            narrow = jnp.max(slack(0, state[1])[:, tq // 2:]) <= -FOX_CUTOFF
            state = sweep_down(tq // 2, narrow, state)
            state = sweep_down(tq, jnp.logical_not(narrow), state)
        else:
            narrow = False
            state = sweep_down(tq, True, state)
        done, m, l, acc = state
        o_ref[...] = acc / l
        lse_ref[...] = m + jnp.log(l)
        first_ref[...] = jnp.full((1, 128), _stop_code(i - done, narrow), F32)

    return pl.pallas_call(
        body, name="fox_fwd", grid=(nh, t // tq),
        in_specs=[_q_cols(AUG, tq), _kv_rows(t, AUG), _kv_tiles(nk),
                  pl.BlockSpec((None, t // tq, 1, 128), lambda h, i: (h, 0, 0, 0)),
                  pl.BlockSpec((None, 1, 128), lambda h, i: (h, 0, 0))],
        out_specs=[_q_cols(HEAD_DIM, tq), _q_cols(1, tq), _q_cols(1, 128)],
        out_shape=[jax.ShapeDtypeStruct((nh, HEAD_DIM, t), F32), jax.ShapeDtypeStruct((nh, 1, t), F32),
                   jax.ShapeDtypeStruct((nh, 1, 128 * (t // tq)), F32)],
        compiler_params=_cparams(("parallel", "arbitrary")),
    )(qt_aug, k_aug, vt, f_end, k_norm)


def fox_bwd(qt_aug, q_aug, k_aug, kt, v, ot, do, dot_, lse, first):
    nh, _, t = qt_aug.shape
    tq, ratio, nk = _blocks(t)

    def body(qt_ref, q_ref, k_ref, kt_ref, v_ref, ot_ref, do_ref, dot_ref, lse_ref, first_ref,
             dqt_ref, dk_ref, dv_ref, rs_ref):
        i = pl.program_id(1)
        start, narrow = _read_stop_code(first_ref)

        @pl.when(i == 0)
        def _():
            dk_ref[...] = jnp.zeros_like(dk_ref)
            dv_ref[...] = jnp.zeros_like(dv_ref)

        qtv = qt_ref[...]
        qv = q_ref[...]
        dob = do_ref[...]
        dotb = dot_ref[...]
        delta = jnp.sum(ot_ref[...] * dotb.astype(F32), axis=0, keepdims=True)
        lse = lse_ref[...]

        def tile(j, carry, lo, hi, masked):
            dqt, rs = carry
            off = pl.multiple_of(j * TK, TK)
            s = _dot(k_ref[pl.ds(off, TK), :], qtv[:, lo:hi])
            p = jnp.exp(s - lse[:, lo:hi])
            if masked:
                p = jnp.where(_rel(s.shape, 0) <= 0, p, 0.0)
            dp = _dot(v_ref[pl.ds(off, TK), :], dotb[:, lo:hi])
            dsb = (p * (dp - delta[:, lo:hi])).astype(BF16)
            dk_ref[pl.ds(off, TK), :] += _dot(dsb, qv[lo:hi, :])
            dv_ref[pl.ds(off, TK), :] += _dot(p.astype(BF16), dob[lo:hi, :])
            return (_put(dqt, lo, dqt[:, lo:hi] + _dot(kt_ref[j], dsb), hi),
                    _put(rs, lo, rs[:, lo:hi] + jnp.sum(dsb.astype(F32), axis=0, keepdims=True), hi))

        def step(first, carry, masked, width=tq):
            for u in range(ratio):
                lo, hi = (u * TK, tq) if masked else (0, width)
                carry = tile(first + u, carry, lo, hi, masked)
            return carry

        carry = (jnp.zeros((HEAD_DIM, tq), F32), jnp.zeros((1, tq), F32))
        if ratio > 1:
            carry = lax.fori_loop(start, jnp.where(narrow, i, start),
                                  lambda jj, c: step(ratio * jj, c, False, tq // 2), carry)
            carry = lax.fori_loop(start, jnp.where(narrow, start, i), lambda jj, c: step(ratio * jj, c, False), carry)
        else:
            carry = lax.fori_loop(start, i, lambda jj, c: step(ratio * jj, c, False), carry)
        dqt, rs = step(ratio * i, carry, True)
        dqt_ref[...] = dqt * SCALE
        rs_ref[...] = rs

    return pl.pallas_call(
        body, name="fox_bwd", grid=(nh, t // tq),
        in_specs=[_q_cols(AUG, tq), _q_rows(AUG, tq), _kv_rows(t, AUG), _kv_tiles(nk), _kv_rows(t, HEAD_DIM),
                  _q_cols(HEAD_DIM, tq), _q_rows(HEAD_DIM, tq), _q_cols(HEAD_DIM, tq), _q_cols(1, tq),
                  _q_cols(1, 128)],
        out_specs=[_q_cols(HEAD_DIM, tq), _kv_rows(t, AUG), _kv_rows(t, HEAD_DIM), _q_cols(1, tq)],
        out_shape=[jax.ShapeDtypeStruct((nh, HEAD_DIM, t), F32), jax.ShapeDtypeStruct((nh, t, AUG), F32),
                   jax.ShapeDtypeStruct((nh, t, HEAD_DIM), F32), jax.ShapeDtypeStruct((nh, 1, t), F32)],
        compiler_params=_cparams(("arbitrary", "arbitrary")),
    )(qt_aug, q_aug, k_aug, kt, v, ot, do, dot_, lse, first)


def _sb_logits(kb, qv, ok):
    z = _dot(kb, qv)
    e = jnp.exp(-jnp.abs(z))
    ll = -(jnp.maximum(z, 0.0) + jnp.log(1.0 + e))
    if ok is not None:
        ll = jnp.where(ok, ll, 0.0)
    return z, e, ll


def _tri(cmp):
    r = lax.broadcasted_iota(jnp.int32, (TK, 2 * TK), 0)
    c = lax.broadcasted_iota(jnp.int32, (TK, 2 * TK), 1) % TK
    return cmp(r, c).astype(BF16)


def sb_fwd(qt, k, vt):
    nh, _, t = qt.shape
    tq, ratio, nk = _blocks(t)

    def body(q_ref, k_ref, v_ref, o_ref, tot_ref, first_ref):
        i = pl.program_id(1)
        qv = q_ref[...]
        tri_after = _tri(lambda r, c: c > r)

        def tile(j, carry, lo, hi, masked):
            c_l, acc = carry
            off = pl.multiple_of(j * TK, TK)
            ok = _rel((TK, hi - lo), 0) < 0 if masked else None
            z, _, ll = _sb_logits(k_ref[pl.ds(off, TK), :], qv[:, lo:hi], ok)
            a = jnp.exp(z + ll + _tri_dot(tri_after, ll) + c_l[:, lo:hi])
            if masked:
                a = jnp.where(ok, a, 0.0)
            return (_put(c_l, lo, c_l[:, lo:hi] + jnp.sum(ll, axis=0, keepdims=True), hi),
                    _put(acc, lo, acc[:, lo:hi] + _dot(v_ref[j], a.astype(BF16)), hi))

        def step(first, carry, masked, width=tq):
            for u in reversed(range(ratio)):
                lo, hi = (u * TK, tq) if masked else (0, width)
                carry = tile(first + u, carry, lo, hi, masked)
            return carry

        def sweep_down(width, go, state):
            def more(c):
                return go & (c[0] < i) & (jnp.max(c[1][:, :width]) > -SB_CUTOFF)

            def sweep(c):
                return (c[0] + 1,) + step(ratio * (i - 1 - c[0]), c[1:], False, width)

            return lax.while_loop(more, sweep, state)

        state = (jnp.int32(0),) + step(ratio * i, (jnp.zeros((1, tq), F32), jnp.zeros((HEAD_DIM, tq), F32)), True)
        if ratio > 1:
            narrow = jnp.max(state[1][:, tq // 2:]) <= -SB_CUTOFF
            state = sweep_down(tq // 2, narrow, state)
            state = sweep_down(tq, jnp.logical_not(narrow), state)
        else:
            narrow = False
            state = sweep_down(tq, True, state)
        done, c_l, acc = state
        o_ref[...] = acc
        tot_ref[...] = c_l
        first_ref[...] = jnp.full((1, 128), _stop_code(i - done, narrow), F32)

    return pl.pallas_call(
        body, name="sb_fwd", grid=(nh, t // tq),
        in_specs=[_q_cols(HEAD_DIM, tq), _kv_rows(t, HEAD_DIM), _kv_tiles(nk)],
        out_specs=[_q_cols(HEAD_DIM, tq), _q_cols(1, tq), _q_cols(1, 128)],
        out_shape=[jax.ShapeDtypeStruct((nh, HEAD_DIM, t), F32), jax.ShapeDtypeStruct((nh, 1, t), F32),
                   jax.ShapeDtypeStruct((nh, 1, 128 * (t // tq)), F32)],
        compiler_params=_cparams(("parallel", "arbitrary")),
    )(qt, k, vt)


def sb_bwd(qt, q, k, kt, v, ltot, first, do, dot_):
    nh, _, t = qt.shape
    tq, ratio, nk = _blocks(t)

    def body(qt_ref, q_ref, k_ref, kt_ref, v_ref, tot_ref, first_ref, do_ref, dot_ref, dqt_ref, dk_ref, dv_ref):
        i = pl.program_id(1)
        start, narrow = _read_stop_code(first_ref)

        @pl.when(i == 0)
        def _():
            dk_ref[...] = jnp.zeros_like(dk_ref)
            dv_ref[...] = jnp.zeros_like(dv_ref)

        qtv = qt_ref[...]
        qv = q_ref[...]
        dob = do_ref[...]
        dotb = dot_ref[...]
        tri_upto = _tri(lambda r, c: c <= r)
        tri_before = _tri(lambda r, c: c < r)

        def tile(j, carry, lo, hi, masked):
            rest, c_w, dqt = carry
            off = pl.multiple_of(j * TK, TK)
            ok = _rel((TK, hi - lo), 0) < 0 if masked else None
            z, e, ll = _sb_logits(k_ref[pl.ds(off, TK), :], qtv[:, lo:hi], ok)
            a = jnp.exp(z + ll + (rest[:, lo:hi] - _tri_dot(tri_upto, ll)))
            if masked:
                a = jnp.where(ok, a, 0.0)
            w = a * _dot(v_ref[pl.ds(off, TK), :], dotb[:, lo:hi])
            before = _tri_dot(tri_before, w) + c_w[:, lo:hi]
            r = 1.0 / (1.0 + e)
            sig = jnp.where(z >= 0.0, r, e * r)
            dz = w - (w + before) * sig
            if masked:
                dz = jnp.where(ok, dz, 0.0)
            dzb = dz.astype(BF16)
            dk_ref[pl.ds(off, TK), :] += _dot(dzb, qv[lo:hi, :])
            dv_ref[pl.ds(off, TK), :] += _dot(a.astype(BF16), dob[lo:hi, :])
            return (_put(rest, lo, rest[:, lo:hi] - jnp.sum(ll, axis=0, keepdims=True), hi),
                    _put(c_w, lo, c_w[:, lo:hi] + jnp.sum(w, axis=0, keepdims=True), hi),
                    _put(dqt, lo, dqt[:, lo:hi] + _dot(kt_ref[j], dzb), hi))

        def step(first, carry, masked, width=tq):
            for u in range(ratio):
                lo, hi = (u * TK, tq) if masked else (0, width)
                carry = tile(first + u, carry, lo, hi, masked)
            return carry

        carry = (tot_ref[...], jnp.zeros((1, tq), F32), jnp.zeros((HEAD_DIM, tq), F32))
        if ratio > 1:
            carry = lax.fori_loop(start, jnp.where(narrow, i, start),
                                  lambda jj, c: step(ratio * jj, c, False, tq // 2), carry)
            carry = lax.fori_loop(start, jnp.where(narrow, start, i), lambda jj, c: step(ratio * jj, c, False), carry)
        else:
            carry = lax.fori_loop(start, i, lambda jj, c: step(ratio * jj, c, False), carry)
        dqt_ref[...] = step(ratio * i, carry, True)[2] * SCALE

    hd = HEAD_DIM
    return pl.pallas_call(
        body, name="sb_bwd", grid=(nh, t // tq),
        in_specs=[_q_cols(hd, tq), _q_rows(hd, tq), _kv_rows(t, hd), _kv_tiles(nk), _kv_rows(t, hd),
                  _q_cols(1, tq), _q_cols(1, 128), _q_rows(hd, tq), _q_cols(hd, tq)],
        out_specs=[_q_cols(hd, tq), _kv_rows(t, hd), _kv_rows(t, hd)],
        out_shape=[jax.ShapeDtypeStruct((nh, hd, t), F32), jax.ShapeDtypeStruct((nh, t, hd), F32),
                   jax.ShapeDtypeStruct((nh, t, hd), F32)],
        compiler_params=_cparams(("arbitrary", "arbitrary")),
    )(qt, q, k, kt, v, ltot, first, do, dot_)


def _swa_q_cols(width):
    return pl.BlockSpec((GROUP, width, WINDOW), lambda g, i: (g, 0, i))


def _swa_q_rows():
    return pl.BlockSpec((GROUP, WINDOW, HEAD_DIM), lambda g, i: (g, i, 0))


def _swa_kv_rows(t):
    return pl.BlockSpec((None, t, HEAD_DIM), lambda g, i: (g, 0, 0))


def _swa_kv_tiles(nk):
    return pl.BlockSpec((None, nk, HEAD_DIM, WINDOW), lambda g, i: (g, 0, 0, 0))


def _swa_sink_spec():
    return pl.BlockSpec((GROUP, 1, 128), lambda g, i: (g, 0, 0))


def _lane_cat(parts):
    return jnp.concatenate(parts, axis=1)


def _swa_window(i):
    jb = jnp.maximum(i - 1, 0)
    start = pl.multiple_of(jb * WINDOW, WINDOW)
    shape = (2 * WINDOW, GROUP * WINDOW)
    query = lax.broadcasted_iota(jnp.int32, shape, 1) % WINDOW
    rel = lax.broadcasted_iota(jnp.int32, shape, 0) - query + (start - i * WINDOW)
    return jb, start, (rel <= 0) & (rel > -WINDOW)


def swa_fwd(qt, k, vt, sinks):
    nh, _, t = qt.shape
    nk = t // WINDOW

    def body(q_ref, k_ref, v_ref, s_ref, o_ref, lse_ref):
        i = pl.program_id(1)
        jb, start, valid = _swa_window(i)
        qv = _lane_cat([q_ref[g] for g in range(GROUP)])
        sink = _lane_cat([s_ref[g] for g in range(GROUP)])
        s = jnp.where(valid, _dot(k_ref[pl.ds(start, 2 * WINDOW), :], qv), NEG_INF)
        m = jnp.maximum(jnp.max(s, axis=0, keepdims=True), sink)
        p = jnp.where(valid, jnp.exp(s - m), 0.0)
        l = jnp.sum(p, axis=0, keepdims=True) + jnp.exp(sink - m)
        pb = p.astype(BF16)
        o = (_dot(v_ref[jb], pb[:WINDOW]) + _dot(v_ref[jb + 1], pb[WINDOW:])) / l
        lse = m + jnp.log(l)
        for g in range(GROUP):
            o_ref[g] = o[:, g * WINDOW:(g + 1) * WINDOW]
            lse_ref[g] = lse[:, g * WINDOW:(g + 1) * WINDOW]

    return pl.pallas_call(
        body, name="swa_fwd", grid=(N_KV, nk),
        in_specs=[_swa_q_cols(HEAD_DIM), _swa_kv_rows(t), _swa_kv_tiles(nk), _swa_sink_spec()],
        out_specs=[_swa_q_cols(HEAD_DIM), _swa_q_cols(1)],
        out_shape=[jax.ShapeDtypeStruct((nh, HEAD_DIM, t), F32), jax.ShapeDtypeStruct((nh, 1, t), F32)],
        compiler_params=_cparams(("parallel", "arbitrary")),
    )(qt, k, vt, sinks)


def swa_bwd(qt, q, k, kt, v, sinks, ot, do, dot_, lse):
    nh, _, t = qt.shape
    nk = t // WINDOW

    def body(qt_ref, q_ref, k_ref, kt_ref, v_ref, s_ref, ot_ref, do_ref, dot_ref, lse_ref,
             dqt_ref, dk_ref, dv_ref, dsink_ref):
        i = pl.program_id(1)

        @pl.when(i == 0)
        def _():
            dk_ref[...] = jnp.zeros_like(dk_ref)
            dv_ref[...] = jnp.zeros_like(dv_ref)
            dsink_ref[...] = jnp.zeros_like(dsink_ref)

        jb, start, valid = _swa_window(i)
        heads = range(GROUP)
        qtv = _lane_cat([qt_ref[g] for g in heads])
        dotb = _lane_cat([dot_ref[g] for g in heads])
        lse = _lane_cat([lse_ref[g] for g in heads])
        sink = _lane_cat([s_ref[g] for g in heads])
        otv = _lane_cat([ot_ref[g] for g in heads])
        q_rows = jnp.concatenate([q_ref[g] for g in heads], axis=0)
        do_rows = jnp.concatenate([do_ref[g] for g in heads], axis=0)
        delta = jnp.sum(otv * dotb.astype(F32), axis=0, keepdims=True)
        p = jnp.where(valid, jnp.exp(_dot(k_ref[pl.ds(start, 2 * WINDOW), :], qtv) - lse), 0.0)
        dsb = (p * (_dot(v_ref[pl.ds(start, 2 * WINDOW), :], dotb) - delta)).astype(BF16)
        dqt = (_dot(kt_ref[jb], dsb[:WINDOW]) + _dot(kt_ref[jb + 1], dsb[WINDOW:])) * SCALE
        dsink = -jnp.exp(sink - lse) * delta
        for g in heads:
            dqt_ref[g] = dqt[:, g * WINDOW:(g + 1) * WINDOW]
            dsink_ref[g] += dsink[:, g * WINDOW:(g + 1) * WINDOW]
        dk_ref[pl.ds(start, 2 * WINDOW), :] += _dot(dsb, q_rows)
        dv_ref[pl.ds(start, 2 * WINDOW), :] += _dot(p.astype(BF16), do_rows)

    hd = HEAD_DIM
    return pl.pallas_call(
        body, name="swa_bwd", grid=(N_KV, nk),
        in_specs=[_swa_q_cols(hd), _swa_q_rows(), _swa_kv_rows(t), _swa_kv_tiles(nk), _swa_kv_rows(t),
                  _swa_sink_spec(), _swa_q_cols(hd), _swa_q_rows(), _swa_q_cols(hd), _swa_q_cols(1)],
        out_specs=[_swa_q_cols(hd), _swa_kv_rows(t), _swa_kv_rows(t), _swa_sink_spec()],
        out_shape=[jax.ShapeDtypeStruct((nh, hd, t), F32), jax.ShapeDtypeStruct((N_KV, t, hd), F32),
                   jax.ShapeDtypeStruct((N_KV, t, hd), F32), jax.ShapeDtypeStruct((nh, 1, 128), F32)],
        compiler_params=_cparams(("arbitrary", "arbitrary")),
    )(qt, q, k, kt, v, sinks, ot, do, dot_, lse)


def head_rows(a, nh):
    t = a.shape[0]
    return a.reshape(t, nh, HEAD_DIM).transpose(1, 0, 2)


def head_cols(a, nh):
    t = a.shape[0]
    return a.reshape(t, nh, HEAD_DIM).transpose(1, 2, 0)


def head_tiles(a, nh):
    t = a.shape[0]
    return a.reshape(t // TK, TK, nh, HEAD_DIM).transpose(2, 0, 3, 1)


def rows_to_flat(a):
    nh, t, _ = a.shape
    return a.transpose(1, 0, 2).reshape(t, nh * HEAD_DIM)


def cols_to_flat(a):
    nh, _, t = a.shape
    return a.transpose(2, 0, 1).reshape(t, nh * HEAD_DIM)


def fox_keys(proj, cum, name):
    t = proj.shape[0]
    tr = _pick(t, (256, 128))
    width = (1 + N_BIAS) * 128

    def body(k_ref, c_ref, o_ref):
        kb = k_ref[...].astype(BF16)
        terms, rest = [], -c_ref[...]
        for _ in range(N_BIAS):
            term = rest.astype(BF16)
            terms.append(term)
            rest = rest - term.astype(F32)
        row = lax.broadcasted_iota(jnp.int32, (width, AUG), 0)
        col = lax.broadcasted_iota(jnp.int32, (width, AUG), 1)
        for h in range(N_FOX):
            pair = kb[:, 128 * (h // 2):128 * (h // 2 + 1)]
            place = (row < 128) & (col < HEAD_DIM) & (row - HEAD_DIM * (h % 2) == col)
            for b in range(N_BIAS):
                place = place | ((row == 128 * (1 + b) + h) & (col == HEAD_DIM + b))
            src = jnp.concatenate([pair] + terms, axis=1)
            o_ref[h] = _dot(src, place.astype(BF16)).astype(BF16)

    return pl.pallas_call(
        body, name=name, grid=(t // tr,),
        in_specs=[pl.BlockSpec((tr, FOX_W), lambda i: (i, 1)), pl.BlockSpec((tr, GATE_PAD), lambda i: (i, 0))],
        out_specs=pl.BlockSpec((N_FOX, tr, AUG), lambda i: (0, i, 0)),
        out_shape=jax.ShapeDtypeStruct((N_FOX, t, AUG), BF16),
        compiler_params=_cparams(("parallel",)),
    )(proj, cum)


def fox_operands(qf, kf, cum_heads):
    t = qf.shape[0]
    nh = cum_heads.shape[0]
    tq = _blocks(t)[0]
    qs = (qf * SCALE).astype(BF16)
    ones_t = jnp.ones((nh, N_BIAS, t), BF16)
    qt_aug = jnp.concatenate([head_cols(qs, nh), ones_t, jnp.zeros((nh, AUG - HEAD_DIM - N_BIAS, t), BF16)], axis=1)
    own_lane = jnp.broadcast_to(jnp.eye(nh, dtype=BF16)[:, None, :], (nh, t, nh))
    q_aug = jnp.concatenate([head_rows(qs, nh), own_lane, jnp.zeros((nh, t, AUG - HEAD_DIM - nh), BF16)], axis=2)
    f_end = jnp.broadcast_to((-cum_heads)[:, tq - 1::tq, None, None], (nh, t // tq, 1, 128))
    k_sq = jnp.sum(jnp.square(kf).reshape(t, nh, HEAD_DIM), axis=2)
    k_norm = jnp.broadcast_to(1.01 * jnp.sqrt(jnp.max(k_sq, axis=0))[:, None, None], (nh, 1, 128))
    return qt_aug, q_aug, f_end, k_norm


def adamw(w, g, m, v, name):
    cols = w.shape[1]

    def fn(ww, gg, mm, vv):
        mn = ADAM_B1 * mm + (1.0 - ADAM_B1) * gg
        vn = ADAM_B2 * vv + (1.0 - ADAM_B2) * (gg * gg)
        m_hat = mn / (1.0 - ADAM_B1 ** ADAM_STEP)
        v_hat = vn / (1.0 - ADAM_B2 ** ADAM_STEP)
        delta = -ADAM_LR * (m_hat / (jnp.sqrt(v_hat) + ADAM_EPS) + ADAM_WD * ww)
        return (delta, mn, vn), ()

    return rowwise(fn, [w, g, m, v], [], [(cols, F32)] * 3, name=name)


ANY = pl.BlockSpec(memory_space=pl.ANY)


def _my_place():
    return lax.axis_index("x"), lax.axis_index("y"), lax.axis_index("c")


def _flip(coord, bit):
    return 1 - coord if bit else coord


def allgather_chips(w):
    r, c = w.shape
    rh = r // 2

    def body(w_ref, out_ref, send_sems, recv_sems, local_sem):
        x, y, cc = _my_place()
        me = 2 * x + y
        sibling = (x, y, 1 - cc)
        chips = [(_flip(x, kbits >> 1), _flip(y, kbits & 1)) for kbits in (1, 2, 3)]

        def half(chip, hc):
            return out_ref.at[chip, pl.ds(pl.multiple_of(hc * rh, 16), rh)]

        def copy(k, src, dst, to):
            return pltpu.make_async_remote_copy(src_ref=src, dst_ref=dst, send_sem=send_sems.at[k],
                                                recv_sem=recv_sems.at[k], device_id=to, device_id_type=MESH)

        local = pltpu.make_async_copy(w_ref, out_ref.at[me], local_sem)
        local.start()
        my_half = w_ref.at[pl.ds(pl.multiple_of(cc * rh, 16), rh)]
        first = [copy(j, my_half, half(me, cc), (px, py, cc)) for j, (px, py) in enumerate(chips)]
        for cp in first:
            cp.start()
        passed = []
        for j, (px, py) in enumerate(chips):
            landed = half(2 * px + py, cc)
            copy(j, my_half, landed, (px, py, cc)).wait_recv()
            fwd = copy(3 + j, landed, landed, sibling)
            fwd.start()
            passed.append(fwd)
        for j, (px, py) in enumerate(chips):
            theirs = half(2 * px + py, 1 - cc)
            copy(3 + j, theirs, theirs, sibling).wait_recv()
        for cp in first + passed:
            cp.wait_send()
        local.wait()

    return pl.pallas_call(
        body, name="allgather_chips", in_specs=[ANY], out_specs=ANY,
        out_shape=jax.ShapeDtypeStruct((N_CHIPS, r, c), w.dtype),
        scratch_shapes=[pltpu.SemaphoreType.DMA((6,)), pltpu.SemaphoreType.DMA((6,)), pltpu.SemaphoreType.DMA],
    )(w)


def pair_send_other_half(g):
    n, _, rh, c = g.shape

    def body(g_ref, out_ref, send_sem, recv_sem):
        x, y, cc = _my_place()
        cp = pltpu.make_async_remote_copy(
            src_ref=g_ref.at[:, 1 - cc], dst_ref=out_ref, send_sem=send_sem, recv_sem=recv_sem,
            device_id=(x, y, 1 - cc), device_id_type=MESH)
        cp.start()
        cp.wait()

    return pl.pallas_call(
        body, name="pair_send_other_half", in_specs=[ANY], out_specs=ANY,
        out_shape=jax.ShapeDtypeStruct((n, rh, c), g.dtype),
        scratch_shapes=[pltpu.SemaphoreType.DMA, pltpu.SemaphoreType.DMA],
    )(g)


def exchange_chips(s):
    n, rh, c = s.shape

    def body(s_ref, out_ref, send_sems, recv_sems, local_sem):
        x, y, cc = _my_place()
        me = 2 * x + y
        local = pltpu.make_async_copy(s_ref.at[me], out_ref.at[me], local_sem)
        local.start()
        copies = []
        for kbits in (1, 2, 3):
            px, py = _flip(x, kbits >> 1), _flip(y, kbits & 1)
            cp = pltpu.make_async_remote_copy(
                src_ref=s_ref.at[2 * px + py], dst_ref=out_ref.at[me], send_sem=send_sems.at[kbits - 1],
                recv_sem=recv_sems.at[kbits - 1], device_id=(px, py, cc), device_id_type=MESH)
            cp.start()
            copies.append(cp)
        for cp in copies:
            cp.wait()
        local.wait()

    return pl.pallas_call(
        body, name="exchange_chips", in_specs=[ANY], out_specs=ANY,
        out_shape=jax.ShapeDtypeStruct((n, rh, c), s.dtype),
        scratch_shapes=[pltpu.SemaphoreType.DMA((3,)), pltpu.SemaphoreType.DMA((3,)), pltpu.SemaphoreType.DMA],
    )(s)


def pair_swap(tt):
    def body(t_ref, out_ref, send_sem, recv_sem):
        x, y, cc = _my_place()
        cp = pltpu.make_async_remote_copy(
            src_ref=t_ref, dst_ref=out_ref, send_sem=send_sem, recv_sem=recv_sem,
            device_id=(x, y, 1 - cc), device_id_type=MESH)
        cp.start()
        cp.wait()

    return pl.pallas_call(
        body, name="pair_swap", in_specs=[ANY], out_specs=ANY,
        out_shape=jax.ShapeDtypeStruct(tt.shape, tt.dtype),
        scratch_shapes=[pltpu.SemaphoreType.DMA, pltpu.SemaphoreType.DMA],
    )(tt)


SMALL_ROWS = 16


def allreduce_small(v):
    r, c = v.shape
    vm = pl.BlockSpec(memory_space=pltpu.VMEM)

    def body(v_ref, out_ref, slots, send_sems, recv_sems):
        x, y, cc = _my_place()
        me = 4 * x + 2 * y + cc
        slots[me] = v_ref[...]
        copies = []
        for kbits in range(1, 8):
            peer = (_flip(x, kbits >> 2), _flip(y, (kbits >> 1) & 1), _flip(cc, kbits & 1))
            cp = pltpu.make_async_remote_copy(
                src_ref=v_ref, dst_ref=slots.at[me], send_sem=send_sems.at[kbits - 1],
                recv_sem=recv_sems.at[kbits - 1], device_id=peer, device_id_type=MESH)
            cp.start()
            copies.append(cp)
        for cp in copies:
            cp.wait()
        total = slots[0]
        for dev in range(1, 8):
            total = total + slots[dev]
        out_ref[...] = total

    return pl.pallas_call(
        body, name="allreduce_small", in_specs=[vm], out_specs=vm,
        out_shape=jax.ShapeDtypeStruct((r, c), F32),
        scratch_shapes=[pltpu.VMEM((8, r, c), F32), pltpu.SemaphoreType.DMA((7,)), pltpu.SemaphoreType.DMA((7,))],
    )(v)


def add_pair(mine, theirs, name):
    return rowwise(lambda a, b: ((a + b,), ()), [mine, theirs], [], [(mine.shape[1], BF16)], name=name)[0]


def sum_chips(r4, name):
    _, rh, c = r4.shape
    tr = _pick(rh, (256, 128, 64, 32, 16))

    def body(r_ref, o_ref):
        total = r_ref[0].astype(F32)
        for j in range(1, N_CHIPS):
            total = total + r_ref[j].astype(F32)
        o_ref[...] = total

    return pl.pallas_call(
        body, name=name, grid=(rh // tr,),
        in_specs=[pl.BlockSpec((N_CHIPS, tr, c), lambda i: (0, i, 0))],
        out_specs=pl.BlockSpec((tr, c), lambda i: (i, 0)),
        out_shape=jax.ShapeDtypeStruct((rh, c), F32),
        compiler_params=_cparams(("parallel",)),
    )(r4)


def _pad_rows(a, rows):
    return jnp.pad(a, ((0, rows - a.shape[0]), (0, 0))) if rows != a.shape[0] else a


def pack_shards(local):
    parts = []
    for name, layer, r, c, by_cols in _PACK:
        w = local[name][layer].astype(BF16)
        flat = w.T if _transposed(r, by_cols) else w.reshape(r * c // D_MODEL, D_MODEL)
        parts.append(_pad_rows(flat, _pack_rows(r, c)))
    used = sum(p.shape[0] for p in parts)
    parts.append(jnp.zeros((PACK_ROWS - used, D_MODEL), BF16))
    return jnp.concatenate(parts, axis=0)


def unpack_full(gathered):
    full, off = {}, 0
    for name, layer, r, c, by_cols in _PACK:
        n = r * c // D_MODEL
        if _transposed(r, by_cols):
            full[(name, layer)] = gathered[:, off:off + c, :].reshape(N_CHIPS * c, D_MODEL)
            off += _pack_rows(r, c)
            continue
        blk = gathered[:, off:off + n, :].reshape(N_CHIPS, r, c)
        if by_cols:
            full[(name, layer)] = blk.transpose(1, 0, 2).reshape(r, N_CHIPS * c)
        else:
            full[(name, layer)] = blk.reshape(N_CHIPS * r, c)
        off += _pack_rows(r, c)
    return full


def pack_grads(grads):
    parts = []
    for name, layer, r, c, by_cols in _PACK:
        g = grads[(name, layer)]
        if _transposed(r, by_cols):
            blk = g.reshape(N_CHIPS, c, r)
        elif by_cols:
            blk = g.reshape(r, N_CHIPS, c).transpose(1, 0, 2)
        else:
            blk = g.reshape(N_CHIPS, r, c)
        flat = blk.reshape(N_CHIPS, r * c // D_MODEL, D_MODEL)
        rows = _pack_rows(r, c)
        if rows != flat.shape[1]:
            flat = jnp.pad(flat, ((0, 0), (0, rows - flat.shape[1]), (0, 0)))
        parts.append(flat)
    used = sum(p.shape[1] for p in parts)
    parts.append(jnp.zeros((N_CHIPS, PACK_ROWS - used, D_MODEL), F32))
    return jnp.concatenate(parts, axis=1)


def unpack_local(flat):
    out, off = {}, 0
    for name, layer, r, c, by_cols in _PACK:
        n = r * c // D_MODEL
        rows = flat[off:off + n, :]
        out[(name, layer)] = rows.T if _transposed(r, by_cols) else rows.reshape(r, c)
        off += _pack_rows(r, c)
    return out


def rope_tables(pos):
    half = HEAD_DIM // 2
    lane = jnp.arange(128)
    inv = ROPE_THETA ** (-(lane % half).astype(F32) / half)
    ang = pos.astype(F32)[:, None] * inv[None, :]
    sign = jnp.where((lane % HEAD_DIM) < half, -1.0, 1.0).astype(F32)
    return jnp.cos(ang), jnp.sin(ang) * sign[None, :]


def local_step(x, p, pos, tgt, norm_mix, norm_ffn, norm_ple, norm_final, ev_b_f, od_sinks, wfull):
    t = x.shape[0]
    w_in0 = wfull[("ev_w_in", 0)]
    w_in0 = jnp.concatenate([w_in0, jnp.zeros((EVEN_IN_PAD - w_in0.shape[0], D_MODEL), w_in0.dtype)], axis=0)
    b_pad = jnp.zeros((1, GATE_PAD), F32).at[0, :N_FOX].set(ev_b_f[0])
    sinks_b = jnp.broadcast_to(od_sinks[0][:, None, None], (N_Q, 1, 128)).astype(F32)
    cosw, sinw = rope_tables(pos)

    saved = []
    h = x
    for i in range(2):
        s = {"h0": h}
        n1 = rmsnorm_fwd(h, norm_mix[i], f"norm_mix_fwd{i}")
        s["n1"] = n1
        if i == 0:
            proj = matmul(n1, w_in0, "nt", name="ev_in")
            cols = [proj[:, j * FOX_W:(j + 1) * FOX_W] for j in range(6)]
            flog = proj[:, EVEN_QKV:]
            cum = forget_cumsum(flog, b_pad, "forget_cumsum")
            fox = dict(zip(("qt", "q", "f_end", "k_norm"), fox_operands(cols[0], cols[1], cum[:, :N_FOX].T)))
            fox["k"] = fox_keys(proj, cum, "fox_keys")
            fox.update(kt=head_tiles(cols[1].astype(BF16), N_FOX), v=head_rows(cols[2].astype(BF16), N_FOX),
                       vt=head_tiles(cols[2].astype(BF16), N_FOX))
            q_sb = (cols[3] * SCALE).astype(BF16)
            sb = dict(qt=head_cols(q_sb, N_SB), q=head_rows(q_sb, N_SB), k=head_rows(cols[4].astype(BF16), N_SB),
                      kt=head_tiles(cols[4].astype(BF16), N_SB), v=head_rows(cols[5].astype(BF16), N_SB),
                      vt=head_tiles(cols[5].astype(BF16), N_SB))
            fox["ot"], fox["lse"], fox["first"] = fox_fwd(fox["qt"], fox["k"], fox["vt"], fox["f_end"], fox["k_norm"])
            sb["ot"], sb["ltot"], sb["first"] = sb_fwd(sb["qt"], sb["k"], sb["vt"])
            s.update(flog=flog, fox=fox, sb=sb)
            mixin_t = jnp.concatenate([fox["ot"].reshape(FOX_W, t), sb["ot"].reshape(SB_W, t)], axis=0).astype(BF16)
            w_out = wfull[("ev_w_out", 0)]
        else:
            proj = matmul(n1, wfull[("od_w_in", 0)], "nt", name="od_in")
            qk = rope_apply(proj[:, :Q_W + KV_W], cosw, sinw, False, "rope_fwd")
            q_sc = (qk[:, :Q_W] * SCALE).astype(BF16)
            k_b = qk[:, Q_W:].astype(BF16)
            v_b = proj[:, Q_W + KV_W:].astype(BF16)
            swa = dict(qt=head_cols(q_sc, N_Q), q=head_rows(q_sc, N_Q), k=head_rows(k_b, N_KV),
                       kt=head_tiles(k_b, N_KV), v=head_rows(v_b, N_KV), vt=head_tiles(v_b, N_KV))
            swa["ot"], swa["lse"] = swa_fwd(swa["qt"], swa["k"], swa["vt"], sinks_b)
            s["swa"] = swa
            mixin_t = swa["ot"].reshape(Q_W, t).astype(BF16)
            w_out = wfull[("od_w_out", 0)]
        s["mixin_t"] = mixin_t
        h = matmul(mixin_t, w_out, "tn", residual=h, name=f"mix_out{i}")
        s["h1"] = h
        n2 = rmsnorm_fwd(h, norm_ffn[i], f"norm_ffn_fwd{i}")
        gate, up, act = ffn_gate_up(n2, wfull[("ffn_w_gate", i)], wfull[("ffn_w_up", i)], f"ffn_gate_up{i}")
        s.update(n2=n2, gate=gate, up=up, act=act)
        h = matmul(act, wfull[("ffn_w_down", i)], residual=h, name=f"ffn_down{i}")
        s["h2"] = h
        n3 = rmsnorm_fwd(h, norm_ple[i], f"norm_ple_fwd{i}")
        pre = matmul(n3, wfull[("ple_w_gate", i)], name=f"ple_gate{i}")
        pp = matmul(p[i], wfull[("ple_w_proj", i)], name=f"ple_proj{i}")
        s.update(n3=n3, pre=pre, pp=pp)
        h = ple_fwd(h, pre, pp, f"ple_fwd{i}")
        saved.append(s)

    dh, dg_final, loss8 = loss_head(h, norm_final, tgt, "loss_head")
    gw = {}
    small = {"norm_final": dg_final, "loss": loss8}
    for i in (1, 0):
        s = saved[i]
        dpre, dpp = ple_bwd(dh, s["pre"], s["pp"], f"ple_bwd{i}")
        gw[("ple_w_gate", i)] = matmul(s["n3"], dpre, "tn", name=f"d_ple_gate{i}")
        gw[("ple_w_proj", i)] = matmul(p[i], dpp, "tn", name=f"d_ple_proj{i}")
        dn3 = matmul(dpre, wfull[("ple_w_gate", i)], "nt", name=f"dn_ple{i}")
        dh, small[("norm_ple", i)] = rmsnorm_bwd(s["h2"], norm_ple[i], dn3, dh, f"norm_ple_bwd{i}")

        dgate, dup = ffn_gate_up_bwd(dh, wfull[("ffn_w_down", i)], s["gate"], s["up"], f"ffn_gate_up_bwd{i}")
        gw[("ffn_w_down", i)] = matmul(s["act"], dh, "tn", name=f"d_ffn_down{i}")
        gw[("ffn_w_gate", i)] = matmul(dgate, s["n2"], "tn", name=f"d_ffn_gate{i}")
        gw[("ffn_w_up", i)] = matmul(dup, s["n2"], "tn", name=f"d_ffn_up{i}")
        dn2 = matmul(dgate, wfull[("ffn_w_gate", i)], name=f"dn_ffn_gate{i}")
        dn2 = matmul(dup, wfull[("ffn_w_up", i)], residual=dn2, name=f"dn_ffn_up{i}")
        dh, small[("norm_ffn", i)] = rmsnorm_bwd(s["h1"], norm_ffn[i], dn2, dh, f"norm_ffn_bwd{i}")

        if i == 0:
            dmb = matmul(dh, wfull[("ev_w_out", 0)], "nt", out_dtype=BF16, name="d_mix0")
            gw[("ev_w_out", 0)] = matmul(s["mixin_t"], dh, name="d_ev_out")
            fox, sb = s["fox"], s["sb"]
            dqt_f, dk_aug, dv_f, ds_rows = fox_bwd(
                fox["qt"], fox["q"], fox["k"], fox["kt"], fox["v"], fox["ot"],
                head_rows(dmb[:, :FOX_W], N_FOX), head_cols(dmb[:, :FOX_W], N_FOX), fox["lse"], fox["first"])
            dqt_s, dk_s, dv_s = sb_bwd(sb["qt"], sb["q"], sb["k"], sb["kt"], sb["v"], sb["ltot"], sb["first"],
                                       head_rows(dmb[:, FOX_W:], N_SB), head_cols(dmb[:, FOX_W:], N_SB))
            ds_cols = jnp.sum(dk_aug[:, :, HEAD_DIM:HEAD_DIM + N_FOX], axis=0)
            d_cum = jnp.pad(ds_rows[:, 0, :].T - ds_cols, ((0, 0), (0, GATE_PAD - N_FOX)))
            dflog, db8 = forget_cumsum_bwd(d_cum, s["flog"], b_pad, "forget_cumsum_bwd")
            small["ev_b_f"] = db8
            parts = (cols_to_flat(dqt_f), rows_to_flat(dk_aug[:, :, :HEAD_DIM]), rows_to_flat(dv_f),
                     cols_to_flat(dqt_s), rows_to_flat(dk_s), rows_to_flat(dv_s), dflog)
            dproj = jnp.concatenate([a.astype(BF16) for a in parts], axis=1)
            dw = matmul(dproj, s["n1"], "tn", name="d_ev_in")
            gw[("ev_w_in", 0)] = dw[:EVEN_QKV + N_FOX]
            dn1 = matmul(dproj, w_in0, name="dn_mix0")
        else:
            dmb = matmul(dh, wfull[("od_w_out", 0)], "nt", out_dtype=BF16, name="d_mix1")
            gw[("od_w_out", 0)] = matmul(s["mixin_t"], dh, name="d_od_out")
            swa = s["swa"]
            dqt, dk, dv, dsink = swa_bwd(swa["qt"], swa["q"], swa["k"], swa["kt"], swa["v"], sinks_b, swa["ot"],
                                         head_rows(dmb, N_Q), head_cols(dmb, N_Q), swa["lse"])
            small["od_sinks"] = dsink
            dqk = rope_apply(jnp.concatenate([cols_to_flat(dqt), rows_to_flat(dk)], axis=1), cosw, sinw, True,
                             "rope_bwd")
            dproj = jnp.concatenate([dqk, rows_to_flat(dv)], axis=1).astype(BF16)
            gw[("od_w_in", 0)] = matmul(dproj, s["n1"], "tn", name="d_od_in")
            dn1 = matmul(dproj, wfull[("od_w_in", 0)], name="dn_mix1")
        dh, small[("norm_mix", i)] = rmsnorm_bwd(s["h0"], norm_mix[i], dn1, dh, f"norm_mix_bwd{i}")
    return dh, gw, small


_SMALL_ROWS = (("norm_mix", 0), ("norm_mix", 1), ("norm_ffn", 0), ("norm_ffn", 1),
               ("norm_ple", 0), ("norm_ple", 1), "norm_final", "misc", "loss")


def pack_small(small):
    rows = []
    for key in _SMALL_ROWS:
        if key == "misc":
            db = jnp.sum(small["ev_b_f"], axis=0)[:N_FOX]
            dsink = jnp.sum(small["od_sinks"][:, 0, :], axis=1)
            rows.append(jnp.zeros((D_MODEL,), F32).at[:N_FOX].set(db).at[128:128 + N_Q].set(dsink))
        else:
            rows.append(jnp.sum(small[key], axis=0))
    rows += [jnp.zeros((D_MODEL,), F32)] * (SMALL_ROWS - len(rows))
    return jnp.stack(rows)


def kernel(x, p, positions, norm_mix, norm_ffn, norm_ple, norm_final, ev_w_in, ev_b_f, ev_w_out, od_w_in, od_sinks, od_w_out, ffn_w_gate, ffn_w_up, ffn_w_down, ple_w_proj, ple_w_gate, loss_target, m_norm_mix, m_norm_ffn, m_norm_ple, m_norm_final, m_ev_w_in, m_ev_b_f, m_ev_w_out, m_od_w_in, m_od_sinks, m_od_w_out, m_ffn_w_gate, m_ffn_w_up, m_ffn_w_down, m_ple_w_proj, m_ple_w_gate, v_norm_mix, v_norm_ffn, v_norm_ple, v_norm_final, v_ev_w_in, v_ev_b_f, v_ev_w_out, v_od_w_in, v_od_sinks, v_od_w_out, v_ffn_w_gate, v_ffn_w_up, v_ffn_w_down, v_ple_w_proj, v_ple_w_gate):
    local_w = dict(ev_w_in=ev_w_in, ev_w_out=ev_w_out, od_w_in=od_w_in, od_w_out=od_w_out,
                   ffn_w_gate=ffn_w_gate, ffn_w_up=ffn_w_up, ffn_w_down=ffn_w_down,
                   ple_w_proj=ple_w_proj, ple_w_gate=ple_w_gate)
    local_m = dict(ev_w_in=m_ev_w_in, ev_w_out=m_ev_w_out, od_w_in=m_od_w_in, od_w_out=m_od_w_out,
                   ffn_w_gate=m_ffn_w_gate, ffn_w_up=m_ffn_w_up, ffn_w_down=m_ffn_w_down,
                   ple_w_proj=m_ple_w_proj, ple_w_gate=m_ple_w_gate)
    local_v = dict(ev_w_in=v_ev_w_in, ev_w_out=v_ev_w_out, od_w_in=v_od_w_in, od_w_out=v_od_w_out,
                   ffn_w_gate=v_ffn_w_gate, ffn_w_up=v_ffn_w_up, ffn_w_down=v_ffn_w_down,
                   ple_w_proj=v_ple_w_proj, ple_w_gate=v_ple_w_gate)

    wfull = unpack_full(allgather_chips(pack_shards(local_w)))
    grad_x, gw, small = local_step(x[0], p[:, 0], positions[0], loss_target[0], norm_mix, norm_ffn, norm_ple,
                                   norm_final, ev_b_f, od_sinks, wfull)

    cc = lax.axis_index("c")
    g4 = pack_grads(gw).reshape(N_CHIPS, 2, PACK_ROWS_HALF, D_MODEL)
    theirs = pair_send_other_half(g4)
    mine = lax.dynamic_index_in_dim(g4, cc, axis=1, keepdims=False)
    pair_sum = add_pair(mine.reshape(N_CHIPS * PACK_ROWS_HALF, D_MODEL),
                        theirs.reshape(N_CHIPS * PACK_ROWS_HALF, D_MODEL), "add_pair")
    from_chips = exchange_chips(pair_sum.reshape(N_CHIPS, PACK_ROWS_HALF, D_MODEL))
    half_sum = sum_chips(from_chips, "sum_chips")
    other_half = pair_swap(half_sum)
    low = jnp.where(cc == 0, half_sum, other_half)
    high = jnp.where(cc == 0, other_half, half_sum)
    g_local = unpack_local(jnp.concatenate([low, high], axis=0))

    red = allreduce_small(pack_small(small))
    loss = 0.5 * jnp.sum(red[8]) / D_MODEL
    pad_small = lambda a: jnp.zeros((D_MODEL,), F32).at[:N_FOX].set(a[0][0]).at[128:128 + N_Q].set(a[1][0])
    stack_small = lambda a: jnp.concatenate(
        [a[0], a[1], a[2], a[3][None], pad_small(a[4:6])[None], jnp.zeros((SMALL_ROWS - 8, D_MODEL), F32)], axis=0)
    w_small = stack_small((norm_mix, norm_ffn, norm_ple, norm_final, ev_b_f, od_sinks))
    m_small = stack_small((m_norm_mix, m_norm_ffn, m_norm_ple, m_norm_final, m_ev_b_f, m_od_sinks))
    v_small = stack_small((v_norm_mix, v_norm_ffn, v_norm_ple, v_norm_final, v_ev_b_f, v_od_sinks))
    g_small = red.at[8].set(0.0)
    upd_small = (g_small,) + tuple(adamw(w_small, g_small, m_small, v_small, "adamw_small"))

    def split_small(a):
        return (a[0:2], a[2:4], a[4:6], a[6], a[7, :N_FOX][None], a[7, 128:128 + N_Q][None])

    small_out = [split_small(a) for a in upd_small]

    big_names = ("ev_w_in", "ev_w_out", "od_w_in", "od_w_out", "ffn_w_gate", "ffn_w_up", "ffn_w_down",
                 "ple_w_proj", "ple_w_gate")
    big_out = {}
    for name in big_names:
        w = local_w[name]
        layers, r, c = w.shape
        g = jnp.concatenate([g_local[(name, i)] for i in range(layers)], axis=0)
        res = adamw(w.reshape(layers * r, c), g, local_m[name].reshape(layers * r, c),
                    local_v[name].reshape(layers * r, c), f"adamw_{name}")
        big_out[name] = [a.reshape(layers, r, c) for a in (g,) + tuple(res)]

    outs = [loss, grad_x[None]]
    for kind in range(4):
        sm = small_out[kind]
        outs += [sm[0], sm[1], sm[2], sm[3],
                 big_out["ev_w_in"][kind], sm[4], big_out["ev_w_out"][kind],
                 big_out["od_w_in"][kind], sm[5], big_out["od_w_out"][kind],
                 big_out["ffn_w_gate"][kind], big_out["ffn_w_up"][kind], big_out["ffn_w_down"][kind],
                 big_out["ple_w_proj"][kind], big_out["ple_w_gate"][kind]]
    return tuple(outs)
```

```python
import jax
import jax.numpy as jnp
from jax import lax
from jax.experimental import pallas as pl
from jax.experimental.pallas import tpu as pltpu

F32 = jnp.float32
BF16 = jnp.bfloat16

D_MODEL = 1024
HEAD_DIM = 64
N_FOX = 8
N_SB = 8
FOX_W = N_FOX * HEAD_DIM
SB_W = N_SB * HEAD_DIM
EVEN_QKV = 3 * FOX_W + 3 * SB_W
GATE_PAD = 128
EVEN_IN_PAD = EVEN_QKV + GATE_PAD
N_Q = 16
N_KV = 4
GROUP = N_Q // N_KV
Q_W = N_Q * HEAD_DIM
KV_W = N_KV * HEAD_DIM
ODD_IN = Q_W + 2 * KV_W
WINDOW = 128
ROPE_THETA = 10000.0
D_FF = 2816
PLE_DIM = 256
EPS = 1e-6
NEG_INF = -1e30
SCALE = HEAD_DIM ** -0.5

ADAM_LR = 0.001
ADAM_B1 = 0.9
ADAM_B2 = 0.999
ADAM_EPS = 1e-08
ADAM_WD = 0.01
ADAM_STEP = 10

N_CHIPS = 4
VMEM_LIMIT = 48 * 1024 * 1024
MESH = pl.DeviceIdType.MESH

_PACK = (
    ("ev_w_in", 0, 1024, 770, True),
    ("ev_w_out", 0, 256, 1024, False),
    ("od_w_in", 0, 1024, 384, True),
    ("od_w_out", 0, 256, 1024, False),
    ("ffn_w_gate", 0, 1024, 704, True),
    ("ffn_w_gate", 1, 1024, 704, True),
    ("ffn_w_up", 0, 1024, 704, True),
    ("ffn_w_up", 1, 1024, 704, True),
    ("ffn_w_down", 0, 704, 1024, False),
    ("ffn_w_down", 1, 704, 1024, False),
    ("ple_w_proj", 0, 256, 256, True),
    ("ple_w_proj", 1, 256, 256, True),
    ("ple_w_gate", 0, 256, 1024, False),
    ("ple_w_gate", 1, 256, 1024, False),
)
_ROW_ALIGN = 16


def _transposed(r, by_cols):
    return by_cols and r == D_MODEL


def _pack_rows(r, c):
    n = r * c // D_MODEL
    return -(-n // _ROW_ALIGN) * _ROW_ALIGN


PACK_ROWS_HALF = 3328
PACK_ROWS = 2 * PACK_ROWS_HALF
assert sum(_pack_rows(r, c) for _, _, r, c, _ in _PACK) <= PACK_ROWS


def _pick(n, cands):
    for c in cands:
        if n % c == 0:
            return c
    return n


def _cparams(sem):
    return pltpu.CompilerParams(dimension_semantics=sem, vmem_limit_bytes=VMEM_LIMIT)


_DIMS = {
    "nn": (((1,), (0,)), ((), ())),
    "nt": (((1,), (1,)), ((), ())),
    "tn": (((0,), (0,)), ((), ())),
}


def matmul(a, b, mode="nn", out_dtype=F32, residual=None, name="mm"):
    if mode == "nn":
        (m, k), (k2, n) = a.shape, b.shape
    elif mode == "nt":
        (m, k), (n, k2) = a.shape, b.shape
    else:
        (k, m), (k2, n) = a.shape, b.shape
    assert k == k2, (a.shape, b.shape, mode)
    tm = _pick(m, (1024, 1408, 640, 512, 256, 128))
    tn = _pick(n, (1024, 1408, 512, 640, 384, 256, 128))
    tk = _pick(k, (1024, 1408, 640, 512, 256, 128))
    nk = k // tk
    dims = _DIMS[mode]
    has_res = residual is not None

    def body(*refs):
        if has_res:
            a_ref, b_ref, r_ref, o_ref, acc = refs
        else:
            a_ref, b_ref, o_ref, acc = refs
        kk = pl.program_id(2)
        part = lax.dot_general(a_ref[...].astype(BF16), b_ref[...].astype(BF16), dims,
                               preferred_element_type=F32)

        def finish(r):
            if has_res:
                r = r + r_ref[...]
            o_ref[...] = r.astype(out_dtype)

        if nk == 1:
            finish(part)
            return

        @pl.when(kk == 0)
        def _():
            acc[...] = part

        @pl.when((kk > 0) & (kk < nk - 1))
        def _():
            acc[...] += part

        @pl.when(kk == nk - 1)
        def _():
            finish(acc[...] + part)

    if mode == "nn":
        a_spec = pl.BlockSpec((tm, tk), lambda i, j, kk: (i, kk))
        b_spec = pl.BlockSpec((tk, tn), lambda i, j, kk: (kk, j))
    elif mode == "nt":
        a_spec = pl.BlockSpec((tm, tk), lambda i, j, kk: (i, kk))
        b_spec = pl.BlockSpec((tn, tk), lambda i, j, kk: (j, kk))
    else:
        a_spec = pl.BlockSpec((tk, tm), lambda i, j, kk: (kk, i))
        b_spec = pl.BlockSpec((tk, tn), lambda i, j, kk: (kk, j))
    o_spec = pl.BlockSpec((tm, tn), lambda i, j, kk: (i, j))
    in_specs = [a_spec, b_spec] + ([o_spec] if has_res else [])
    args = (a, b) + ((residual,) if has_res else ())
    return pl.pallas_call(
        body, name=name, grid=(m // tm, n // tn, nk),
        in_specs=in_specs, out_specs=o_spec,
        out_shape=jax.ShapeDtypeStruct((m, n), out_dtype),
        scratch_shapes=[pltpu.VMEM((tm, tn), F32)],
        compiler_params=_cparams(("parallel", "parallel", "arbitrary")),
    )(*args)


def _fold8(v):
    r, w = v.shape
    return v.reshape(r // 8, 8, w).sum(axis=0)


ROW_BLOCK_BYTES = 12 * 1024 * 1024


def rowwise(fn, rows, bcasts, outs, accs=(), name="rowwise", reverse=False, row_widths=None):
    t = rows[0].shape[0]
    row_bytes = sum(x.shape[1] * x.dtype.itemsize for x in rows) + sum(w * jnp.dtype(dt).itemsize for w, dt in outs)
    tr = _pick(t, tuple(c for c in (512, 256, 128, 64, 32, 16, 8) if c * row_bytes <= ROW_BLOCK_BYTES or c == 8))
    nr, nb, no, na = len(rows), len(bcasts), len(outs), len(accs)
    steps = t // tr

    def body(*refs):
        ins = [r[...] for r in refs[:nr + nb]]
        out_refs = refs[nr + nb:nr + nb + no]
        acc_refs = refs[nr + nb + no:]
        o, a = fn(*ins)
        for r, v in zip(out_refs, o):
            r[...] = v.astype(r.dtype)
        if na:
            @pl.when(pl.program_id(0) == 0)
            def _():
                for r in acc_refs:
                    r[...] = jnp.zeros_like(r)

            for r, v in zip(acc_refs, a):
                r[...] += v

    if reverse:
        ridx = lambda i: (steps - 1 - i, 0)
    else:
        ridx = lambda i: (i, 0)
    widths = row_widths or [x.shape[1] for x in rows]
    in_specs = [pl.BlockSpec((tr, w), ridx) for w in widths]
    in_specs += [pl.BlockSpec(x.shape, lambda i: (0, 0)) for x in bcasts]
    out_specs = [pl.BlockSpec((tr, w), ridx) for w, _ in outs]
    out_specs += [pl.BlockSpec((8, w), lambda i: (0, 0)) for w in accs]
    out_shape = [jax.ShapeDtypeStruct((t, w), dt) for w, dt in outs]
    out_shape += [jax.ShapeDtypeStruct((8, w), F32) for w in accs]
    res = pl.pallas_call(
        body, name=name, grid=(steps,), in_specs=in_specs, out_specs=out_specs, out_shape=out_shape,
        compiler_params=_cparams(("arbitrary",)),
    )(*rows, *bcasts)
    return res


def _rstd(x):
    return lax.rsqrt(jnp.mean(x * x, axis=-1, keepdims=True) + EPS)


def rmsnorm_fwd(h, g, name):
    def fn(x, gg):
        return ((x * _rstd(x)) * gg,), ()

    return rowwise(fn, [h], [g.reshape(1, -1)], [(D_MODEL, BF16)], name=name)[0]


def _rms_bwd_math(x, gg, dy):
    r = _rstd(x)
    xh = x * r
    u = dy * gg
    dx = r * (u - xh * jnp.mean(u * xh, axis=-1, keepdims=True))
    return dx, dy * xh


def rmsnorm_bwd(h, g, dn, dres, name):
    def fn(x, dy, dr, gg):
        dx, dgp = _rms_bwd_math(x, gg, dy)
        return (dr + dx,), (_fold8(dgp),)

    return rowwise(fn, [h, dn, dres], [g.reshape(1, -1)], [(D_MODEL, F32)], [D_MODEL], name=name)


def loss_head(h, g, tgt, name):
    def fn(x, tg, gg):
        y = (x * _rstd(x)) * gg
        e = y - tg
        dy = e * (1.0 / D_MODEL)
        dx, dgp = _rms_bwd_math(x, gg, dy)
        return (dx,), (_fold8(dgp), _fold8(e * e))

    return rowwise(fn, [h, tgt], [g.reshape(1, -1)], [(D_MODEL, F32)], [D_MODEL, D_MODEL], name=name)


def _sigmoid(x):
    return 1.0 / (1.0 + jnp.exp(-x))


FFN_TILE = 256


def ffn_gate_up(n2, w_gate, w_up, name):
    t = n2.shape[0]
    tm = _pick(t, (1024, 512, 256, 128))

    def body(a_ref, wg_ref, wu_ref, g_ref, u_ref, act_ref):
        a = a_ref[...]
        g = lax.dot_general(a, wg_ref[...], _DIMS["nt"], preferred_element_type=F32)
        u = lax.dot_general(a, wu_ref[...], _DIMS["nt"], preferred_element_type=F32)
        g_ref[...] = g
        u_ref[...] = u
        act_ref[...] = ((g * _sigmoid(g)) * u).astype(BF16)

    w_spec = pl.BlockSpec((FFN_TILE, D_MODEL), lambda i, j: (j, 0))
    o_spec = pl.BlockSpec((tm, FFN_TILE), lambda i, j: (i, j))
    return pl.pallas_call(
        body, name=name, grid=(t // tm, D_FF // FFN_TILE),
        in_specs=[pl.BlockSpec((tm, D_MODEL), lambda i, j: (i, 0)), w_spec, w_spec],
        out_specs=[o_spec, o_spec, o_spec],
        out_shape=[jax.ShapeDtypeStruct((t, D_FF), F32), jax.ShapeDtypeStruct((t, D_FF), F32),
                   jax.ShapeDtypeStruct((t, D_FF), BF16)],
        compiler_params=_cparams(("parallel", "parallel")),
    )(n2, w_gate, w_up)


def ffn_gate_up_bwd(dh, w_down, gate, up, name):
    t = dh.shape[0]
    tm = _pick(t, (1024, 512, 256, 128))

    def body(dh_ref, wd_ref, g_ref, u_ref, dg_ref, du_ref, dh_bf):
        @pl.when(pl.program_id(1) == 0)
        def _():
            dh_bf[...] = dh_ref[...].astype(BF16)

        d = lax.dot_general(dh_bf[...], wd_ref[...], _DIMS["nt"], preferred_element_type=F32)
        g = g_ref[...]
        s = _sigmoid(g)
        silu = g * s
        dg_ref[...] = (d * u_ref[...] * (s + silu * (1.0 - s))).astype(BF16)
        du_ref[...] = (d * silu).astype(BF16)

    o_spec = pl.BlockSpec((tm, FFN_TILE), lambda i, j: (i, j))
    return pl.pallas_call(
        body, name=name, grid=(t // tm, D_FF // FFN_TILE),
        in_specs=[pl.BlockSpec((tm, D_MODEL), lambda i, j: (i, 0)),
                  pl.BlockSpec((FFN_TILE, D_MODEL), lambda i, j: (j, 0)), o_spec, o_spec],
        out_specs=[o_spec, o_spec],
        out_shape=[jax.ShapeDtypeStruct((t, D_FF), BF16)] * 2,
        scratch_shapes=[pltpu.VMEM((tm, D_MODEL), BF16)],
        compiler_params=_cparams(("parallel", "arbitrary")),
    )(dh, w_down, gate, up)


def ple_fwd(h, pre, pp, name):
    def fn(x, a, b):
        return (x + _sigmoid(a) * b,), ()

    return rowwise(fn, [h, pre, pp], [], [(D_MODEL, F32)], name=name)[0]


def ple_bwd(dh, pre, pp, name):
    def fn(d, a, b):
        s = _sigmoid(a)
        return (d * b * s * (1.0 - s), d * s), ()

    return rowwise(fn, [dh, pre, pp], [], [(D_MODEL, BF16), (D_MODEL, BF16)], name=name)


def _rot_half_partner(x, first_half):
    w = x.shape[1]
    return jnp.where(first_half, pltpu.roll(x, w - HEAD_DIM // 2, 1), pltpu.roll(x, HEAD_DIM // 2, 1))


def rope_apply(xx, cosw, sinw, backward, name, width=None):
    width = width or xx.shape[1]
    reps = width // 128

    def fn(x, c, s):
        cw = jnp.tile(c, (1, reps))
        sw = jnp.tile(s, (1, reps))
        lane = lax.broadcasted_iota(jnp.int32, x.shape, 1)
        first = (lane % HEAD_DIM) < (HEAD_DIM // 2)
        if backward:
            return (x * cw + _rot_half_partner(x * sw, first),), ()
        return (x * cw + _rot_half_partner(x, first) * sw,), ()

    return rowwise(fn, [xx, cosw, sinw], [], [(width, F32)], name=name, row_widths=[width, 128, 128])[0]


def _log_sigmoid(x):
    return jnp.minimum(x, 0.0) - jnp.log(1.0 + jnp.exp(-jnp.abs(x)))


CUM_BLOCK = 256


def forget_cumsum(flog, bias, name):
    t = flog.shape[0]
    tb = _pick(t, (CUM_BLOCK,))

    def body(x_ref, b_ref, o_ref, carry):
        @pl.when(pl.program_id(0) == 0)
        def _():
            carry[...] = jnp.zeros_like(carry)

        lf = _log_sigmoid(x_ref[...] + b_ref[...])
        r = lax.broadcasted_iota(jnp.int32, (tb, tb), 0)
        c = lax.broadcasted_iota(jnp.int32, (tb, tb), 1)
        tri = (c <= r).astype(F32)
        cum = jnp.dot(tri, lf, preferred_element_type=F32, precision=lax.Precision.HIGHEST) + carry[...]
        o_ref[...] = cum
        carry[...] = cum[tb - 1:tb, :]

    return pl.pallas_call(
        body, name=name, grid=(t // tb,),
        in_specs=[pl.BlockSpec((tb, GATE_PAD), lambda i: (i, 0)), pl.BlockSpec((1, GATE_PAD), lambda i: (0, 0))],
        out_specs=pl.BlockSpec((tb, GATE_PAD), lambda i: (i, 0)),
        out_shape=jax.ShapeDtypeStruct((t, GATE_PAD), F32),
        scratch_shapes=[pltpu.VMEM((1, GATE_PAD), F32)],
        compiler_params=_cparams(("arbitrary",)),
    )(flog, bias)


def forget_cumsum_bwd(d_cum, flog, bias, name):
    t = flog.shape[0]
    tb = _pick(t, (CUM_BLOCK,))
    nb = t // tb

    def body(d_ref, x_ref, b_ref, o_ref, db_ref, carry):
        @pl.when(pl.program_id(0) == 0)
        def _():
            carry[...] = jnp.zeros_like(carry)
            db_ref[...] = jnp.zeros_like(db_ref)

        r = lax.broadcasted_iota(jnp.int32, (tb, tb), 0)
        c = lax.broadcasted_iota(jnp.int32, (tb, tb), 1)
        tri = (c >= r).astype(F32)
        dlf = jnp.dot(tri, d_ref[...], preferred_element_type=F32, precision=lax.Precision.HIGHEST) + carry[...]
        carry[...] = dlf[0:1, :]
        dx = dlf * (1.0 - _sigmoid(x_ref[...] + b_ref[...]))
        o_ref[...] = dx
        db_ref[...] += _fold8(dx)

    rev = lambda i: (nb - 1 - i, 0)
    return pl.pallas_call(
        body, name=name, grid=(nb,),
        in_specs=[pl.BlockSpec((tb, GATE_PAD), rev), pl.BlockSpec((tb, GATE_PAD), rev),
                  pl.BlockSpec((1, GATE_PAD), lambda i: (0, 0))],
        out_specs=[pl.BlockSpec((tb, GATE_PAD), rev), pl.BlockSpec((8, GATE_PAD), lambda i: (0, 0))],
        out_shape=[jax.ShapeDtypeStruct((t, GATE_PAD), F32), jax.ShapeDtypeStruct((8, GATE_PAD), F32)],
        scratch_shapes=[pltpu.VMEM((1, GATE_PAD), F32)],
        compiler_params=_cparams(("arbitrary",)),
    )(d_cum, flog, bias)


TQ = 512
TK = 128
AUG = 128
N_BIAS = 3
SB_CUTOFF = 110.0
FOX_CUTOFF = 112.0


def _dot(a, b):
    return jnp.dot(a, b, preferred_element_type=F32)


def _rel(shape, d):
    return lax.broadcasted_iota(jnp.int32, shape, 0) - lax.broadcasted_iota(jnp.int32, shape, 1) + d


def _put(x, lo, part, hi=None):
    hi = x.shape[1] if hi is None else hi
    pieces = ([x[:, :lo]] if lo else []) + [part] + ([x[:, hi:]] if hi < x.shape[1] else [])
    return part if len(pieces) == 1 else jnp.concatenate(pieces, axis=1)


def _stop_code(first, narrow):
    return first.astype(F32) + jnp.where(narrow, 0.5, 0.0)


def _read_stop_code(code_ref):
    code = jnp.max(code_ref[...])
    first = code.astype(jnp.int32)
    return first, (code - first.astype(F32)) > 0.25


def _split_bf16(x):
    hi = x.astype(BF16)
    return hi, (x - hi.astype(F32)).astype(BF16)


def _tri_dot(tri2, x):
    hi, lo = _split_bf16(x)
    return _dot(tri2, jnp.concatenate([hi, lo], axis=0))


def _q_cols(width, tq):
    return pl.BlockSpec((None, width, tq), lambda h, i: (h, 0, i))


def _q_rows(width, tq):
    return pl.BlockSpec((None, tq, width), lambda h, i: (h, i, 0))


def _kv_rows(t, width):
    return pl.BlockSpec((None, t, width), lambda h, i: (h, 0, 0))


def _kv_tiles(nk):
    return pl.BlockSpec((None, nk, HEAD_DIM, TK), lambda h, i: (h, 0, 0, 0))


def _blocks(t):
    tq = TQ if t % TQ == 0 else TK
    return tq, tq // TK, t // TK


def fox_fwd(qt_aug, k_aug, vt, f_end, k_norm):
    nh, _, t = qt_aug.shape
    tq, ratio, nk = _blocks(t)
    lanes = tq // 128

    def body(q_ref, k_ref, v_ref, fe_ref, kn_ref, o_ref, lse_ref, first_ref):
        i = pl.program_id(1)
        qv = q_ref[...]
        qf = qv[:HEAD_DIM].astype(F32)
        reach = jnp.sqrt(jnp.sum(qf * qf, axis=0, keepdims=True)) * jnp.tile(kn_ref[...], (1, lanes))

        def step(first, carry, masked, width=tq):
            m, l, acc = carry
            scores = []
            for u in range(ratio):
                lo, hi = (u * TK, tq) if masked else (0, width)
                off = pl.multiple_of((first + u) * TK, TK)
                s = _dot(k_ref[pl.ds(off, TK), :], qv[:, lo:hi])
                if masked:
                    s = jnp.where(_rel(s.shape, 0) <= 0, s, NEG_INF)
                scores.append((lo, hi, s))
            m_new = m
            for lo, hi, s in scores:
                m_new = _put(m_new, lo, jnp.maximum(m_new[:, lo:hi], jnp.max(s, axis=0, keepdims=True)), hi)
            alpha = jnp.exp(m - m_new)
            l = alpha * l
            acc = alpha * acc
            for u, (lo, hi, s) in enumerate(scores):
                p = jnp.exp(s - m_new[:, lo:hi])
                l = _put(l, lo, l[:, lo:hi] + jnp.sum(p, axis=0, keepdims=True), hi)
                acc = _put(acc, lo, acc[:, lo:hi] + _dot(v_ref[first + u], p.astype(BF16)), hi)
            return m_new, l, acc

        def slack(done, m):
            return reach + jnp.tile(fe_ref[jnp.maximum(i - 1 - done, 0)], (1, lanes)) - m

        def sweep_down(width, go, state):
            def more(c):
                return go & (c[0] < i) & (jnp.max(slack(c[0], c[1])[:, :width]) > -FOX_CUTOFF)

            def sweep(c):
                return (c[0] + 1,) + step(ratio * (i - 1 - c[0]), c[1:], False, width)

            return lax.while_loop(more, sweep, state)

        init = (jnp.full((1, tq), NEG_INF, F32), jnp.zeros((1, tq), F32), jnp.zeros((HEAD_DIM, tq), F32))
        state = (jnp.int32(0),) + step(ratio * i, init, True)
        if ratio > 1:
            narrow = jnp.max(slack(0, state[1])[:, tq // 2:]) <= -FOX_CUTOFF
            state = sweep_down(tq // 2, narrow, state)
            state = sweep_down(tq, jnp.logical_not(narrow), state)
        else:
            narrow = False
            state = sweep_down(tq, True, state)
        done, m, l, acc = state
        o_ref[...] = acc / l
        lse_ref[...] = m + jnp.log(l)
        first_ref[...] = jnp.full((1, 128), _stop_code(i - done, narrow), F32)

    return pl.pallas_call(
        body, name="fox_fwd", grid=(nh, t // tq),
        in_specs=[_q_cols(AUG, tq), _kv_rows(t, AUG), _kv_tiles(nk),
                  pl.BlockSpec((None, t // tq, 1, 128), lambda h, i: (h, 0, 0, 0)),
                  pl.BlockSpec((None, 1, 128), lambda h, i: (h, 0, 0))],
        out_specs=[_q_cols(HEAD_DIM, tq), _q_cols(1, tq), _q_cols(1, 128)],
        out_shape=[jax.ShapeDtypeStruct((nh, HEAD_DIM, t), F32), jax.ShapeDtypeStruct((nh, 1, t), F32),
                   jax.ShapeDtypeStruct((nh, 1, 128 * (t // tq)), F32)],
        compiler_params=_cparams(("parallel", "arbitrary")),
    )(qt_aug, k_aug, vt, f_end, k_norm)


def fox_bwd(qt_aug, q_aug, k_aug, kt, v, ot, do, dot_, lse, first):
    nh, _, t = qt_aug.shape
    tq, ratio, nk = _blocks(t)

    def body(qt_ref, q_ref, k_ref, kt_ref, v_ref, ot_ref, do_ref, dot_ref, lse_ref, first_ref,
             dqt_ref, dk_ref, dv_ref, rs_ref):
        i = pl.program_id(1)
        start, narrow = _read_stop_code(first_ref)

        @pl.when(i == 0)
        def _():
            dk_ref[...] = jnp.zeros_like(dk_ref)
            dv_ref[...] = jnp.zeros_like(dv_ref)

        qtv = qt_ref[...]
        qv = q_ref[...]
        dob = do_ref[...]
        dotb = dot_ref[...]
        delta = jnp.sum(ot_ref[...] * dotb.astype(F32), axis=0, keepdims=True)
        lse = lse_ref[...]

        def tile(j, carry, lo, hi, masked):
            dqt, rs = carry
            off = pl.multiple_of(j * TK, TK)
            s = _dot(k_ref[pl.ds(off, TK), :], qtv[:, lo:hi])
            p = jnp.exp(s - lse[:, lo:hi])
            if masked:
                p = jnp.where(_rel(s.shape, 0) <= 0, p, 0.0)
            dp = _dot(v_ref[pl.ds(off, TK), :], dotb[:, lo:hi])
            dsb = (p * (dp - delta[:, lo:hi])).astype(BF16)
            dk_ref[pl.ds(off, TK), :] += _dot(dsb, qv[lo:hi, :])
            dv_ref[pl.ds(off, TK), :] += _dot(p.astype(BF16), dob[lo:hi, :])
            return (_put(dqt, lo, dqt[:, lo:hi] + _dot(kt_ref[j], dsb), hi),
                    _put(rs, lo, rs[:, lo:hi] + jnp.sum(dsb.astype(F32), axis=0, keepdims=True), hi))

        def step(first, carry, masked, width=tq):
            for u in range(ratio):
                lo, hi = (u * TK, tq) if masked else (0, width)
                carry = tile(first + u, carry, lo, hi, masked)
            return carry

        carry = (jnp.zeros((HEAD_DIM, tq), F32), jnp.zeros((1, tq), F32))
        if ratio > 1:
            carry = lax.fori_loop(start, jnp.where(narrow, i, start),
                                  lambda jj, c: step(ratio * jj, c, False, tq // 2), carry)
            carry = lax.fori_loop(start, jnp.where(narrow, start, i), lambda jj, c: step(ratio * jj, c, False), carry)
        else:
            carry = lax.fori_loop(start, i, lambda jj, c: step(ratio * jj, c, False), carry)
        dqt, rs = step(ratio * i, carry, True)
        dqt_ref[...] = dqt * SCALE
        rs_ref[...] = rs

    return pl.pallas_call(
        body, name="fox_bwd", grid=(nh, t // tq),
        in_specs=[_q_cols(AUG, tq), _q_rows(AUG, tq), _kv_rows(t, AUG), _kv_tiles(nk), _kv_rows(t, HEAD_DIM),
                  _q_cols(HEAD_DIM, tq), _q_rows(HEAD_DIM, tq), _q_cols(HEAD_DIM, tq), _q_cols(1, tq),
                  _q_cols(1, 128)],
        out_specs=[_q_cols(HEAD_DIM, tq), _kv_rows(t, AUG), _kv_rows(t, HEAD_DIM), _q_cols(1, tq)],
        out_shape=[jax.ShapeDtypeStruct((nh, HEAD_DIM, t), F32), jax.ShapeDtypeStruct((nh, t, AUG), F32),
                   jax.ShapeDtypeStruct((nh, t, HEAD_DIM), F32), jax.ShapeDtypeStruct((nh, 1, t), F32)],
        compiler_params=_cparams(("arbitrary", "arbitrary")),
    )(qt_aug, q_aug, k_aug, kt, v, ot, do, dot_, lse, first)


def _sb_logits(kb, qv, ok):
    z = _dot(kb, qv)
    e = jnp.exp(-jnp.abs(z))
    ll = -(jnp.maximum(z, 0.0) + jnp.log(1.0 + e))
    if ok is not None:
        ll = jnp.where(ok, ll, 0.0)
    return z, e, ll


def _tri(cmp):
    r = lax.broadcasted_iota(jnp.int32, (TK, 2 * TK), 0)
    c = lax.broadcasted_iota(jnp.int32, (TK, 2 * TK), 1) % TK
    return cmp(r, c).astype(BF16)


def sb_fwd(qt, k, vt):
    nh, _, t = qt.shape
    tq, ratio, nk = _blocks(t)

    def body(q_ref, k_ref, v_ref, o_ref, tot_ref, first_ref):
        i = pl.program_id(1)
        qv = q_ref[...]
        tri_after = _tri(lambda r, c: c > r)

        def tile(j, carry, lo, hi, masked):
            c_l, acc = carry
            off = pl.multiple_of(j * TK, TK)
            ok = _rel((TK, hi - lo), 0) < 0 if masked else None
            z, _, ll = _sb_logits(k_ref[pl.ds(off, TK), :], qv[:, lo:hi], ok)
            a = jnp.exp(z + ll + _tri_dot(tri_after, ll) + c_l[:, lo:hi])
            if masked:
                a = jnp.where(ok, a, 0.0)
            return (_put(c_l, lo, c_l[:, lo:hi] + jnp.sum(ll, axis=0, keepdims=True), hi),
                    _put(acc, lo, acc[:, lo:hi] + _dot(v_ref[j], a.astype(BF16)), hi))

        def step(first, carry, masked, width=tq):
            for u in reversed(range(ratio)):
                lo, hi = (u * TK, tq) if masked else (0, width)
                carry = tile(first + u, carry, lo, hi, masked)
            return carry

        def sweep_down(width, go, state):
            def more(c):
                return go & (c[0] < i) & (jnp.max(c[1][:, :width]) > -SB_CUTOFF)

            def sweep(c):
                return (c[0] + 1,) + step(ratio * (i - 1 - c[0]), c[1:], False, width)

            return lax.while_loop(more, sweep, state)

        state = (jnp.int32(0),) + step(ratio * i, (jnp.zeros((1, tq), F32), jnp.zeros((HEAD_DIM, tq), F32)), True)
        if ratio > 1:
            narrow = jnp.max(state[1][:, tq // 2:]) <= -SB_CUTOFF
            state = sweep_down(tq // 2, narrow, state)
            state = sweep_down(tq, jnp.logical_not(narrow), state)
        else:
            narrow = False
            state = sweep_down(tq, True, state)
        done, c_l, acc = state
        o_ref[...] = acc
        tot_ref[...] = c_l
        first_ref[...] = jnp.full((1, 128), _stop_code(i - done, narrow), F32)

    return pl.pallas_call(
        body, name="sb_fwd", grid=(nh, t // tq),
        in_specs=[_q_cols(HEAD_DIM, tq), _kv_rows(t, HEAD_DIM), _kv_tiles(nk)],
        out_specs=[_q_cols(HEAD_DIM, tq), _q_cols(1, tq), _q_cols(1, 128)],
        out_shape=[jax.ShapeDtypeStruct((nh, HEAD_DIM, t), F32), jax.ShapeDtypeStruct((nh, 1, t), F32),
                   jax.ShapeDtypeStruct((nh, 1, 128 * (t // tq)), F32)],
        compiler_params=_cparams(("parallel", "arbitrary")),
    )(qt, k, vt)


def sb_bwd(qt, q, k, kt, v, ltot, first, do, dot_):
    nh, _, t = qt.shape
    tq, ratio, nk = _blocks(t)

    def body(qt_ref, q_ref, k_ref, kt_ref, v_ref, tot_ref, first_ref, do_ref, dot_ref, dqt_ref, dk_ref, dv_ref):
        i = pl.program_id(1)
        start, narrow = _read_stop_code(first_ref)

        @pl.when(i == 0)
        def _():
            dk_ref[...] = jnp.zeros_like(dk_ref)
            dv_ref[...] = jnp.zeros_like(dv_ref)

        qtv = qt_ref[...]
        qv = q_ref[...]
        dob = do_ref[...]
        dotb = dot_ref[...]
        tri_upto = _tri(lambda r, c: c <= r)
        tri_before = _tri(lambda r, c: c < r)

        def tile(j, carry, lo, hi, masked):
            rest, c_w, dqt = carry
            off = pl.multiple_of(j * TK, TK)
            ok = _rel((TK, hi - lo), 0) < 0 if masked else None
            z, e, ll = _sb_logits(k_ref[pl.ds(off, TK), :], qtv[:, lo:hi], ok)
            a = jnp.exp(z + ll + (rest[:, lo:hi] - _tri_dot(tri_upto, ll)))
            if masked:
                a = jnp.where(ok, a, 0.0)
            w = a * _dot(v_ref[pl.ds(off, TK), :], dotb[:, lo:hi])
            before = _tri_dot(tri_before, w) + c_w[:, lo:hi]
            r = 1.0 / (1.0 + e)
            sig = jnp.where(z >= 0.0, r, e * r)
            dz = w - (w + before) * sig
            if masked:
                dz = jnp.where(ok, dz, 0.0)
            dzb = dz.astype(BF16)
            dk_ref[pl.ds(off, TK), :] += _dot(dzb, qv[lo:hi, :])
            dv_ref[pl.ds(off, TK), :] += _dot(a.astype(BF16), dob[lo:hi, :])
            return (_put(rest, lo, rest[:, lo:hi] - jnp.sum(ll, axis=0, keepdims=True), hi),
                    _put(c_w, lo, c_w[:, lo:hi] + jnp.sum(w, axis=0, keepdims=True), hi),
                    _put(dqt, lo, dqt[:, lo:hi] + _dot(kt_ref[j], dzb), hi))

        def step(first, carry, masked, width=tq):
            for u in range(ratio):
                lo, hi = (u * TK, tq) if masked else (0, width)
                carry = tile(first + u, carry, lo, hi, masked)
            return carry

        carry = (tot_ref[...], jnp.zeros((1, tq), F32), jnp.zeros((HEAD_DIM, tq), F32))
        if ratio > 1:
            carry = lax.fori_loop(start, jnp.where(narrow, i, start),
                                  lambda jj, c: step(ratio * jj, c, False, tq // 2), carry)
            carry = lax.fori_loop(start, jnp.where(narrow, start, i), lambda jj, c: step(ratio * jj, c, False), carry)
        else:
            carry = lax.fori_loop(start, i, lambda jj, c: step(ratio * jj, c, False), carry)
        dqt_ref[...] = step(ratio * i, carry, True)[2] * SCALE

    hd = HEAD_DIM
    return pl.pallas_call(
        body, name="sb_bwd", grid=(nh, t // tq),
        in_specs=[_q_cols(hd, tq), _q_rows(hd, tq), _kv_rows(t, hd), _kv_tiles(nk), _kv_rows(t, hd),
                  _q_cols(1, tq), _q_cols(1, 128), _q_rows(hd, tq), _q_cols(hd, tq)],
        out_specs=[_q_cols(hd, tq), _kv_rows(t, hd), _kv_rows(t, hd)],
        out_shape=[jax.ShapeDtypeStruct((nh, hd, t), F32), jax.ShapeDtypeStruct((nh, t, hd), F32),
                   jax.ShapeDtypeStruct((nh, t, hd), F32)],
        compiler_params=_cparams(("arbitrary", "arbitrary")),
    )(qt, q, k, kt, v, ltot, first, do, dot_)


def _swa_q_cols(width):
    return pl.BlockSpec((GROUP, width, WINDOW), lambda g, i: (g, 0, i))


def _swa_q_rows():
    return pl.BlockSpec((GROUP, WINDOW, HEAD_DIM), lambda g, i: (g, i, 0))


def _swa_kv_rows(t):
    return pl.BlockSpec((None, t, HEAD_DIM), lambda g, i: (g, 0, 0))


def _swa_kv_tiles(nk):
    return pl.BlockSpec((None, nk, HEAD_DIM, WINDOW), lambda g, i: (g, 0, 0, 0))


def _swa_sink_spec():
    return pl.BlockSpec((GROUP, 1, 128), lambda g, i: (g, 0, 0))


def _lane_cat(parts):
    return jnp.concatenate(parts, axis=1)


def _swa_window(i):
    jb = jnp.maximum(i - 1, 0)
    start = pl.multiple_of(jb * WINDOW, WINDOW)
    shape = (2 * WINDOW, GROUP * WINDOW)
    query = lax.broadcasted_iota(jnp.int32, shape, 1) % WINDOW
    rel = lax.broadcasted_iota(jnp.int32, shape, 0) - query + (start - i * WINDOW)
    return jb, start, (rel <= 0) & (rel > -WINDOW)


def swa_fwd(qt, k, vt, sinks):
    nh, _, t = qt.shape
    nk = t // WINDOW

    def body(q_ref, k_ref, v_ref, s_ref, o_ref, lse_ref):
        i = pl.program_id(1)
        jb, start, valid = _swa_window(i)
        qv = _lane_cat([q_ref[g] for g in range(GROUP)])
        sink = _lane_cat([s_ref[g] for g in range(GROUP)])
        s = jnp.where(valid, _dot(k_ref[pl.ds(start, 2 * WINDOW), :], qv), NEG_INF)
        m = jnp.maximum(jnp.max(s, axis=0, keepdims=True), sink)
        p = jnp.where(valid, jnp.exp(s - m), 0.0)
        l = jnp.sum(p, axis=0, keepdims=True) + jnp.exp(sink - m)
        pb = p.astype(BF16)
        o = (_dot(v_ref[jb], pb[:WINDOW]) + _dot(v_ref[jb + 1], pb[WINDOW:])) / l
        lse = m + jnp.log(l)
        for g in range(GROUP):
            o_ref[g] = o[:, g * WINDOW:(g + 1) * WINDOW]
            lse_ref[g] = lse[:, g * WINDOW:(g + 1) * WINDOW]

    return pl.pallas_call(
        body, name="swa_fwd", grid=(N_KV, nk),
        in_specs=[_swa_q_cols(HEAD_DIM), _swa_kv_rows(t), _swa_kv_tiles(nk), _swa_sink_spec()],
        out_specs=[_swa_q_cols(HEAD_DIM), _swa_q_cols(1)],
        out_shape=[jax.ShapeDtypeStruct((nh, HEAD_DIM, t), F32), jax.ShapeDtypeStruct((nh, 1, t), F32)],
        compiler_params=_cparams(("parallel", "arbitrary")),
    )(qt, k, vt, sinks)


def swa_bwd(qt, q, k, kt, v, sinks, ot, do, dot_, lse):
    nh, _, t = qt.shape
    nk = t // WINDOW

    def body(qt_ref, q_ref, k_ref, kt_ref, v_ref, s_ref, ot_ref, do_ref, dot_ref, lse_ref,
             dqt_ref, dk_ref, dv_ref, dsink_ref):
        i = pl.program_id(1)

        @pl.when(i == 0)
        def _():
            dk_ref[...] = jnp.zeros_like(dk_ref)
            dv_ref[...] = jnp.zeros_like(dv_ref)
            dsink_ref[...] = jnp.zeros_like(dsink_ref)

        jb, start, valid = _swa_window(i)
        heads = range(GROUP)
        qtv = _lane_cat([qt_ref[g] for g in heads])
        dotb = _lane_cat([dot_ref[g] for g in heads])
        lse = _lane_cat([lse_ref[g] for g in heads])
        sink = _lane_cat([s_ref[g] for g in heads])
        otv = _lane_cat([ot_ref[g] for g in heads])
        q_rows = jnp.concatenate([q_ref[g] for g in heads], axis=0)
        do_rows = jnp.concatenate([do_ref[g] for g in heads], axis=0)
        delta = jnp.sum(otv * dotb.astype(F32), axis=0, keepdims=True)
        p = jnp.where(valid, jnp.exp(_dot(k_ref[pl.ds(start, 2 * WINDOW), :], qtv) - lse), 0.0)
        dsb = (p * (_dot(v_ref[pl.ds(start, 2 * WINDOW), :], dotb) - delta)).astype(BF16)
        dqt = (_dot(kt_ref[jb], dsb[:WINDOW]) + _dot(kt_ref[jb + 1], dsb[WINDOW:])) * SCALE
        dsink = -jnp.exp(sink - lse) * delta
        for g in heads:
            dqt_ref[g] = dqt[:, g * WINDOW:(g + 1) * WINDOW]
            dsink_ref[g] += dsink[:, g * WINDOW:(g + 1) * WINDOW]
        dk_ref[pl.ds(start, 2 * WINDOW), :] += _dot(dsb, q_rows)
        dv_ref[pl.ds(start, 2 * WINDOW), :] += _dot(p.astype(BF16), do_rows)

    hd = HEAD_DIM
    return pl.pallas_call(
        body, name="swa_bwd", grid=(N_KV, nk),
        in_specs=[_swa_q_cols(hd), _swa_q_rows(), _swa_kv_rows(t), _swa_kv_tiles(nk), _swa_kv_rows(t),
                  _swa_sink_spec(), _swa_q_cols(hd), _swa_q_rows(), _swa_q_cols(hd), _swa_q_cols(1)],
        out_specs=[_swa_q_cols(hd), _swa_kv_rows(t), _swa_kv_rows(t), _swa_sink_spec()],
        out_shape=[jax.ShapeDtypeStruct((nh, hd, t), F32), jax.ShapeDtypeStruct((N_KV, t, hd), F32),
                   jax.ShapeDtypeStruct((N_KV, t, hd), F32), jax.ShapeDtypeStruct((nh, 1, 128), F32)],
        compiler_params=_cparams(("arbitrary", "arbitrary")),
    )(qt, q, k, kt, v, sinks, ot, do, dot_, lse)


def head_rows(a, nh):
    t = a.shape[0]
    return a.reshape(t, nh, HEAD_DIM).transpose(1, 0, 2)


def head_cols(a, nh):
    t = a.shape[0]
    return a.reshape(t, nh, HEAD_DIM).transpose(1, 2, 0)


def head_tiles(a, nh):
    t = a.shape[0]
    return a.reshape(t // TK, TK, nh, HEAD_DIM).transpose(2, 0, 3, 1)


def rows_to_flat(a):
    nh, t, _ = a.shape
    return a.transpose(1, 0, 2).reshape(t, nh * HEAD_DIM)


def cols_to_flat(a):
    nh, _, t = a.shape
    return a.transpose(2, 0, 1).reshape(t, nh * HEAD_DIM)


def fox_keys(proj, cum, name):
    t = proj.shape[0]
    tr = _pick(t, (256, 128))
    width = (1 + N_BIAS) * 128

    def body(k_ref, c_ref, o_ref):
        kb = k_ref[...].astype(BF16)
        terms, rest = [], -c_ref[...]
        for _ in range(N_BIAS):
            term = rest.astype(BF16)
            terms.append(term)
            rest = rest - term.astype(F32)
        row = lax.broadcasted_iota(jnp.int32, (width, AUG), 0)
        col = lax.broadcasted_iota(jnp.int32, (width, AUG), 1)
        for h in range(N_FOX):
            pair = kb[:, 128 * (h // 2):128 * (h // 2 + 1)]
            place = (row < 128) & (col < HEAD_DIM) & (row - HEAD_DIM * (h % 2) == col)
            for b in range(N_BIAS):
                place = place | ((row == 128 * (1 + b) + h) & (col == HEAD_DIM + b))
            src = jnp.concatenate([pair] + terms, axis=1)
            o_ref[h] = _dot(src, place.astype(BF16)).astype(BF16)

    return pl.pallas_call(
        body, name=name, grid=(t // tr,),
        in_specs=[pl.BlockSpec((tr, FOX_W), lambda i: (i, 1)), pl.BlockSpec((tr, GATE_PAD), lambda i: (i, 0))],
        out_specs=pl.BlockSpec((N_FOX, tr, AUG), lambda i: (0, i, 0)),
        out_shape=jax.ShapeDtypeStruct((N_FOX, t, AUG), BF16),
        compiler_params=_cparams(("parallel",)),
    )(proj, cum)


def fox_operands(qf, kf, cum_heads):
    t = qf.shape[0]
    nh = cum_heads.shape[0]
    tq = _blocks(t)[0]
    qs = (qf * SCALE).astype(BF16)
    ones_t = jnp.ones((nh, N_BIAS, t), BF16)
    qt_aug = jnp.concatenate([head_cols(qs, nh), ones_t, jnp.zeros((nh, AUG - HEAD_DIM - N_BIAS, t), BF16)], axis=1)
    own_lane = jnp.broadcast_to(jnp.eye(nh, dtype=BF16)[:, None, :], (nh, t, nh))
    q_aug = jnp.concatenate([head_rows(qs, nh), own_lane, jnp.zeros((nh, t, AUG - HEAD_DIM - nh), BF16)], axis=2)
    f_end = jnp.broadcast_to((-cum_heads)[:, tq - 1::tq, None, None], (nh, t // tq, 1, 128))
    k_sq = jnp.sum(jnp.square(kf.astype(F32)).reshape(t, nh, HEAD_DIM), axis=2)
    k_norm = jnp.broadcast_to(1.01 * jnp.sqrt(jnp.max(k_sq, axis=0))[:, None, None], (nh, 1, 128))
    return qt_aug, q_aug, f_end, k_norm


def adamw(w, g, m, v, name):
    cols = w.shape[1]

    def fn(ww, gg, mm, vv):
        mn = ADAM_B1 * mm + (1.0 - ADAM_B1) * gg
        vn = ADAM_B2 * vv + (1.0 - ADAM_B2) * (gg * gg)
        m_hat = mn / (1.0 - ADAM_B1 ** ADAM_STEP)
        v_hat = vn / (1.0 - ADAM_B2 ** ADAM_STEP)
        delta = -ADAM_LR * (m_hat / (jnp.sqrt(v_hat) + ADAM_EPS) + ADAM_WD * ww)
        return (delta, mn, vn), ()

    return rowwise(fn, [w, g, m, v], [], [(cols, F32)] * 3, name=name)


ANY = pl.BlockSpec(memory_space=pl.ANY)


def _my_place():
    return lax.axis_index("x"), lax.axis_index("y"), lax.axis_index("c")


def _flip(coord, bit):
    return 1 - coord if bit else coord


def allgather_chips(w):
    r, c = w.shape
    rh = r // 2

    def body(w_ref, out_ref, send_sems, recv_sems, local_sem):
        x, y, cc = _my_place()
        me = 2 * x + y
        sibling = (x, y, 1 - cc)
        chips = [(_flip(x, kbits >> 1), _flip(y, kbits & 1)) for kbits in (1, 2, 3)]

        def half(chip, hc):
            return out_ref.at[chip, pl.ds(pl.multiple_of(hc * rh, 16), rh)]

        def copy(k, src, dst, to):
            return pltpu.make_async_remote_copy(src_ref=src, dst_ref=dst, send_sem=send_sems.at[k],
                                                recv_sem=recv_sems.at[k], device_id=to, device_id_type=MESH)

        local = pltpu.make_async_copy(w_ref, out_ref.at[me], local_sem)
        local.start()
        my_half = w_ref.at[pl.ds(pl.multiple_of(cc * rh, 16), rh)]
        first = [copy(j, my_half, half(me, cc), (px, py, cc)) for j, (px, py) in enumerate(chips)]
        for cp in first:
            cp.start()
        passed = []
        for j, (px, py) in enumerate(chips):
            landed = half(2 * px + py, cc)
            copy(j, my_half, landed, (px, py, cc)).wait_recv()
            fwd = copy(3 + j, landed, landed, sibling)
            fwd.start()
            passed.append(fwd)
        for j, (px, py) in enumerate(chips):
            theirs = half(2 * px + py, 1 - cc)
            copy(3 + j, theirs, theirs, sibling).wait_recv()
        for cp in first + passed:
            cp.wait_send()
        local.wait()

    return pl.pallas_call(
        body, name="allgather_chips", in_specs=[ANY], out_specs=ANY,
        out_shape=jax.ShapeDtypeStruct((N_CHIPS, r, c), w.dtype),
        scratch_shapes=[pltpu.SemaphoreType.DMA((6,)), pltpu.SemaphoreType.DMA((6,)), pltpu.SemaphoreType.DMA],
    )(w)


def pair_send_other_half(g):
    n, _, rh, c = g.shape

    def body(g_ref, out_ref, send_sem, recv_sem):
        x, y, cc = _my_place()
        cp = pltpu.make_async_remote_copy(
            src_ref=g_ref.at[:, 1 - cc], dst_ref=out_ref, send_sem=send_sem, recv_sem=recv_sem,
            device_id=(x, y, 1 - cc), device_id_type=MESH)
        cp.start()
        cp.wait()

    return pl.pallas_call(
        body, name="pair_send_other_half", in_specs=[ANY], out_specs=ANY,
        out_shape=jax.ShapeDtypeStruct((n, rh, c), g.dtype),
        scratch_shapes=[pltpu.SemaphoreType.DMA, pltpu.SemaphoreType.DMA],
    )(g)


def exchange_chips(s):
    n, rh, c = s.shape

    def body(s_ref, out_ref, send_sems, recv_sems, local_sem):
        x, y, cc = _my_place()
        me = 2 * x + y
        local = pltpu.make_async_copy(s_ref.at[me], out_ref.at[me], local_sem)
        local.start()
        copies = []
        for kbits in (1, 2, 3):
            px, py = _flip(x, kbits >> 1), _flip(y, kbits & 1)
            cp = pltpu.make_async_remote_copy(
                src_ref=s_ref.at[2 * px + py], dst_ref=out_ref.at[me], send_sem=send_sems.at[kbits - 1],
                recv_sem=recv_sems.at[kbits - 1], device_id=(px, py, cc), device_id_type=MESH)
            cp.start()
            copies.append(cp)
        for cp in copies:
            cp.wait()
        local.wait()

    return pl.pallas_call(
        body, name="exchange_chips", in_specs=[ANY], out_specs=ANY,
        out_shape=jax.ShapeDtypeStruct((n, rh, c), s.dtype),
        scratch_shapes=[pltpu.SemaphoreType.DMA((3,)), pltpu.SemaphoreType.DMA((3,)), pltpu.SemaphoreType.DMA],
    )(s)


def pair_swap(tt):
    def body(t_ref, out_ref, send_sem, recv_sem):
        x, y, cc = _my_place()
        cp = pltpu.make_async_remote_copy(
            src_ref=t_ref, dst_ref=out_ref, send_sem=send_sem, recv_sem=recv_sem,
            device_id=(x, y, 1 - cc), device_id_type=MESH)
        cp.start()
        cp.wait()

    return pl.pallas_call(
        body, name="pair_swap", in_specs=[ANY], out_specs=ANY,
        out_shape=jax.ShapeDtypeStruct(tt.shape, tt.dtype),
        scratch_shapes=[pltpu.SemaphoreType.DMA, pltpu.SemaphoreType.DMA],
    )(tt)


SMALL_ROWS = 16


def allreduce_small(v):
    r, c = v.shape
    vm = pl.BlockSpec(memory_space=pltpu.VMEM)

    def body(v_ref, out_ref, slots, send_sems, recv_sems):
        x, y, cc = _my_place()
        me = 4 * x + 2 * y + cc
        slots[me] = v_ref[...]
        copies = []
        for kbits in range(1, 8):
            peer = (_flip(x, kbits >> 2), _flip(y, (kbits >> 1) & 1), _flip(cc, kbits & 1))
            cp = pltpu.make_async_remote_copy(
                src_ref=v_ref, dst_ref=slots.at[me], send_sem=send_sems.at[kbits - 1],
                recv_sem=recv_sems.at[kbits - 1], device_id=peer, device_id_type=MESH)
            cp.start()
            copies.append(cp)
        for cp in copies:
            cp.wait()
        total = slots[0]
        for dev in range(1, 8):
            total = total + slots[dev]
        out_ref[...] = total

    return pl.pallas_call(
        body, name="allreduce_small", in_specs=[vm], out_specs=vm,
        out_shape=jax.ShapeDtypeStruct((r, c), F32),
        scratch_shapes=[pltpu.VMEM((8, r, c), F32), pltpu.SemaphoreType.DMA((7,)), pltpu.SemaphoreType.DMA((7,))],
    )(v)


def add_pair(mine, theirs, name):
    return rowwise(lambda a, b: ((a + b,), ()), [mine, theirs], [], [(mine.shape[1], BF16)], name=name)[0]


def sum_chips(r4, name):
    _, rh, c = r4.shape
    tr = _pick(rh, (256, 128, 64, 32, 16))

    def body(r_ref, o_ref):
        total = r_ref[0].astype(F32)
        for j in range(1, N_CHIPS):
            total = total + r_ref[j].astype(F32)
        o_ref[...] = total

    return pl.pallas_call(
        body, name=name, grid=(rh // tr,),
        in_specs=[pl.BlockSpec((N_CHIPS, tr, c), lambda i: (0, i, 0))],
        out_specs=pl.BlockSpec((tr, c), lambda i: (i, 0)),
        out_shape=jax.ShapeDtypeStruct((rh, c), F32),
        compiler_params=_cparams(("parallel",)),
    )(r4)


def _pad_rows(a, rows):
    return jnp.pad(a, ((0, rows - a.shape[0]), (0, 0))) if rows != a.shape[0] else a


def pack_shards(local):
    parts = []
    for name, layer, r, c, by_cols in _PACK:
        w = local[name][layer].astype(BF16)
        flat = w.T if _transposed(r, by_cols) else w.reshape(r * c // D_MODEL, D_MODEL)
        parts.append(_pad_rows(flat, _pack_rows(r, c)))
    used = sum(p.shape[0] for p in parts)
    parts.append(jnp.zeros((PACK_ROWS - used, D_MODEL), BF16))
    return jnp.concatenate(parts, axis=0)


def unpack_full(gathered):
    full, off = {}, 0
    for name, layer, r, c, by_cols in _PACK:
        n = r * c // D_MODEL
        if _transposed(r, by_cols):
            full[(name, layer)] = gathered[:, off:off + c, :].reshape(N_CHIPS * c, D_MODEL)
            off += _pack_rows(r, c)
            continue
        blk = gathered[:, off:off + n, :].reshape(N_CHIPS, r, c)
        if by_cols:
            full[(name, layer)] = blk.transpose(1, 0, 2).reshape(r, N_CHIPS * c)
        else:
            full[(name, layer)] = blk.reshape(N_CHIPS * r, c)
        off += _pack_rows(r, c)
    return full


def pack_grads(grads):
    parts = []
    for name, layer, r, c, by_cols in _PACK:
        g = grads[(name, layer)]
        if _transposed(r, by_cols):
            blk = g.reshape(N_CHIPS, c, r)
        elif by_cols:
            blk = g.reshape(r, N_CHIPS, c).transpose(1, 0, 2)
        else:
            blk = g.reshape(N_CHIPS, r, c)
        flat = blk.reshape(N_CHIPS, r * c // D_MODEL, D_MODEL)
        rows = _pack_rows(r, c)
        if rows != flat.shape[1]:
            flat = jnp.pad(flat, ((0, 0), (0, rows - flat.shape[1]), (0, 0)))
        parts.append(flat)
    used = sum(p.shape[1] for p in parts)
    parts.append(jnp.zeros((N_CHIPS, PACK_ROWS - used, D_MODEL), F32))
    return jnp.concatenate(parts, axis=1)


def unpack_local(flat):
    out, off = {}, 0
    for name, layer, r, c, by_cols in _PACK:
        n = r * c // D_MODEL
        rows = flat[off:off + n, :]
        out[(name, layer)] = rows.T if _transposed(r, by_cols) else rows.reshape(r, c)
        off += _pack_rows(r, c)
    return out


def rope_tables(pos):
    half = HEAD_DIM // 2
    lane = jnp.arange(128)
    inv = ROPE_THETA ** (-(lane % half).astype(F32) / half)
    ang = pos.astype(F32)[:, None] * inv[None, :]
    sign = jnp.where((lane % HEAD_DIM) < half, -1.0, 1.0).astype(F32)
    return jnp.cos(ang), jnp.sin(ang) * sign[None, :]


def local_step(x, p, pos, tgt, norm_mix, norm_ffn, norm_ple, norm_final, ev_b_f, od_sinks, wfull):
    t = x.shape[0]
    w_in0 = wfull[("ev_w_in", 0)]
    w_in0 = jnp.concatenate([w_in0, jnp.zeros((EVEN_IN_PAD - w_in0.shape[0], D_MODEL), w_in0.dtype)], axis=0)
    b_pad = jnp.zeros((1, GATE_PAD), F32).at[0, :N_FOX].set(ev_b_f[0])
    sinks_b = jnp.broadcast_to(od_sinks[0][:, None, None], (N_Q, 1, 128)).astype(F32)
    cosw, sinw = rope_tables(pos)

    saved = []
    h = x
    for i in range(2):
        s = {"h0": h}
        n1 = rmsnorm_fwd(h, norm_mix[i], f"norm_mix_fwd{i}")
        s["n1"] = n1
        if i == 0:
            proj = matmul(n1, w_in0[:EVEN_QKV], "nt", out_dtype=BF16, name="ev_in")
            flog = matmul(n1, w_in0[EVEN_QKV:], "nt", name="ev_gate")
            cols = [proj[:, j * FOX_W:(j + 1) * FOX_W] for j in range(6)]
            cum = forget_cumsum(flog, b_pad, "forget_cumsum")
            fox = dict(zip(("qt", "q", "f_end", "k_norm"), fox_operands(cols[0], cols[1], cum[:, :N_FOX].T)))
            fox["k"] = fox_keys(proj, cum, "fox_keys")
            fox.update(kt=head_tiles(cols[1].astype(BF16), N_FOX), v=head_rows(cols[2].astype(BF16), N_FOX),
                       vt=head_tiles(cols[2].astype(BF16), N_FOX))
            q_sb = (cols[3] * SCALE).astype(BF16)
            sb = dict(qt=head_cols(q_sb, N_SB), q=head_rows(q_sb, N_SB), k=head_rows(cols[4].astype(BF16), N_SB),
                      kt=head_tiles(cols[4].astype(BF16), N_SB), v=head_rows(cols[5].astype(BF16), N_SB),
                      vt=head_tiles(cols[5].astype(BF16), N_SB))
            fox["ot"], fox["lse"], fox["first"] = fox_fwd(fox["qt"], fox["k"], fox["vt"], fox["f_end"], fox["k_norm"])
            sb["ot"], sb["ltot"], sb["first"] = sb_fwd(sb["qt"], sb["k"], sb["vt"])
            s.update(flog=flog, fox=fox, sb=sb)
            mixin_t = jnp.concatenate([fox["ot"].reshape(FOX_W, t), sb["ot"].reshape(SB_W, t)], axis=0).astype(BF16)
            w_out = wfull[("ev_w_out", 0)]
        else:
            proj = matmul(n1, wfull[("od_w_in", 0)], "nt", name="od_in")
            qk = rope_apply(proj, cosw, sinw, False, "rope_fwd", width=Q_W + KV_W)
            q_sc = (qk[:, :Q_W] * SCALE).astype(BF16)
            k_b = qk[:, Q_W:].astype(BF16)
            v_b = proj[:, Q_W + KV_W:].astype(BF16)
            swa = dict(qt=head_cols(q_sc, N_Q), q=head_rows(q_sc, N_Q), k=head_rows(k_b, N_KV),
                       kt=head_tiles(k_b, N_KV), v=head_rows(v_b, N_KV), vt=head_tiles(v_b, N_KV))
            swa["ot"], swa["lse"] = swa_fwd(swa["qt"], swa["k"], swa["vt"], sinks_b)
            s["swa"] = swa
            mixin_t = swa["ot"].reshape(Q_W, t).astype(BF16)
            w_out = wfull[("od_w_out", 0)]
        s["mixin_t"] = mixin_t
        h = matmul(mixin_t, w_out, "tn", residual=h, name=f"mix_out{i}")
        s["h1"] = h
        n2 = rmsnorm_fwd(h, norm_ffn[i], f"norm_ffn_fwd{i}")
        gate, up, act = ffn_gate_up(n2, wfull[("ffn_w_gate", i)], wfull[("ffn_w_up", i)], f"ffn_gate_up{i}")
        s.update(n2=n2, gate=gate, up=up, act=act)
        h = matmul(act, wfull[("ffn_w_down", i)], residual=h, name=f"ffn_down{i}")
        s["h2"] = h
        n3 = rmsnorm_fwd(h, norm_ple[i], f"norm_ple_fwd{i}")
        pre = matmul(n3, wfull[("ple_w_gate", i)], name=f"ple_gate{i}")
        pp = matmul(p[i], wfull[("ple_w_proj", i)], name=f"ple_proj{i}")
        s.update(n3=n3, pre=pre, pp=pp)
        h = ple_fwd(h, pre, pp, f"ple_fwd{i}")
        saved.append(s)

    dh, dg_final, loss8 = loss_head(h, norm_final, tgt, "loss_head")
    gw = {}
    small = {"norm_final": dg_final, "loss": loss8}
    for i in (1, 0):
        s = saved[i]
        dpre, dpp = ple_bwd(dh, s["pre"], s["pp"], f"ple_bwd{i}")
        gw[("ple_w_gate", i)] = matmul(s["n3"], dpre, "tn", name=f"d_ple_gate{i}")
        gw[("ple_w_proj", i)] = matmul(p[i], dpp, "tn", name=f"d_ple_proj{i}")
        dn3 = matmul(dpre, wfull[("ple_w_gate", i)], "nt", name=f"dn_ple{i}")
        dh, small[("norm_ple", i)] = rmsnorm_bwd(s["h2"], norm_ple[i], dn3, dh, f"norm_ple_bwd{i}")

        dgate, dup = ffn_gate_up_bwd(dh, wfull[("ffn_w_down", i)], s["gate"], s["up"], f"ffn_gate_up_bwd{i}")
        gw[("ffn_w_down", i)] = matmul(s["act"], dh, "tn", name=f"d_ffn_down{i}")
        gw[("ffn_w_gate", i)] = matmul(dgate, s["n2"], "tn", name=f"d_ffn_gate{i}")
        gw[("ffn_w_up", i)] = matmul(dup, s["n2"], "tn", name=f"d_ffn_up{i}")
        dn2 = matmul(dgate, wfull[("ffn_w_gate", i)], name=f"dn_ffn_gate{i}")
        dn2 = matmul(dup, wfull[("ffn_w_up", i)], residual=dn2, name=f"dn_ffn_up{i}")
        dh, small[("norm_ffn", i)] = rmsnorm_bwd(s["h1"], norm_ffn[i], dn2, dh, f"norm_ffn_bwd{i}")

        if i == 0:
            dmb = matmul(dh, wfull[("ev_w_out", 0)], "nt", out_dtype=BF16, name="d_mix0")
            gw[("ev_w_out", 0)] = matmul(s["mixin_t"], dh, name="d_ev_out")
            fox, sb = s["fox"], s["sb"]
            dqt_f, dk_aug, dv_f, ds_rows = fox_bwd(
                fox["qt"], fox["q"], fox["k"], fox["kt"], fox["v"], fox["ot"],
                head_rows(dmb[:, :FOX_W], N_FOX), head_cols(dmb[:, :FOX_W], N_FOX), fox["lse"], fox["first"])
            dqt_s, dk_s, dv_s = sb_bwd(sb["qt"], sb["q"], sb["k"], sb["kt"], sb["v"], sb["ltot"], sb["first"],
                                       head_rows(dmb[:, FOX_W:], N_SB), head_cols(dmb[:, FOX_W:], N_SB))
            ds_cols = jnp.sum(dk_aug[:, :, HEAD_DIM:HEAD_DIM + N_FOX], axis=0)
            d_cum = jnp.pad(ds_rows[:, 0, :].T - ds_cols, ((0, 0), (0, GATE_PAD - N_FOX)))
            dflog, db8 = forget_cumsum_bwd(d_cum, s["flog"], b_pad, "forget_cumsum_bwd")
            small["ev_b_f"] = db8
            parts = (cols_to_flat(dqt_f), rows_to_flat(dk_aug[:, :, :HEAD_DIM]), rows_to_flat(dv_f),
                     cols_to_flat(dqt_s), rows_to_flat(dk_s), rows_to_flat(dv_s), dflog)
            dproj = jnp.concatenate([a.astype(BF16) for a in parts], axis=1)
            dw = matmul(dproj, s["n1"], "tn", name="d_ev_in")
            gw[("ev_w_in", 0)] = dw[:EVEN_QKV + N_FOX]
            dn1 = matmul(dproj, w_in0, name="dn_mix0")
        else:
            dmb = matmul(dh, wfull[("od_w_out", 0)], "nt", out_dtype=BF16, name="d_mix1")
            gw[("od_w_out", 0)] = matmul(s["mixin_t"], dh, name="d_od_out")
            swa = s["swa"]
            dqt, dk, dv, dsink = swa_bwd(swa["qt"], swa["q"], swa["k"], swa["kt"], swa["v"], sinks_b, swa["ot"],
                                         head_rows(dmb, N_Q), head_cols(dmb, N_Q), swa["lse"])
            small["od_sinks"] = dsink
            dqk = rope_apply(jnp.concatenate([cols_to_flat(dqt), rows_to_flat(dk)], axis=1), cosw, sinw, True,
                             "rope_bwd")
            dproj = jnp.concatenate([dqk, rows_to_flat(dv)], axis=1).astype(BF16)
            gw[("od_w_in", 0)] = matmul(dproj, s["n1"], "tn", name="d_od_in")
            dn1 = matmul(dproj, wfull[("od_w_in", 0)], name="dn_mix1")
        dh, small[("norm_mix", i)] = rmsnorm_bwd(s["h0"], norm_mix[i], dn1, dh, f"norm_mix_bwd{i}")
    return dh, gw, small


_SMALL_ROWS = (("norm_mix", 0), ("norm_mix", 1), ("norm_ffn", 0), ("norm_ffn", 1),
               ("norm_ple", 0), ("norm_ple", 1), "norm_final", "misc", "loss")


def pack_small(small):
    rows = []
    for key in _SMALL_ROWS:
        if key == "misc":
            db = jnp.sum(small["ev_b_f"], axis=0)[:N_FOX]
            dsink = jnp.sum(small["od_sinks"][:, 0, :], axis=1)
            rows.append(jnp.zeros((D_MODEL,), F32).at[:N_FOX].set(db).at[128:128 + N_Q].set(dsink))
        else:
            rows.append(jnp.sum(small[key], axis=0))
    rows += [jnp.zeros((D_MODEL,), F32)] * (SMALL_ROWS - len(rows))
    return jnp.stack(rows)


def kernel(x, p, positions, norm_mix, norm_ffn, norm_ple, norm_final, ev_w_in, ev_b_f, ev_w_out, od_w_in, od_sinks, od_w_out, ffn_w_gate, ffn_w_up, ffn_w_down, ple_w_proj, ple_w_gate, loss_target, m_norm_mix, m_norm_ffn, m_norm_ple, m_norm_final, m_ev_w_in, m_ev_b_f, m_ev_w_out, m_od_w_in, m_od_sinks, m_od_w_out, m_ffn_w_gate, m_ffn_w_up, m_ffn_w_down, m_ple_w_proj, m_ple_w_gate, v_norm_mix, v_norm_ffn, v_norm_ple, v_norm_final, v_ev_w_in, v_ev_b_f, v_ev_w_out, v_od_w_in, v_od_sinks, v_od_w_out, v_ffn_w_gate, v_ffn_w_up, v_ffn_w_down, v_ple_w_proj, v_ple_w_gate):
    local_w = dict(ev_w_in=ev_w_in, ev_w_out=ev_w_out, od_w_in=od_w_in, od_w_out=od_w_out,
                   ffn_w_gate=ffn_w_gate, ffn_w_up=ffn_w_up, ffn_w_down=ffn_w_down,
                   ple_w_proj=ple_w_proj, ple_w_gate=ple_w_gate)
    local_m = dict(ev_w_in=m_ev_w_in, ev_w_out=m_ev_w_out, od_w_in=m_od_w_in, od_w_out=m_od_w_out,
                   ffn_w_gate=m_ffn_w_gate, ffn_w_up=m_ffn_w_up, ffn_w_down=m_ffn_w_down,
                   ple_w_proj=m_ple_w_proj, ple_w_gate=m_ple_w_gate)
    local_v = dict(ev_w_in=v_ev_w_in, ev_w_out=v_ev_w_out, od_w_in=v_od_w_in, od_w_out=v_od_w_out,
                   ffn_w_gate=v_ffn_w_gate, ffn_w_up=v_ffn_w_up, ffn_w_down=v_ffn_w_down,
                   ple_w_proj=v_ple_w_proj, ple_w_gate=v_ple_w_gate)

    wfull = unpack_full(allgather_chips(pack_shards(local_w)))
    grad_x, gw, small = local_step(x[0], p[:, 0], positions[0], loss_target[0], norm_mix, norm_ffn, norm_ple,
                                   norm_final, ev_b_f, od_sinks, wfull)

    cc = lax.axis_index("c")
    g4 = pack_grads(gw).reshape(N_CHIPS, 2, PACK_ROWS_HALF, D_MODEL)
    theirs = pair_send_other_half(g4)
    mine = lax.dynamic_index_in_dim(g4, cc, axis=1, keepdims=False)
    pair_sum = add_pair(mine.reshape(N_CHIPS * PACK_ROWS_HALF, D_MODEL),
                        theirs.reshape(N_CHIPS * PACK_ROWS_HALF, D_MODEL), "add_pair")
    from_chips = exchange_chips(pair_sum.reshape(N_CHIPS, PACK_ROWS_HALF, D_MODEL))
    half_sum = sum_chips(from_chips, "sum_chips")
    other_half = pair_swap(half_sum)
    low = jnp.where(cc == 0, half_sum, other_half)
    high = jnp.where(cc == 0, other_half, half_sum)
    g_local = unpack_local(jnp.concatenate([low, high], axis=0))

    red = allreduce_small(pack_small(small))
    loss = 0.5 * jnp.sum(red[8]) / D_MODEL
    pad_small = lambda a: jnp.zeros((D_MODEL,), F32).at[:N_FOX].set(a[0][0]).at[128:128 + N_Q].set(a[1][0])
    stack_small = lambda a: jnp.concatenate(
        [a[0], a[1], a[2], a[3][None], pad_small(a[4:6])[None], jnp.zeros((SMALL_ROWS - 8, D_MODEL), F32)], axis=0)
    w_small = stack_small((norm_mix, norm_ffn, norm_ple, norm_final, ev_b_f, od_sinks))
    m_small = stack_small((m_norm_mix, m_norm_ffn, m_norm_ple, m_norm_final, m_ev_b_f, m_od_sinks))
    v_small = stack_small((v_norm_mix, v_norm_ffn, v_norm_ple, v_norm_final, v_ev_b_f, v_od_sinks))
    g_small = red.at[8].set(0.0)
    upd_small = (g_small,) + tuple(adamw(w_small, g_small, m_small, v_small, "adamw_small"))

    def split_small(a):
        return (a[0:2], a[2:4], a[4:6], a[6], a[7, :N_FOX][None], a[7, 128:128 + N_Q][None])

    small_out = [split_small(a) for a in upd_small]

    big_names = ("ev_w_in", "ev_w_out", "od_w_in", "od_w_out", "ffn_w_gate", "ffn_w_up", "ffn_w_down",
                 "ple_w_proj", "ple_w_gate")
    big_out = {}
    for name in big_names:
        w = local_w[name]
        layers, r, c = w.shape
        g = jnp.concatenate([g_local[(name, i)] for i in range(layers)], axis=0)
        res = adamw(w.reshape(layers * r, c), g, local_m[name].reshape(layers * r, c),
                    local_v[name].reshape(layers * r, c), f"adamw_{name}")
        big_out[name] = [a.reshape(layers, r, c) for a in (g,) + tuple(res)]

    outs = [loss, grad_x[None]]
    for kind in range(4):
        sm = small_out[kind]
        outs += [sm[0], sm[1], sm[2], sm[3],
                 big_out["ev_w_in"][kind], sm[4], big_out["ev_w_out"][kind],
                 big_out["od_w_in"][kind], sm[5], big_out["od_w_out"][kind],
                 big_out["ffn_w_gate"][kind], big_out["ffn_w_up"][kind], big_out["ffn_w_down"][kind],
                 big_out["ple_w_proj"][kind], big_out["ple_w_gate"][kind]]
    return tuple(outs)
```

```python
import jax
import jax.numpy as jnp
from jax import lax
from jax.experimental import pallas as pl
from jax.experimental.pallas import tpu as pltpu

F32 = jnp.float32
BF16 = jnp.bfloat16

D_MODEL = 1024
HEAD_DIM = 64
N_FOX = 8
N_SB = 8
FOX_W = N_FOX * HEAD_DIM
SB_W = N_SB * HEAD_DIM
EVEN_QKV = 3 * FOX_W + 3 * SB_W
GATE_PAD = 128
EVEN_IN_PAD = EVEN_QKV + GATE_PAD
N_Q = 16
N_KV = 4
GROUP = N_Q // N_KV
Q_W = N_Q * HEAD_DIM
KV_W = N_KV * HEAD_DIM
ODD_IN = Q_W + 2 * KV_W
WINDOW = 128
ROPE_THETA = 10000.0
D_FF = 2816
PLE_DIM = 256
EPS = 1e-6
NEG_INF = -1e30
SCALE = HEAD_DIM ** -0.5

ADAM_LR = 0.001
ADAM_B1 = 0.9
ADAM_B2 = 0.999
ADAM_EPS = 1e-08
ADAM_WD = 0.01
ADAM_STEP = 10

N_CHIPS = 4
VMEM_LIMIT = 48 * 1024 * 1024
MESH = pl.DeviceIdType.MESH

_PACK = (
    ("ev_w_in", 0, 1024, 770, True),
    ("ev_w_out", 0, 256, 1024, False),
    ("od_w_in", 0, 1024, 384, True),
    ("od_w_out", 0, 256, 1024, False),
    ("ffn_w_gate", 0, 1024, 704, True),
    ("ffn_w_gate", 1, 1024, 704, True),
    ("ffn_w_up", 0, 1024, 704, True),
    ("ffn_w_up", 1, 1024, 704, True),
    ("ffn_w_down", 0, 704, 1024, False),
    ("ffn_w_down", 1, 704, 1024, False),
    ("ple_w_proj", 0, 256, 256, True),
    ("ple_w_proj", 1, 256, 256, True),
    ("ple_w_gate", 0, 256, 1024, False),
    ("ple_w_gate", 1, 256, 1024, False),
)
_ROW_ALIGN = 16


def _transposed(r, by_cols):
    return by_cols and r == D_MODEL


def _pack_rows(r, c):
    n = r * c // D_MODEL
    return -(-n // _ROW_ALIGN) * _ROW_ALIGN


PACK_ROWS_HALF = 3328
PACK_ROWS = 2 * PACK_ROWS_HALF
assert sum(_pack_rows(r, c) for _, _, r, c, _ in _PACK) <= PACK_ROWS


def _pick(n, cands):
    for c in cands:
        if n % c == 0:
            return c
    return n


def _cparams(sem):
    return pltpu.CompilerParams(dimension_semantics=sem, vmem_limit_bytes=VMEM_LIMIT)


_DIMS = {
    "nn": (((1,), (0,)), ((), ())),
    "nt": (((1,), (1,)), ((), ())),
    "tn": (((0,), (0,)), ((), ())),
}


def matmul(a, b, mode="nn", out_dtype=F32, residual=None, name="mm"):
    if mode == "nn":
        (m, k), (k2, n) = a.shape, b.shape
    elif mode == "nt":
        (m, k), (n, k2) = a.shape, b.shape
    else:
        (k, m), (k2, n) = a.shape, b.shape
    assert k == k2, (a.shape, b.shape, mode)
    tm = _pick(m, (1024, 1408, 640, 512, 256, 128))
    tn = _pick(n, (1024, 1408, 512, 640, 384, 256, 128))
    tk = _pick(k, (1024, 1408, 640, 512, 256, 128))
    nk = k // tk
    dims = _DIMS[mode]
    has_res = residual is not None

    def body(*refs):
        if has_res:
            a_ref, b_ref, r_ref, o_ref, acc = refs
        else:
            a_ref, b_ref, o_ref, acc = refs
        kk = pl.program_id(2)
        part = lax.dot_general(a_ref[...].astype(BF16), b_ref[...].astype(BF16), dims,
                               preferred_element_type=F32)

        def finish(r):
            if has_res:
                r = r + r_ref[...]
            o_ref[...] = r.astype(out_dtype)

        if nk == 1:
            finish(part)
            return

        @pl.when(kk == 0)
        def _():
            acc[...] = part

        @pl.when((kk > 0) & (kk < nk - 1))
        def _():
            acc[...] += part

        @pl.when(kk == nk - 1)
        def _():
            finish(acc[...] + part)

    if mode == "nn":
        a_spec = pl.BlockSpec((tm, tk), lambda i, j, kk: (i, kk))
        b_spec = pl.BlockSpec((tk, tn), lambda i, j, kk: (kk, j))
    elif mode == "nt":
        a_spec = pl.BlockSpec((tm, tk), lambda i, j, kk: (i, kk))
        b_spec = pl.BlockSpec((tn, tk), lambda i, j, kk: (j, kk))
    else:
        a_spec = pl.BlockSpec((tk, tm), lambda i, j, kk: (kk, i))
        b_spec = pl.BlockSpec((tk, tn), lambda i, j, kk: (kk, j))
    o_spec = pl.BlockSpec((tm, tn), lambda i, j, kk: (i, j))
    in_specs = [a_spec, b_spec] + ([o_spec] if has_res else [])
    args = (a, b) + ((residual,) if has_res else ())
    return pl.pallas_call(
        body, name=name, grid=(m // tm, n // tn, nk),
        in_specs=in_specs, out_specs=o_spec,
        out_shape=jax.ShapeDtypeStruct((m, n), out_dtype),
        scratch_shapes=[pltpu.VMEM((tm, tn), F32)],
        compiler_params=_cparams(("parallel", "parallel", "arbitrary")),
    )(*args)


def _fold8(v):
    r, w = v.shape
    return v.reshape(r // 8, 8, w).sum(axis=0)


ROW_BLOCK_BYTES = 12 * 1024 * 1024


def rowwise(fn, rows, bcasts, outs, accs=(), name="rowwise", reverse=False, row_widths=None):
    t = rows[0].shape[0]
    row_bytes = sum(x.shape[1] * x.dtype.itemsize for x in rows) + sum(w * jnp.dtype(dt).itemsize for w, dt in outs)
    tr = _pick(t, tuple(c for c in (512, 256, 128, 64, 32, 16, 8) if c * row_bytes <= ROW_BLOCK_BYTES or c == 8))
    nr, nb, no, na = len(rows), len(bcasts), len(outs), len(accs)
    steps = t // tr

    def body(*refs):
        ins = [r[...] for r in refs[:nr + nb]]
        out_refs = refs[nr + nb:nr + nb + no]
        acc_refs = refs[nr + nb + no:]
        o, a = fn(*ins)
        for r, v in zip(out_refs, o):
            r[...] = v.astype(r.dtype)
        if na:
            @pl.when(pl.program_id(0) == 0)
            def _():
                for r in acc_refs:
                    r[...] = jnp.zeros_like(r)

            for r, v in zip(acc_refs, a):
                r[...] += v

    if reverse:
        ridx = lambda i: (steps - 1 - i, 0)
    else:
        ridx = lambda i: (i, 0)
    widths = row_widths or [x.shape[1] for x in rows]
    in_specs = [pl.BlockSpec((tr, w), ridx) for w in widths]
    in_specs += [pl.BlockSpec(x.shape, lambda i: (0, 0)) for x in bcasts]
    out_specs = [pl.BlockSpec((tr, w), ridx) for w, _ in outs]
    out_specs += [pl.BlockSpec((8, w), lambda i: (0, 0)) for w in accs]
    out_shape = [jax.ShapeDtypeStruct((t, w), dt) for w, dt in outs]
    out_shape += [jax.ShapeDtypeStruct((8, w), F32) for w in accs]
    res = pl.pallas_call(
        body, name=name, grid=(steps,), in_specs=in_specs, out_specs=out_specs, out_shape=out_shape,
        compiler_params=_cparams(("arbitrary",)),
    )(*rows, *bcasts)
    return res


def _rstd(x):
    return lax.rsqrt(jnp.mean(x * x, axis=-1, keepdims=True) + EPS)


def rmsnorm_fwd(h, g, name):
    def fn(x, gg):
        return ((x * _rstd(x)) * gg,), ()

    return rowwise(fn, [h], [g.reshape(1, -1)], [(D_MODEL, BF16)], name=name)[0]


def _rms_bwd_math(x, gg, dy):
    r = _rstd(x)
    xh = x * r
    u = dy * gg
    dx = r * (u - xh * jnp.mean(u * xh, axis=-1, keepdims=True))
    return dx, dy * xh


def rmsnorm_bwd(h, g, dn, dres, name):
    def fn(x, dy, dr, gg):
        dx, dgp = _rms_bwd_math(x, gg, dy)
        return (dr + dx,), (_fold8(dgp),)

    return rowwise(fn, [h, dn, dres], [g.reshape(1, -1)], [(D_MODEL, F32)], [D_MODEL], name=name)


def loss_head(h, g, tgt, name):
    def fn(x, tg, gg):
        y = (x * _rstd(x)) * gg
        e = y - tg
        dy = e * (1.0 / D_MODEL)
        dx, dgp = _rms_bwd_math(x, gg, dy)
        return (dx,), (_fold8(dgp), _fold8(e * e))

    return rowwise(fn, [h, tgt], [g.reshape(1, -1)], [(D_MODEL, F32)], [D_MODEL, D_MODEL], name=name)


def _sigmoid(x):
    return 1.0 / (1.0 + jnp.exp(-x))


FFN_TILE = 256


def ffn_gate_up(n2, w_gate, w_up, name):
    t = n2.shape[0]
    tm = _pick(t, (1024, 512, 256, 128))

    def body(a_ref, wg_ref, wu_ref, g_ref, u_ref, act_ref):
        a = a_ref[...]
        g = lax.dot_general(a, wg_ref[...], _DIMS["nt"], preferred_element_type=F32)
        u = lax.dot_general(a, wu_ref[...], _DIMS["nt"], preferred_element_type=F32)
        g_ref[...] = g
        u_ref[...] = u
        act_ref[...] = ((g * _sigmoid(g)) * u).astype(BF16)

    w_spec = pl.BlockSpec((FFN_TILE, D_MODEL), lambda i, j: (j, 0))
    o_spec = pl.BlockSpec((tm, FFN_TILE), lambda i, j: (i, j))
    return pl.pallas_call(
        body, name=name, grid=(t // tm, D_FF // FFN_TILE),
        in_specs=[pl.BlockSpec((tm, D_MODEL), lambda i, j: (i, 0)), w_spec, w_spec],
        out_specs=[o_spec, o_spec, o_spec],
        out_shape=[jax.ShapeDtypeStruct((t, D_FF), F32), jax.ShapeDtypeStruct((t, D_FF), F32),
                   jax.ShapeDtypeStruct((t, D_FF), BF16)],
        compiler_params=_cparams(("parallel", "parallel")),
    )(n2, w_gate, w_up)


def ffn_gate_up_bwd(dh, w_down, gate, up, name):
    t = dh.shape[0]
    tm = _pick(t, (1024, 512, 256, 128))

    def body(dh_ref, wd_ref, g_ref, u_ref, dg_ref, du_ref, dh_bf):
        @pl.when(pl.program_id(1) == 0)
        def _():
            dh_bf[...] = dh_ref[...].astype(BF16)

        d = lax.dot_general(dh_bf[...], wd_ref[...], _DIMS["nt"], preferred_element_type=F32)
        g = g_ref[...]
        s = _sigmoid(g)
        silu = g * s
        dg_ref[...] = (d * u_ref[...] * (s + silu * (1.0 - s))).astype(BF16)
        du_ref[...] = (d * silu).astype(BF16)

    o_spec = pl.BlockSpec((tm, FFN_TILE), lambda i, j: (i, j))
    return pl.pallas_call(
        body, name=name, grid=(t // tm, D_FF // FFN_TILE),
        in_specs=[pl.BlockSpec((tm, D_MODEL), lambda i, j: (i, 0)),
                  pl.BlockSpec((FFN_TILE, D_MODEL), lambda i, j: (j, 0)), o_spec, o_spec],
        out_specs=[o_spec, o_spec],
        out_shape=[jax.ShapeDtypeStruct((t, D_FF), BF16)] * 2,
        scratch_shapes=[pltpu.VMEM((tm, D_MODEL), BF16)],
        compiler_params=_cparams(("parallel", "arbitrary")),
    )(dh, w_down, gate, up)


def ple_fwd(h, pre, pp, name):
    def fn(x, a, b):
        return (x + _sigmoid(a) * b,), ()

    return rowwise(fn, [h, pre, pp], [], [(D_MODEL, F32)], name=name)[0]


def ple_bwd(dh, pre, pp, name):
    def fn(d, a, b):
        s = _sigmoid(a)
        return (d * b * s * (1.0 - s), d * s), ()

    return rowwise(fn, [dh, pre, pp], [], [(D_MODEL, BF16), (D_MODEL, BF16)], name=name)


def _rot_half_partner(x, first_half):
    w = x.shape[1]
    return jnp.where(first_half, pltpu.roll(x, w - HEAD_DIM // 2, 1), pltpu.roll(x, HEAD_DIM // 2, 1))


def rope_apply(xx, cosw, sinw, backward, name, width=None):
    width = width or xx.shape[1]
    reps = width // 128

    def fn(x, c, s):
        cw = jnp.tile(c, (1, reps))
        sw = jnp.tile(s, (1, reps))
        lane = lax.broadcasted_iota(jnp.int32, x.shape, 1)
        first = (lane % HEAD_DIM) < (HEAD_DIM // 2)
        if backward:
            return (x * cw + _rot_half_partner(x * sw, first),), ()
        return (x * cw + _rot_half_partner(x, first) * sw,), ()

    return rowwise(fn, [xx, cosw, sinw], [], [(width, F32)], name=name, row_widths=[width, 128, 128])[0]


def _log_sigmoid(x):
    return jnp.minimum(x, 0.0) - jnp.log(1.0 + jnp.exp(-jnp.abs(x)))


CUM_BLOCK = 256


def forget_cumsum(flog, bias, name):
    t = flog.shape[0]
    tb = _pick(t, (CUM_BLOCK,))

    def body(x_ref, b_ref, o_ref, carry):
        @pl.when(pl.program_id(0) == 0)
        def _():
            carry[...] = jnp.zeros_like(carry)

        lf = _log_sigmoid(x_ref[...] + b_ref[...])
        r = lax.broadcasted_iota(jnp.int32, (tb, tb), 0)
        c = lax.broadcasted_iota(jnp.int32, (tb, tb), 1)
        tri = (c <= r).astype(F32)
        cum = jnp.dot(tri, lf, preferred_element_type=F32, precision=lax.Precision.HIGHEST) + carry[...]
        o_ref[...] = cum
        carry[...] = cum[tb - 1:tb, :]

    return pl.pallas_call(
        body, name=name, grid=(t // tb,),
        in_specs=[pl.BlockSpec((tb, GATE_PAD), lambda i: (i, 0)), pl.BlockSpec((1, GATE_PAD), lambda i: (0, 0))],
        out_specs=pl.BlockSpec((tb, GATE_PAD), lambda i: (i, 0)),
        out_shape=jax.ShapeDtypeStruct((t, GATE_PAD), F32),
        scratch_shapes=[pltpu.VMEM((1, GATE_PAD), F32)],
        compiler_params=_cparams(("arbitrary",)),
    )(flog, bias)


def forget_cumsum_bwd(d_cum, flog, bias, name):
    t = flog.shape[0]
    tb = _pick(t, (CUM_BLOCK,))
    nb = t // tb

    def body(d_ref, x_ref, b_ref, o_ref, db_ref, carry):
        @pl.when(pl.program_id(0) == 0)
        def _():
            carry[...] = jnp.zeros_like(carry)
            db_ref[...] = jnp.zeros_like(db_ref)

        r = lax.broadcasted_iota(jnp.int32, (tb, tb), 0)
        c = lax.broadcasted_iota(jnp.int32, (tb, tb), 1)
        tri = (c >= r).astype(F32)
        dlf = jnp.dot(tri, d_ref[...], preferred_element_type=F32, precision=lax.Precision.HIGHEST) + carry[...]
        carry[...] = dlf[0:1, :]
        dx = dlf * (1.0 - _sigmoid(x_ref[...] + b_ref[...]))
        o_ref[...] = dx
        db_ref[...] += _fold8(dx)

    rev = lambda i: (nb - 1 - i, 0)
    return pl.pallas_call(
        body, name=name, grid=(nb,),
        in_specs=[pl.BlockSpec((tb, GATE_PAD), rev), pl.BlockSpec((tb, GATE_PAD), rev),
                  pl.BlockSpec((1, GATE_PAD), lambda i: (0, 0))],
        out_specs=[pl.BlockSpec((tb, GATE_PAD), rev), pl.BlockSpec((8, GATE_PAD), lambda i: (0, 0))],
        out_shape=[jax.ShapeDtypeStruct((t, GATE_PAD), F32), jax.ShapeDtypeStruct((8, GATE_PAD), F32)],
        scratch_shapes=[pltpu.VMEM((1, GATE_PAD), F32)],
        compiler_params=_cparams(("arbitrary",)),
    )(d_cum, flog, bias)


TQ = 512
TK = 128
AUG = 128
N_BIAS = 3
SB_CUTOFF = 110.0
FOX_CUTOFF = 112.0


def _dot(a, b):
    return jnp.dot(a, b, preferred_element_type=F32)


def _rel(shape, d):
    return lax.broadcasted_iota(jnp.int32, shape, 0) - lax.broadcasted_iota(jnp.int32, shape, 1) + d


def _put(x, lo, part, hi=None):
    hi = x.shape[1] if hi is None else hi
    pieces = ([x[:, :lo]] if lo else []) + [part] + ([x[:, hi:]] if hi < x.shape[1] else [])
    return part if len(pieces) == 1 else jnp.concatenate(pieces, axis=1)


def _stop_code(first, narrow):
    return first.astype(F32) + jnp.where(narrow, 0.5, 0.0)


def _read_stop_code(code_ref):
    code = jnp.max(code_ref[...])
    first = code.astype(jnp.int32)
    return first, (code - first.astype(F32)) > 0.25


def _split_bf16(x):
    hi = x.astype(BF16)
    return hi, (x - hi.astype(F32)).astype(BF16)


def _tri_dot(tri2, x):
    hi, lo = _split_bf16(x)
    return _dot(tri2, jnp.concatenate([hi, lo], axis=0))


def _q_cols(width, tq):
    return pl.BlockSpec((None, width, tq), lambda h, i: (h, 0, i))


def _q_rows(width, tq):
    return pl.BlockSpec((None, tq, width), lambda h, i: (h, i, 0))


def _kv_rows(t, width):
    return pl.BlockSpec((None, t, width), lambda h, i: (h, 0, 0))


def _kv_tiles(nk):
    return pl.BlockSpec((None, nk, HEAD_DIM, TK), lambda h, i: (h, 0, 0, 0))


def _blocks(t):
    tq = TQ if t % TQ == 0 else TK
    return tq, tq // TK, t // TK


def fox_fwd(qt_aug, k_aug, vt, f_end, k_norm):
    nh, _, t = qt_aug.shape
    tq, ratio, nk = _blocks(t)
    lanes = tq // 128

    def body(q_ref, k_ref, v_ref, fe_ref, kn_ref, o_ref, lse_ref, first_ref):
        i = pl.program_id(1)
        qv = q_ref[...]
        qf = qv[:HEAD_DIM].astype(F32)
        reach = jnp.sqrt(jnp.sum(qf * qf, axis=0, keepdims=True)) * jnp.tile(kn_ref[...], (1, lanes))

        def step(first, carry, masked, width=tq):
            m, l, acc = carry
            scores = []
            for u in range(ratio):
                lo, hi = (u * TK, tq) if masked else (0, width)
                off = pl.multiple_of((first + u) * TK, TK)
                s = _dot(k_ref[pl.ds(off, TK), :], qv[:, lo:hi])
                if masked:
                    s = jnp.where(_rel(s.shape, 0) <= 0, s, NEG_INF)
                scores.append((lo, hi, s))
            m_new = m
            for lo, hi, s in scores:
                m_new = _put(m_new, lo, jnp.maximum(m_new[:, lo:hi], jnp.max(s, axis=0, keepdims=True)), hi)
            alpha = jnp.exp(m - m_new)
            l = alpha * l
            acc = alpha * acc
            for u, (lo, hi, s) in enumerate(scores):
                p = jnp.exp(s - m_new[:, lo:hi])
                l = _put(l, lo, l[:, lo:hi] + jnp.sum(p, axis=0, keepdims=True), hi)
                acc = _put(acc, lo, acc[:, lo:hi] + _dot(v_ref[first + u], p.astype(BF16)), hi)
            return m_new, l, acc

        def slack(done, m):
            return reach + jnp.tile(fe_ref[jnp.maximum(i - 1 - done, 0)], (1, lanes)) - m

        def sweep_down(width, go, state):
            def more(c):
                return go & (c[0] < i) & (jnp.max(slack(c[0], c[1])[:, :width]) > -FOX_CUTOFF)

            def sweep(c):
                return (c[0] + 1,) + step(ratio * (i - 1 - c[0]), c[1:], False, width)

            return lax.while_loop(more, sweep, state)

        init = (jnp.full((1, tq), NEG_INF, F32), jnp.zeros((1, tq), F32), jnp.zeros((HEAD_DIM, tq), F32))
        state = (jnp.int32(0),) + step(ratio * i, init, True)
        if ratio > 1:
            narrow = jnp.max(slack(0, state[1])[:, tq // 2:]) <= -FOX_CUTOFF
            state = sweep_down(tq // 2, narrow, state)
            state = sweep_down(tq, jnp.logical_not(narrow), state)
        else:
            narrow = False
            state = sweep_down(tq, True, state)
        done, m, l, acc = state
        o_ref[...] = acc / l
        lse_ref[...] = m + jnp.log(l)
        first_ref[...] = jnp.full((1, 128), _stop_code(i - done, narrow), F32)

    return pl.pallas_call(
        body, name="fox_fwd", grid=(nh, t // tq),
        in_specs=[_q_cols(AUG, tq), _kv_rows(t, AUG), _kv_tiles(nk),
                  pl.BlockSpec((None, t // tq, 1, 128), lambda h, i: (h, 0, 0, 0)),
                  pl.BlockSpec((None, 1, 128), lambda h, i: (h, 0, 0))],
        out_specs=[_q_cols(HEAD_DIM, tq), _q_cols(1, tq), _q_cols(1, 128)],
        out_shape=[jax.ShapeDtypeStruct((nh, HEAD_DIM, t), F32), jax.ShapeDtypeStruct((nh, 1, t), F32),
                   jax.ShapeDtypeStruct((nh, 1, 128 * (t // tq)), F32)],
        compiler_params=_cparams(("parallel", "arbitrary")),
    )(qt_aug, k_aug, vt, f_end, k_norm)


def fox_bwd(qt_aug, q_aug, k_aug, kt, v, ot, do, dot_, lse, first):
    nh, _, t = qt_aug.shape
    tq, ratio, nk = _blocks(t)

    def body(qt_ref, q_ref, k_ref, kt_ref, v_ref, ot_ref, do_ref, dot_ref, lse_ref, first_ref,
             dqt_ref, dk_ref, dv_ref, rs_ref):
        i = pl.program_id(1)
        start, narrow = _read_stop_code(first_ref)

        @pl.when(i == 0)
        def _():
            dk_ref[...] = jnp.zeros_like(dk_ref)
            dv_ref[...] = jnp.zeros_like(dv_ref)

        qtv = qt_ref[...]
        qv = q_ref[...]
        dob = do_ref[...]
        dotb = dot_ref[...]
        delta = jnp.sum(ot_ref[...] * dotb.astype(F32), axis=0, keepdims=True)
        lse = lse_ref[...]

        def tile(j, carry, lo, hi, masked):
            dqt, rs = carry
            off = pl.multiple_of(j * TK, TK)
            s = _dot(k_ref[pl.ds(off, TK), :], qtv[:, lo:hi])
            p = jnp.exp(s - lse[:, lo:hi])
            if masked:
                p = jnp.where(_rel(s.shape, 0) <= 0, p, 0.0)
            dp = _dot(v_ref[pl.ds(off, TK), :], dotb[:, lo:hi])
            dsb = (p * (dp - delta[:, lo:hi])).astype(BF16)
            dk_ref[pl.ds(off, TK), :] += _dot(dsb, qv[lo:hi, :])
            dv_ref[pl.ds(off, TK), :] += _dot(p.astype(BF16), dob[lo:hi, :])
            return (_put(dqt, lo, dqt[:, lo:hi] + _dot(kt_ref[j], dsb), hi),
                    _put(rs, lo, rs[:, lo:hi] + jnp.sum(dsb.astype(F32), axis=0, keepdims=True), hi))

        def step(first, carry, masked, width=tq):
            for u in range(ratio):
                lo, hi = (u * TK, tq) if masked else (0, width)
                carry = tile(first + u, carry, lo, hi, masked)
            return carry

        carry = (jnp.zeros((HEAD_DIM, tq), F32), jnp.zeros((1, tq), F32))
        if ratio > 1:
            carry = lax.fori_loop(start, jnp.where(narrow, i, start),
                                  lambda jj, c: step(ratio * jj, c, False, tq // 2), carry)
            carry = lax.fori_loop(start, jnp.where(narrow, start, i), lambda jj, c: step(ratio * jj, c, False), carry)
        else:
            carry = lax.fori_loop(start, i, lambda jj, c: step(ratio * jj, c, False), carry)
        dqt, rs = step(ratio * i, carry, True)
        dqt_ref[...] = dqt * SCALE
        rs_ref[...] = rs

    return pl.pallas_call(
        body, name="fox_bwd", grid=(nh, t // tq),
        in_specs=[_q_cols(AUG, tq), _q_rows(AUG, tq), _kv_rows(t, AUG), _kv_tiles(nk), _kv_rows(t, HEAD_DIM),
                  _q_cols(HEAD_DIM, tq), _q_rows(HEAD_DIM, tq), _q_cols(HEAD_DIM, tq), _q_cols(1, tq),
                  _q_cols(1, 128)],
        out_specs=[_q_cols(HEAD_DIM, tq), _kv_rows(t, AUG), _kv_rows(t, HEAD_DIM), _q_cols(1, tq)],
        out_shape=[jax.ShapeDtypeStruct((nh, HEAD_DIM, t), F32), jax.ShapeDtypeStruct((nh, t, AUG), F32),
                   jax.ShapeDtypeStruct((nh, t, HEAD_DIM), F32), jax.ShapeDtypeStruct((nh, 1, t), F32)],
        compiler_params=_cparams(("arbitrary", "arbitrary")),
    )(qt_aug, q_aug, k_aug, kt, v, ot, do, dot_, lse, first)


def _sb_logits(kb, qv, ok):
    z = _dot(kb, qv)
    e = jnp.exp(-jnp.abs(z))
    ll = -(jnp.maximum(z, 0.0) + jnp.log(1.0 + e))
    if ok is not None:
        ll = jnp.where(ok, ll, 0.0)
    return z, e, ll


def _tri(cmp):
    r = lax.broadcasted_iota(jnp.int32, (TK, 2 * TK), 0)
    c = lax.broadcasted_iota(jnp.int32, (TK, 2 * TK), 1) % TK
    return cmp(r, c).astype(BF16)


def sb_fwd(qt, k, vt):
    nh, _, t = qt.shape
    tq, ratio, nk = _blocks(t)

    def body(q_ref, k_ref, v_ref, o_ref, tot_ref, first_ref):
        i = pl.program_id(1)
        qv = q_ref[...]
        tri_after = _tri(lambda r, c: c > r)

        def tile(j, carry, lo, hi, masked):
            c_l, acc = carry
            off = pl.multiple_of(j * TK, TK)
            ok = _rel((TK, hi - lo), 0) < 0 if masked else None
            z, _, ll = _sb_logits(k_ref[pl.ds(off, TK), :], qv[:, lo:hi], ok)
            a = jnp.exp(z + ll + _tri_dot(tri_after, ll) + c_l[:, lo:hi])
            if masked:
                a = jnp.where(ok, a, 0.0)
            return (_put(c_l, lo, c_l[:, lo:hi] + jnp.sum(ll, axis=0, keepdims=True), hi),
                    _put(acc, lo, acc[:, lo:hi] + _dot(v_ref[j], a.astype(BF16)), hi))

        def step(first, carry, masked, width=tq):
            for u in reversed(range(ratio)):
                lo, hi = (u * TK, tq) if masked else (0, width)
                carry = tile(first + u, carry, lo, hi, masked)
            return carry

        def sweep_down(width, go, state):
            def more(c):
                return go & (c[0] < i) & (jnp.max(c[1][:, :width]) > -SB_CUTOFF)

            def sweep(c):
                return (c[0] + 1,) + step(ratio * (i - 1 - c[0]), c[1:], False, width)

            return lax.while_loop(more, sweep, state)

        state = (jnp.int32(0),) + step(ratio * i, (jnp.zeros((1, tq), F32), jnp.zeros((HEAD_DIM, tq), F32)), True)
        if ratio > 1:
            narrow = jnp.max(state[1][:, tq // 2:]) <= -SB_CUTOFF
            state = sweep_down(tq // 2, narrow, state)
            state = sweep_down(tq, jnp.logical_not(narrow), state)
        else:
            narrow = False
            state = sweep_down(tq, True, state)
        done, c_l, acc = state
        o_ref[...] = acc
        tot_ref[...] = c_l
        first_ref[...] = jnp.full((1, 128), _stop_code(i - done, narrow), F32)

    return pl.pallas_call(
        body, name="sb_fwd", grid=(nh, t // tq),
        in_specs=[_q_cols(HEAD_DIM, tq), _kv_rows(t, HEAD_DIM), _kv_tiles(nk)],
        out_specs=[_q_cols(HEAD_DIM, tq), _q_cols(1, tq), _q_cols(1, 128)],
        out_shape=[jax.ShapeDtypeStruct((nh, HEAD_DIM, t), F32), jax.ShapeDtypeStruct((nh, 1, t), F32),
                   jax.ShapeDtypeStruct((nh, 1, 128 * (t // tq)), F32)],
        compiler_params=_cparams(("parallel", "arbitrary")),
    )(qt, k, vt)


def sb_bwd(qt, q, k, kt, v, ltot, first, do, dot_):
    nh, _, t = qt.shape
    tq, ratio, nk = _blocks(t)

    def body(qt_ref, q_ref, k_ref, kt_ref, v_ref, tot_ref, first_ref, do_ref, dot_ref, dqt_ref, dk_ref, dv_ref):
        i = pl.program_id(1)
        start, narrow = _read_stop_code(first_ref)

        @pl.when(i == 0)
        def _():
            dk_ref[...] = jnp.zeros_like(dk_ref)
            dv_ref[...] = jnp.zeros_like(dv_ref)

        qtv = qt_ref[...]
        qv = q_ref[...]
        dob = do_ref[...]
        dotb = dot_ref[...]
        tri_upto = _tri(lambda r, c: c <= r)
        tri_before = _tri(lambda r, c: c < r)

        def tile(j, carry, lo, hi, masked):
            rest, c_w, dqt = carry
            off = pl.multiple_of(j * TK, TK)
            ok = _rel((TK, hi - lo), 0) < 0 if masked else None
            z, e, ll = _sb_logits(k_ref[pl.ds(off, TK), :], qtv[:, lo:hi], ok)
            a = jnp.exp(z + ll + (rest[:, lo:hi] - _tri_dot(tri_upto, ll)))
            if masked:
                a = jnp.where(ok, a, 0.0)
            w = a * _dot(v_ref[pl.ds(off, TK), :], dotb[:, lo:hi])
            before = _tri_dot(tri_before, w) + c_w[:, lo:hi]
            r = 1.0 / (1.0 + e)
            sig = jnp.where(z >= 0.0, r, e * r)
            dz = w - (w + before) * sig
            if masked:
                dz = jnp.where(ok, dz, 0.0)
            dzb = dz.astype(BF16)
            dk_ref[pl.ds(off, TK), :] += _dot(dzb, qv[lo:hi, :])
            dv_ref[pl.ds(off, TK), :] += _dot(a.astype(BF16), dob[lo:hi, :])
            return (_put(rest, lo, rest[:, lo:hi] - jnp.sum(ll, axis=0, keepdims=True), hi),
                    _put(c_w, lo, c_w[:, lo:hi] + jnp.sum(w, axis=0, keepdims=True), hi),
                    _put(dqt, lo, dqt[:, lo:hi] + _dot(kt_ref[j], dzb), hi))

        def step(first, carry, masked, width=tq):
            for u in range(ratio):
                lo, hi = (u * TK, tq) if masked else (0, width)
                carry = tile(first + u, carry, lo, hi, masked)
            return carry

        carry = (tot_ref[...], jnp.zeros((1, tq), F32), jnp.zeros((HEAD_DIM, tq), F32))
        if ratio > 1:
            carry = lax.fori_loop(start, jnp.where(narrow, i, start),
                                  lambda jj, c: step(ratio * jj, c, False, tq // 2), carry)
            carry = lax.fori_loop(start, jnp.where(narrow, start, i), lambda jj, c: step(ratio * jj, c, False), carry)
        else:
            carry = lax.fori_loop(start, i, lambda jj, c: step(ratio * jj, c, False), carry)
        dqt_ref[...] = step(ratio * i, carry, True)[2] * SCALE

    hd = HEAD_DIM
    return pl.pallas_call(
        body, name="sb_bwd", grid=(nh, t // tq),
        in_specs=[_q_cols(hd, tq), _q_rows(hd, tq), _kv_rows(t, hd), _kv_tiles(nk), _kv_rows(t, hd),
                  _q_cols(1, tq), _q_cols(1, 128), _q_rows(hd, tq), _q_cols(hd, tq)],
        out_specs=[_q_cols(hd, tq), _kv_rows(t, hd), _kv_rows(t, hd)],
        out_shape=[jax.ShapeDtypeStruct((nh, hd, t), F32), jax.ShapeDtypeStruct((nh, t, hd), F32),
                   jax.ShapeDtypeStruct((nh, t, hd), F32)],
        compiler_params=_cparams(("arbitrary", "arbitrary")),
    )(qt, q, k, kt, v, ltot, first, do, dot_)


def _swa_q_cols(width):
    return pl.BlockSpec((GROUP, width, WINDOW), lambda g, i: (g, 0, i))


def _swa_q_rows():
    return pl.BlockSpec((GROUP, WINDOW, HEAD_DIM), lambda g, i: (g, i, 0))


def _swa_kv_rows(t):
    return pl.BlockSpec((None, t, HEAD_DIM), lambda g, i: (g, 0, 0))


def _swa_kv_tiles(nk):
    return pl.BlockSpec((None, nk, HEAD_DIM, WINDOW), lambda g, i: (g, 0, 0, 0))


def _swa_sink_spec():
    return pl.BlockSpec((GROUP, 1, 128), lambda g, i: (g, 0, 0))


def _lane_cat(parts):
    return jnp.concatenate(parts, axis=1)


def _swa_window(i):
    jb = jnp.maximum(i - 1, 0)
    start = pl.multiple_of(jb * WINDOW, WINDOW)
    shape = (2 * WINDOW, GROUP * WINDOW)
    query = lax.broadcasted_iota(jnp.int32, shape, 1) % WINDOW
    rel = lax.broadcasted_iota(jnp.int32, shape, 0) - query + (start - i * WINDOW)
    return jb, start, (rel <= 0) & (rel > -WINDOW)


def swa_fwd(qt, k, vt, sinks):
    nh, _, t = qt.shape
    nk = t // WINDOW

    def body(q_ref, k_ref, v_ref, s_ref, o_ref, lse_ref):
        i = pl.program_id(1)
        jb, start, valid = _swa_window(i)
        qv = _lane_cat([q_ref[g] for g in range(GROUP)])
        sink = _lane_cat([s_ref[g] for g in range(GROUP)])
        s = jnp.where(valid, _dot(k_ref[pl.ds(start, 2 * WINDOW), :], qv), NEG_INF)
        m = jnp.maximum(jnp.max(s, axis=0, keepdims=True), sink)
        p = jnp.where(valid, jnp.exp(s - m), 0.0)
        l = jnp.sum(p, axis=0, keepdims=True) + jnp.exp(sink - m)
        pb = p.astype(BF16)
        o = (_dot(v_ref[jb], pb[:WINDOW]) + _dot(v_ref[jb + 1], pb[WINDOW:])) / l
        lse = m + jnp.log(l)
        for g in range(GROUP):
            o_ref[g] = o[:, g * WINDOW:(g + 1) * WINDOW]
            lse_ref[g] = lse[:, g * WINDOW:(g + 1) * WINDOW]

    return pl.pallas_call(
        body, name="swa_fwd", grid=(N_KV, nk),
        in_specs=[_swa_q_cols(HEAD_DIM), _swa_kv_rows(t), _swa_kv_tiles(nk), _swa_sink_spec()],
        out_specs=[_swa_q_cols(HEAD_DIM), _swa_q_cols(1)],
        out_shape=[jax.ShapeDtypeStruct((nh, HEAD_DIM, t), F32), jax.ShapeDtypeStruct((nh, 1, t), F32)],
        compiler_params=_cparams(("parallel", "arbitrary")),
    )(qt, k, vt, sinks)


def swa_bwd(qt, q, k, kt, v, sinks, ot, do, dot_, lse):
    nh, _, t = qt.shape
    nk = t // WINDOW

    def body(qt_ref, q_ref, k_ref, kt_ref, v_ref, s_ref, ot_ref, do_ref, dot_ref, lse_ref,
             dqt_ref, dk_ref, dv_ref, dsink_ref):
        i = pl.program_id(1)

        @pl.when(i == 0)
        def _():
            dk_ref[...] = jnp.zeros_like(dk_ref)
            dv_ref[...] = jnp.zeros_like(dv_ref)
            dsink_ref[...] = jnp.zeros_like(dsink_ref)

        jb, start, valid = _swa_window(i)
        heads = range(GROUP)
        qtv = _lane_cat([qt_ref[g] for g in heads])
        dotb = _lane_cat([dot_ref[g] for g in heads])
        lse = _lane_cat([lse_ref[g] for g in heads])
        sink = _lane_cat([s_ref[g] for g in heads])
        otv = _lane_cat([ot_ref[g] for g in heads])
        q_rows = jnp.concatenate([q_ref[g] for g in heads], axis=0)
        do_rows = jnp.concatenate([do_ref[g] for g in heads], axis=0)
        delta = jnp.sum(otv * dotb.astype(F32), axis=0, keepdims=True)
        p = jnp.where(valid, jnp.exp(_dot(k_ref[pl.ds(start, 2 * WINDOW), :], qtv) - lse), 0.0)
        dsb = (p * (_dot(v_ref[pl.ds(start, 2 * WINDOW), :], dotb) - delta)).astype(BF16)
        dqt = (_dot(kt_ref[jb], dsb[:WINDOW]) + _dot(kt_ref[jb + 1], dsb[WINDOW:])) * SCALE
        dsink = -jnp.exp(sink - lse) * delta
        for g in heads:
            dqt_ref[g] = dqt[:, g * WINDOW:(g + 1) * WINDOW]
            dsink_ref[g] += dsink[:, g * WINDOW:(g + 1) * WINDOW]
        dk_ref[pl.ds(start, 2 * WINDOW), :] += _dot(dsb, q_rows)
        dv_ref[pl.ds(start, 2 * WINDOW), :] += _dot(p.astype(BF16), do_rows)

    hd = HEAD_DIM
    return pl.pallas_call(
        body, name="swa_bwd", grid=(N_KV, nk),
        in_specs=[_swa_q_cols(hd), _swa_q_rows(), _swa_kv_rows(t), _swa_kv_tiles(nk), _swa_kv_rows(t),
                  _swa_sink_spec(), _swa_q_cols(hd), _swa_q_rows(), _swa_q_cols(hd), _swa_q_cols(1)],
        out_specs=[_swa_q_cols(hd), _swa_kv_rows(t), _swa_kv_rows(t), _swa_sink_spec()],
        out_shape=[jax.ShapeDtypeStruct((nh, hd, t), F32), jax.ShapeDtypeStruct((N_KV, t, hd), F32),
                   jax.ShapeDtypeStruct((N_KV, t, hd), F32), jax.ShapeDtypeStruct((nh, 1, 128), F32)],
        compiler_params=_cparams(("arbitrary", "arbitrary")),
    )(qt, q, k, kt, v, sinks, ot, do, dot_, lse)


def head_rows(a, nh):
    t = a.shape[0]
    return a.reshape(t, nh, HEAD_DIM).transpose(1, 0, 2)


def head_cols(a, nh):
    t = a.shape[0]
    return a.reshape(t, nh, HEAD_DIM).transpose(1, 2, 0)


def head_tiles(a, nh):
    t = a.shape[0]
    return a.reshape(t // TK, TK, nh, HEAD_DIM).transpose(2, 0, 3, 1)


def rows_to_flat(a):
    nh, t, _ = a.shape
    return a.transpose(1, 0, 2).reshape(t, nh * HEAD_DIM)


def cols_to_flat(a):
    nh, _, t = a.shape
    return a.transpose(2, 0, 1).reshape(t, nh * HEAD_DIM)


def fox_keys(proj, cum, name):
    t = proj.shape[0]
    tr = _pick(t, (256, 128))
    width = (1 + N_BIAS) * 128

    def body(k_ref, c_ref, o_ref):
        kb = k_ref[...].astype(BF16)
        terms, rest = [], -c_ref[...]
        for _ in range(N_BIAS):
            term = rest.astype(BF16)
            terms.append(term)
            rest = rest - term.astype(F32)
        row = lax.broadcasted_iota(jnp.int32, (width, AUG), 0)
        col = lax.broadcasted_iota(jnp.int32, (width, AUG), 1)
        for h in range(N_FOX):
            pair = kb[:, 128 * (h // 2):128 * (h // 2 + 1)]
            place = (row < 128) & (col < HEAD_DIM) & (row - HEAD_DIM * (h % 2) == col)
            for b in range(N_BIAS):
                place = place | ((row == 128 * (1 + b) + h) & (col == HEAD_DIM + b))
            src = jnp.concatenate([pair] + terms, axis=1)
            o_ref[h] = _dot(src, place.astype(BF16)).astype(BF16)

    return pl.pallas_call(
        body, name=name, grid=(t // tr,),
        in_specs=[pl.BlockSpec((tr, FOX_W), lambda i: (i, 1)), pl.BlockSpec((tr, GATE_PAD), lambda i: (i, 0))],
        out_specs=pl.BlockSpec((N_FOX, tr, AUG), lambda i: (0, i, 0)),
        out_shape=jax.ShapeDtypeStruct((N_FOX, t, AUG), BF16),
        compiler_params=_cparams(("parallel",)),
    )(proj, cum)


def fox_operands(qf, kf, cum_heads):
    t = qf.shape[0]
    nh = cum_heads.shape[0]
    tq = _blocks(t)[0]
    qs = (qf * SCALE).astype(BF16)
    ones_t = jnp.ones((nh, N_BIAS, t), BF16)
    qt_aug = jnp.concatenate([head_cols(qs, nh), ones_t, jnp.zeros((nh, AUG - HEAD_DIM - N_BIAS, t), BF16)], axis=1)
    own_lane = jnp.broadcast_to(jnp.eye(nh, dtype=BF16)[:, None, :], (nh, t, nh))
    q_aug = jnp.concatenate([head_rows(qs, nh), own_lane, jnp.zeros((nh, t, AUG - HEAD_DIM - nh), BF16)], axis=2)
    f_end = jnp.broadcast_to((-cum_heads)[:, tq - 1::tq, None, None], (nh, t // tq, 1, 128))
    k_sq = jnp.sum(jnp.square(kf.astype(F32)).reshape(t, nh, HEAD_DIM), axis=2)
    k_norm = jnp.broadcast_to(1.01 * jnp.sqrt(jnp.max(k_sq, axis=0))[:, None, None], (nh, 1, 128))
    return qt_aug, q_aug, f_end, k_norm


def adamw(w, g, m, v, name):
    cols = w.shape[1]

    def fn(ww, gg, mm, vv):
        mn = ADAM_B1 * mm + (1.0 - ADAM_B1) * gg
        vn = ADAM_B2 * vv + (1.0 - ADAM_B2) * (gg * gg)
        m_hat = mn / (1.0 - ADAM_B1 ** ADAM_STEP)
        v_hat = vn / (1.0 - ADAM_B2 ** ADAM_STEP)
        delta = -ADAM_LR * (m_hat / (jnp.sqrt(v_hat) + ADAM_EPS) + ADAM_WD * ww)
        return (delta, mn, vn), ()

    return rowwise(fn, [w, g, m, v], [], [(cols, F32)] * 3, name=name)


ANY = pl.BlockSpec(memory_space=pl.ANY)


def _my_place():
    return lax.axis_index("x"), lax.axis_index("y"), lax.axis_index("c")


def _flip(coord, bit):
    return 1 - coord if bit else coord


def allgather_chips(w):
    r, c = w.shape
    rh = r // 2

    def body(w_ref, out_ref, send_sems, recv_sems, local_sem):
        x, y, cc = _my_place()
        me = 2 * x + y
        sibling = (x, y, 1 - cc)
        chips = [(_flip(x, kbits >> 1), _flip(y, kbits & 1)) for kbits in (1, 2, 3)]

        def half(chip, hc):
            return out_ref.at[chip, pl.ds(pl.multiple_of(hc * rh, 16), rh)]

        def copy(k, src, dst, to):
            return pltpu.make_async_remote_copy(src_ref=src, dst_ref=dst, send_sem=send_sems.at[k],
                                                recv_sem=recv_sems.at[k], device_id=to, device_id_type=MESH)

        local = pltpu.make_async_copy(w_ref, out_ref.at[me], local_sem)
        local.start()
        my_half = w_ref.at[pl.ds(pl.multiple_of(cc * rh, 16), rh)]
        first = [copy(j, my_half, half(me, cc), (px, py, cc)) for j, (px, py) in enumerate(chips)]
        for cp in first:
            cp.start()
        passed = []
        for j, (px, py) in enumerate(chips):
            landed = half(2 * px + py, cc)
            copy(j, my_half, landed, (px, py, cc)).wait_recv()
            fwd = copy(3 + j, landed, landed, sibling)
            fwd.start()
            passed.append(fwd)
        for j, (px, py) in enumerate(chips):
            theirs = half(2 * px + py, 1 - cc)
            copy(3 + j, theirs, theirs, sibling).wait_recv()
        for cp in first + passed:
            cp.wait_send()
        local.wait()

    return pl.pallas_call(
        body, name="allgather_chips", in_specs=[ANY], out_specs=ANY,
        out_shape=jax.ShapeDtypeStruct((N_CHIPS, r, c), w.dtype),
        scratch_shapes=[pltpu.SemaphoreType.DMA((6,)), pltpu.SemaphoreType.DMA((6,)), pltpu.SemaphoreType.DMA],
    )(w)


def pair_send_other_half(g):
    n, _, rh, c = g.shape

    def body(g_ref, out_ref, send_sem, recv_sem):
        x, y, cc = _my_place()
        cp = pltpu.make_async_remote_copy(
            src_ref=g_ref.at[:, 1 - cc], dst_ref=out_ref, send_sem=send_sem, recv_sem=recv_sem,
            device_id=(x, y, 1 - cc), device_id_type=MESH)
        cp.start()
        cp.wait()

    return pl.pallas_call(
        body, name="pair_send_other_half", in_specs=[ANY], out_specs=ANY,
        out_shape=jax.ShapeDtypeStruct((n, rh, c), g.dtype),
        scratch_shapes=[pltpu.SemaphoreType.DMA, pltpu.SemaphoreType.DMA],
    )(g)


def exchange_chips(s):
    n, rh, c = s.shape

    def body(s_ref, out_ref, send_sems, recv_sems, local_sem):
        x, y, cc = _my_place()
        me = 2 * x + y
        local = pltpu.make_async_copy(s_ref.at[me], out_ref.at[me], local_sem)
        local.start()
        copies = []
        for kbits in (1, 2, 3):
            px, py = _flip(x, kbits >> 1), _flip(y, kbits & 1)
            cp = pltpu.make_async_remote_copy(
                src_ref=s_ref.at[2 * px + py], dst_ref=out_ref.at[me], send_sem=send_sems.at[kbits - 1],
                recv_sem=recv_sems.at[kbits - 1], device_id=(px, py, cc), device_id_type=MESH)
            cp.start()
            copies.append(cp)
        for cp in copies:
            cp.wait()
        local.wait()

    return pl.pallas_call(
        body, name="exchange_chips", in_specs=[ANY], out_specs=ANY,
        out_shape=jax.ShapeDtypeStruct((n, rh, c), s.dtype),
        scratch_shapes=[pltpu.SemaphoreType.DMA((3,)), pltpu.SemaphoreType.DMA((3,)), pltpu.SemaphoreType.DMA],
    )(s)


def pair_swap(tt):
    def body(t_ref, out_ref, send_sem, recv_sem):
        x, y, cc = _my_place()
        cp = pltpu.make_async_remote_copy(
            src_ref=t_ref, dst_ref=out_ref, send_sem=send_sem, recv_sem=recv_sem,
            device_id=(x, y, 1 - cc), device_id_type=MESH)
        cp.start()
        cp.wait()

    return pl.pallas_call(
        body, name="pair_swap", in_specs=[ANY], out_specs=ANY,
        out_shape=jax.ShapeDtypeStruct(tt.shape, tt.dtype),
        scratch_shapes=[pltpu.SemaphoreType.DMA, pltpu.SemaphoreType.DMA],
    )(tt)


SMALL_ROWS = 16


def allreduce_small(v):
    r, c = v.shape
    vm = pl.BlockSpec(memory_space=pltpu.VMEM)

    def body(v_ref, out_ref, slots, send_sems, recv_sems):
        x, y, cc = _my_place()
        me = 4 * x + 2 * y + cc
        slots[me] = v_ref[...]
        copies = []
        for kbits in range(1, 8):
            peer = (_flip(x, kbits >> 2), _flip(y, (kbits >> 1) & 1), _flip(cc, kbits & 1))
            cp = pltpu.make_async_remote_copy(
                src_ref=v_ref, dst_ref=slots.at[me], send_sem=send_sems.at[kbits - 1],
                recv_sem=recv_sems.at[kbits - 1], device_id=peer, device_id_type=MESH)
            cp.start()
            copies.append(cp)
        for cp in copies:
            cp.wait()
        total = slots[0]
        for dev in range(1, 8):
            total = total + slots[dev]
        out_ref[...] = total

    return pl.pallas_call(
        body, name="allreduce_small", in_specs=[vm], out_specs=vm,
        out_shape=jax.ShapeDtypeStruct((r, c), F32),
        scratch_shapes=[pltpu.VMEM((8, r, c), F32), pltpu.SemaphoreType.DMA((7,)), pltpu.SemaphoreType.DMA((7,))],
    )(v)


def add_pair(mine, theirs, name):
    return rowwise(lambda a, b: ((a + b,), ()), [mine, theirs], [], [(mine.shape[1], BF16)], name=name)[0]


def sum_chips(r4, name):
    _, rh, c = r4.shape
    tr = _pick(rh, (256, 128, 64, 32, 16))

    def body(r_ref, o_ref):
        total = r_ref[0].astype(F32)
        for j in range(1, N_CHIPS):
            total = total + r_ref[j].astype(F32)
        o_ref[...] = total

    return pl.pallas_call(
        body, name=name, grid=(rh // tr,),
        in_specs=[pl.BlockSpec((N_CHIPS, tr, c), lambda i: (0, i, 0))],
        out_specs=pl.BlockSpec((tr, c), lambda i: (i, 0)),
        out_shape=jax.ShapeDtypeStruct((rh, c), F32),
        compiler_params=_cparams(("parallel",)),
    )(r4)


def _pad_rows(a, rows):
    return jnp.pad(a, ((0, rows - a.shape[0]), (0, 0))) if rows != a.shape[0] else a


def pack_shards(local):
    parts = []
    for name, layer, r, c, by_cols in _PACK:
        w = local[name][layer].astype(BF16)
        flat = w.T if _transposed(r, by_cols) else w.reshape(r * c // D_MODEL, D_MODEL)
        parts.append(_pad_rows(flat, _pack_rows(r, c)))
    used = sum(p.shape[0] for p in parts)
    parts.append(jnp.zeros((PACK_ROWS - used, D_MODEL), BF16))
    return jnp.concatenate(parts, axis=0)


def unpack_full(gathered):
    full, off = {}, 0
    for name, layer, r, c, by_cols in _PACK:
        n = r * c // D_MODEL
        if _transposed(r, by_cols):
            full[(name, layer)] = gathered[:, off:off + c, :].reshape(N_CHIPS * c, D_MODEL)
            off += _pack_rows(r, c)
            continue
        blk = gathered[:, off:off + n, :].reshape(N_CHIPS, r, c)
        if by_cols:
            full[(name, layer)] = blk.transpose(1, 0, 2).reshape(r, N_CHIPS * c)
        else:
            full[(name, layer)] = blk.reshape(N_CHIPS * r, c)
        off += _pack_rows(r, c)
    return full


def pack_grads(grads):
    parts = []
    for name, layer, r, c, by_cols in _PACK:
        g = grads[(name, layer)]
        if _transposed(r, by_cols):
            blk = g.reshape(N_CHIPS, c, r)
        elif by_cols:
            blk = g.reshape(r, N_CHIPS, c).transpose(1, 0, 2)
        else:
            blk = g.reshape(N_CHIPS, r, c)
        flat = blk.reshape(N_CHIPS, r * c // D_MODEL, D_MODEL)
        rows = _pack_rows(r, c)
        if rows != flat.shape[1]:
            flat = jnp.pad(flat, ((0, 0), (0, rows - flat.shape[1]), (0, 0)))
        parts.append(flat)
    used = sum(p.shape[1] for p in parts)
    parts.append(jnp.zeros((N_CHIPS, PACK_ROWS - used, D_MODEL), F32))
    return jnp.concatenate(parts, axis=1)


def unpack_local(flat):
    out, off = {}, 0
    for name, layer, r, c, by_cols in _PACK:
        n = r * c // D_MODEL
        rows = flat[off:off + n, :]
        out[(name, layer)] = rows.T if _transposed(r, by_cols) else rows.reshape(r, c)
        off += _pack_rows(r, c)
    return out


def rope_tables(pos):
    half = HEAD_DIM // 2
    lane = jnp.arange(128)
    inv = ROPE_THETA ** (-(lane % half).astype(F32) / half)
    ang = pos.astype(F32)[:, None] * inv[None, :]
    sign = jnp.where((lane % HEAD_DIM) < half, -1.0, 1.0).astype(F32)
    return jnp.cos(ang), jnp.sin(ang) * sign[None, :]


def local_step(x, p, pos, tgt, norm_mix, norm_ffn, norm_ple, norm_final, ev_b_f, od_sinks, wfull):
    t = x.shape[0]
    w_in0 = wfull[("ev_w_in", 0)]
    w_in0 = jnp.concatenate([w_in0, jnp.zeros((EVEN_IN_PAD - w_in0.shape[0], D_MODEL), w_in0.dtype)], axis=0)
    b_pad = jnp.zeros((1, GATE_PAD), F32).at[0, :N_FOX].set(ev_b_f[0])
    sinks_b = jnp.broadcast_to(od_sinks[0][:, None, None], (N_Q, 1, 128)).astype(F32)
    cosw, sinw = rope_tables(pos)

    saved = []
    h = x
    for i in range(2):
        s = {"h0": h}
        n1 = rmsnorm_fwd(h, norm_mix[i], f"norm_mix_fwd{i}")
        s["n1"] = n1
        if i == 0:
            proj = matmul(n1, w_in0[:EVEN_QKV], "nt", out_dtype=BF16, name="ev_in")
            flog = matmul(n1, w_in0[EVEN_QKV:], "nt", name="ev_gate")
            cols = [proj[:, j * FOX_W:(j + 1) * FOX_W] for j in range(6)]
            cum = forget_cumsum(flog, b_pad, "forget_cumsum")
            fox = dict(zip(("qt", "q", "f_end", "k_norm"), fox_operands(cols[0], cols[1], cum[:, :N_FOX].T)))
            fox["k"] = fox_keys(proj, cum, "fox_keys")
            fox.update(kt=head_tiles(cols[1].astype(BF16), N_FOX), v=head_rows(cols[2].astype(BF16), N_FOX),
                       vt=head_tiles(cols[2].astype(BF16), N_FOX))
            q_sb = (cols[3] * SCALE).astype(BF16)
            sb = dict(qt=head_cols(q_sb, N_SB), q=head_rows(q_sb, N_SB), k=head_rows(cols[4].astype(BF16), N_SB),
                      kt=head_tiles(cols[4].astype(BF16), N_SB), v=head_rows(cols[5].astype(BF16), N_SB),
                      vt=head_tiles(cols[5].astype(BF16), N_SB))
            fox["ot"], fox["lse"], fox["first"] = fox_fwd(fox["qt"], fox["k"], fox["vt"], fox["f_end"], fox["k_norm"])
            sb["ot"], sb["ltot"], sb["first"] = sb_fwd(sb["qt"], sb["k"], sb["vt"])
            s.update(flog=flog, fox=fox, sb=sb)
            mixin_t = jnp.concatenate([fox["ot"].reshape(FOX_W, t), sb["ot"].reshape(SB_W, t)], axis=0).astype(BF16)
            w_out = wfull[("ev_w_out", 0)]
        else:
            proj = matmul(n1, wfull[("od_w_in", 0)], "nt", name="od_in")
            qk = rope_apply(proj, cosw, sinw, False, "rope_fwd", width=Q_W + KV_W)
            q_sc = (qk[:, :Q_W] * SCALE).astype(BF16)
            k_b = qk[:, Q_W:].astype(BF16)
            v_b = proj[:, Q_W + KV_W:].astype(BF16)
            swa = dict(qt=head_cols(q_sc, N_Q), q=head_rows(q_sc, N_Q), k=head_rows(k_b, N_KV),
                       kt=head_tiles(k_b, N_KV), v=head_rows(v_b, N_KV), vt=head_tiles(v_b, N_KV))
            swa["ot"], swa["lse"] = swa_fwd(swa["qt"], swa["k"], swa["vt"], sinks_b)
            s["swa"] = swa
            mixin_t = swa["ot"].reshape(Q_W, t).astype(BF16)
            w_out = wfull[("od_w_out", 0)]
        s["mixin_t"] = mixin_t
        h = matmul(mixin_t, w_out, "tn", residual=h, name=f"mix_out{i}")
        s["h1"] = h
        n2 = rmsnorm_fwd(h, norm_ffn[i], f"norm_ffn_fwd{i}")
        gate, up, act = ffn_gate_up(n2, wfull[("ffn_w_gate", i)], wfull[("ffn_w_up", i)], f"ffn_gate_up{i}")
        s.update(n2=n2, gate=gate, up=up, act=act)
        h = matmul(act, wfull[("ffn_w_down", i)], residual=h, name=f"ffn_down{i}")
        s["h2"] = h
        n3 = rmsnorm_fwd(h, norm_ple[i], f"norm_ple_fwd{i}")
        pre = matmul(n3, wfull[("ple_w_gate", i)], name=f"ple_gate{i}")
        pp = matmul(p[i], wfull[("ple_w_proj", i)], name=f"ple_proj{i}")
        s.update(n3=n3, pre=pre, pp=pp)
        h = ple_fwd(h, pre, pp, f"ple_fwd{i}")
        saved.append(s)

    dh, dg_final, loss8 = loss_head(h, norm_final, tgt, "loss_head")
    gw = {}
    small = {"norm_final": dg_final, "loss": loss8}
    for i in (1, 0):
        s = saved[i]
        dpre, dpp = ple_bwd(dh, s["pre"], s["pp"], f"ple_bwd{i}")
        gw[("ple_w_gate", i)] = matmul(s["n3"], dpre, "tn", name=f"d_ple_gate{i}")
        gw[("ple_w_proj", i)] = matmul(p[i], dpp, "tn", name=f"d_ple_proj{i}")
        dn3 = matmul(dpre, wfull[("ple_w_gate", i)], "nt", name=f"dn_ple{i}")
        dh, small[("norm_ple", i)] = rmsnorm_bwd(s["h2"], norm_ple[i], dn3, dh, f"norm_ple_bwd{i}")

        dgate, dup = ffn_gate_up_bwd(dh, wfull[("ffn_w_down", i)], s["gate"], s["up"], f"ffn_gate_up_bwd{i}")
        gw[("ffn_w_down", i)] = matmul(s["act"], dh, "tn", name=f"d_ffn_down{i}")
        gw[("ffn_w_gate", i)] = matmul(dgate, s["n2"], "tn", name=f"d_ffn_gate{i}")
        gw[("ffn_w_up", i)] = matmul(dup, s["n2"], "tn", name=f"d_ffn_up{i}")
        dn2 = matmul(dgate, wfull[("ffn_w_gate", i)], name=f"dn_ffn_gate{i}")
        dn2 = matmul(dup, wfull[("ffn_w_up", i)], residual=dn2, name=f"dn_ffn_up{i}")
        dh, small[("norm_ffn", i)] = rmsnorm_bwd(s["h1"], norm_ffn[i], dn2, dh, f"norm_ffn_bwd{i}")

        if i == 0:
            dmb = matmul(dh, wfull[("ev_w_out", 0)], "nt", out_dtype=BF16, name="d_mix0")
            gw[("ev_w_out", 0)] = matmul(s["mixin_t"], dh, name="d_ev_out")
            fox, sb = s["fox"], s["sb"]
            dqt_f, dk_aug, dv_f, ds_rows = fox_bwd(
                fox["qt"], fox["q"], fox["k"], fox["kt"], fox["v"], fox["ot"],
                head_rows(dmb[:, :FOX_W], N_FOX), head_cols(dmb[:, :FOX_W], N_FOX), fox["lse"], fox["first"])
            dqt_s, dk_s, dv_s = sb_bwd(sb["qt"], sb["q"], sb["k"], sb["kt"], sb["v"], sb["ltot"], sb["first"],
                                       head_rows(dmb[:, FOX_W:], N_SB), head_cols(dmb[:, FOX_W:], N_SB))
            ds_cols = jnp.sum(dk_aug[:, :, HEAD_DIM:HEAD_DIM + N_FOX], axis=0)
            d_cum = jnp.pad(ds_rows[:, 0, :].T - ds_cols, ((0, 0), (0, GATE_PAD - N_FOX)))
            dflog, db8 = forget_cumsum_bwd(d_cum, s["flog"], b_pad, "forget_cumsum_bwd")
            small["ev_b_f"] = db8
            parts = (cols_to_flat(dqt_f), rows_to_flat(dk_aug[:, :, :HEAD_DIM]), rows_to_flat(dv_f),
                     cols_to_flat(dqt_s), rows_to_flat(dk_s), rows_to_flat(dv_s), dflog)
            dproj = jnp.concatenate([a.astype(BF16) for a in parts], axis=1)
            dw = matmul(dproj, s["n1"], "tn", name="d_ev_in")
            gw[("ev_w_in", 0)] = dw[:EVEN_QKV + N_FOX]
            dn1 = matmul(dproj, w_in0, name="dn_mix0")
        else:
            dmb = matmul(dh, wfull[("od_w_out", 0)], "nt", out_dtype=BF16, name="d_mix1")
            gw[("od_w_out", 0)] = matmul(s["mixin_t"], dh, name="d_od_out")
            swa = s["swa"]
            dqt, dk, dv, dsink = swa_bwd(swa["qt"], swa["q"], swa["k"], swa["kt"], swa["v"], sinks_b, swa["ot"],
                                         head_rows(dmb, N_Q), head_cols(dmb, N_Q), swa["lse"])
            small["od_sinks"] = dsink
            dq_flat = rope_apply(cols_to_flat(dqt), cosw, sinw, True, "rope_bwd_q")
            dk_flat = rope_apply(rows_to_flat(dk), cosw, sinw, True, "rope_bwd_k")
            dproj = jnp.concatenate([dq_flat, dk_flat, rows_to_flat(dv)], axis=1).astype(BF16)
            gw[("od_w_in", 0)] = matmul(dproj, s["n1"], "tn", name="d_od_in")
            dn1 = matmul(dproj, wfull[("od_w_in", 0)], name="dn_mix1")
        dh, small[("norm_mix", i)] = rmsnorm_bwd(s["h0"], norm_mix[i], dn1, dh, f"norm_mix_bwd{i}")
    return dh, gw, small


_SMALL_ROWS = (("norm_mix", 0), ("norm_mix", 1), ("norm_ffn", 0), ("norm_ffn", 1),
               ("norm_ple", 0), ("norm_ple", 1), "norm_final", "misc", "loss")


def pack_small(small):
    rows = []
    for key in _SMALL_ROWS:
        if key == "misc":
            db = jnp.sum(small["ev_b_f"], axis=0)[:N_FOX]
            dsink = jnp.sum(small["od_sinks"][:, 0, :], axis=1)
            rows.append(jnp.zeros((D_MODEL,), F32).at[:N_FOX].set(db).at[128:128 + N_Q].set(dsink))
        else:
            rows.append(jnp.sum(small[key], axis=0))
    rows += [jnp.zeros((D_MODEL,), F32)] * (SMALL_ROWS - len(rows))
    return jnp.stack(rows)


def kernel(x, p, positions, norm_mix, norm_ffn, norm_ple, norm_final, ev_w_in, ev_b_f, ev_w_out, od_w_in, od_sinks, od_w_out, ffn_w_gate, ffn_w_up, ffn_w_down, ple_w_proj, ple_w_gate, loss_target, m_norm_mix, m_norm_ffn, m_norm_ple, m_norm_final, m_ev_w_in, m_ev_b_f, m_ev_w_out, m_od_w_in, m_od_sinks, m_od_w_out, m_ffn_w_gate, m_ffn_w_up, m_ffn_w_down, m_ple_w_proj, m_ple_w_gate, v_norm_mix, v_norm_ffn, v_norm_ple, v_norm_final, v_ev_w_in, v_ev_b_f, v_ev_w_out, v_od_w_in, v_od_sinks, v_od_w_out, v_ffn_w_gate, v_ffn_w_up, v_ffn_w_down, v_ple_w_proj, v_ple_w_gate):
    local_w = dict(ev_w_in=ev_w_in, ev_w_out=ev_w_out, od_w_in=od_w_in, od_w_out=od_w_out,
                   ffn_w_gate=ffn_w_gate, ffn_w_up=ffn_w_up, ffn_w_down=ffn_w_down,
                   ple_w_proj=ple_w_proj, ple_w_gate=ple_w_gate)
    local_m = dict(ev_w_in=m_ev_w_in, ev_w_out=m_ev_w_out, od_w_in=m_od_w_in, od_w_out=m_od_w_out,
                   ffn_w_gate=m_ffn_w_gate, ffn_w_up=m_ffn_w_up, ffn_w_down=m_ffn_w_down,
                   ple_w_proj=m_ple_w_proj, ple_w_gate=m_ple_w_gate)
    local_v = dict(ev_w_in=v_ev_w_in, ev_w_out=v_ev_w_out, od_w_in=v_od_w_in, od_w_out=v_od_w_out,
                   ffn_w_gate=v_ffn_w_gate, ffn_w_up=v_ffn_w_up, ffn_w_down=v_ffn_w_down,
                   ple_w_proj=v_ple_w_proj, ple_w_gate=v_ple_w_gate)

    wfull = unpack_full(allgather_chips(pack_shards(local_w)))
    grad_x, gw, small = local_step(x[0], p[:, 0], positions[0], loss_target[0], norm_mix, norm_ffn, norm_ple,
                                   norm_final, ev_b_f, od_sinks, wfull)

    cc = lax.axis_index("c")
    g4 = pack_grads(gw).reshape(N_CHIPS, 2, PACK_ROWS_HALF, D_MODEL)
    theirs = pair_send_other_half(g4)
    mine = lax.dynamic_index_in_dim(g4, cc, axis=1, keepdims=False)
    pair_sum = add_pair(mine.reshape(N_CHIPS * PACK_ROWS_HALF, D_MODEL),
                        theirs.reshape(N_CHIPS * PACK_ROWS_HALF, D_MODEL), "add_pair")
    from_chips = exchange_chips(pair_sum.reshape(N_CHIPS, PACK_ROWS_HALF, D_MODEL))
    half_sum = sum_chips(from_chips, "sum_chips")
    other_half = pair_swap(half_sum)
    low = jnp.where(cc == 0, half_sum, other_half)
    high = jnp.where(cc == 0, other_half, half_sum)
    g_local = unpack_local(jnp.concatenate([low, high], axis=0))

    red = allreduce_small(pack_small(small))
    loss = 0.5 * jnp.sum(red[8]) / D_MODEL
    pad_small = lambda a: jnp.zeros((D_MODEL,), F32).at[:N_FOX].set(a[0][0]).at[128:128 + N_Q].set(a[1][0])
    stack_small = lambda a: jnp.concatenate(
        [a[0], a[1], a[2], a[3][None], pad_small(a[4:6])[None], jnp.zeros((SMALL_ROWS - 8, D_MODEL), F32)], axis=0)
    w_small = stack_small((norm_mix, norm_ffn, norm_ple, norm_final, ev_b_f, od_sinks))
    m_small = stack_small((m_norm_mix, m_norm_ffn, m_norm_ple, m_norm_final, m_ev_b_f, m_od_sinks))
    v_small = stack_small((v_norm_mix, v_norm_ffn, v_norm_ple, v_norm_final, v_ev_b_f, v_od_sinks))
    g_small = red.at[8].set(0.0)
    upd_small = (g_small,) + tuple(adamw(w_small, g_small, m_small, v_small, "adamw_small"))

    def split_small(a):
        return (a[0:2], a[2:4], a[4:6], a[6], a[7, :N_FOX][None], a[7, 128:128 + N_Q][None])

    small_out = [split_small(a) for a in upd_small]

    big_names = ("ev_w_in", "ev_w_out", "od_w_in", "od_w_out", "ffn_w_gate", "ffn_w_up", "ffn_w_down",
                 "ple_w_proj", "ple_w_gate")
    big_out = {}
    for name in big_names:
        w = local_w[name]
        layers, r, c = w.shape
        g = jnp.concatenate([g_local[(name, i)] for i in range(layers)], axis=0)
        res = adamw(w.reshape(layers * r, c), g, local_m[name].reshape(layers * r, c),
                    local_v[name].reshape(layers * r, c), f"adamw_{name}")
        big_out[name] = [a.reshape(layers, r, c) for a in (g,) + tuple(res)]

    outs = [loss, grad_x[None]]
    for kind in range(4):
        sm = small_out[kind]
        outs += [sm[0], sm[1], sm[2], sm[3],
                 big_out["ev_w_in"][kind], sm[4], big_out["ev_w_out"][kind],
                 big_out["od_w_in"][kind], sm[5], big_out["od_w_out"][kind],
                 big_out["ffn_w_gate"][kind], big_out["ffn_w_up"][kind], big_out["ffn_w_down"][kind],
                 big_out["ple_w_proj"][kind], big_out["ple_w_gate"][kind]]
    return tuple(outs)
```

```python
import jax
import jax.numpy as jnp
from jax import lax
from jax.experimental import pallas as pl
from jax.experimental.pallas import tpu as pltpu

F32 = jnp.float32
BF16 = jnp.bfloat16

D_MODEL = 1024
HEAD_DIM = 64
N_FOX = 8
N_SB = 8
FOX_W = N_FOX * HEAD_DIM
SB_W = N_SB * HEAD_DIM
EVEN_QKV = 3 * FOX_W + 3 * SB_W
GATE_PAD = 128
EVEN_IN_PAD = EVEN_QKV + GATE_PAD
N_Q = 16
N_KV = 4
GROUP = N_Q // N_KV
Q_W = N_Q * HEAD_DIM
KV_W = N_KV * HEAD_DIM
ODD_IN = Q_W + 2 * KV_W
WINDOW = 128
ROPE_THETA = 10000.0
D_FF = 2816
PLE_DIM = 256
EPS = 1e-6
NEG_INF = -1e30
SCALE = HEAD_DIM ** -0.5

ADAM_LR = 0.001
ADAM_B1 = 0.9
ADAM_B2 = 0.999
ADAM_EPS = 1e-08
ADAM_WD = 0.01
ADAM_STEP = 10

N_CHIPS = 4
VMEM_LIMIT = 48 * 1024 * 1024
MESH = pl.DeviceIdType.MESH

_PACK = (
    ("ev_w_in", 0, 1024, 770, True),
    ("ev_w_out", 0, 256, 1024, False),
    ("od_w_in", 0, 1024, 384, True),
    ("od_w_out", 0, 256, 1024, False),
    ("ffn_w_gate", 0, 1024, 704, True),
    ("ffn_w_gate", 1, 1024, 704, True),
    ("ffn_w_up", 0, 1024, 704, True),
    ("ffn_w_up", 1, 1024, 704, True),
    ("ffn_w_down", 0, 704, 1024, False),
    ("ffn_w_down", 1, 704, 1024, False),
    ("ple_w_proj", 0, 256, 256, True),
    ("ple_w_proj", 1, 256, 256, True),
    ("ple_w_gate", 0, 256, 1024, False),
    ("ple_w_gate", 1, 256, 1024, False),
)
_ROW_ALIGN = 16


def _transposed(r, by_cols):
    return by_cols and r == D_MODEL


def _pack_rows(r, c):
    n = r * c // D_MODEL
    return -(-n // _ROW_ALIGN) * _ROW_ALIGN


PACK_ROWS_HALF = 3328
PACK_ROWS = 2 * PACK_ROWS_HALF
assert sum(_pack_rows(r, c) for _, _, r, c, _ in _PACK) <= PACK_ROWS


def _pick(n, cands):
    for c in cands:
        if n % c == 0:
            return c
    return n


def _cparams(sem):
    return pltpu.CompilerParams(dimension_semantics=sem, vmem_limit_bytes=VMEM_LIMIT)


_DIMS = {
    "nn": (((1,), (0,)), ((), ())),
    "nt": (((1,), (1,)), ((), ())),
    "tn": (((0,), (0,)), ((), ())),
}


def matmul(a, b, mode="nn", out_dtype=F32, residual=None, name="mm"):
    if mode == "nn":
        (m, k), (k2, n) = a.shape, b.shape
    elif mode == "nt":
        (m, k), (n, k2) = a.shape, b.shape
    else:
        (k, m), (k2, n) = a.shape, b.shape
    assert k == k2, (a.shape, b.shape, mode)
    tm = _pick(m, (1024, 1408, 640, 512, 256, 128))
    tn = _pick(n, (1024, 1408, 512, 640, 384, 256, 128))
    tk = _pick(k, (1024, 1408, 640, 512, 256, 128))
    nk = k // tk
    dims = _DIMS[mode]
    has_res = residual is not None

    def body(*refs):
        if has_res:
            a_ref, b_ref, r_ref, o_ref, acc = refs
        else:
            a_ref, b_ref, o_ref, acc = refs
        kk = pl.program_id(2)
        part = lax.dot_general(a_ref[...].astype(BF16), b_ref[...].astype(BF16), dims,
                               preferred_element_type=F32)

        def finish(r):
            if has_res:
                r = r + r_ref[...]
            o_ref[...] = r.astype(out_dtype)

        if nk == 1:
            finish(part)
            return

        @pl.when(kk == 0)
        def _():
            acc[...] = part

        @pl.when((kk > 0) & (kk < nk - 1))
        def _():
            acc[...] += part

        @pl.when(kk == nk - 1)
        def _():
            finish(acc[...] + part)

    if mode == "nn":
        a_spec = pl.BlockSpec((tm, tk), lambda i, j, kk: (i, kk))
        b_spec = pl.BlockSpec((tk, tn), lambda i, j, kk: (kk, j))
    elif mode == "nt":
        a_spec = pl.BlockSpec((tm, tk), lambda i, j, kk: (i, kk))
        b_spec = pl.BlockSpec((tn, tk), lambda i, j, kk: (j, kk))
    else:
        a_spec = pl.BlockSpec((tk, tm), lambda i, j, kk: (kk, i))
        b_spec = pl.BlockSpec((tk, tn), lambda i, j, kk: (kk, j))
    o_spec = pl.BlockSpec((tm, tn), lambda i, j, kk: (i, j))
    in_specs = [a_spec, b_spec] + ([o_spec] if has_res else [])
    args = (a, b) + ((residual,) if has_res else ())
    return pl.pallas_call(
        body, name=name, grid=(m // tm, n // tn, nk),
        in_specs=in_specs, out_specs=o_spec,
        out_shape=jax.ShapeDtypeStruct((m, n), out_dtype),
        scratch_shapes=[pltpu.VMEM((tm, tn), F32)],
        compiler_params=_cparams(("parallel", "parallel", "arbitrary")),
    )(*args)


def _fold8(v):
    r, w = v.shape
    return v.reshape(r // 8, 8, w).sum(axis=0)


ROW_BLOCK_BYTES = 12 * 1024 * 1024


def rowwise(fn, rows, bcasts, outs, accs=(), name="rowwise", reverse=False, row_widths=None):
    t = rows[0].shape[0]
    row_bytes = sum(x.shape[1] * x.dtype.itemsize for x in rows) + sum(w * jnp.dtype(dt).itemsize for w, dt in outs)
    tr = _pick(t, tuple(c for c in (512, 256, 128, 64, 32, 16, 8) if c * row_bytes <= ROW_BLOCK_BYTES or c == 8))
    nr, nb, no, na = len(rows), len(bcasts), len(outs), len(accs)
    steps = t // tr

    def body(*refs):
        ins = [r[...] for r in refs[:nr + nb]]
        out_refs = refs[nr + nb:nr + nb + no]
        acc_refs = refs[nr + nb + no:]
        o, a = fn(*ins)
        for r, v in zip(out_refs, o):
            r[...] = v.astype(r.dtype)
        if na:
            @pl.when(pl.program_id(0) == 0)
            def _():
                for r in acc_refs:
                    r[...] = jnp.zeros_like(r)

            for r, v in zip(acc_refs, a):
                r[...] += v

    if reverse:
        ridx = lambda i: (steps - 1 - i, 0)
    else:
        ridx = lambda i: (i, 0)
    widths = row_widths or [x.shape[1] for x in rows]
    in_specs = [pl.BlockSpec((tr, w), ridx) for w in widths]
    in_specs += [pl.BlockSpec(x.shape, lambda i: (0, 0)) for x in bcasts]
    out_specs = [pl.BlockSpec((tr, w), ridx) for w, _ in outs]
    out_specs += [pl.BlockSpec((8, w), lambda i: (0, 0)) for w in accs]
    out_shape = [jax.ShapeDtypeStruct((t, w), dt) for w, dt in outs]
    out_shape += [jax.ShapeDtypeStruct((8, w), F32) for w in accs]
    res = pl.pallas_call(
        body, name=name, grid=(steps,), in_specs=in_specs, out_specs=out_specs, out_shape=out_shape,
        compiler_params=_cparams(("arbitrary",)),
    )(*rows, *bcasts)
    return res


def _rstd(x):
    return lax.rsqrt(jnp.mean(x * x, axis=-1, keepdims=True) + EPS)


def rmsnorm_fwd(h, g, name):
    def fn(x, gg):
        return ((x * _rstd(x)) * gg,), ()

    return rowwise(fn, [h], [g.reshape(1, -1)], [(D_MODEL, BF16)], name=name)[0]


def _rms_bwd_math(x, gg, dy):
    r = _rstd(x)
    xh = x * r
    u = dy * gg
    dx = r * (u - xh * jnp.mean(u * xh, axis=-1, keepdims=True))
    return dx, dy * xh


def rmsnorm_bwd(h, g, dn, dres, name):
    def fn(x, dy, dr, gg):
        dx, dgp = _rms_bwd_math(x, gg, dy)
        return (dr + dx, dr + dx), (_fold8(dgp),)

    return rowwise(fn, [h, dn, dres], [g.reshape(1, -1)], [(D_MODEL, F32), (D_MODEL, BF16)], [D_MODEL], name=name)


def loss_head(h, g, tgt, name):
    def fn(x, tg, gg):
        y = (x * _rstd(x)) * gg
        e = y - tg
        dy = e * (1.0 / D_MODEL)
        dx, dgp = _rms_bwd_math(x, gg, dy)
        return (dx,), (_fold8(dgp), _fold8(e * e))

    return rowwise(fn, [h, tgt], [g.reshape(1, -1)], [(D_MODEL, F32)], [D_MODEL, D_MODEL], name=name)


def _sigmoid(x):
    return 1.0 / (1.0 + jnp.exp(-x))


FFN_TILE = 256


def ffn_gate_up(n2, w_gate, w_up, name):
    t = n2.shape[0]
    tm = _pick(t, (1024, 512, 256, 128))

    def body(a_ref, wg_ref, wu_ref, g_ref, u_ref, act_ref):
        a = a_ref[...]
        g = lax.dot_general(a, wg_ref[...], _DIMS["nt"], preferred_element_type=F32)
        u = lax.dot_general(a, wu_ref[...], _DIMS["nt"], preferred_element_type=F32)
        g_ref[...] = g
        u_ref[...] = u
        act_ref[...] = ((g * _sigmoid(g)) * u).astype(BF16)

    w_spec = pl.BlockSpec((FFN_TILE, D_MODEL), lambda i, j: (j, 0))
    o_spec = pl.BlockSpec((tm, FFN_TILE), lambda i, j: (i, j))
    return pl.pallas_call(
        body, name=name, grid=(t // tm, D_FF // FFN_TILE),
        in_specs=[pl.BlockSpec((tm, D_MODEL), lambda i, j: (i, 0)), w_spec, w_spec],
        out_specs=[o_spec, o_spec, o_spec],
        out_shape=[jax.ShapeDtypeStruct((t, D_FF), F32), jax.ShapeDtypeStruct((t, D_FF), F32),
                   jax.ShapeDtypeStruct((t, D_FF), BF16)],
        compiler_params=_cparams(("parallel", "parallel")),
    )(n2, w_gate, w_up)


def ffn_gate_up_bwd(dh, w_down, gate, up, name):
    t = dh.shape[0]
    tm = _pick(t, (1024, 512, 256, 128))

    def body(dh_ref, wd_ref, g_ref, u_ref, dg_ref, du_ref, dh_bf):
        @pl.when(pl.program_id(1) == 0)
        def _():
            dh_bf[...] = dh_ref[...].astype(BF16)

        d = lax.dot_general(dh_bf[...], wd_ref[...], _DIMS["nt"], preferred_element_type=F32)
        g = g_ref[...]
        s = _sigmoid(g)
        silu = g * s
        dg_ref[...] = (d * u_ref[...] * (s + silu * (1.0 - s))).astype(BF16)
        du_ref[...] = (d * silu).astype(BF16)

    o_spec = pl.BlockSpec((tm, FFN_TILE), lambda i, j: (i, j))
    return pl.pallas_call(
        body, name=name, grid=(t // tm, D_FF // FFN_TILE),
        in_specs=[pl.BlockSpec((tm, D_MODEL), lambda i, j: (i, 0)),
                  pl.BlockSpec((FFN_TILE, D_MODEL), lambda i, j: (j, 0)), o_spec, o_spec],
        out_specs=[o_spec, o_spec],
        out_shape=[jax.ShapeDtypeStruct((t, D_FF), BF16)] * 2,
        scratch_shapes=[pltpu.VMEM((tm, D_MODEL), BF16)],
        compiler_params=_cparams(("parallel", "arbitrary")),
    )(dh, w_down, gate, up)


def ple_fwd(h, pre, pp, name):
    def fn(x, a, b):
        return (x + _sigmoid(a) * b,), ()

    return rowwise(fn, [h, pre, pp], [], [(D_MODEL, F32)], name=name)[0]


def ple_bwd(dh, pre, pp, name):
    def fn(d, a, b):
        s = _sigmoid(a)
        return (d * b * s * (1.0 - s), d * s), ()

    return rowwise(fn, [dh, pre, pp], [], [(D_MODEL, BF16), (D_MODEL, BF16)], name=name)


def _rot_half_partner(x, first_half):
    w = x.shape[1]
    return jnp.where(first_half, pltpu.roll(x, w - HEAD_DIM // 2, 1), pltpu.roll(x, HEAD_DIM // 2, 1))


def rope_apply(xx, cosw, sinw, backward, name, width=None):
    width = width or xx.shape[1]
    reps = width // 128

    def fn(x, c, s):
        cw = jnp.tile(c, (1, reps))
        sw = jnp.tile(s, (1, reps))
        lane = lax.broadcasted_iota(jnp.int32, x.shape, 1)
        first = (lane % HEAD_DIM) < (HEAD_DIM // 2)
        if backward:
            return (x * cw + _rot_half_partner(x * sw, first),), ()
        return (x * cw + _rot_half_partner(x, first) * sw,), ()

    return rowwise(fn, [xx, cosw, sinw], [], [(width, F32)], name=name, row_widths=[width, 128, 128])[0]


def _log_sigmoid(x):
    return jnp.minimum(x, 0.0) - jnp.log(1.0 + jnp.exp(-jnp.abs(x)))


CUM_BLOCK = 256


def forget_cumsum(flog, bias, name):
    t = flog.shape[0]
    tb = _pick(t, (CUM_BLOCK,))

    def body(x_ref, b_ref, o_ref, carry):
        @pl.when(pl.program_id(0) == 0)
        def _():
            carry[...] = jnp.zeros_like(carry)

        lf = _log_sigmoid(x_ref[...] + b_ref[...])
        r = lax.broadcasted_iota(jnp.int32, (tb, tb), 0)
        c = lax.broadcasted_iota(jnp.int32, (tb, tb), 1)
        tri = (c <= r).astype(F32)
        cum = jnp.dot(tri, lf, preferred_element_type=F32, precision=lax.Precision.HIGHEST) + carry[...]
        o_ref[...] = cum
        carry[...] = cum[tb - 1:tb, :]

    return pl.pallas_call(
        body, name=name, grid=(t // tb,),
        in_specs=[pl.BlockSpec((tb, GATE_PAD), lambda i: (i, 0)), pl.BlockSpec((1, GATE_PAD), lambda i: (0, 0))],
        out_specs=pl.BlockSpec((tb, GATE_PAD), lambda i: (i, 0)),
        out_shape=jax.ShapeDtypeStruct((t, GATE_PAD), F32),
        scratch_shapes=[pltpu.VMEM((1, GATE_PAD), F32)],
        compiler_params=_cparams(("arbitrary",)),
    )(flog, bias)


def forget_cumsum_bwd(d_cum, flog, bias, name):
    t = flog.shape[0]
    tb = _pick(t, (CUM_BLOCK,))
    nb = t // tb

    def body(d_ref, x_ref, b_ref, o_ref, db_ref, carry):
        @pl.when(pl.program_id(0) == 0)
        def _():
            carry[...] = jnp.zeros_like(carry)
            db_ref[...] = jnp.zeros_like(db_ref)

        r = lax.broadcasted_iota(jnp.int32, (tb, tb), 0)
        c = lax.broadcasted_iota(jnp.int32, (tb, tb), 1)
        tri = (c >= r).astype(F32)
        dlf = jnp.dot(tri, d_ref[...], preferred_element_type=F32, precision=lax.Precision.HIGHEST) + carry[...]
        carry[...] = dlf[0:1, :]
        dx = dlf * (1.0 - _sigmoid(x_ref[...] + b_ref[...]))
        o_ref[...] = dx
        db_ref[...] += _fold8(dx)

    rev = lambda i: (nb - 1 - i, 0)
    return pl.pallas_call(
        body, name=name, grid=(nb,),
        in_specs=[pl.BlockSpec((tb, GATE_PAD), rev), pl.BlockSpec((tb, GATE_PAD), rev),
                  pl.BlockSpec((1, GATE_PAD), lambda i: (0, 0))],
        out_specs=[pl.BlockSpec((tb, GATE_PAD), rev), pl.BlockSpec((8, GATE_PAD), lambda i: (0, 0))],
        out_shape=[jax.ShapeDtypeStruct((t, GATE_PAD), F32), jax.ShapeDtypeStruct((8, GATE_PAD), F32)],
        scratch_shapes=[pltpu.VMEM((1, GATE_PAD), F32)],
        compiler_params=_cparams(("arbitrary",)),
    )(d_cum, flog, bias)


TQ = 512
TK = 128
AUG = 128
N_BIAS = 3
SB_CUTOFF = 110.0
FOX_CUTOFF = 112.0


def _dot(a, b):
    return jnp.dot(a, b, preferred_element_type=F32)


def _rel(shape, d):
    return lax.broadcasted_iota(jnp.int32, shape, 0) - lax.broadcasted_iota(jnp.int32, shape, 1) + d


def _put(x, lo, part, hi=None):
    hi = x.shape[1] if hi is None else hi
    pieces = ([x[:, :lo]] if lo else []) + [part] + ([x[:, hi:]] if hi < x.shape[1] else [])
    return part if len(pieces) == 1 else jnp.concatenate(pieces, axis=1)


def _stop_code(first, narrow):
    return first.astype(F32) + jnp.where(narrow, 0.5, 0.0)


def _read_stop_code(code_ref):
    code = jnp.max(code_ref[...])
    first = code.astype(jnp.int32)
    return first, (code - first.astype(F32)) > 0.25


def _split_bf16(x):
    hi = x.astype(BF16)
    return hi, (x - hi.astype(F32)).astype(BF16)


def _tri_dot(tri2, x):
    hi, lo = _split_bf16(x)
    return _dot(tri2, jnp.concatenate([hi, lo], axis=0))


def _q_cols(width, tq):
    return pl.BlockSpec((None, width, tq), lambda h, i: (h, 0, i))


def _q_rows(width, tq):
    return pl.BlockSpec((None, tq, width), lambda h, i: (h, i, 0))


def _kv_rows(t, width):
    return pl.BlockSpec((None, t, width), lambda h, i: (h, 0, 0))


def _kv_tiles(nk):
    return pl.BlockSpec((None, nk, HEAD_DIM, TK), lambda h, i: (h, 0, 0, 0))


def _blocks(t):
    tq = TQ if t % TQ == 0 else TK
    return tq, tq // TK, t // TK


def fox_fwd(qt_aug, k_aug, vt, f_end, k_norm):
    nh, _, t = qt_aug.shape
    tq, ratio, nk = _blocks(t)
    lanes = tq // 128

    def body(q_ref, k_ref, v_ref, fe_ref, kn_ref, o_ref, lse_ref, first_ref):
        i = pl.program_id(1)
        qv = q_ref[...]
        qf = qv[:HEAD_DIM].astype(F32)
        reach = jnp.sqrt(jnp.sum(qf * qf, axis=0, keepdims=True)) * jnp.tile(kn_ref[...], (1, lanes))

        def step(first, carry, masked, width=tq):
            m, l, acc = carry
            scores = []
            for u in range(ratio):
                lo, hi = (u * TK, tq) if masked else (0, width)
                off = pl.multiple_of((first + u) * TK, TK)
                s = _dot(k_ref[pl.ds(off, TK), :], qv[:, lo:hi])
                if masked:
                    s = jnp.where(_rel(s.shape, 0) <= 0, s, NEG_INF)
                scores.append((lo, hi, s))
            m_new = m
            for lo, hi, s in scores:
                m_new = _put(m_new, lo, jnp.maximum(m_new[:, lo:hi], jnp.max(s, axis=0, keepdims=True)), hi)
            alpha = jnp.exp(m - m_new)
            l = alpha * l
            acc = alpha * acc
            for u, (lo, hi, s) in enumerate(scores):
                p = jnp.exp(s - m_new[:, lo:hi])
                l = _put(l, lo, l[:, lo:hi] + jnp.sum(p, axis=0, keepdims=True), hi)
                acc = _put(acc, lo, acc[:, lo:hi] + _dot(v_ref[first + u], p.astype(BF16)), hi)
            return m_new, l, acc

        def slack(done, m):
            return reach + jnp.tile(fe_ref[jnp.maximum(i - 1 - done, 0)], (1, lanes)) - m

        def sweep_down(width, go, state):
            def more(c):
                return go & (c[0] < i) & (jnp.max(slack(c[0], c[1])[:, :width]) > -FOX_CUTOFF)

            def sweep(c):
                return (c[0] + 1,) + step(ratio * (i - 1 - c[0]), c[1:], False, width)

            return lax.while_loop(more, sweep, state)

        init = (jnp.full((1, tq), NEG_INF, F32), jnp.zeros((1, tq), F32), jnp.zeros((HEAD_DIM, tq), F32))
        state = (jnp.int32(0),) + step(ratio * i, init, True)
        if ratio > 1:
            narrow = jnp.max(slack(0, state[1])[:, tq // 2:]) <= -FOX_CUTOFF
            state = sweep_down(tq // 2, narrow, state)
            state = sweep_down(tq, jnp.logical_not(narrow), state)
        else:
            narrow = False
            state = sweep_down(tq, True, state)
        done, m, l, acc = state
        o_ref[...] = acc / l
        lse_ref[...] = m + jnp.log(l)
        first_ref[...] = jnp.full((1, 128), _stop_code(i - done, narrow), F32)

    return pl.pallas_call(
        body, name="fox_fwd", grid=(nh, t // tq),
        in_specs=[_q_cols(AUG, tq), _kv_rows(t, AUG), _kv_tiles(nk),
                  pl.BlockSpec((None, t // tq, 1, 128), lambda h, i: (h, 0, 0, 0)),
                  pl.BlockSpec((None, 1, 128), lambda h, i: (h, 0, 0))],
        out_specs=[_q_cols(HEAD_DIM, tq), _q_cols(1, tq), _q_cols(1, 128)],
        out_shape=[jax.ShapeDtypeStruct((nh, HEAD_DIM, t), F32), jax.ShapeDtypeStruct((nh, 1, t), F32),
                   jax.ShapeDtypeStruct((nh, 1, 128 * (t // tq)), F32)],
        compiler_params=_cparams(("parallel", "arbitrary")),
    )(qt_aug, k_aug, vt, f_end, k_norm)


def fox_bwd(qt_aug, q_aug, k_aug, kt, v, ot, do, dot_, lse, first):
    nh, _, t = qt_aug.shape
    tq, ratio, nk = _blocks(t)

    def body(qt_ref, q_ref, k_ref, kt_ref, v_ref, ot_ref, do_ref, dot_ref, lse_ref, first_ref,
             dqt_ref, dk_ref, dv_ref, rs_ref):
        i = pl.program_id(1)
        start, narrow = _read_stop_code(first_ref)

        @pl.when(i == 0)
        def _():
            dk_ref[...] = jnp.zeros_like(dk_ref)
            dv_ref[...] = jnp.zeros_like(dv_ref)

        qtv = qt_ref[...]
        qv = q_ref[...]
        dob = do_ref[...]
        dotb = dot_ref[...]
        delta = jnp.sum(ot_ref[...] * dotb.astype(F32), axis=0, keepdims=True)
        lse = lse_ref[...]

        def tile(j, carry, lo, hi, masked):
            dqt, rs = carry
            off = pl.multiple_of(j * TK, TK)
            s = _dot(k_ref[pl.ds(off, TK), :], qtv[:, lo:hi])
            p = jnp.exp(s - lse[:, lo:hi])
            if masked:
                p = jnp.where(_rel(s.shape, 0) <= 0, p, 0.0)
            dp = _dot(v_ref[pl.ds(off, TK), :], dotb[:, lo:hi])
            dsb = (p * (dp - delta[:, lo:hi])).astype(BF16)
            dk_ref[pl.ds(off, TK), :] += _dot(dsb, qv[lo:hi, :])
            dv_ref[pl.ds(off, TK), :] += _dot(p.astype(BF16), dob[lo:hi, :])
            return (_put(dqt, lo, dqt[:, lo:hi] + _dot(kt_ref[j], dsb), hi),
                    _put(rs, lo, rs[:, lo:hi] + jnp.sum(dsb.astype(F32), axis=0, keepdims=True), hi))

        def step(first, carry, masked, width=tq):
            for u in range(ratio):
                lo, hi = (u * TK, tq) if masked else (0, width)
                carry = tile(first + u, carry, lo, hi, masked)
            return carry

        carry = (jnp.zeros((HEAD_DIM, tq), F32), jnp.zeros((1, tq), F32))
        if ratio > 1:
            carry = lax.fori_loop(start, jnp.where(narrow, i, start),
                                  lambda jj, c: step(ratio * jj, c, False, tq // 2), carry)
            carry = lax.fori_loop(start, jnp.where(narrow, start, i), lambda jj, c: step(ratio * jj, c, False), carry)
        else:
            carry = lax.fori_loop(start, i, lambda jj, c: step(ratio * jj, c, False), carry)
        dqt, rs = step(ratio * i, carry, True)
        dqt_ref[...] = dqt * SCALE
        rs_ref[...] = rs

    return pl.pallas_call(
        body, name="fox_bwd", grid=(nh, t // tq),
        in_specs=[_q_cols(AUG, tq), _q_rows(AUG, tq), _kv_rows(t, AUG), _kv_tiles(nk), _kv_rows(t, HEAD_DIM),
                  _q_cols(HEAD_DIM, tq), _q_rows(HEAD_DIM, tq), _q_cols(HEAD_DIM, tq), _q_cols(1, tq),
                  _q_cols(1, 128)],
        out_specs=[_q_cols(HEAD_DIM, tq), _kv_rows(t, AUG), _kv_rows(t, HEAD_DIM), _q_cols(1, tq)],
        out_shape=[jax.ShapeDtypeStruct((nh, HEAD_DIM, t), F32), jax.ShapeDtypeStruct((nh, t, AUG), F32),
                   jax.ShapeDtypeStruct((nh, t, HEAD_DIM), F32), jax.ShapeDtypeStruct((nh, 1, t), F32)],
        compiler_params=_cparams(("arbitrary", "arbitrary")),
    )(qt_aug, q_aug, k_aug, kt, v, ot, do, dot_, lse, first)


def _sb_logits(kb, qv, ok):
    z = _dot(kb, qv)
    e = jnp.exp(-jnp.abs(z))
    ll = -(jnp.maximum(z, 0.0) + jnp.log(1.0 + e))
    if ok is not None:
        ll = jnp.where(ok, ll, 0.0)
    return z, e, ll


def _tri(cmp):
    r = lax.broadcasted_iota(jnp.int32, (TK, 2 * TK), 0)
    c = lax.broadcasted_iota(jnp.int32, (TK, 2 * TK), 1) % TK
    return cmp(r, c).astype(BF16)


def sb_fwd(qt, k, vt):
    nh, _, t = qt.shape
    tq, ratio, nk = _blocks(t)

    def body(q_ref, k_ref, v_ref, o_ref, tot_ref, first_ref):
        i = pl.program_id(1)
        qv = q_ref[...]
        tri_after = _tri(lambda r, c: c > r)

        def tile(j, carry, lo, hi, masked):
            c_l, acc = carry
            off = pl.multiple_of(j * TK, TK)
            ok = _rel((TK, hi - lo), 0) < 0 if masked else None
            z, _, ll = _sb_logits(k_ref[pl.ds(off, TK), :], qv[:, lo:hi], ok)
            a = jnp.exp(z + ll + _tri_dot(tri_after, ll) + c_l[:, lo:hi])
            if masked:
                a = jnp.where(ok, a, 0.0)
            return (_put(c_l, lo, c_l[:, lo:hi] + jnp.sum(ll, axis=0, keepdims=True), hi),
                    _put(acc, lo, acc[:, lo:hi] + _dot(v_ref[j], a.astype(BF16)), hi))

        def step(first, carry, masked, width=tq):
            for u in reversed(range(ratio)):
                lo, hi = (u * TK, tq) if masked else (0, width)
                carry = tile(first + u, carry, lo, hi, masked)
            return carry

        def sweep_down(width, go, state):
            def more(c):
                return go & (c[0] < i) & (jnp.max(c[1][:, :width]) > -SB_CUTOFF)

            def sweep(c):
                return (c[0] + 1,) + step(ratio * (i - 1 - c[0]), c[1:], False, width)

            return lax.while_loop(more, sweep, state)

        state = (jnp.int32(0),) + step(ratio * i, (jnp.zeros((1, tq), F32), jnp.zeros((HEAD_DIM, tq), F32)), True)
        if ratio > 1:
            narrow = jnp.max(state[1][:, tq // 2:]) <= -SB_CUTOFF
            state = sweep_down(tq // 2, narrow, state)
            state = sweep_down(tq, jnp.logical_not(narrow), state)
        else:
            narrow = False
            state = sweep_down(tq, True, state)
        done, c_l, acc = state
        o_ref[...] = acc
        tot_ref[...] = c_l
        first_ref[...] = jnp.full((1, 128), _stop_code(i - done, narrow), F32)

    return pl.pallas_call(
        body, name="sb_fwd", grid=(nh, t // tq),
        in_specs=[_q_cols(HEAD_DIM, tq), _kv_rows(t, HEAD_DIM), _kv_tiles(nk)],
        out_specs=[_q_cols(HEAD_DIM, tq), _q_cols(1, tq), _q_cols(1, 128)],
        out_shape=[jax.ShapeDtypeStruct((nh, HEAD_DIM, t), F32), jax.ShapeDtypeStruct((nh, 1, t), F32),
                   jax.ShapeDtypeStruct((nh, 1, 128 * (t // tq)), F32)],
        compiler_params=_cparams(("parallel", "arbitrary")),
    )(qt, k, vt)


def sb_bwd(qt, q, k, kt, v, ltot, first, do, dot_):
    nh, _, t = qt.shape
    tq, ratio, nk = _blocks(t)

    def body(qt_ref, q_ref, k_ref, kt_ref, v_ref, tot_ref, first_ref, do_ref, dot_ref, dqt_ref, dk_ref, dv_ref):
        i = pl.program_id(1)
        start, narrow = _read_stop_code(first_ref)

        @pl.when(i == 0)
        def _():
            dk_ref[...] = jnp.zeros_like(dk_ref)
            dv_ref[...] = jnp.zeros_like(dv_ref)

        qtv = qt_ref[...]
        qv = q_ref[...]
        dob = do_ref[...]
        dotb = dot_ref[...]
        tri_upto = _tri(lambda r, c: c <= r)
        tri_before = _tri(lambda r, c: c < r)

        def tile(j, carry, lo, hi, masked):
            rest, c_w, dqt = carry
            off = pl.multiple_of(j * TK, TK)
            ok = _rel((TK, hi - lo), 0) < 0 if masked else None
            z, e, ll = _sb_logits(k_ref[pl.ds(off, TK), :], qtv[:, lo:hi], ok)
            a = jnp.exp(z + ll + (rest[:, lo:hi] - _tri_dot(tri_upto, ll)))
            if masked:
                a = jnp.where(ok, a, 0.0)
            w = a * _dot(v_ref[pl.ds(off, TK), :], dotb[:, lo:hi])
            before = _tri_dot(tri_before, w) + c_w[:, lo:hi]
            r = 1.0 / (1.0 + e)
            sig = jnp.where(z >= 0.0, r, e * r)
            dz = w - (w + before) * sig
            if masked:
                dz = jnp.where(ok, dz, 0.0)
            dzb = dz.astype(BF16)
            dk_ref[pl.ds(off, TK), :] += _dot(dzb, qv[lo:hi, :])
            dv_ref[pl.ds(off, TK), :] += _dot(a.astype(BF16), dob[lo:hi, :])
            return (_put(rest, lo, rest[:, lo:hi] - jnp.sum(ll, axis=0, keepdims=True), hi),
                    _put(c_w, lo, c_w[:, lo:hi] + jnp.sum(w, axis=0, keepdims=True), hi),
                    _put(dqt, lo, dqt[:, lo:hi] + _dot(kt_ref[j], dzb), hi))

        def step(first, carry, masked, width=tq):
            for u in range(ratio):
                lo, hi = (u * TK, tq) if masked else (0, width)
                carry = tile(first + u, carry, lo, hi, masked)
            return carry

        carry = (tot_ref[...], jnp.zeros((1, tq), F32), jnp.zeros((HEAD_DIM, tq), F32))
        if ratio > 1:
            carry = lax.fori_loop(start, jnp.where(narrow, i, start),
                                  lambda jj, c: step(ratio * jj, c, False, tq // 2), carry)
            carry = lax.fori_loop(start, jnp.where(narrow, start, i), lambda jj, c: step(ratio * jj, c, False), carry)
        else:
            carry = lax.fori_loop(start, i, lambda jj, c: step(ratio * jj, c, False), carry)
        dqt_ref[...] = step(ratio * i, carry, True)[2] * SCALE

    hd = HEAD_DIM
    return pl.pallas_call(
        body, name="sb_bwd", grid=(nh, t // tq),
        in_specs=[_q_cols(hd, tq), _q_rows(hd, tq), _kv_rows(t, hd), _kv_tiles(nk), _kv_rows(t, hd),
                  _q_cols(1, tq), _q_cols(1, 128), _q_rows(hd, tq), _q_cols(hd, tq)],
        out_specs=[_q_cols(hd, tq), _kv_rows(t, hd), _kv_rows(t, hd)],
        out_shape=[jax.ShapeDtypeStruct((nh, hd, t), F32), jax.ShapeDtypeStruct((nh, t, hd), F32),
                   jax.ShapeDtypeStruct((nh, t, hd), F32)],
        compiler_params=_cparams(("arbitrary", "arbitrary")),
    )(qt, q, k, kt, v, ltot, first, do, dot_)


def _swa_q_cols(width):
    return pl.BlockSpec((GROUP, width, WINDOW), lambda g, i: (g, 0, i))


def _swa_q_rows():
    return pl.BlockSpec((GROUP, WINDOW, HEAD_DIM), lambda g, i: (g, i, 0))


def _swa_kv_rows(t):
    return pl.BlockSpec((None, t, HEAD_DIM), lambda g, i: (g, 0, 0))


def _swa_kv_tiles(nk):
    return pl.BlockSpec((None, nk, HEAD_DIM, WINDOW), lambda g, i: (g, 0, 0, 0))


def _swa_sink_spec():
    return pl.BlockSpec((GROUP, 1, 128), lambda g, i: (g, 0, 0))


def _lane_cat(parts):
    return jnp.concatenate(parts, axis=1)


def _swa_window(i):
    jb = jnp.maximum(i - 1, 0)
    start = pl.multiple_of(jb * WINDOW, WINDOW)
    shape = (2 * WINDOW, GROUP * WINDOW)
    query = lax.broadcasted_iota(jnp.int32, shape, 1) % WINDOW
    rel = lax.broadcasted_iota(jnp.int32, shape, 0) - query + (start - i * WINDOW)
    return jb, start, (rel <= 0) & (rel > -WINDOW)


def swa_fwd(qt, k, vt, sinks):
    nh, _, t = qt.shape
    nk = t // WINDOW

    def body(q_ref, k_ref, v_ref, s_ref, o_ref, lse_ref):
        i = pl.program_id(1)
        jb, start, valid = _swa_window(i)
        qv = _lane_cat([q_ref[g] for g in range(GROUP)])
        sink = _lane_cat([s_ref[g] for g in range(GROUP)])
        s = jnp.where(valid, _dot(k_ref[pl.ds(start, 2 * WINDOW), :], qv), NEG_INF)
        m = jnp.maximum(jnp.max(s, axis=0, keepdims=True), sink)
        p = jnp.where(valid, jnp.exp(s - m), 0.0)
        l = jnp.sum(p, axis=0, keepdims=True) + jnp.exp(sink - m)
        pb = p.astype(BF16)
        o = (_dot(v_ref[jb], pb[:WINDOW]) + _dot(v_ref[jb + 1], pb[WINDOW:])) / l
        lse = m + jnp.log(l)
        for g in range(GROUP):
            o_ref[g] = o[:, g * WINDOW:(g + 1) * WINDOW]
            lse_ref[g] = lse[:, g * WINDOW:(g + 1) * WINDOW]

    return pl.pallas_call(
        body, name="swa_fwd", grid=(N_KV, nk),
        in_specs=[_swa_q_cols(HEAD_DIM), _swa_kv_rows(t), _swa_kv_tiles(nk), _swa_sink_spec()],
        out_specs=[_swa_q_cols(HEAD_DIM), _swa_q_cols(1)],
        out_shape=[jax.ShapeDtypeStruct((nh, HEAD_DIM, t), F32), jax.ShapeDtypeStruct((nh, 1, t), F32)],
        compiler_params=_cparams(("parallel", "arbitrary")),
    )(qt, k, vt, sinks)


def swa_bwd(qt, q, k, kt, v, sinks, ot, do, dot_, lse):
    nh, _, t = qt.shape
    nk = t // WINDOW

    def body(qt_ref, q_ref, k_ref, kt_ref, v_ref, s_ref, ot_ref, do_ref, dot_ref, lse_ref,
             dqt_ref, dk_ref, dv_ref, dsink_ref):
        i = pl.program_id(1)

        @pl.when(i == 0)
        def _():
            dk_ref[...] = jnp.zeros_like(dk_ref)
            dv_ref[...] = jnp.zeros_like(dv_ref)
            dsink_ref[...] = jnp.zeros_like(dsink_ref)

        jb, start, valid = _swa_window(i)
        heads = range(GROUP)
        qtv = _lane_cat([qt_ref[g] for g in heads])
        dotb = _lane_cat([dot_ref[g] for g in heads])
        lse = _lane_cat([lse_ref[g] for g in heads])
        sink = _lane_cat([s_ref[g] for g in heads])
        otv = _lane_cat([ot_ref[g] for g in heads])
        q_rows = jnp.concatenate([q_ref[g] for g in heads], axis=0)
        do_rows = jnp.concatenate([do_ref[g] for g in heads], axis=0)
        delta = jnp.sum(otv * dotb.astype(F32), axis=0, keepdims=True)
        p = jnp.where(valid, jnp.exp(_dot(k_ref[pl.ds(start, 2 * WINDOW), :], qtv) - lse), 0.0)
        dsb = (p * (_dot(v_ref[pl.ds(start, 2 * WINDOW), :], dotb) - delta)).astype(BF16)
        dqt = (_dot(kt_ref[jb], dsb[:WINDOW]) + _dot(kt_ref[jb + 1], dsb[WINDOW:])) * SCALE
        dsink = -jnp.exp(sink - lse) * delta
        for g in heads:
            dqt_ref[g] = dqt[:, g * WINDOW:(g + 1) * WINDOW]
            dsink_ref[g] += dsink[:, g * WINDOW:(g + 1) * WINDOW]
        dk_ref[pl.ds(start, 2 * WINDOW), :] += _dot(dsb, q_rows)
        dv_ref[pl.ds(start, 2 * WINDOW), :] += _dot(p.astype(BF16), do_rows)

    hd = HEAD_DIM
    return pl.pallas_call(
        body, name="swa_bwd", grid=(N_KV, nk),
        in_specs=[_swa_q_cols(hd), _swa_q_rows(), _swa_kv_rows(t), _swa_kv_tiles(nk), _swa_kv_rows(t),
                  _swa_sink_spec(), _swa_q_cols(hd), _swa_q_rows(), _swa_q_cols(hd), _swa_q_cols(1)],
        out_specs=[_swa_q_cols(hd), _swa_kv_rows(t), _swa_kv_rows(t), _swa_sink_spec()],
        out_shape=[jax.ShapeDtypeStruct((nh, hd, t), F32), jax.ShapeDtypeStruct((N_KV, t, hd), F32),
                   jax.ShapeDtypeStruct((N_KV, t, hd), F32), jax.ShapeDtypeStruct((nh, 1, 128), F32)],
        compiler_params=_cparams(("arbitrary", "arbitrary")),
    )(qt, q, k, kt, v, sinks, ot, do, dot_, lse)


def head_rows(a, nh):
    t = a.shape[0]
    return a.reshape(t, nh, HEAD_DIM).transpose(1, 0, 2)


def head_cols(a, nh):
    t = a.shape[0]
    return a.reshape(t, nh, HEAD_DIM).transpose(1, 2, 0)


def head_tiles(a, nh):
    t = a.shape[0]
    return a.reshape(t // TK, TK, nh, HEAD_DIM).transpose(2, 0, 3, 1)


def rows_to_flat(a):
    nh, t, _ = a.shape
    return a.transpose(1, 0, 2).reshape(t, nh * HEAD_DIM)


def cols_to_flat(a):
    nh, _, t = a.shape
    return a.transpose(2, 0, 1).reshape(t, nh * HEAD_DIM)


def fox_keys(proj, cum, name):
    t = proj.shape[0]
    tr = _pick(t, (256, 128))
    width = (1 + N_BIAS) * 128

    def body(k_ref, c_ref, o_ref):
        kb = k_ref[...].astype(BF16)
        terms, rest = [], -c_ref[...]
        for _ in range(N_BIAS):
            term = rest.astype(BF16)
            terms.append(term)
            rest = rest - term.astype(F32)
        row = lax.broadcasted_iota(jnp.int32, (width, AUG), 0)
        col = lax.broadcasted_iota(jnp.int32, (width, AUG), 1)
        for h in range(N_FOX):
            pair = kb[:, 128 * (h // 2):128 * (h // 2 + 1)]
            place = (row < 128) & (col < HEAD_DIM) & (row - HEAD_DIM * (h % 2) == col)
            for b in range(N_BIAS):
                place = place | ((row == 128 * (1 + b) + h) & (col == HEAD_DIM + b))
            src = jnp.concatenate([pair] + terms, axis=1)
            o_ref[h] = _dot(src, place.astype(BF16)).astype(BF16)

    return pl.pallas_call(
        body, name=name, grid=(t // tr,),
        in_specs=[pl.BlockSpec((tr, FOX_W), lambda i: (i, 1)), pl.BlockSpec((tr, GATE_PAD), lambda i: (i, 0))],
        out_specs=pl.BlockSpec((N_FOX, tr, AUG), lambda i: (0, i, 0)),
        out_shape=jax.ShapeDtypeStruct((N_FOX, t, AUG), BF16),
        compiler_params=_cparams(("parallel",)),
    )(proj, cum)


def fox_operands(qf, kf, cum_heads):
    t = qf.shape[0]
    nh = cum_heads.shape[0]
    tq = _blocks(t)[0]
    qs = (qf * SCALE).astype(BF16)
    ones_t = jnp.ones((nh, N_BIAS, t), BF16)
    qt_aug = jnp.concatenate([head_cols(qs, nh), ones_t, jnp.zeros((nh, AUG - HEAD_DIM - N_BIAS, t), BF16)], axis=1)
    own_lane = jnp.broadcast_to(jnp.eye(nh, dtype=BF16)[:, None, :], (nh, t, nh))
    q_aug = jnp.concatenate([head_rows(qs, nh), own_lane, jnp.zeros((nh, t, AUG - HEAD_DIM - nh), BF16)], axis=2)
    f_end = jnp.broadcast_to((-cum_heads)[:, tq - 1::tq, None, None], (nh, t // tq, 1, 128))
    k_sq = jnp.sum(jnp.square(kf.astype(F32)).reshape(t, nh, HEAD_DIM), axis=2)
    k_norm = jnp.broadcast_to(1.01 * jnp.sqrt(jnp.max(k_sq, axis=0))[:, None, None], (nh, 1, 128))
    return qt_aug, q_aug, f_end, k_norm


def adamw(w, g, m, v, name):
    cols = w.shape[1]

    def fn(ww, gg, mm, vv):
        mn = ADAM_B1 * mm + (1.0 - ADAM_B1) * gg
        vn = ADAM_B2 * vv + (1.0 - ADAM_B2) * (gg * gg)
        m_hat = mn / (1.0 - ADAM_B1 ** ADAM_STEP)
        v_hat = vn / (1.0 - ADAM_B2 ** ADAM_STEP)
        delta = -ADAM_LR * (m_hat / (jnp.sqrt(v_hat) + ADAM_EPS) + ADAM_WD * ww)
        return (delta, mn, vn), ()

    return rowwise(fn, [w, g, m, v], [], [(cols, F32)] * 3, name=name)


ANY = pl.BlockSpec(memory_space=pl.ANY)


def _my_place():
    return lax.axis_index("x"), lax.axis_index("y"), lax.axis_index("c")


def _flip(coord, bit):
    return 1 - coord if bit else coord


def allgather_chips(w):
    r, c = w.shape
    rh = r // 2

    def body(w_ref, out_ref, send_sems, recv_sems, local_sem):
        x, y, cc = _my_place()
        me = 2 * x + y
        sibling = (x, y, 1 - cc)
        chips = [(_flip(x, kbits >> 1), _flip(y, kbits & 1)) for kbits in (1, 2, 3)]

        def half(chip, hc):
            return out_ref.at[chip, pl.ds(pl.multiple_of(hc * rh, 16), rh)]

        def copy(k, src, dst, to):
            return pltpu.make_async_remote_copy(src_ref=src, dst_ref=dst, send_sem=send_sems.at[k],
                                                recv_sem=recv_sems.at[k], device_id=to, device_id_type=MESH)

        local = pltpu.make_async_copy(w_ref, out_ref.at[me], local_sem)
        local.start()
        my_half = w_ref.at[pl.ds(pl.multiple_of(cc * rh, 16), rh)]
        first = [copy(j, my_half, half(me, cc), (px, py, cc)) for j, (px, py) in enumerate(chips)]
        for cp in first:
            cp.start()
        passed = []
        for j, (px, py) in enumerate(chips):
            landed = half(2 * px + py, cc)
            copy(j, my_half, landed, (px, py, cc)).wait_recv()
            fwd = copy(3 + j, landed, landed, sibling)
            fwd.start()
            passed.append(fwd)
        for j, (px, py) in enumerate(chips):
            theirs = half(2 * px + py, 1 - cc)
            copy(3 + j, theirs, theirs, sibling).wait_recv()
        for cp in first + passed:
            cp.wait_send()
        local.wait()

    return pl.pallas_call(
        body, name="allgather_chips", in_specs=[ANY], out_specs=ANY,
        out_shape=jax.ShapeDtypeStruct((N_CHIPS, r, c), w.dtype),
        scratch_shapes=[pltpu.SemaphoreType.DMA((6,)), pltpu.SemaphoreType.DMA((6,)), pltpu.SemaphoreType.DMA],
    )(w)


def pair_send_other_half(g):
    n, _, rh, c = g.shape

    def body(g_ref, out_ref, send_sem, recv_sem):
        x, y, cc = _my_place()
        cp = pltpu.make_async_remote_copy(
            src_ref=g_ref.at[:, 1 - cc], dst_ref=out_ref, send_sem=send_sem, recv_sem=recv_sem,
            device_id=(x, y, 1 - cc), device_id_type=MESH)
        cp.start()
        cp.wait()

    return pl.pallas_call(
        body, name="pair_send_other_half", in_specs=[ANY], out_specs=ANY,
        out_shape=jax.ShapeDtypeStruct((n, rh, c), g.dtype),
        scratch_shapes=[pltpu.SemaphoreType.DMA, pltpu.SemaphoreType.DMA],
    )(g)


def exchange_chips(s):
    n, rh, c = s.shape

    def body(s_ref, out_ref, send_sems, recv_sems, local_sem):
        x, y, cc = _my_place()
        me = 2 * x + y
        local = pltpu.make_async_copy(s_ref.at[me], out_ref.at[me], local_sem)
        local.start()
        copies = []
        for kbits in (1, 2, 3):
            px, py = _flip(x, kbits >> 1), _flip(y, kbits & 1)
            cp = pltpu.make_async_remote_copy(
                src_ref=s_ref.at[2 * px + py], dst_ref=out_ref.at[me], send_sem=send_sems.at[kbits - 1],
                recv_sem=recv_sems.at[kbits - 1], device_id=(px, py, cc), device_id_type=MESH)
            cp.start()
            copies.append(cp)
        for cp in copies:
            cp.wait()
        local.wait()

    return pl.pallas_call(
        body, name="exchange_chips", in_specs=[ANY], out_specs=ANY,
        out_shape=jax.ShapeDtypeStruct((n, rh, c), s.dtype),
        scratch_shapes=[pltpu.SemaphoreType.DMA((3,)), pltpu.SemaphoreType.DMA((3,)), pltpu.SemaphoreType.DMA],
    )(s)


def pair_swap(tt):
    def body(t_ref, out_ref, send_sem, recv_sem):
        x, y, cc = _my_place()
        cp = pltpu.make_async_remote_copy(
            src_ref=t_ref, dst_ref=out_ref, send_sem=send_sem, recv_sem=recv_sem,
            device_id=(x, y, 1 - cc), device_id_type=MESH)
        cp.start()
        cp.wait()

    return pl.pallas_call(
        body, name="pair_swap", in_specs=[ANY], out_specs=ANY,
        out_shape=jax.ShapeDtypeStruct(tt.shape, tt.dtype),
        scratch_shapes=[pltpu.SemaphoreType.DMA, pltpu.SemaphoreType.DMA],
    )(tt)


SMALL_ROWS = 16


def allreduce_small(v):
    r, c = v.shape
    vm = pl.BlockSpec(memory_space=pltpu.VMEM)

    def body(v_ref, out_ref, slots, send_sems, recv_sems):
        x, y, cc = _my_place()
        me = 4 * x + 2 * y + cc
        slots[me] = v_ref[...]
        copies = []
        for kbits in range(1, 8):
            peer = (_flip(x, kbits >> 2), _flip(y, (kbits >> 1) & 1), _flip(cc, kbits & 1))
            cp = pltpu.make_async_remote_copy(
                src_ref=v_ref, dst_ref=slots.at[me], send_sem=send_sems.at[kbits - 1],
                recv_sem=recv_sems.at[kbits - 1], device_id=peer, device_id_type=MESH)
            cp.start()
            copies.append(cp)
        for cp in copies:
            cp.wait()
        total = slots[0]
        for dev in range(1, 8):
            total = total + slots[dev]
        out_ref[...] = total

    return pl.pallas_call(
        body, name="allreduce_small", in_specs=[vm], out_specs=vm,
        out_shape=jax.ShapeDtypeStruct((r, c), F32),
        scratch_shapes=[pltpu.VMEM((8, r, c), F32), pltpu.SemaphoreType.DMA((7,)), pltpu.SemaphoreType.DMA((7,))],
    )(v)


def add_pair(mine, theirs, name):
    return rowwise(lambda a, b: ((a + b,), ()), [mine, theirs], [], [(mine.shape[1], BF16)], name=name)[0]


def sum_chips(r4, name):
    _, rh, c = r4.shape
    tr = _pick(rh, (256, 128, 64, 32, 16))

    def body(r_ref, o_ref):
        total = r_ref[0].astype(F32)
        for j in range(1, N_CHIPS):
            total = total + r_ref[j].astype(F32)
        o_ref[...] = total

    return pl.pallas_call(
        body, name=name, grid=(rh // tr,),
        in_specs=[pl.BlockSpec((N_CHIPS, tr, c), lambda i: (0, i, 0))],
        out_specs=pl.BlockSpec((tr, c), lambda i: (i, 0)),
        out_shape=jax.ShapeDtypeStruct((rh, c), F32),
        compiler_params=_cparams(("parallel",)),
    )(r4)


def _pad_rows(a, rows):
    return jnp.pad(a, ((0, rows - a.shape[0]), (0, 0))) if rows != a.shape[0] else a


def pack_shards(local):
    parts = []
    for name, layer, r, c, by_cols in _PACK:
        w = local[name][layer].astype(BF16)
        flat = w.T if _transposed(r, by_cols) else w.reshape(r * c // D_MODEL, D_MODEL)
        parts.append(_pad_rows(flat, _pack_rows(r, c)))
    used = sum(p.shape[0] for p in parts)
    parts.append(jnp.zeros((PACK_ROWS - used, D_MODEL), BF16))
    return jnp.concatenate(parts, axis=0)


def unpack_full(gathered):
    full, off = {}, 0
    for name, layer, r, c, by_cols in _PACK:
        n = r * c // D_MODEL
        if _transposed(r, by_cols):
            full[(name, layer)] = gathered[:, off:off + c, :].reshape(N_CHIPS * c, D_MODEL)
            off += _pack_rows(r, c)
            continue
        blk = gathered[:, off:off + n, :].reshape(N_CHIPS, r, c)
        if by_cols:
            full[(name, layer)] = blk.transpose(1, 0, 2).reshape(r, N_CHIPS * c)
        else:
            full[(name, layer)] = blk.reshape(N_CHIPS * r, c)
        off += _pack_rows(r, c)
    return full


def pack_grads(grads):
    parts = []
    for name, layer, r, c, by_cols in _PACK:
        g = grads[(name, layer)]
        if _transposed(r, by_cols):
            blk = g.reshape(N_CHIPS, c, r)
        elif by_cols:
            blk = g.reshape(r, N_CHIPS, c).transpose(1, 0, 2)
        else:
            blk = g.reshape(N_CHIPS, r, c)
        flat = blk.reshape(N_CHIPS, r * c // D_MODEL, D_MODEL)
        rows = _pack_rows(r, c)
        if rows != flat.shape[1]:
            flat = jnp.pad(flat, ((0, 0), (0, rows - flat.shape[1]), (0, 0)))
        parts.append(flat)
    used = sum(p.shape[1] for p in parts)
    parts.append(jnp.zeros((N_CHIPS, PACK_ROWS - used, D_MODEL), F32))
    return jnp.concatenate(parts, axis=1)


def unpack_local(flat):
    out, off = {}, 0
    for name, layer, r, c, by_cols in _PACK:
        n = r * c // D_MODEL
        rows = flat[off:off + n, :]
        out[(name, layer)] = rows.T if _transposed(r, by_cols) else rows.reshape(r, c)
        off += _pack_rows(r, c)
    return out


def rope_tables(pos):
    half = HEAD_DIM // 2
    lane = jnp.arange(128)
    inv = ROPE_THETA ** (-(lane % half).astype(F32) / half)
    ang = pos.astype(F32)[:, None] * inv[None, :]
    sign = jnp.where((lane % HEAD_DIM) < half, -1.0, 1.0).astype(F32)
    return jnp.cos(ang), jnp.sin(ang) * sign[None, :]


def local_step(x, p, pos, tgt, norm_mix, norm_ffn, norm_ple, norm_final, ev_b_f, od_sinks, wfull):
    t = x.shape[0]
    w_in0 = wfull[("ev_w_in", 0)]
    w_in0 = jnp.concatenate([w_in0, jnp.zeros((EVEN_IN_PAD - w_in0.shape[0], D_MODEL), w_in0.dtype)], axis=0)
    b_pad = jnp.zeros((1, GATE_PAD), F32).at[0, :N_FOX].set(ev_b_f[0])
    sinks_b = jnp.broadcast_to(od_sinks[0][:, None, None], (N_Q, 1, 128)).astype(F32)
    cosw, sinw = rope_tables(pos)

    saved = []
    h = x
    for i in range(2):
        s = {"h0": h}
        n1 = rmsnorm_fwd(h, norm_mix[i], f"norm_mix_fwd{i}")
        s["n1"] = n1
        if i == 0:
            proj = matmul(n1, w_in0[:EVEN_QKV], "nt", out_dtype=BF16, name="ev_in")
            flog = matmul(n1, w_in0[EVEN_QKV:], "nt", name="ev_gate")
            cols = [proj[:, j * FOX_W:(j + 1) * FOX_W] for j in range(6)]
            cum = forget_cumsum(flog, b_pad, "forget_cumsum")
            fox = dict(zip(("qt", "q", "f_end", "k_norm"), fox_operands(cols[0], cols[1], cum[:, :N_FOX].T)))
            fox["k"] = fox_keys(proj, cum, "fox_keys")
            fox.update(kt=head_tiles(cols[1].astype(BF16), N_FOX), v=head_rows(cols[2].astype(BF16), N_FOX),
                       vt=head_tiles(cols[2].astype(BF16), N_FOX))
            q_sb = (cols[3] * SCALE).astype(BF16)
            sb = dict(qt=head_cols(q_sb, N_SB), q=head_rows(q_sb, N_SB), k=head_rows(cols[4].astype(BF16), N_SB),
                      kt=head_tiles(cols[4].astype(BF16), N_SB), v=head_rows(cols[5].astype(BF16), N_SB),
                      vt=head_tiles(cols[5].astype(BF16), N_SB))
            fox["ot"], fox["lse"], fox["first"] = fox_fwd(fox["qt"], fox["k"], fox["vt"], fox["f_end"], fox["k_norm"])
            sb["ot"], sb["ltot"], sb["first"] = sb_fwd(sb["qt"], sb["k"], sb["vt"])
            s.update(flog=flog, fox=fox, sb=sb)
            mixin_t = jnp.concatenate([fox["ot"].reshape(FOX_W, t), sb["ot"].reshape(SB_W, t)], axis=0).astype(BF16)
            w_out = wfull[("ev_w_out", 0)]
        else:
            proj = matmul(n1, wfull[("od_w_in", 0)], "nt", name="od_in")
            qk = rope_apply(proj, cosw, sinw, False, "rope_fwd", width=Q_W + KV_W)
            q_sc = (qk[:, :Q_W] * SCALE).astype(BF16)
            k_b = qk[:, Q_W:].astype(BF16)
            v_b = proj[:, Q_W + KV_W:].astype(BF16)
            swa = dict(qt=head_cols(q_sc, N_Q), q=head_rows(q_sc, N_Q), k=head_rows(k_b, N_KV),
                       kt=head_tiles(k_b, N_KV), v=head_rows(v_b, N_KV), vt=head_tiles(v_b, N_KV))
            swa["ot"], swa["lse"] = swa_fwd(swa["qt"], swa["k"], swa["vt"], sinks_b)
            s["swa"] = swa
            mixin_t = swa["ot"].reshape(Q_W, t).astype(BF16)
            w_out = wfull[("od_w_out", 0)]
        s["mixin_t"] = mixin_t
        h = matmul(mixin_t, w_out, "tn", residual=h, name=f"mix_out{i}")
        s["h1"] = h
        n2 = rmsnorm_fwd(h, norm_ffn[i], f"norm_ffn_fwd{i}")
        gate, up, act = ffn_gate_up(n2, wfull[("ffn_w_gate", i)], wfull[("ffn_w_up", i)], f"ffn_gate_up{i}")
        s.update(n2=n2, gate=gate, up=up, act=act)
        h = matmul(act, wfull[("ffn_w_down", i)], residual=h, name=f"ffn_down{i}")
        s["h2"] = h
        n3 = rmsnorm_fwd(h, norm_ple[i], f"norm_ple_fwd{i}")
        pre = matmul(n3, wfull[("ple_w_gate", i)], name=f"ple_gate{i}")
        pp = matmul(p[i], wfull[("ple_w_proj", i)], name=f"ple_proj{i}")
        s.update(n3=n3, pre=pre, pp=pp)
        h = ple_fwd(h, pre, pp, f"ple_fwd{i}")
        saved.append(s)

    dh, dg_final, loss8 = loss_head(h, norm_final, tgt, "loss_head")
    gw = {}
    small = {"norm_final": dg_final, "loss": loss8}
    for i in (1, 0):
        s = saved[i]
        dpre, dpp = ple_bwd(dh, s["pre"], s["pp"], f"ple_bwd{i}")
        gw[("ple_w_gate", i)] = matmul(s["n3"], dpre, "tn", name=f"d_ple_gate{i}")
        gw[("ple_w_proj", i)] = matmul(p[i], dpp, "tn", name=f"d_ple_proj{i}")
        dn3 = matmul(dpre, wfull[("ple_w_gate", i)], "nt", name=f"dn_ple{i}")
        dh, dhb, small[("norm_ple", i)] = rmsnorm_bwd(s["h2"], norm_ple[i], dn3, dh, f"norm_ple_bwd{i}")

        dgate, dup = ffn_gate_up_bwd(dhb, wfull[("ffn_w_down", i)], s["gate"], s["up"], f"ffn_gate_up_bwd{i}")
        gw[("ffn_w_down", i)] = matmul(s["act"], dhb, "tn", name=f"d_ffn_down{i}")
        gw[("ffn_w_gate", i)] = matmul(dgate, s["n2"], "tn", name=f"d_ffn_gate{i}")
        gw[("ffn_w_up", i)] = matmul(dup, s["n2"], "tn", name=f"d_ffn_up{i}")
        dn2 = matmul(dgate, wfull[("ffn_w_gate", i)], name=f"dn_ffn_gate{i}")
        dn2 = matmul(dup, wfull[("ffn_w_up", i)], residual=dn2, name=f"dn_ffn_up{i}")
        dh, dhb, small[("norm_ffn", i)] = rmsnorm_bwd(s["h1"], norm_ffn[i], dn2, dh, f"norm_ffn_bwd{i}")

        if i == 0:
            dmb = matmul(dhb, wfull[("ev_w_out", 0)], "nt", out_dtype=BF16, name="d_mix0")
            gw[("ev_w_out", 0)] = matmul(s["mixin_t"], dhb, name="d_ev_out")
            fox, sb = s["fox"], s["sb"]
            dqt_f, dk_aug, dv_f, ds_rows = fox_bwd(
                fox["qt"], fox["q"], fox["k"], fox["kt"], fox["v"], fox["ot"],
                head_rows(dmb[:, :FOX_W], N_FOX), head_cols(dmb[:, :FOX_W], N_FOX), fox["lse"], fox["first"])
            dqt_s, dk_s, dv_s = sb_bwd(sb["qt"], sb["q"], sb["k"], sb["kt"], sb["v"], sb["ltot"], sb["first"],
                                       head_rows(dmb[:, FOX_W:], N_SB), head_cols(dmb[:, FOX_W:], N_SB))
            ds_cols = jnp.sum(dk_aug[:, :, HEAD_DIM:HEAD_DIM + N_FOX], axis=0)
            d_cum = jnp.pad(ds_rows[:, 0, :].T - ds_cols, ((0, 0), (0, GATE_PAD - N_FOX)))
            dflog, db8 = forget_cumsum_bwd(d_cum, s["flog"], b_pad, "forget_cumsum_bwd")
            small["ev_b_f"] = db8
            parts = (cols_to_flat(dqt_f), rows_to_flat(dk_aug[:, :, :HEAD_DIM]), rows_to_flat(dv_f),
                     cols_to_flat(dqt_s), rows_to_flat(dk_s), rows_to_flat(dv_s), dflog)
            dproj = jnp.concatenate([a.astype(BF16) for a in parts], axis=1)
            dw = matmul(dproj, s["n1"], "tn", name="d_ev_in")
            gw[("ev_w_in", 0)] = dw[:EVEN_QKV + N_FOX]
            dn1 = matmul(dproj, w_in0, name="dn_mix0")
        else:
            dmb = matmul(dhb, wfull[("od_w_out", 0)], "nt", out_dtype=BF16, name="d_mix1")
            gw[("od_w_out", 0)] = matmul(s["mixin_t"], dhb, name="d_od_out")
            swa = s["swa"]
            dqt, dk, dv, dsink = swa_bwd(swa["qt"], swa["q"], swa["k"], swa["kt"], swa["v"], sinks_b, swa["ot"],
                                         head_rows(dmb, N_Q), head_cols(dmb, N_Q), swa["lse"])
            small["od_sinks"] = dsink
            dqk = rope_apply(jnp.concatenate([cols_to_flat(dqt), rows_to_flat(dk)], axis=1), cosw, sinw, True,
                             "rope_bwd")
            dproj = jnp.concatenate([dqk, rows_to_flat(dv)], axis=1).astype(BF16)
            gw[("od_w_in", 0)] = matmul(dproj, s["n1"], "tn", name="d_od_in")
            dn1 = matmul(dproj, wfull[("od_w_in", 0)], name="dn_mix1")
        dh, _, small[("norm_mix", i)] = rmsnorm_bwd(s["h0"], norm_mix[i], dn1, dh, f"norm_mix_bwd{i}")
    return dh, gw, small


_SMALL_ROWS = (("norm_mix", 0), ("norm_mix", 1), ("norm_ffn", 0), ("norm_ffn", 1),
               ("norm_ple", 0), ("norm_ple", 1), "norm_final", "misc", "loss")


def pack_small(small):
    rows = []
    for key in _SMALL_ROWS:
        if key == "misc":
            db = jnp.sum(small["ev_b_f"], axis=0)[:N_FOX]
            dsink = jnp.sum(small["od_sinks"][:, 0, :], axis=1)
            rows.append(jnp.zeros((D_MODEL,), F32).at[:N_FOX].set(db).at[128:128 + N_Q].set(dsink))
        else:
            rows.append(jnp.sum(small[key], axis=0))
    rows += [jnp.zeros((D_MODEL,), F32)] * (SMALL_ROWS - len(rows))
    return jnp.stack(rows)


def kernel(x, p, positions, norm_mix, norm_ffn, norm_ple, norm_final, ev_w_in, ev_b_f, ev_w_out, od_w_in, od_sinks, od_w_out, ffn_w_gate, ffn_w_up, ffn_w_down, ple_w_proj, ple_w_gate, loss_target, m_norm_mix, m_norm_ffn, m_norm_ple, m_norm_final, m_ev_w_in, m_ev_b_f, m_ev_w_out, m_od_w_in, m_od_sinks, m_od_w_out, m_ffn_w_gate, m_ffn_w_up, m_ffn_w_down, m_ple_w_proj, m_ple_w_gate, v_norm_mix, v_norm_ffn, v_norm_ple, v_norm_final, v_ev_w_in, v_ev_b_f, v_ev_w_out, v_od_w_in, v_od_sinks, v_od_w_out, v_ffn_w_gate, v_ffn_w_up, v_ffn_w_down, v_ple_w_proj, v_ple_w_gate):
    local_w = dict(ev_w_in=ev_w_in, ev_w_out=ev_w_out, od_w_in=od_w_in, od_w_out=od_w_out,
                   ffn_w_gate=ffn_w_gate, ffn_w_up=ffn_w_up, ffn_w_down=ffn_w_down,
                   ple_w_proj=ple_w_proj, ple_w_gate=ple_w_gate)
    local_m = dict(ev_w_in=m_ev_w_in, ev_w_out=m_ev_w_out, od_w_in=m_od_w_in, od_w_out=m_od_w_out,
                   ffn_w_gate=m_ffn_w_gate, ffn_w_up=m_ffn_w_up, ffn_w_down=m_ffn_w_down,
                   ple_w_proj=m_ple_w_proj, ple_w_gate=m_ple_w_gate)
    local_v = dict(ev_w_in=v_ev_w_in, ev_w_out=v_ev_w_out, od_w_in=v_od_w_in, od_w_out=v_od_w_out,
                   ffn_w_gate=v_ffn_w_gate, ffn_w_up=v_ffn_w_up, ffn_w_down=v_ffn_w_down,
                   ple_w_proj=v_ple_w_proj, ple_w_gate=v_ple_w_gate)

    wfull = unpack_full(allgather_chips(pack_shards(local_w)))
    grad_x, gw, small = local_step(x[0], p[:, 0], positions[0], loss_target[0], norm_mix, norm_ffn, norm_ple,
                                   norm_final, ev_b_f, od_sinks, wfull)

    cc = lax.axis_index("c")
    g4 = pack_grads(gw).reshape(N_CHIPS, 2, PACK_ROWS_HALF, D_MODEL)
    theirs = pair_send_other_half(g4)
    mine = lax.dynamic_index_in_dim(g4, cc, axis=1, keepdims=False)
    pair_sum = add_pair(mine.reshape(N_CHIPS * PACK_ROWS_HALF, D_MODEL),
                        theirs.reshape(N_CHIPS * PACK_ROWS_HALF, D_MODEL), "add_pair")
    from_chips = exchange_chips(pair_sum.reshape(N_CHIPS, PACK_ROWS_HALF, D_MODEL))
    half_sum = sum_chips(from_chips, "sum_chips")
    other_half = pair_swap(half_sum)
    low = jnp.where(cc == 0, half_sum, other_half)
    high = jnp.where(cc == 0, other_half, half_sum)
    g_local = unpack_local(jnp.concatenate([low, high], axis=0))

    red = allreduce_small(pack_small(small))
    loss = 0.5 * jnp.sum(red[8]) / D_MODEL
    pad_small = lambda a: jnp.zeros((D_MODEL,), F32).at[:N_FOX].set(a[0][0]).at[128:128 + N_Q].set(a[1][0])
    stack_small = lambda a: jnp.concatenate(
        [a[0], a[1], a[2], a[3][None], pad_small(a[4:6])[None], jnp.zeros((SMALL_ROWS - 8, D_MODEL), F32)], axis=0)
    w_small = stack_small((norm_mix, norm_ffn, norm_ple, norm_final, ev_b_f, od_sinks))
    m_small = stack_small((m_norm_mix, m_norm_ffn, m_norm_ple, m_norm_final, m_ev_b_f, m_od_sinks))
    v_small = stack_small((v_norm_mix, v_norm_ffn, v_norm_ple, v_norm_final, v_ev_b_f, v_od_sinks))
    g_small = red.at[8].set(0.0)
    upd_small = (g_small,) + tuple(adamw(w_small, g_small, m_small, v_small, "adamw_small"))

    def split_small(a):
        return (a[0:2], a[2:4], a[4:6], a[6], a[7, :N_FOX][None], a[7, 128:128 + N_Q][None])

    small_out = [split_small(a) for a in upd_small]

    big_names = ("ev_w_in", "ev_w_out", "od_w_in", "od_w_out", "ffn_w_gate", "ffn_w_up", "ffn_w_down",
                 "ple_w_proj", "ple_w_gate")
    big_out = {}
    for name in big_names:
        w = local_w[name]
        layers, r, c = w.shape
        g = jnp.concatenate([g_local[(name, i)] for i in range(layers)], axis=0)
        res = adamw(w.reshape(layers * r, c), g, local_m[name].reshape(layers * r, c),
                    local_v[name].reshape(layers * r, c), f"adamw_{name}")
        big_out[name] = [a.reshape(layers, r, c) for a in (g,) + tuple(res)]

    outs = [loss, grad_x[None]]
    for kind in range(4):
        sm = small_out[kind]
        outs += [sm[0], sm[1], sm[2], sm[3],
                 big_out["ev_w_in"][kind], sm[4], big_out["ev_w_out"][kind],
                 big_out["od_w_in"][kind], sm[5], big_out["od_w_out"][kind],
                 big_out["ffn_w_gate"][kind], big_out["ffn_w_up"][kind], big_out["ffn_w_down"][kind],
                 big_out["ple_w_proj"][kind], big_out["ple_w_gate"][kind]]
    return tuple(outs)
```
